```python
import math
import jax, jax.numpy as jnp
from jax import lax
import numpy as np

D_MODEL = 2048
BATCH = 1
SEQ = 16384
DEPTH = 1
DEC_BATCH = 1
DEC_SEQ = 8192
PAST_LEN = 128

HEAD_DIM = 128
N_Q_HEADS = 8
N_KV_HEADS = 2
Q_PER_KV = N_Q_HEADS // N_KV_HEADS
ATTN_WIDTH = N_Q_HEADS * HEAD_DIM
KV_WIDTH = N_KV_HEADS * HEAD_DIM
WINDOW = 128
BLOCK = 128
GMLP_WIDTH = D_MODEL // 2
GMLP_GROUPS = 8
GMLP_GROUP_DIM = GMLP_WIDTH // GMLP_GROUPS
CHUNK = 128
N_EXPERT_GROUPS = 4
EXPERTS_PER_GROUP = 4
N_EXPERTS = N_EXPERT_GROUPS * EXPERTS_PER_GROUP
EXPERT_TOP_K = 2
EXPERT_FF = 512
LN_EPS = 1e-5
DEEPNORM_ALPHA = (2.0 * DEPTH) ** 0.25
DEEPNORM_BETA = (8.0 * DEPTH) ** -0.25
NEG_INF = -1e9
IN_SPLITS = tuple(np.cumsum([ATTN_WIDTH, KV_WIDTH, KV_WIDTH, GMLP_WIDTH, GMLP_WIDTH, D_MODEL]).tolist())
IN_COLS = ATTN_WIDTH + 2 * KV_WIDTH + 2 * GMLP_WIDTH + 2 * D_MODEL

kernel_name = "hybrid_swa_gmlp_hmoe_deepnorm_encoder"


def layer_norm(x, g, b):
    xf = x.astype(jnp.float32)
    mu = jnp.mean(xf, axis=-1, keepdims=True)
    xc = xf - mu
    var = jnp.mean(xc * xc, axis=-1, keepdims=True)
    y = xc * lax.rsqrt(var + LN_EPS) * g.astype(jnp.float32) + b.astype(jnp.float32)
    return y.astype(x.dtype)


def alibi_slopes():
    h = jnp.arange(1, N_Q_HEADS + 1, dtype=jnp.float32)
    return jnp.exp2(-8.0 * h / N_Q_HEADS)


def windowed_attention(q, k, v, sink):
    B, S, _ = q.shape
    nb = S // BLOCK
    qb = q.reshape(B, nb, BLOCK, N_KV_HEADS, Q_PER_KV, HEAD_DIM)

    def bands(t):
        t = t.reshape(B, S, N_KV_HEADS, HEAD_DIM)
        tp = jnp.pad(t, ((0, 0), (BLOCK, BLOCK), (0, 0), (0, 0)))
        tb = tp.reshape(B, nb + 2, BLOCK, N_KV_HEADS, HEAD_DIM)
        return jnp.concatenate([tb[:, :-2], tb[:, 1:-1], tb[:, 2:]], axis=2)

    kb, vb = bands(k), bands(v)
    scores = jnp.einsum('bnqhgd,bnkhd->bnhgqk', qb, kb,
                        preferred_element_type=jnp.float32) * (HEAD_DIM ** -0.5)
    qi = jnp.arange(BLOCK)
    kj = jnp.arange(3 * BLOCK)
    dist = jnp.abs(kj[None, :] - BLOCK - qi[:, None])
    slopes = alibi_slopes().reshape(N_KV_HEADS, Q_PER_KV, 1, 1)
    bias = -slopes * dist.astype(jnp.float32)
    key_pos = jnp.arange(nb)[:, None] * BLOCK - BLOCK + kj[None, :]
    in_seq = (key_pos >= 0) & (key_pos < S)
    mask = (dist <= WINDOW)[None, :, :] & in_seq[:, None, :]
    scores = jnp.where(mask[None, :, None, None], scores + bias, NEG_INF)
    sink_f = sink.astype(jnp.float32).reshape(N_KV_HEADS, Q_PER_KV, 1, 1)
    m = jnp.maximum(jnp.max(scores, axis=-1, keepdims=True), sink_f)
    p = jnp.exp(scores - m)
    denom = jnp.sum(p, axis=-1, keepdims=True) + jnp.exp(sink_f - m)
    o = jnp.einsum('bnhgqk,bnkhd->bnqhgd', (p / denom).astype(vb.dtype), vb)
    return o.reshape(B, S, ATTN_WIDTH)


def chunked_spatial_gating(u, vg, w_s, b_s, ln_g, ln_b):
    B, S, _ = u.shape
    nc = S // CHUNK
    u = jax.nn.gelu(u)
    vg = jax.nn.gelu(vg).reshape(B, S, GMLP_GROUPS, GMLP_GROUP_DIM)
    vg = layer_norm(vg, ln_g, ln_b).reshape(B, nc, CHUNK, GMLP_GROUPS, GMLP_GROUP_DIM)
    s = jnp.einsum('gts,bcsgd->bctgd', w_s, vg) + b_s.T[:, :, None]
    return u * s.reshape(B, S, GMLP_WIDTH)


def hierarchical_moe(x, w_group, b_group, w_router, b_router, w_gate, w_up, w_down):
    B, S, D = x.shape
    t = x.reshape(B * S, D)
    group_logits = (t @ w_group).astype(jnp.float32) + b_group.astype(jnp.float32)
    group_probs = jax.nn.softmax(group_logits, axis=-1)
    _, g_idx = lax.top_k(group_logits, 1)
    p_group = jnp.take_along_axis(group_probs, g_idx, axis=1)[:, 0]
    exp_logits = ((t @ w_router).astype(jnp.float32) + b_router.astype(jnp.float32))
    exp_logits = exp_logits.reshape(-1, N_EXPERT_GROUPS, EXPERTS_PER_GROUP)
    in_group = jnp.take_along_axis(exp_logits, g_idx[:, :, None], axis=1)[:, 0]
    top_v, top_i = lax.top_k(in_group, EXPERT_TOP_K)
    top_w = jax.nn.softmax(top_v, axis=-1)
    within = jnp.sum(jax.nn.one_hot(top_i, EXPERTS_PER_GROUP) * top_w[..., None], axis=1)
    combine = (jax.nn.one_hot(g_idx[:, 0], N_EXPERT_GROUPS)[:, :, None]
               * within[:, None, :] * p_group[:, None, None]).reshape(-1, N_EXPERTS)
    h = jax.nn.silu(jnp.einsum('td,edf->tef', t, w_gate)) * jnp.einsum('td,edf->tef', t, w_up)
    h = h * combine[:, :, None].astype(h.dtype)
    y = jnp.einsum('tef,efd->td', h, w_down)
    return y.reshape(B, S, D)


def encoder_trunk(x, in_ln_g, in_ln_b, w_in, attn_sink, gmlp_w_s, gmlp_b_s, gmlp_ln_g, gmlp_ln_b,
                  w_attn_branch, w_gmlp_branch, w_out, ln1_g, ln1_b,
                  router_w_group, router_b_group, router_w_expert, router_b_expert,
                  w_expert_gate, w_expert_up, w_expert_down, ln2_g, ln2_b):
    x = layer_norm(x, in_ln_g, in_ln_b)
    for l in range(DEPTH):
        proj = x @ w_in[l]
        q, k, v, u, vg, gate_a, gate_b = jnp.split(proj, IN_SPLITS, axis=-1)
        a = windowed_attention(q, k, v, attn_sink[l])
        s = chunked_spatial_gating(u, vg, gmlp_w_s[l], gmlp_b_s[l], gmlp_ln_g[l], gmlp_ln_b[l])
        merged = (jax.nn.sigmoid(gate_a) * (a @ w_attn_branch[l])
                  + jax.nn.sigmoid(gate_b) * (s @ w_gmlp_branch[l]))
        mix = merged @ w_out[l]
        x = layer_norm(DEEPNORM_ALPHA * x + mix, ln1_g[l], ln1_b[l])
        f = hierarchical_moe(x, router_w_group[l], router_b_group[l], router_w_expert[l], router_b_expert[l],
                             w_expert_gate[l], w_expert_up[l], w_expert_down[l])
        x = layer_norm(DEEPNORM_ALPHA * x + f, ln2_g[l], ln2_b[l])
    return x


def setup_inputs(seed: int = 0) -> dict:
    key = jax.random.key(seed)
    ks = jax.random.split(key, 32)
    f32 = jnp.float32
    nrm = lambda k, shape, scale: jax.random.normal(k, shape, f32) * scale
    L, D = DEPTH, D_MODEL
    return {
        "x_prompt": nrm(ks[0], (BATCH, SEQ, D), 1.0),
        "x_sample": nrm(ks[1], (DEC_BATCH, DEC_SEQ, D), 1.0),
        "in_ln_g": 1.0 + nrm(ks[2], (D,), 0.1),
        "in_ln_b": nrm(ks[3], (D,), 0.1),
        "w_in": nrm(ks[4], (L, D, IN_COLS), D ** -0.5),
        "attn_sink": nrm(ks[5], (L, N_Q_HEADS), 0.5),
        "gmlp_w_s": nrm(ks[6], (L, GMLP_GROUPS, CHUNK, CHUNK), CHUNK ** -0.5),
        "gmlp_b_s": 1.0 + nrm(ks[7], (L, GMLP_GROUPS, CHUNK), 0.1),
        "gmlp_ln_g": 1.0 + nrm(ks[8], (L, GMLP_GROUPS, GMLP_GROUP_DIM), 0.1),
        "gmlp_ln_b": nrm(ks[9], (L, GMLP_GROUPS, GMLP_GROUP_DIM), 0.1),
        "w_attn_branch": nrm(ks[10], (L, ATTN_WIDTH, D), ATTN_WIDTH ** -0.5),
        "w_gmlp_branch": nrm(ks[11], (L, GMLP_WIDTH, D), GMLP_WIDTH ** -0.5),
        "w_out": nrm(ks[12], (L, D, D), DEEPNORM_BETA * D ** -0.5),
        "ln1_g": 1.0 + nrm(ks[13], (L, D), 0.1),
        "ln1_b": nrm(ks[14], (L, D), 0.1),
        "router_w_group": nrm(ks[15], (L, D, N_EXPERT_GROUPS), D ** -0.5),
        "router_b_group": nrm(ks[16], (L, N_EXPERT_GROUPS), 0.01),
        "router_w_expert": nrm(ks[17], (L, D, N_EXPERTS), D ** -0.5),
        "router_b_expert": nrm(ks[18], (L, N_EXPERTS), 0.01),
        "w_expert_gate": nrm(ks[19], (L, N_EXPERTS, D, EXPERT_FF), D ** -0.5),
        "w_expert_up": nrm(ks[20], (L, N_EXPERTS, D, EXPERT_FF), D ** -0.5),
        "w_expert_down": nrm(ks[21], (L, N_EXPERTS, EXPERT_FF, D), DEEPNORM_BETA * EXPERT_FF ** -0.5),
        "ln2_g": 1.0 + nrm(ks[22], (L, D), 0.1),
        "ln2_b": nrm(ks[23], (L, D), 0.1),
    }


def reference(x_prompt, x_sample, in_ln_g, in_ln_b, w_in, attn_sink, gmlp_w_s, gmlp_b_s, gmlp_ln_g, gmlp_ln_b,
              w_attn_branch, w_gmlp_branch, w_out, ln1_g, ln1_b,
              router_w_group, router_b_group, router_w_expert, router_b_expert,
              w_expert_gate, w_expert_up, w_expert_down, ln2_g, ln2_b):
    y_prompt = encoder_trunk(x_prompt, in_ln_g, in_ln_b, w_in, attn_sink, gmlp_w_s, gmlp_b_s, gmlp_ln_g, gmlp_ln_b,
                             w_attn_branch, w_gmlp_branch, w_out, ln1_g, ln1_b,
                             router_w_group, router_b_group, router_w_expert, router_b_expert,
                             w_expert_gate, w_expert_up, w_expert_down, ln2_g, ln2_b)
    y_sample = encoder_trunk(x_sample, in_ln_g, in_ln_b, w_in, attn_sink, gmlp_w_s, gmlp_b_s, gmlp_ln_g, gmlp_ln_b,
                             w_attn_branch, w_gmlp_branch, w_out, ln1_g, ln1_b,
                             router_w_group, router_b_group, router_w_expert, router_b_expert,
                             w_expert_gate, w_expert_up, w_expert_down, ln2_g, ln2_b)
    return (y_prompt, y_sample)
```

```python
import functools

import numpy as np
import jax
import jax.numpy as jnp
from jax import lax
from jax.experimental import pallas as pl
from jax.experimental.pallas import tpu as pltpu

F32 = jnp.float32
BF16 = jnp.bfloat16

D_MODEL = 2048
HEAD_DIM = 128
N_Q_HEADS = 8
N_KV_HEADS = 2
Q_PER_KV = N_Q_HEADS // N_KV_HEADS
ATTN_WIDTH = N_Q_HEADS * HEAD_DIM
KV_WIDTH = N_KV_HEADS * HEAD_DIM
WINDOW = 128
BLOCK = 128
GMLP_WIDTH = D_MODEL // 2
GMLP_GROUPS = 8
GMLP_GROUP_DIM = GMLP_WIDTH // GMLP_GROUPS
N_EXPERT_GROUPS = 4
EXPERTS_PER_GROUP = 4
N_EXPERTS = N_EXPERT_GROUPS * EXPERTS_PER_GROUP
EXPERT_FF = 512
LN_EPS = 1e-5
DEEPNORM_ALPHA = 2.0 ** 0.25
NEG_INF = -1e9
ATTN_SCALE = HEAD_DIM ** -0.5

_C_Q = 0
_C_K = _C_Q + ATTN_WIDTH
_C_V = _C_K + KV_WIDTH
_C_U = _C_V + KV_WIDTH
_C_VG = _C_U + GMLP_WIDTH
_C_GA = _C_VG + GMLP_WIDTH
_C_GB = _C_GA + D_MODEL
IN_COLS = _C_GB + D_MODEL

LANES = 128
N_PAIRS = 6
N_BUCKETS = N_EXPERT_GROUPS * N_PAIRS
ROUTE_COLS = 128
ROW_W = D_MODEL + ROUTE_COLS
TM = 256
COL_CHUNK = 512
VMEM_LIMIT = 56 * 1024 * 1024

_SLOPES = [float(2.0 ** (-8.0 * (h + 1) / N_Q_HEADS)) for h in range(N_Q_HEADS)]


def _layer_norm(x, g, b):
    mu = jnp.mean(x, axis=-1, keepdims=True)
    xc = x - mu
    var = jnp.mean(xc * xc, axis=-1, keepdims=True)
    return xc * lax.rsqrt(var + LN_EPS) * g + b


def _const_spec(shape):
    nd = len(shape)
    return pl.BlockSpec(shape, lambda i, *_: (0,) * nd, pipeline_mode=pl.Buffered(1))


def _proj_kernel(n_prompt_tiles, xp_ref, xs_ref, g0_ref, b0_ref, w_ref, lg_ref, lb_ref,
                 q_ref, k_ref, v_ref, gu_ref, vgn_ref, sa_ref, sb_ref, xn_scr):
    i = pl.program_id(0)

    @pl.when(i < n_prompt_tiles)
    def _():
        xn_scr[...] = _layer_norm(xp_ref[...], g0_ref[...], b0_ref[...]).astype(BF16)

    @pl.when(i >= n_prompt_tiles)
    def _():
        xn_scr[...] = _layer_norm(xs_ref[...], g0_ref[...], b0_ref[...]).astype(BF16)

    xn = xn_scr[...]

    def proj(c0, width):
        return jnp.dot(xn, w_ref[:, c0:c0 + width], preferred_element_type=F32)

    for c in range(0, ATTN_WIDTH, COL_CHUNK):
        q_ref[:, c:c + COL_CHUNK] = proj(_C_Q + c, COL_CHUNK).astype(BF16)
    kv = proj(_C_K, 2 * KV_WIDTH)
    k_ref[...] = kv[:, :KV_WIDTH].astype(BF16)
    v_ref[...] = kv[:, KV_WIDTH:].astype(BF16)
    for c in range(0, GMLP_WIDTH, COL_CHUNK):
        gu_ref[:, c:c + COL_CHUNK] = jax.nn.gelu(proj(_C_U + c, COL_CHUNK)).astype(BF16)
    for c in range(0, GMLP_WIDTH, COL_CHUNK):
        vg = jax.nn.gelu(proj(_C_VG + c, COL_CHUNK))
        for j in range(COL_CHUNK // GMLP_GROUP_DIM):
            grp = c // GMLP_GROUP_DIM + j
            blk = vg[:, j * GMLP_GROUP_DIM:(j + 1) * GMLP_GROUP_DIM]
            y = _layer_norm(blk, lg_ref[grp:grp + 1, :], lb_ref[grp:grp + 1, :])
            vgn_ref[:, grp * GMLP_GROUP_DIM:(grp + 1) * GMLP_GROUP_DIM] = y.astype(BF16)
    for c in range(0, D_MODEL, COL_CHUNK):
        sa_ref[:, c:c + COL_CHUNK] = jax.nn.sigmoid(proj(_C_GA + c, COL_CHUNK)).astype(BF16)
    for c in range(0, D_MODEL, COL_CHUNK):
        sb_ref[:, c:c + COL_CHUNK] = jax.nn.sigmoid(proj(_C_GB + c, COL_CHUNK)).astype(BF16)


def _x_specs(n_prompt_tiles):
    xp = pl.BlockSpec((TM, D_MODEL), lambda i, *_: (jnp.minimum(i, n_prompt_tiles - 1), 0))
    xs = pl.BlockSpec((TM, D_MODEL), lambda i, *_: (jnp.maximum(i - n_prompt_tiles, 0), 0))
    return xp, xs


def _proj_call(xp, xs, g0, b0, w_in, lg, lb):
    n_p, n_s = xp.shape[0] // TM, xs.shape[0] // TM
    t_all = xp.shape[0] + xs.shape[0]
    xp_spec, xs_spec = _x_specs(n_p)

    def tok(width):
        return pl.BlockSpec((TM, width), lambda i: (i, 0))

    widths = (ATTN_WIDTH, KV_WIDTH, KV_WIDTH, GMLP_WIDTH, GMLP_WIDTH, D_MODEL, D_MODEL)
    return pl.pallas_call(
        functools.partial(_proj_kernel, n_p),
        out_shape=[jax.ShapeDtypeStruct((t_all, w), BF16) for w in widths],
        grid=(n_p + n_s,),
        in_specs=[xp_spec, xs_spec, _const_spec((1, D_MODEL)), _const_spec((1, D_MODEL)),
                  _const_spec((D_MODEL, IN_COLS)),
                  _const_spec((GMLP_GROUPS, GMLP_GROUP_DIM)), _const_spec((GMLP_GROUPS, GMLP_GROUP_DIM))],
        out_specs=[tok(w) for w in widths],
        scratch_shapes=[pltpu.VMEM((TM, D_MODEL), BF16)],
        compiler_params=pltpu.CompilerParams(dimension_semantics=("arbitrary",),
                                             vmem_limit_bytes=VMEM_LIMIT),
        name="proj",
    )(xp, xs, g0, b0, w_in, lg, lb)


def _route(logits_t):
    row = lambda j: logits_t[j:j + 1, :]
    gl = [row(j) for j in range(N_EXPERT_GROUPS)]
    gmax, gidx = gl[0], jnp.zeros(gl[0].shape, jnp.int32)
    for j in range(1, N_EXPERT_GROUPS):
        better = gl[j] > gmax
        gmax = jnp.where(better, gl[j], gmax)
        gidx = jnp.where(better, j, gidx)
    gsum = jnp.exp(gl[0] - gmax)
    for j in range(1, N_EXPERT_GROUPS):
        gsum = gsum + jnp.exp(gl[j] - gmax)
    p_group = 1.0 / gsum

    ig = []
    for e in range(EXPERTS_PER_GROUP):
        v = row(N_EXPERT_GROUPS + (N_EXPERT_GROUPS - 1) * EXPERTS_PER_GROUP + e)
        for g in range(N_EXPERT_GROUPS - 2, -1, -1):
            v = jnp.where(gidx == g, row(N_EXPERT_GROUPS + g * EXPERTS_PER_GROUP + e), v)
        ig.append(v)
    v1, i1 = ig[0], jnp.zeros(ig[0].shape, jnp.int32)
    for e in range(1, EXPERTS_PER_GROUP):
        better = ig[e] > v1
        v1 = jnp.where(better, ig[e], v1)
        i1 = jnp.where(better, e, i1)
    v2 = jnp.where(i1 == 0, ig[1], ig[0])
    i2 = jnp.where(i1 == 0, 1, 0).astype(jnp.int32)
    for e in range(1, EXPERTS_PER_GROUP):
        better = jnp.logical_and(i1 != e, ig[e] > v2)
        v2 = jnp.where(better, ig[e], v2)
        i2 = jnp.where(better, e, i2)
    ev = jnp.exp(v2 - v1)
    ssum = 1.0 + ev
    w1 = 1.0 / ssum
    w2 = ev / ssum
    first_lo = i1 < i2
    lo = jnp.minimum(i1, i2)
    hi = jnp.maximum(i1, i2)
    c_lo = jnp.where(first_lo, w1, w2) * p_group
    c_hi = jnp.where(first_lo, w2, w1) * p_group
    pair = jnp.where(lo == 0, hi - 1, jnp.where(lo == 1, hi + 1, N_PAIRS - 1))
    bucket = gidx * N_PAIRS + pair
    return c_lo, c_hi, bucket.astype(F32)


def _mix_kernel(n_prompt_tiles, seq_edges_first, seq_edges_last,
                sink_ref, xp_ref, xs_ref, g0_ref, b0_ref, q_ref,
                kp_ref, km_ref, kn_ref, vp_ref, vm_ref, vn_ref,
                gu_ref, vgn_ref, sa_ref, sb_ref, ws_ref, bs_ref,
                wa_ref, wb_ref, wo_ref, g1_ref, b1_ref, wr_ref, br_ref,
                x1e_ref, rt_ref,
                xn_scr, kf_scr, vf_scr, a_scr, sg_scr, mg_scr, mix_scr):
    i = pl.program_id(0)
    nblk = TM // BLOCK

    @pl.when(i < n_prompt_tiles)
    def _():
        xn_scr[...] = _layer_norm(xp_ref[...], g0_ref[...], b0_ref[...])

    @pl.when(i >= n_prompt_tiles)
    def _():
        xn_scr[...] = _layer_norm(xs_ref[...], g0_ref[...], b0_ref[...])

    kf_scr[0:BLOCK, :] = kp_ref[...]
    kf_scr[BLOCK:BLOCK + TM, :] = km_ref[...]
    kf_scr[BLOCK + TM:, :] = kn_ref[...]
    vf_scr[0:BLOCK, :] = vp_ref[...]
    vf_scr[BLOCK:BLOCK + TM, :] = vm_ref[...]
    vf_scr[BLOCK + TM:, :] = vn_ref[...]

    first_blk = i * nblk
    last_blk = i * nblk + nblk - 1
    has_prev = jnp.logical_and(*[first_blk != e for e in seq_edges_first])
    has_next = jnp.logical_and(*[last_blk != e for e in seq_edges_last])
    kj = lax.broadcasted_iota(jnp.int32, (BLOCK, 3 * BLOCK), 1)
    qi = lax.broadcasted_iota(jnp.int32, (BLOCK, 3 * BLOCK), 0)
    dist = jnp.abs(kj - BLOCK - qi)
    in_window = dist <= WINDOW
    dist_f = dist.astype(F32)
    lo_key = jnp.where(has_prev, 0, BLOCK)
    hi_key = jnp.where(has_next, 3 * BLOCK, 2 * BLOCK)

    for j in range(nblk):
        mask = in_window
        if j == 0:
            mask = jnp.logical_and(mask, kj >= lo_key)
        if j == nblk - 1:
            mask = jnp.logical_and(mask, kj < hi_key)
        r0 = j * BLOCK
        for kvh in range(N_KV_HEADS):
            c0 = kvh * HEAD_DIM
            kb = kf_scr[r0:r0 + 3 * BLOCK, c0:c0 + HEAD_DIM]
            vb = vf_scr[r0:r0 + 3 * BLOCK, c0:c0 + HEAD_DIM]
            heads = [kvh * Q_PER_KV + g for g in range(Q_PER_KV)]
            qs = jnp.concatenate(
                [q_ref[r0:r0 + BLOCK, h * HEAD_DIM:(h + 1) * HEAD_DIM] for h in heads], axis=0)
            s_all = lax.dot_general(qs, kb, (((1,), (1,)), ((), ())), preferred_element_type=F32)
            for g, h in enumerate(heads):
                s = s_all[g * BLOCK:(g + 1) * BLOCK, :] * ATTN_SCALE
                s = jnp.where(mask, s + dist_f * (-_SLOPES[h]), NEG_INF)
                sink = sink_ref[h]
                m = jnp.maximum(jnp.max(s, axis=-1, keepdims=True), sink)
                p = jnp.exp(s - m)
                denom = jnp.sum(p, axis=-1, keepdims=True) + jnp.exp(sink - m)
                pn = (p * (1.0 / denom)).astype(BF16)
                o = jnp.dot(pn, vb, preferred_element_type=F32)
                a_scr[r0:r0 + BLOCK, h * HEAD_DIM:(h + 1) * HEAD_DIM] = o.astype(BF16)
        for grp in range(GMLP_GROUPS):
            c0 = grp * GMLP_GROUP_DIM
            sp = jnp.dot(ws_ref[grp], vgn_ref[r0:r0 + BLOCK, c0:c0 + GMLP_GROUP_DIM],
                         preferred_element_type=F32) + bs_ref[grp]
            u = gu_ref[r0:r0 + BLOCK, c0:c0 + GMLP_GROUP_DIM].astype(F32)
            sg_scr[r0:r0 + BLOCK, c0:c0 + GMLP_GROUP_DIM] = (u * sp).astype(BF16)

    a = a_scr[...]
    sg = sg_scr[...]
    for c in range(0, D_MODEL, COL_CHUNK):
        ma = jnp.dot(a, wa_ref[:, c:c + COL_CHUNK], preferred_element_type=F32)
        mb = jnp.dot(sg, wb_ref[:, c:c + COL_CHUNK], preferred_element_type=F32)
        merged = (sa_ref[:, c:c + COL_CHUNK].astype(F32) * ma
                  + sb_ref[:, c:c + COL_CHUNK].astype(F32) * mb)
        mg_scr[:, c:c + COL_CHUNK] = merged.astype(BF16)
    mg = mg_scr[...]
    for c in range(0, D_MODEL, COL_CHUNK):
        mix = jnp.dot(mg, wo_ref[:, c:c + COL_CHUNK], preferred_element_type=F32)
        mix_scr[:, c:c + COL_CHUNK] = DEEPNORM_ALPHA * xn_scr[:, c:c + COL_CHUNK] + mix
    x1 = _layer_norm(mix_scr[...], g1_ref[...], b1_ref[...])
    x1e_ref[:, :D_MODEL] = x1

    x_hi = x1.astype(BF16)
    x_lo = (x1 - x_hi.astype(F32)).astype(BF16)
    r = (jnp.dot(x_hi, wr_ref[...], preferred_element_type=F32)
         + jnp.dot(x_lo, wr_ref[...], preferred_element_type=F32))
    logits = r[:, :LANES] + r[:, LANES:] + br_ref[...]
    c_lo, c_hi, bucket = _route(logits.T)
    routed = jnp.concatenate([c_lo, c_hi, bucket, jnp.zeros((5, TM), F32)], axis=0)
    rt_ref[...] = routed
    padded = jnp.concatenate([routed, jnp.zeros((ROUTE_COLS - 8, TM), F32)], axis=0)
    x1e_ref[:, D_MODEL:] = padded.T


def _mix_call(sink, xp, xs, g0, b0, q, k, v, gu, vgn, sa, sb, ws, bs, wa, wb, wo, g1, b1, wr, br):
    n_p, n_s = xp.shape[0] // TM, xs.shape[0] // TM
    t_all = xp.shape[0] + xs.shape[0]
    nblk = TM // BLOCK
    blk_p, blk_all = xp.shape[0] // BLOCK, t_all // BLOCK
    xp_spec, xs_spec = _x_specs(n_p)

    def tok(width):
        return pl.BlockSpec((TM, width), lambda i: (i, 0))

    prev_spec = pl.BlockSpec((BLOCK, KV_WIDTH), lambda i: (jnp.maximum(i * nblk - 1, 0), 0))
    next_spec = pl.BlockSpec((BLOCK, KV_WIDTH), lambda i: (jnp.minimum((i + 1) * nblk, blk_all - 1), 0))
    kernel = functools.partial(_mix_kernel, n_p, (0, blk_p), (blk_p - 1, blk_all - 1))
    return pl.pallas_call(
        kernel,
        out_shape=[jax.ShapeDtypeStruct((t_all, ROW_W), F32), jax.ShapeDtypeStruct((8, t_all), F32)],
        grid=(n_p + n_s,),
        in_specs=[pl.BlockSpec(memory_space=pltpu.SMEM),
                  xp_spec, xs_spec, _const_spec((1, D_MODEL)), _const_spec((1, D_MODEL)),
                  tok(ATTN_WIDTH), prev_spec, tok(KV_WIDTH), next_spec, prev_spec, tok(KV_WIDTH), next_spec,
                  tok(GMLP_WIDTH), tok(GMLP_WIDTH), tok(D_MODEL), tok(D_MODEL),
                  _const_spec((GMLP_GROUPS, BLOCK, BLOCK)), _const_spec((GMLP_GROUPS, BLOCK, BLOCK)),
                  _const_spec((ATTN_WIDTH, D_MODEL)), _const_spec((GMLP_WIDTH, D_MODEL)),
                  _const_spec((D_MODEL, D_MODEL)), _const_spec((1, D_MODEL)), _const_spec((1, D_MODEL)),
                  _const_spec((D_MODEL, 2 * LANES)), _const_spec((1, LANES))],
        out_specs=[pl.BlockSpec((TM, ROW_W), lambda i: (i, 0)), pl.BlockSpec((8, TM), lambda i: (0, i))],
        scratch_shapes=[pltpu.VMEM((TM, D_MODEL), F32),
                        pltpu.VMEM((TM + 2 * BLOCK, KV_WIDTH), BF16),
                        pltpu.VMEM((TM + 2 * BLOCK, KV_WIDTH), BF16),
                        pltpu.VMEM((TM, ATTN_WIDTH), BF16),
                        pltpu.VMEM((TM, GMLP_WIDTH), BF16),
                        pltpu.VMEM((TM, D_MODEL), BF16),
                        pltpu.VMEM((TM, D_MODEL), F32)],
        compiler_params=pltpu.CompilerParams(dimension_semantics=("arbitrary",),
                                             vmem_limit_bytes=VMEM_LIMIT),
        name="mix",
    )(sink, xp, xs, g0, b0, q, k, k, k, v, v, v, gu, vgn, sa, sb, ws, bs, wa, wb, wo, g1, b1, wr, br)


def _row_gather_kernel(idx_ref, src_ref, dst_ref, sem):
    base = pl.program_id(0) * TM

    def issue(r, carry):
        pltpu.make_async_copy(src_ref.at[pl.ds(idx_ref[0, r], 1)], dst_ref.at[pl.ds(base + r, 1)], sem).start()
        return carry

    lax.fori_loop(0, TM, issue, 0)
    pltpu.make_async_copy(src_ref.at[pl.ds(0, TM)], dst_ref.at[pl.ds(base, TM)], sem).wait()


def _row_gather_call(idx, src, n_rows, name):
    width = src.shape[1]
    n_tiles = n_rows // TM
    return pl.pallas_call(
        _row_gather_kernel,
        out_shape=jax.ShapeDtypeStruct((n_rows, width), src.dtype),
        grid=(n_tiles,),
        in_specs=[pl.BlockSpec((None, 1, TM), lambda i: (i, 0, 0), memory_space=pltpu.SMEM),
                  pl.BlockSpec(memory_space=pl.ANY)],
        out_specs=pl.BlockSpec(memory_space=pl.ANY),
        scratch_shapes=[pltpu.SemaphoreType.DMA],
        compiler_params=pltpu.CompilerParams(dimension_semantics=("arbitrary",), has_side_effects=True),
        name=name,
    )(idx.reshape(n_tiles, 1, TM), src)


def _moe_kernel(elo_ref, ehi_ref, nvalid_ref, last_ref, xs_ref,
                wg_lo, wu_lo, wd_lo, wg_hi, wu_hi, wd_hi, g2_ref, b2_ref, out_ref):
    del elo_ref, ehi_ref, last_ref
    i = pl.program_id(0)

    @pl.when(nvalid_ref[i] > 0)
    def _():
        x = xs_ref[:, :D_MODEL]
        xb = x.astype(BF16)

        def expert(wg, wu, wd, c):
            gate = jnp.dot(xb, wg[...], preferred_element_type=F32)
            up = jnp.dot(xb, wu[...], preferred_element_type=F32)
            h = (jax.nn.silu(gate) * up) * c
            return jnp.dot(h.astype(BF16), wd[...], preferred_element_type=F32)

        y = (expert(wg_lo, wu_lo, wd_lo, xs_ref[:, D_MODEL:D_MODEL + 1])
             + expert(wg_hi, wu_hi, wd_hi, xs_ref[:, D_MODEL + 1:D_MODEL + 2]))
        out_ref[...] = _layer_norm(DEEPNORM_ALPHA * x + y, g2_ref[...], b2_ref[...])

    @pl.when(nvalid_ref[i] == 0)
    def _():
        out_ref[...] = jnp.zeros(out_ref.shape, out_ref.dtype)


def _moe_call(e_lo, e_hi, nvalid, last, xsorted, wg, wu, wd, g2, b2):
    n_tiles = xsorted.shape[0] // TM

    def tile_map(i, elo, ehi, nv, last):
        return (jnp.minimum(i, last[0]), 0)

    def w_spec(shape, which):
        def imap(i, elo, ehi, nv, last):
            return ((elo, ehi)[which][i], 0, 0)
        return pl.BlockSpec((None,) + shape, imap)

    up_shape, down_shape = (D_MODEL, EXPERT_FF), (EXPERT_FF, D_MODEL)
    grid_spec = pltpu.PrefetchScalarGridSpec(
        num_scalar_prefetch=4,
        grid=(n_tiles,),
        in_specs=[pl.BlockSpec((TM, ROW_W), tile_map),
                  w_spec(up_shape, 0), w_spec(up_shape, 0), w_spec(down_shape, 0),
                  w_spec(up_shape, 1), w_spec(up_shape, 1), w_spec(down_shape, 1),
                  _const_spec((1, D_MODEL)), _const_spec((1, D_MODEL))],
        out_specs=pl.BlockSpec((TM, D_MODEL), lambda i, *_: (i, 0)),
    )
    return pl.pallas_call(
        _moe_kernel,
        out_shape=jax.ShapeDtypeStruct((xsorted.shape[0], D_MODEL), F32),
        grid_spec=grid_spec,
        compiler_params=pltpu.CompilerParams(dimension_semantics=("arbitrary",),
                                             vmem_limit_bytes=VMEM_LIMIT),
        name="moe",
    )(e_lo, e_hi, nvalid, last, xsorted, wg, wu, wd, wg, wu, wd, g2, b2)


_PAIR_LO = np.array([0, 0, 0, 1, 1, 2], np.int32)
_PAIR_HI = np.array([1, 2, 3, 2, 3, 3], np.int32)


def _bucket_layout(bucket, n_slots):
    t_all = bucket.shape[0]
    n_tiles = n_slots // TM
    onehot = (bucket[:, None] == jnp.arange(N_BUCKETS, dtype=jnp.int32)[None, :]).astype(jnp.int32)
    counts = jnp.sum(onehot, axis=0)
    rank = jnp.sum((jnp.cumsum(onehot, axis=0) - onehot) * onehot, axis=1)
    padded = ((counts + TM - 1) // TM) * TM
    ends = jnp.cumsum(padded)
    starts = ends - padded
    dest = starts[bucket] + rank
    src = jnp.zeros((n_slots,), jnp.int32).at[dest].set(jnp.arange(t_all, dtype=jnp.int32))
    tile_start = jnp.arange(n_tiles, dtype=jnp.int32) * TM
    last = jnp.maximum(ends[-1] // TM - 1, 0)
    tile_bucket = jnp.searchsorted(ends, jnp.minimum(tile_start, last * TM), side="right").astype(jnp.int32)
    tile_bucket = jnp.minimum(tile_bucket, N_BUCKETS - 1)
    nvalid = jnp.clip(counts[tile_bucket] - (tile_start - starts[tile_bucket]), 0, TM)
    nvalid = jnp.where(tile_start < ends[-1], nvalid, 0).astype(jnp.int32)
    group, pair = tile_bucket // N_PAIRS, tile_bucket % N_PAIRS
    e_lo = group * EXPERTS_PER_GROUP + jnp.asarray(_PAIR_LO)[pair]
    e_hi = group * EXPERTS_PER_GROUP + jnp.asarray(_PAIR_HI)[pair]
    return src, dest, e_lo.astype(jnp.int32), e_hi.astype(jnp.int32), nvalid, last.reshape(1).astype(jnp.int32)


def _layer(xp, xs, in_ln_g, in_ln_b, w_in, attn_sink, gmlp_w_s, gmlp_b_s, gmlp_ln_g, gmlp_ln_b,
           w_attn_branch, w_gmlp_branch, w_out, ln1_g, ln1_b,
           router_w_group, router_b_group, router_w_expert, router_b_expert,
           w_expert_gate, w_expert_up, w_expert_down, ln2_g, ln2_b):
    t_p, t_s = xp.shape[0], xs.shape[0]
    t_all = t_p + t_s
    row = lambda p: p.reshape(1, -1).astype(F32)

    q, k, v, gu, vgn, sa, sb = _proj_call(xp, xs, row(in_ln_g), row(in_ln_b), w_in.astype(BF16),
                                          gmlp_ln_g.astype(F32), gmlp_ln_b.astype(F32))

    wr = jnp.concatenate([router_w_group, router_w_expert], axis=1).astype(F32)
    wr = jnp.pad(wr, ((0, 0), (0, LANES - wr.shape[1])))
    wr_hi = wr.astype(BF16)
    wr_lo = (wr - wr_hi.astype(F32)).astype(BF16)
    br = jnp.pad(jnp.concatenate([router_b_group, router_b_expert]).astype(F32),
                 (0, LANES - N_EXPERT_GROUPS - N_EXPERTS)).reshape(1, LANES)
    bs = jnp.broadcast_to(gmlp_b_s.astype(F32)[:, :, None], (GMLP_GROUPS, BLOCK, BLOCK))

    x1e, routed = _mix_call(attn_sink.astype(F32), xp, xs, row(in_ln_g), row(in_ln_b),
                            q, k, v, gu, vgn, sa, sb, gmlp_w_s.astype(BF16), bs,
                            w_attn_branch.astype(BF16), w_gmlp_branch.astype(BF16), w_out.astype(BF16),
                            row(ln1_g), row(ln1_b), jnp.concatenate([wr_hi, wr_lo], axis=1), br)

    n_slots = t_all + N_BUCKETS * TM
    bucket = routed[2].astype(jnp.int32)
    src, dest, e_lo, e_hi, nvalid, last = _bucket_layout(bucket, n_slots)
    xsorted = _row_gather_call(src, x1e, n_slots, "bucket_gather")
    ysorted = _moe_call(e_lo, e_hi, nvalid, last, xsorted,
                        w_expert_gate.astype(BF16), w_expert_up.astype(BF16), w_expert_down.astype(BF16),
                        row(ln2_g), row(ln2_b))
    yp = _row_gather_call(dest[:t_p], ysorted, t_p, "unsort_prompt")
    ys = _row_gather_call(dest[t_p:], ysorted, t_s, "unsort_sample")
    return yp, ys


def kernel(x_prompt, x_sample, in_ln_g, in_ln_b, w_in, attn_sink, gmlp_w_s, gmlp_b_s, gmlp_ln_g, gmlp_ln_b,
           w_attn_branch, w_gmlp_branch, w_out, ln1_g, ln1_b,
           router_w_group, router_b_group, router_w_expert, router_b_expert,
           w_expert_gate, w_expert_up, w_expert_down, ln2_g, ln2_b):
    bp, sp, d = x_prompt.shape
    bs, ss, _ = x_sample.shape
    assert bp == 1 and bs == 1 and d == D_MODEL and sp % TM == 0 and ss % TM == 0
    assert w_in.shape[0] == 1, "one layer"
    yp, ys = _layer(x_prompt.reshape(sp, d), x_sample.reshape(ss, d), in_ln_g, in_ln_b, w_in[0], attn_sink[0],
                    gmlp_w_s[0], gmlp_b_s[0], gmlp_ln_g[0], gmlp_ln_b[0],
                    w_attn_branch[0], w_gmlp_branch[0], w_out[0], ln1_g[0], ln1_b[0],
                    router_w_group[0], router_b_group[0], router_w_expert[0], router_b_expert[0],
                    w_expert_gate[0], w_expert_up[0], w_expert_down[0], ln2_g[0], ln2_b[0])
    return yp.reshape(1, sp, d), ys.reshape(1, ss, d)
```

```python
import functools

import numpy as np
import jax
import jax.numpy as jnp
from jax import lax
from jax.experimental import pallas as pl
from jax.experimental.pallas import tpu as pltpu

F32 = jnp.float32
BF16 = jnp.bfloat16

D_MODEL = 2048
HEAD_DIM = 128
N_Q_HEADS = 8
N_KV_HEADS = 2
Q_PER_KV = N_Q_HEADS // N_KV_HEADS
ATTN_WIDTH = N_Q_HEADS * HEAD_DIM
KV_WIDTH = N_KV_HEADS * HEAD_DIM
WINDOW = 128
BLOCK = 128
GMLP_WIDTH = D_MODEL // 2
GMLP_GROUPS = 8
GMLP_GROUP_DIM = GMLP_WIDTH // GMLP_GROUPS
N_EXPERT_GROUPS = 4
EXPERTS_PER_GROUP = 4
N_EXPERTS = N_EXPERT_GROUPS * EXPERTS_PER_GROUP
EXPERT_FF = 512
LN_EPS = 1e-5
DEEPNORM_ALPHA = 2.0 ** 0.25
NEG_INF = -1e9
ATTN_SCALE = HEAD_DIM ** -0.5

_C_Q = 0
_C_K = _C_Q + ATTN_WIDTH
_C_V = _C_K + KV_WIDTH
_C_U = _C_V + KV_WIDTH
_C_VG = _C_U + GMLP_WIDTH
_C_GA = _C_VG + GMLP_WIDTH
_C_GB = _C_GA + D_MODEL
IN_COLS = _C_GB + D_MODEL

LANES = 128
N_PAIRS = 6
N_BUCKETS = N_EXPERT_GROUPS * N_PAIRS
X_ROWS = D_MODEL // LANES
ROW_PITCH = 24
TM = 256
COL_CHUNK = 512
VMEM_LIMIT = 56 * 1024 * 1024

_SLOPES = [float(2.0 ** (-8.0 * (h + 1) / N_Q_HEADS)) for h in range(N_Q_HEADS)]


def _layer_norm(x, g, b):
    mu = jnp.mean(x, axis=-1, keepdims=True)
    xc = x - mu
    var = jnp.mean(xc * xc, axis=-1, keepdims=True)
    return xc * lax.rsqrt(var + LN_EPS) * g + b


def _store_token_major(ref, x, pad_from):
    for r in range(X_ROWS):
        ref[pl.ds(r, TM, stride=ROW_PITCH), :] = x[:, r * LANES:(r + 1) * LANES]
    for r in range(pad_from, ROW_PITCH):
        ref[pl.ds(r, TM, stride=ROW_PITCH), :] = jnp.zeros((TM, LANES), ref.dtype)


def _load_token_major(ref, r):
    return ref[pl.ds(r, TM, stride=ROW_PITCH), :]


def _const_spec(shape):
    nd = len(shape)
    return pl.BlockSpec(shape, lambda i, *_: (0,) * nd, pipeline_mode=pl.Buffered(1))


def _proj_kernel(n_prompt_tiles, xp_ref, xs_ref, g0_ref, b0_ref, w_ref, lg_ref, lb_ref,
                 q_ref, k_ref, v_ref, gu_ref, vgn_ref, sa_ref, sb_ref, xn_scr):
    i = pl.program_id(0)

    @pl.when(i < n_prompt_tiles)
    def _():
        xn_scr[...] = _layer_norm(xp_ref[...], g0_ref[...], b0_ref[...]).astype(BF16)

    @pl.when(i >= n_prompt_tiles)
    def _():
        xn_scr[...] = _layer_norm(xs_ref[...], g0_ref[...], b0_ref[...]).astype(BF16)

    xn = xn_scr[...]

    def proj(c0, width):
        return jnp.dot(xn, w_ref[:, c0:c0 + width], preferred_element_type=F32)

    for c in range(0, ATTN_WIDTH, COL_CHUNK):
        q_ref[:, c:c + COL_CHUNK] = proj(_C_Q + c, COL_CHUNK).astype(BF16)
    kv = proj(_C_K, 2 * KV_WIDTH)
    k_ref[...] = kv[:, :KV_WIDTH].astype(BF16)
    v_ref[...] = kv[:, KV_WIDTH:].astype(BF16)
    for c in range(0, GMLP_WIDTH, COL_CHUNK):
        gu_ref[:, c:c + COL_CHUNK] = jax.nn.gelu(proj(_C_U + c, COL_CHUNK)).astype(BF16)
    for c in range(0, GMLP_WIDTH, COL_CHUNK):
        vg = jax.nn.gelu(proj(_C_VG + c, COL_CHUNK))
        for j in range(COL_CHUNK // GMLP_GROUP_DIM):
            grp = c // GMLP_GROUP_DIM + j
            blk = vg[:, j * GMLP_GROUP_DIM:(j + 1) * GMLP_GROUP_DIM]
            y = _layer_norm(blk, lg_ref[grp:grp + 1, :], lb_ref[grp:grp + 1, :])
            vgn_ref[:, grp * GMLP_GROUP_DIM:(grp + 1) * GMLP_GROUP_DIM] = y.astype(BF16)
    for c in range(0, D_MODEL, COL_CHUNK):
        sa_ref[:, c:c + COL_CHUNK] = jax.nn.sigmoid(proj(_C_GA + c, COL_CHUNK)).astype(BF16)
    for c in range(0, D_MODEL, COL_CHUNK):
        sb_ref[:, c:c + COL_CHUNK] = jax.nn.sigmoid(proj(_C_GB + c, COL_CHUNK)).astype(BF16)


def _x_specs(n_prompt_tiles):
    xp = pl.BlockSpec((TM, D_MODEL), lambda i, *_: (jnp.minimum(i, n_prompt_tiles - 1), 0))
    xs = pl.BlockSpec((TM, D_MODEL), lambda i, *_: (jnp.maximum(i - n_prompt_tiles, 0), 0))
    return xp, xs


def _proj_call(xp, xs, g0, b0, w_in, lg, lb):
    n_p, n_s = xp.shape[0] // TM, xs.shape[0] // TM
    t_all = xp.shape[0] + xs.shape[0]
    xp_spec, xs_spec = _x_specs(n_p)

    def tok(width):
        return pl.BlockSpec((TM, width), lambda i: (i, 0))

    widths = (ATTN_WIDTH, KV_WIDTH, KV_WIDTH, GMLP_WIDTH, GMLP_WIDTH, D_MODEL, D_MODEL)
    return pl.pallas_call(
        functools.partial(_proj_kernel, n_p),
        out_shape=[jax.ShapeDtypeStruct((t_all, w), BF16) for w in widths],
        grid=(n_p + n_s,),
        in_specs=[xp_spec, xs_spec, _const_spec((1, D_MODEL)), _const_spec((1, D_MODEL)),
                  _const_spec((D_MODEL, IN_COLS)),
                  _const_spec((GMLP_GROUPS, GMLP_GROUP_DIM)), _const_spec((GMLP_GROUPS, GMLP_GROUP_DIM))],
        out_specs=[tok(w) for w in widths],
        scratch_shapes=[pltpu.VMEM((TM, D_MODEL), BF16)],
        compiler_params=pltpu.CompilerParams(dimension_semantics=("arbitrary",),
                                             vmem_limit_bytes=VMEM_LIMIT),
        name="proj",
    )(xp, xs, g0, b0, w_in, lg, lb)


def _route(logits_t):
    row = lambda j: logits_t[j:j + 1, :]
    gl = [row(j) for j in range(N_EXPERT_GROUPS)]
    gmax, gidx = gl[0], jnp.zeros(gl[0].shape, jnp.int32)
    for j in range(1, N_EXPERT_GROUPS):
        better = gl[j] > gmax
        gmax = jnp.where(better, gl[j], gmax)
        gidx = jnp.where(better, j, gidx)
    gsum = jnp.exp(gl[0] - gmax)
    for j in range(1, N_EXPERT_GROUPS):
        gsum = gsum + jnp.exp(gl[j] - gmax)
    p_group = 1.0 / gsum

    ig = []
    for e in range(EXPERTS_PER_GROUP):
        v = row(N_EXPERT_GROUPS + (N_EXPERT_GROUPS - 1) * EXPERTS_PER_GROUP + e)
        for g in range(N_EXPERT_GROUPS - 2, -1, -1):
            v = jnp.where(gidx == g, row(N_EXPERT_GROUPS + g * EXPERTS_PER_GROUP + e), v)
        ig.append(v)
    v1, i1 = ig[0], jnp.zeros(ig[0].shape, jnp.int32)
    for e in range(1, EXPERTS_PER_GROUP):
        better = ig[e] > v1
        v1 = jnp.where(better, ig[e], v1)
        i1 = jnp.where(better, e, i1)
    v2 = jnp.where(i1 == 0, ig[1], ig[0])
    i2 = jnp.where(i1 == 0, 1, 0).astype(jnp.int32)
    for e in range(1, EXPERTS_PER_GROUP):
        better = jnp.logical_and(i1 != e, ig[e] > v2)
        v2 = jnp.where(better, ig[e], v2)
        i2 = jnp.where(better, e, i2)
    ev = jnp.exp(v2 - v1)
    ssum = 1.0 + ev
    w1 = 1.0 / ssum
    w2 = ev / ssum
    first_lo = i1 < i2
    lo = jnp.minimum(i1, i2)
    hi = jnp.maximum(i1, i2)
    c_lo = jnp.where(first_lo, w1, w2) * p_group
    c_hi = jnp.where(first_lo, w2, w1) * p_group
    pair = jnp.where(lo == 0, hi - 1, jnp.where(lo == 1, hi + 1, N_PAIRS - 1))
    bucket = gidx * N_PAIRS + pair
    return c_lo, c_hi, bucket.astype(F32)


def _mix_kernel(n_prompt_tiles, seq_edges_first, seq_edges_last,
                sink_ref, xp_ref, xs_ref, g0_ref, b0_ref, q_ref,
                kp_ref, km_ref, kn_ref, vp_ref, vm_ref, vn_ref,
                gu_ref, vgn_ref, sa_ref, sb_ref, ws_ref, bs_ref,
                wa_ref, wb_ref, wo_ref, g1_ref, b1_ref, wr_ref, br_ref,
                x1t_ref, rt_ref,
                xn_scr, kf_scr, vf_scr, a_scr, sg_scr, mg_scr, mix_scr):
    i = pl.program_id(0)
    nblk = TM // BLOCK

    @pl.when(i < n_prompt_tiles)
    def _():
        xn_scr[...] = _layer_norm(xp_ref[...], g0_ref[...], b0_ref[...])

    @pl.when(i >= n_prompt_tiles)
    def _():
        xn_scr[...] = _layer_norm(xs_ref[...], g0_ref[...], b0_ref[...])

    kf_scr[0:BLOCK, :] = kp_ref[...]
    kf_scr[BLOCK:BLOCK + TM, :] = km_ref[...]
    kf_scr[BLOCK + TM:, :] = kn_ref[...]
    vf_scr[0:BLOCK, :] = vp_ref[...]
    vf_scr[BLOCK:BLOCK + TM, :] = vm_ref[...]
    vf_scr[BLOCK + TM:, :] = vn_ref[...]

    first_blk = i * nblk
    last_blk = i * nblk + nblk - 1
    has_prev = jnp.logical_and(*[first_blk != e for e in seq_edges_first])
    has_next = jnp.logical_and(*[last_blk != e for e in seq_edges_last])
    kj = lax.broadcasted_iota(jnp.int32, (BLOCK, 3 * BLOCK), 1)
    qi = lax.broadcasted_iota(jnp.int32, (BLOCK, 3 * BLOCK), 0)
    dist = jnp.abs(kj - BLOCK - qi)
    in_window = dist <= WINDOW
    dist_f = dist.astype(F32)
    lo_key = jnp.where(has_prev, 0, BLOCK)
    hi_key = jnp.where(has_next, 3 * BLOCK, 2 * BLOCK)

    for j in range(nblk):
        mask = in_window
        if j == 0:
            mask = jnp.logical_and(mask, kj >= lo_key)
        if j == nblk - 1:
            mask = jnp.logical_and(mask, kj < hi_key)
        r0 = j * BLOCK
        for kvh in range(N_KV_HEADS):
            c0 = kvh * HEAD_DIM
            kb = kf_scr[r0:r0 + 3 * BLOCK, c0:c0 + HEAD_DIM]
            vb = vf_scr[r0:r0 + 3 * BLOCK, c0:c0 + HEAD_DIM]
            heads = [kvh * Q_PER_KV + g for g in range(Q_PER_KV)]
            qs = jnp.concatenate(
                [q_ref[r0:r0 + BLOCK, h * HEAD_DIM:(h + 1) * HEAD_DIM] for h in heads], axis=0)
            s_all = lax.dot_general(qs, kb, (((1,), (1,)), ((), ())), preferred_element_type=F32)
            for g, h in enumerate(heads):
                s = s_all[g * BLOCK:(g + 1) * BLOCK, :] * ATTN_SCALE
                s = jnp.where(mask, s + dist_f * (-_SLOPES[h]), NEG_INF)
                sink = sink_ref[h]
                m = jnp.maximum(jnp.max(s, axis=-1, keepdims=True), sink)
                p = jnp.exp(s - m)
                denom = jnp.sum(p, axis=-1, keepdims=True) + jnp.exp(sink - m)
                pn = (p * (1.0 / denom)).astype(BF16)
                o = jnp.dot(pn, vb, preferred_element_type=F32)
                a_scr[r0:r0 + BLOCK, h * HEAD_DIM:(h + 1) * HEAD_DIM] = o.astype(BF16)
        for grp in range(GMLP_GROUPS):
            c0 = grp * GMLP_GROUP_DIM
            sp = jnp.dot(ws_ref[grp], vgn_ref[r0:r0 + BLOCK, c0:c0 + GMLP_GROUP_DIM],
                         preferred_element_type=F32) + bs_ref[grp]
            u = gu_ref[r0:r0 + BLOCK, c0:c0 + GMLP_GROUP_DIM].astype(F32)
            sg_scr[r0:r0 + BLOCK, c0:c0 + GMLP_GROUP_DIM] = (u * sp).astype(BF16)

    a = a_scr[...]
    sg = sg_scr[...]
    for c in range(0, D_MODEL, COL_CHUNK):
        ma = jnp.dot(a, wa_ref[:, c:c + COL_CHUNK], preferred_element_type=F32)
        mb = jnp.dot(sg, wb_ref[:, c:c + COL_CHUNK], preferred_element_type=F32)
        merged = (sa_ref[:, c:c + COL_CHUNK].astype(F32) * ma
                  + sb_ref[:, c:c + COL_CHUNK].astype(F32) * mb)
        mg_scr[:, c:c + COL_CHUNK] = merged.astype(BF16)
    mg = mg_scr[...]
    for c in range(0, D_MODEL, COL_CHUNK):
        mix = jnp.dot(mg, wo_ref[:, c:c + COL_CHUNK], preferred_element_type=F32)
        mix_scr[:, c:c + COL_CHUNK] = DEEPNORM_ALPHA * xn_scr[:, c:c + COL_CHUNK] + mix
    x1 = _layer_norm(mix_scr[...], g1_ref[...], b1_ref[...])
    _store_token_major(x1t_ref, x1, X_ROWS + 1)

    x_hi = x1.astype(BF16)
    x_lo = (x1 - x_hi.astype(F32)).astype(BF16)
    r = (jnp.dot(x_hi, wr_ref[...], preferred_element_type=F32)
         + jnp.dot(x_lo, wr_ref[...], preferred_element_type=F32))
    logits = r[:, :LANES] + r[:, LANES:] + br_ref[...]
    c_lo, c_hi, bucket = _route(logits.T)
    routed = jnp.concatenate([c_lo, c_hi, bucket, jnp.zeros((5, TM), F32)], axis=0)
    rt_ref[...] = routed
    padded = jnp.concatenate([routed, jnp.zeros((LANES - 8, TM), F32)], axis=0)
    x1t_ref[pl.ds(X_ROWS, TM, stride=ROW_PITCH), :] = padded.T


def _mix_call(sink, xp, xs, g0, b0, q, k, v, gu, vgn, sa, sb, ws, bs, wa, wb, wo, g1, b1, wr, br):
    n_p, n_s = xp.shape[0] // TM, xs.shape[0] // TM
    t_all = xp.shape[0] + xs.shape[0]
    nblk = TM // BLOCK
    blk_p, blk_all = xp.shape[0] // BLOCK, t_all // BLOCK
    xp_spec, xs_spec = _x_specs(n_p)

    def tok(width):
        return pl.BlockSpec((TM, width), lambda i: (i, 0))

    prev_spec = pl.BlockSpec((BLOCK, KV_WIDTH), lambda i: (jnp.maximum(i * nblk - 1, 0), 0))
    next_spec = pl.BlockSpec((BLOCK, KV_WIDTH), lambda i: (jnp.minimum((i + 1) * nblk, blk_all - 1), 0))
    kernel = functools.partial(_mix_kernel, n_p, (0, blk_p), (blk_p - 1, blk_all - 1))
    return pl.pallas_call(
        kernel,
        out_shape=[jax.ShapeDtypeStruct((t_all * ROW_PITCH, LANES), F32), jax.ShapeDtypeStruct((8, t_all), F32)],
        grid=(n_p + n_s,),
        in_specs=[pl.BlockSpec(memory_space=pltpu.SMEM),
                  xp_spec, xs_spec, _const_spec((1, D_MODEL)), _const_spec((1, D_MODEL)),
                  tok(ATTN_WIDTH), prev_spec, tok(KV_WIDTH), next_spec, prev_spec, tok(KV_WIDTH), next_spec,
                  tok(GMLP_WIDTH), tok(GMLP_WIDTH), tok(D_MODEL), tok(D_MODEL),
                  _const_spec((GMLP_GROUPS, BLOCK, BLOCK)), _const_spec((GMLP_GROUPS, BLOCK, BLOCK)),
                  _const_spec((ATTN_WIDTH, D_MODEL)), _const_spec((GMLP_WIDTH, D_MODEL)),
                  _const_spec((D_MODEL, D_MODEL)), _const_spec((1, D_MODEL)), _const_spec((1, D_MODEL)),
                  _const_spec((D_MODEL, 2 * LANES)), _const_spec((1, LANES))],
        out_specs=[pl.BlockSpec((TM * ROW_PITCH, LANES), lambda i: (i, 0)),
                   pl.BlockSpec((8, TM), lambda i: (0, i))],
        scratch_shapes=[pltpu.VMEM((TM, D_MODEL), F32),
                        pltpu.VMEM((TM + 2 * BLOCK, KV_WIDTH), BF16),
                        pltpu.VMEM((TM + 2 * BLOCK, KV_WIDTH), BF16),
                        pltpu.VMEM((TM, ATTN_WIDTH), BF16),
                        pltpu.VMEM((TM, GMLP_WIDTH), BF16),
                        pltpu.VMEM((TM, D_MODEL), BF16),
                        pltpu.VMEM((TM, D_MODEL), F32)],
        compiler_params=pltpu.CompilerParams(dimension_semantics=("arbitrary",),
                                             vmem_limit_bytes=VMEM_LIMIT),
        name="mix",
    )(sink, xp, xs, g0, b0, q, k, k, k, v, v, v, gu, vgn, sa, sb, ws, bs, wa, wb, wo, g1, b1, wr, br)


def _gather_copy(src_hbm, dst_buf, sem, src_row, dst_row, n_rows):
    return pltpu.make_async_copy(src_hbm.at[pl.ds(src_row, n_rows)], dst_buf.at[pl.ds(dst_row, n_rows)], sem)


def _start_gather(idx_ref, src_hbm, dst_buf, sem, n_rows):
    def issue(r, carry):
        src_row = pl.multiple_of(idx_ref[0, r] * ROW_PITCH, 8)
        dst_row = pl.multiple_of(r * ROW_PITCH, 8)
        _gather_copy(src_hbm, dst_buf, sem, src_row, dst_row, n_rows).start()
        return carry

    lax.fori_loop(0, TM, issue, 0, unroll=8)


def _wait_gather(src_hbm, dst_buf, sem, n_rows):
    _gather_copy(src_hbm, dst_buf, sem, 0, 0, TM * n_rows).wait()


def _idx_specs(n_tiles):
    first = pl.BlockSpec((None, 1, TM), lambda i, *_: (0, 0, 0), memory_space=pltpu.SMEM)
    nxt = pl.BlockSpec((None, 1, TM), lambda i, *_: (jnp.minimum(i + 1, n_tiles - 1), 0, 0),
                       memory_space=pltpu.SMEM)
    return first, nxt


def _unsort_kernel(idx0_ref, idxn_ref, ys_hbm, out_ref, buf, sem):
    i = pl.program_id(0)
    slot = i % 2

    @pl.when(i == 0)
    def _():
        _start_gather(idx0_ref, ys_hbm, buf.at[0], sem.at[0], X_ROWS)

    @pl.when(i + 1 < pl.num_programs(0))
    def _():
        _start_gather(idxn_ref, ys_hbm, buf.at[1 - slot], sem.at[1 - slot], X_ROWS)

    cur = buf.at[slot]
    _wait_gather(ys_hbm, cur, sem.at[slot], X_ROWS)
    for r in range(X_ROWS):
        out_ref[:, r * LANES:(r + 1) * LANES] = _load_token_major(cur, r)


def _unsort_call(idx, ysorted, name):
    n_tiles = idx.shape[0] // TM
    idx = idx.reshape(n_tiles, 1, TM)
    first, nxt = _idx_specs(n_tiles)
    return pl.pallas_call(
        _unsort_kernel,
        out_shape=jax.ShapeDtypeStruct((n_tiles * TM, D_MODEL), F32),
        grid=(n_tiles,),
        in_specs=[first, nxt, pl.BlockSpec(memory_space=pl.ANY)],
        out_specs=pl.BlockSpec((TM, D_MODEL), lambda i: (i, 0)),
        scratch_shapes=[pltpu.VMEM((2, TM * ROW_PITCH, LANES), F32), pltpu.SemaphoreType.DMA((2,))],
        compiler_params=pltpu.CompilerParams(dimension_semantics=("arbitrary",)),
        name=name,
    )(idx, idx, ysorted)


def _moe_kernel(elo_ref, ehi_ref, nvalid_ref, idx0_ref, idxn_ref, x1t_hbm,
                wg_lo, wu_lo, wd_lo, wg_hi, wu_hi, wd_hi, g2_ref, b2_ref, out_ref,
                buf, sem, xb_scr, z_scr):
    del elo_ref, ehi_ref
    i = pl.program_id(0)
    n = pl.num_programs(0)
    slot = i % 2

    @pl.when(jnp.logical_and(i == 0, nvalid_ref[0] > 0))
    def _():
        _start_gather(idx0_ref, x1t_hbm, buf.at[0], sem.at[0], ROW_PITCH)

    @pl.when(jnp.logical_and(i + 1 < n, nvalid_ref[jnp.minimum(i + 1, n - 1)] > 0))
    def _():
        _start_gather(idxn_ref, x1t_hbm, buf.at[1 - slot], sem.at[1 - slot], ROW_PITCH)

    @pl.when(nvalid_ref[i] > 0)
    def _():
        cur = buf.at[slot]
        _wait_gather(x1t_hbm, cur, sem.at[slot], ROW_PITCH)
        for r in range(X_ROWS):
            xb_scr[:, r * LANES:(r + 1) * LANES] = _load_token_major(cur, r).astype(BF16)
        route = _load_token_major(cur, X_ROWS)
        xb = xb_scr[...]

        def hidden(wg, wu, c):
            gate = jnp.dot(xb, wg[...], preferred_element_type=F32)
            up = jnp.dot(xb, wu[...], preferred_element_type=F32)
            return ((jax.nn.silu(gate) * up) * c).astype(BF16)

        h_lo = hidden(wg_lo, wu_lo, route[:, 0:1])
        h_hi = hidden(wg_hi, wu_hi, route[:, 1:2])
        for c in range(0, D_MODEL, COL_CHUNK):
            y = (jnp.dot(h_lo, wd_lo[:, c:c + COL_CHUNK], preferred_element_type=F32)
                 + jnp.dot(h_hi, wd_hi[:, c:c + COL_CHUNK], preferred_element_type=F32))
            for r in range(c // LANES, (c + COL_CHUNK) // LANES):
                z_scr[:, r * LANES:(r + 1) * LANES] = (DEEPNORM_ALPHA * _load_token_major(cur, r)
                                                       + y[:, r * LANES - c:(r + 1) * LANES - c])
        _store_token_major(out_ref, _layer_norm(z_scr[...], g2_ref[...], b2_ref[...]), X_ROWS)

    @pl.when(nvalid_ref[i] == 0)
    def _():
        out_ref[...] = jnp.zeros(out_ref.shape, out_ref.dtype)


def _moe_call(e_lo, e_hi, nvalid, src, x1t, wg, wu, wd, g2, b2):
    n_tiles = src.shape[0] // TM
    src = src.reshape(n_tiles, 1, TM)
    first, nxt = _idx_specs(n_tiles)

    def w_spec(shape, which):
        def imap(i, elo, ehi, nv):
            return ((elo, ehi)[which][i], 0, 0)
        return pl.BlockSpec((None,) + shape, imap)

    up_shape, down_shape = (D_MODEL, EXPERT_FF), (EXPERT_FF, D_MODEL)
    grid_spec = pltpu.PrefetchScalarGridSpec(
        num_scalar_prefetch=3,
        grid=(n_tiles,),
        in_specs=[first, nxt, pl.BlockSpec(memory_space=pl.ANY),
                  w_spec(up_shape, 0), w_spec(up_shape, 0), w_spec(down_shape, 0),
                  w_spec(up_shape, 1), w_spec(up_shape, 1), w_spec(down_shape, 1),
                  _const_spec((1, D_MODEL)), _const_spec((1, D_MODEL))],
        out_specs=pl.BlockSpec((TM * ROW_PITCH, LANES), lambda i, *_: (i, 0)),
        scratch_shapes=[pltpu.VMEM((2, TM * ROW_PITCH, LANES), F32), pltpu.SemaphoreType.DMA((2,)),
                        pltpu.VMEM((TM, D_MODEL), BF16), pltpu.VMEM((TM, D_MODEL), F32)],
    )
    return pl.pallas_call(
        _moe_kernel,
        out_shape=jax.ShapeDtypeStruct((n_tiles * TM * ROW_PITCH, LANES), F32),
        grid_spec=grid_spec,
        compiler_params=pltpu.CompilerParams(dimension_semantics=("arbitrary",),
                                             vmem_limit_bytes=VMEM_LIMIT),
        name="moe",
    )(e_lo, e_hi, nvalid, src, src, x1t, wg, wu, wd, wg, wu, wd, g2, b2)


_PAIR_LO = np.array([0, 0, 0, 1, 1, 2], np.int32)
_PAIR_HI = np.array([1, 2, 3, 2, 3, 3], np.int32)


def _bucket_layout(bucket, n_slots):
    t_all = bucket.shape[0]
    n_tiles = n_slots // TM
    onehot = (bucket[:, None] == jnp.arange(N_BUCKETS, dtype=jnp.int32)[None, :]).astype(jnp.int32)
    counts = jnp.sum(onehot, axis=0)
    rank = jnp.sum((jnp.cumsum(onehot, axis=0) - onehot) * onehot, axis=1)
    padded = ((counts + TM - 1) // TM) * TM
    ends = jnp.cumsum(padded)
    starts = ends - padded
    dest = starts[bucket] + rank
    src = jnp.zeros((n_slots,), jnp.int32).at[dest].set(jnp.arange(t_all, dtype=jnp.int32))
    tile_start = jnp.arange(n_tiles, dtype=jnp.int32) * TM
    last_start = jnp.maximum(ends[-1] - TM, 0)
    tile_bucket = jnp.sum((jnp.minimum(tile_start, last_start)[:, None] >= ends[None, :]).astype(jnp.int32), axis=1)
    tile_bucket = jnp.minimum(tile_bucket, N_BUCKETS - 1)
    nvalid = jnp.clip(counts[tile_bucket] - (tile_start - starts[tile_bucket]), 0, TM)
    nvalid = jnp.where(tile_start < ends[-1], nvalid, 0).astype(jnp.int32)
    group, pair = tile_bucket // N_PAIRS, tile_bucket % N_PAIRS
    e_lo = group * EXPERTS_PER_GROUP + jnp.asarray(_PAIR_LO)[pair]
    e_hi = group * EXPERTS_PER_GROUP + jnp.asarray(_PAIR_HI)[pair]
    return src, dest, e_lo.astype(jnp.int32), e_hi.astype(jnp.int32), nvalid


def _layer(xp, xs, in_ln_g, in_ln_b, w_in, attn_sink, gmlp_w_s, gmlp_b_s, gmlp_ln_g, gmlp_ln_b,
           w_attn_branch, w_gmlp_branch, w_out, ln1_g, ln1_b,
           router_w_group, router_b_group, router_w_expert, router_b_expert,
           w_expert_gate, w_expert_up, w_expert_down, ln2_g, ln2_b):
    t_p, t_s = xp.shape[0], xs.shape[0]
    t_all = t_p + t_s
    row = lambda p: p.reshape(1, -1).astype(F32)

    q, k, v, gu, vgn, sa, sb = _proj_call(xp, xs, row(in_ln_g), row(in_ln_b), w_in.astype(BF16),
                                          gmlp_ln_g.astype(F32), gmlp_ln_b.astype(F32))

    wr = jnp.concatenate([router_w_group, router_w_expert], axis=1).astype(F32)
    wr = jnp.pad(wr, ((0, 0), (0, LANES - wr.shape[1])))
    wr_hi = wr.astype(BF16)
    wr_lo = (wr - wr_hi.astype(F32)).astype(BF16)
    br = jnp.pad(jnp.concatenate([router_b_group, router_b_expert]).astype(F32),
                 (0, LANES - N_EXPERT_GROUPS - N_EXPERTS)).reshape(1, LANES)
    bs = jnp.broadcast_to(gmlp_b_s.astype(F32)[:, :, None], (GMLP_GROUPS, BLOCK, BLOCK))

    x1t, routed = _mix_call(attn_sink.astype(F32), xp, xs, row(in_ln_g), row(in_ln_b),
                            q, k, v, gu, vgn, sa, sb, gmlp_w_s.astype(BF16), bs,
                            w_attn_branch.astype(BF16), w_gmlp_branch.astype(BF16), w_out.astype(BF16),
                            row(ln1_g), row(ln1_b), jnp.concatenate([wr_hi, wr_lo], axis=1), br)

    n_slots = t_all + N_BUCKETS * TM
    bucket = routed[2].astype(jnp.int32)
    src, dest, e_lo, e_hi, nvalid = _bucket_layout(bucket, n_slots)
    ysorted = _moe_call(e_lo, e_hi, nvalid, src, x1t,
                        w_expert_gate.astype(BF16), w_expert_up.astype(BF16), w_expert_down.astype(BF16),
                        row(ln2_g), row(ln2_b))
    return _unsort_call(dest[:t_p], ysorted, "unsort_prompt"), _unsort_call(dest[t_p:], ysorted, "unsort_sample")


def kernel(x_prompt, x_sample, in_ln_g, in_ln_b, w_in, attn_sink, gmlp_w_s, gmlp_b_s, gmlp_ln_g, gmlp_ln_b,
           w_attn_branch, w_gmlp_branch, w_out, ln1_g, ln1_b,
           router_w_group, router_b_group, router_w_expert, router_b_expert,
           w_expert_gate, w_expert_up, w_expert_down, ln2_g, ln2_b):
    bp, sp, d = x_prompt.shape
    bs, ss, _ = x_sample.shape
    assert bp == 1 and bs == 1 and d == D_MODEL and sp % TM == 0 and ss % TM == 0
    assert w_in.shape[0] == 1, "one layer"
    yp, ys = _layer(x_prompt.reshape(sp, d), x_sample.reshape(ss, d), in_ln_g, in_ln_b, w_in[0], attn_sink[0],
                    gmlp_w_s[0], gmlp_b_s[0], gmlp_ln_g[0], gmlp_ln_b[0],
                    w_attn_branch[0], w_gmlp_branch[0], w_out[0], ln1_g[0], ln1_b[0],
                    router_w_group[0], router_b_group[0], router_w_expert[0], router_b_expert[0],
                    w_expert_gate[0], w_expert_up[0], w_expert_down[0], ln2_g[0], ln2_b[0])
    return yp.reshape(1, sp, d), ys.reshape(1, ss, d)
```

```python
import functools

import numpy as np
import jax
import jax.numpy as jnp
from jax import lax
from jax.experimental import pallas as pl
from jax.experimental.pallas import tpu as pltpu

F32 = jnp.float32
BF16 = jnp.bfloat16

D_MODEL = 2048
HEAD_DIM = 128
N_Q_HEADS = 8
N_KV_HEADS = 2
Q_PER_KV = N_Q_HEADS // N_KV_HEADS
ATTN_WIDTH = N_Q_HEADS * HEAD_DIM
KV_WIDTH = N_KV_HEADS * HEAD_DIM
WINDOW = 128
BLOCK = 128
GMLP_WIDTH = D_MODEL // 2
GMLP_GROUPS = 8
GMLP_GROUP_DIM = GMLP_WIDTH // GMLP_GROUPS
N_EXPERT_GROUPS = 4
EXPERTS_PER_GROUP = 4
N_EXPERTS = N_EXPERT_GROUPS * EXPERTS_PER_GROUP
EXPERT_FF = 512
LN_EPS = 1e-5
DEEPNORM_ALPHA = 2.0 ** 0.25
NEG_INF = -1e9
ATTN_SCALE = HEAD_DIM ** -0.5

_C_Q = 0
_C_K = _C_Q + ATTN_WIDTH
_C_V = _C_K + KV_WIDTH
_C_U = _C_V + KV_WIDTH
_C_VG = _C_U + GMLP_WIDTH
_C_GA = _C_VG + GMLP_WIDTH
_C_GB = _C_GA + D_MODEL
IN_COLS = _C_GB + D_MODEL

LANES = 128
N_PAIRS = 6
N_BUCKETS = N_EXPERT_GROUPS * N_PAIRS
X_ROWS = D_MODEL // LANES
ROW_PITCH = 24
TM = 256
COL_CHUNK = 512
MIX_CHUNK = 256
PROJ_CHUNK = 256
VMEM_LIMIT = 56 * 1024 * 1024

_SLOPES = [float(2.0 ** (-8.0 * (h + 1) / N_Q_HEADS)) for h in range(N_Q_HEADS)]


def _layer_norm(x, g, b):
    mu = jnp.mean(x, axis=-1, keepdims=True)
    xc = x - mu
    var = jnp.mean(xc * xc, axis=-1, keepdims=True)
    return xc * lax.rsqrt(var + LN_EPS) * g + b


def _store_token_major(ref, x, pad_from):
    for r in range(X_ROWS):
        ref[pl.ds(r, TM, stride=ROW_PITCH), :] = x[:, r * LANES:(r + 1) * LANES]
    for r in range(pad_from, ROW_PITCH):
        ref[pl.ds(r, TM, stride=ROW_PITCH), :] = jnp.zeros((TM, LANES), ref.dtype)


def _load_token_major(ref, r):
    return ref[pl.ds(r, TM, stride=ROW_PITCH), :]


def _emit_interleaved(*streams):
    order = []
    for s, (_, costs) in enumerate(streams):
        done = 0.0
        for c in costs:
            order.append(((done + c / 2) / sum(costs), s))
            done += c
    for _, s in sorted(order):
        next(streams[s][0])
    for gen, _ in streams:
        assert next(gen, "done") == "done", "stream has more units than declared"


def _const_spec(shape):
    nd = len(shape)
    return pl.BlockSpec(shape, lambda i, *_: (0,) * nd, pipeline_mode=pl.Buffered(1))


def _proj_kernel(n_prompt_tiles, xp_ref, xs_ref, g0_ref, b0_ref, w_ref, lg_ref, lb_ref,
                 q_ref, k_ref, v_ref, gu_ref, vgn_ref, sa_ref, sb_ref, xn_scr):
    i = pl.program_id(0)

    @pl.when(i < n_prompt_tiles)
    def _():
        xn_scr[...] = _layer_norm(xp_ref[...], g0_ref[...], b0_ref[...]).astype(BF16)

    @pl.when(i >= n_prompt_tiles)
    def _():
        xn_scr[...] = _layer_norm(xs_ref[...], g0_ref[...], b0_ref[...]).astype(BF16)

    xn = xn_scr[...]

    def proj(c0, width):
        return jnp.dot(xn, w_ref[:, c0:c0 + width], preferred_element_type=F32)

    for c in range(0, ATTN_WIDTH, COL_CHUNK):
        q_ref[:, c:c + COL_CHUNK] = proj(_C_Q + c, COL_CHUNK).astype(BF16)
    kv = proj(_C_K, 2 * KV_WIDTH)
    k_ref[...] = kv[:, :KV_WIDTH].astype(BF16)
    v_ref[...] = kv[:, KV_WIDTH:].astype(BF16)
    for c in range(0, GMLP_WIDTH, COL_CHUNK):
        gu_ref[:, c:c + COL_CHUNK] = jax.nn.gelu(proj(_C_U + c, COL_CHUNK)).astype(BF16)
    for c in range(0, GMLP_WIDTH, COL_CHUNK):
        vg = jax.nn.gelu(proj(_C_VG + c, COL_CHUNK))
        for j in range(COL_CHUNK // GMLP_GROUP_DIM):
            grp = c // GMLP_GROUP_DIM + j
            blk = vg[:, j * GMLP_GROUP_DIM:(j + 1) * GMLP_GROUP_DIM]
            y = _layer_norm(blk, lg_ref[grp:grp + 1, :], lb_ref[grp:grp + 1, :])
            vgn_ref[:, grp * GMLP_GROUP_DIM:(grp + 1) * GMLP_GROUP_DIM] = y.astype(BF16)
    for c in range(0, D_MODEL, COL_CHUNK):
        sa_ref[:, c:c + COL_CHUNK] = jax.nn.sigmoid(proj(_C_GA + c, COL_CHUNK)).astype(BF16)
    for c in range(0, D_MODEL, COL_CHUNK):
        sb_ref[:, c:c + COL_CHUNK] = jax.nn.sigmoid(proj(_C_GB + c, COL_CHUNK)).astype(BF16)


def _x_specs(n_prompt_tiles):
    xp = pl.BlockSpec((TM, D_MODEL), lambda i, *_: (jnp.minimum(i, n_prompt_tiles - 1), 0))
    xs = pl.BlockSpec((TM, D_MODEL), lambda i, *_: (jnp.maximum(i - n_prompt_tiles, 0), 0))
    return xp, xs


def _proj_call(xp, xs, g0, b0, w_in, lg, lb):
    n_p, n_s = xp.shape[0] // TM, xs.shape[0] // TM
    t_all = xp.shape[0] + xs.shape[0]
    xp_spec, xs_spec = _x_specs(n_p)

    def tok(width):
        return pl.BlockSpec((TM, width), lambda i: (i, 0))

    widths = (ATTN_WIDTH, KV_WIDTH, KV_WIDTH, GMLP_WIDTH, GMLP_WIDTH, D_MODEL, D_MODEL)
    return pl.pallas_call(
        functools.partial(_proj_kernel, n_p),
        out_shape=[jax.ShapeDtypeStruct((t_all, w), BF16) for w in widths],
        grid=(n_p + n_s,),
        in_specs=[xp_spec, xs_spec, _const_spec((1, D_MODEL)), _const_spec((1, D_MODEL)),
                  _const_spec((D_MODEL, IN_COLS)),
                  _const_spec((GMLP_GROUPS, GMLP_GROUP_DIM)), _const_spec((GMLP_GROUPS, GMLP_GROUP_DIM))],
        out_specs=[tok(w) for w in widths],
        scratch_shapes=[pltpu.VMEM((TM, D_MODEL), BF16)],
        compiler_params=pltpu.CompilerParams(dimension_semantics=("arbitrary",),
                                             vmem_limit_bytes=VMEM_LIMIT),
        name="proj",
    )(xp, xs, g0, b0, w_in, lg, lb)


def _route(logits_t):
    row = lambda j: logits_t[j:j + 1, :]
    gl = [row(j) for j in range(N_EXPERT_GROUPS)]
    gmax, gidx = gl[0], jnp.zeros(gl[0].shape, jnp.int32)
    for j in range(1, N_EXPERT_GROUPS):
        better = gl[j] > gmax
        gmax = jnp.where(better, gl[j], gmax)
        gidx = jnp.where(better, j, gidx)
    gsum = jnp.exp(gl[0] - gmax)
    for j in range(1, N_EXPERT_GROUPS):
        gsum = gsum + jnp.exp(gl[j] - gmax)
    p_group = 1.0 / gsum

    ig = []
    for e in range(EXPERTS_PER_GROUP):
        v = row(N_EXPERT_GROUPS + (N_EXPERT_GROUPS - 1) * EXPERTS_PER_GROUP + e)
        for g in range(N_EXPERT_GROUPS - 2, -1, -1):
            v = jnp.where(gidx == g, row(N_EXPERT_GROUPS + g * EXPERTS_PER_GROUP + e), v)
        ig.append(v)
    v1, i1 = ig[0], jnp.zeros(ig[0].shape, jnp.int32)
    for e in range(1, EXPERTS_PER_GROUP):
        better = ig[e] > v1
        v1 = jnp.where(better, ig[e], v1)
        i1 = jnp.where(better, e, i1)
    v2 = jnp.where(i1 == 0, ig[1], ig[0])
    i2 = jnp.where(i1 == 0, 1, 0).astype(jnp.int32)
    for e in range(1, EXPERTS_PER_GROUP):
        better = jnp.logical_and(i1 != e, ig[e] > v2)
        v2 = jnp.where(better, ig[e], v2)
        i2 = jnp.where(better, e, i2)
    ev = jnp.exp(v2 - v1)
    ssum = 1.0 + ev
    w1 = 1.0 / ssum
    w2 = ev / ssum
    first_lo = i1 < i2
    lo = jnp.minimum(i1, i2)
    hi = jnp.maximum(i1, i2)
    c_lo = jnp.where(first_lo, w1, w2) * p_group
    c_hi = jnp.where(first_lo, w2, w1) * p_group
    pair = jnp.where(lo == 0, hi - 1, jnp.where(lo == 1, hi + 1, N_PAIRS - 1))
    bucket = gidx * N_PAIRS + pair
    return c_lo, c_hi, bucket.astype(F32)


def _mix_kernel(n_prompt_tiles, n_tiles, seq_edges_first, seq_edges_last,
                sink_ref, xp_ref, xs_ref, g0_ref, b0_ref, q_ref,
                kp_ref, km_ref, kn_ref, vp_ref, vm_ref, vn_ref,
                gu_ref, vgn_ref, sa_ref, sb_ref, ws_ref, bs_ref,
                wa_ref, wb_ref, wo_ref, g1_ref, b1_ref, wr_ref, br_ref,
                x1t_ref, rt_ref,
                xn_scr, kf_scr, vf_scr, a_scr, sg_scr, mg_scr, mix_scr):
    i = pl.program_id(0)
    nblk = TM // BLOCK
    cur = 0

    @pl.when(i == 0)
    def _():
        a_scr[1] = jnp.zeros(a_scr.shape[1:], a_scr.dtype)
        sg_scr[1] = jnp.zeros(sg_scr.shape[1:], sg_scr.dtype)
        mix_scr[0] = jnp.zeros(mix_scr.shape[1:], mix_scr.dtype)

    def stage_a():
        tile = jnp.minimum(i, n_tiles - 1)
        kf_scr[0:BLOCK, :] = kp_ref[...]
        kf_scr[BLOCK:BLOCK + TM, :] = km_ref[...]
        kf_scr[BLOCK + TM:, :] = kn_ref[...]
        vf_scr[0:BLOCK, :] = vp_ref[...]
        vf_scr[BLOCK:BLOCK + TM, :] = vm_ref[...]
        vf_scr[BLOCK + TM:, :] = vn_ref[...]
        first_blk = tile * nblk
        last_blk = tile * nblk + nblk - 1
        has_prev = jnp.logical_and(*[first_blk != e for e in seq_edges_first])
        has_next = jnp.logical_and(*[last_blk != e for e in seq_edges_last])
        kj = lax.broadcasted_iota(jnp.int32, (BLOCK, 3 * BLOCK), 1)
        qi = lax.broadcasted_iota(jnp.int32, (BLOCK, 3 * BLOCK), 0)
        dist = jnp.abs(kj - BLOCK - qi)
        in_window = dist <= WINDOW
        dist_f = dist.astype(F32)
        lo_key = jnp.where(has_prev, 0, BLOCK)
        hi_key = jnp.where(has_next, 3 * BLOCK, 2 * BLOCK)
        yield
        for j in range(nblk):
            mask = in_window
            if j == 0:
                mask = jnp.logical_and(mask, kj >= lo_key)
            if j == nblk - 1:
                mask = jnp.logical_and(mask, kj < hi_key)
            r0 = j * BLOCK
            for kvh in range(N_KV_HEADS):
                c0 = kvh * HEAD_DIM
                kb = kf_scr[r0:r0 + 3 * BLOCK, c0:c0 + HEAD_DIM]
                vb = vf_scr[r0:r0 + 3 * BLOCK, c0:c0 + HEAD_DIM]
                heads = [kvh * Q_PER_KV + g for g in range(Q_PER_KV)]
                qs = jnp.concatenate(
                    [q_ref[r0:r0 + BLOCK, h * HEAD_DIM:(h + 1) * HEAD_DIM] for h in heads], axis=0)
                s_all = lax.dot_general(qs, kb, (((1,), (1,)), ((), ())), preferred_element_type=F32)
                for g, h in enumerate(heads):
                    s = s_all[g * BLOCK:(g + 1) * BLOCK, :] * ATTN_SCALE
                    s = jnp.where(mask, s + dist_f * (-_SLOPES[h]), NEG_INF)
                    sink = sink_ref[h]
                    m = jnp.maximum(jnp.max(s, axis=-1, keepdims=True), sink)
                    p = jnp.exp(s - m)
                    denom = jnp.sum(p, axis=-1, keepdims=True) + jnp.exp(sink - m)
                    pn = (p * (1.0 / denom)).astype(BF16)
                    o = jnp.dot(pn, vb, preferred_element_type=F32)
                    a_scr[cur, r0:r0 + BLOCK, h * HEAD_DIM:(h + 1) * HEAD_DIM] = o.astype(BF16)
                    yield
            for grp in range(GMLP_GROUPS):
                c0 = grp * GMLP_GROUP_DIM
                sp = jnp.dot(ws_ref[grp], vgn_ref[r0:r0 + BLOCK, c0:c0 + GMLP_GROUP_DIM],
                             preferred_element_type=F32) + bs_ref[grp]
                u = gu_ref[r0:r0 + BLOCK, c0:c0 + GMLP_GROUP_DIM].astype(F32)
                sg_scr[cur, r0:r0 + BLOCK, c0:c0 + GMLP_GROUP_DIM] = (u * sp).astype(BF16)
            yield

    def stage_b():
        x = jnp.where(i - 1 < n_prompt_tiles, xp_ref[...], xs_ref[...])
        mu0 = jnp.mean(x, axis=-1, keepdims=True)
        xc = x - mu0
        xn_scr[...] = xc
        inv0 = lax.rsqrt(jnp.mean(xc * xc, axis=-1, keepdims=True) + LN_EPS)
        yield
        a = a_scr[1 - cur]
        sg = sg_scr[1 - cur]
        for c in range(0, D_MODEL, PROJ_CHUNK):
            ma = jnp.dot(a, wa_ref[:, c:c + PROJ_CHUNK], preferred_element_type=F32)
            mb = jnp.dot(sg, wb_ref[:, c:c + PROJ_CHUNK], preferred_element_type=F32)
            merged = (sa_ref[:, c:c + PROJ_CHUNK].astype(F32) * ma
                      + sb_ref[:, c:c + PROJ_CHUNK].astype(F32) * mb)
            mg_scr[:, c:c + PROJ_CHUNK] = merged.astype(BF16)
            yield
        mg = mg_scr[...]
        for c in range(0, D_MODEL, PROJ_CHUNK):
            mix = jnp.dot(mg, wo_ref[:, c:c + PROJ_CHUNK], preferred_element_type=F32)
            xn = xn_scr[:, c:c + PROJ_CHUNK] * inv0 * g0_ref[:, c:c + PROJ_CHUNK] + b0_ref[:, c:c + PROJ_CHUNK]
            mix_scr[1 - cur, :, c:c + PROJ_CHUNK] = DEEPNORM_ALPHA * xn + mix
            yield

    def stage_c():
        mu = jnp.mean(mix_scr[cur], axis=-1, keepdims=True)
        yield
        zc = mix_scr[cur] - mu
        inv = lax.rsqrt(jnp.mean(zc * zc, axis=-1, keepdims=True) + LN_EPS)
        yield
        r = None
        for c in range(0, D_MODEL, MIX_CHUNK):
            x1 = ((mix_scr[cur, :, c:c + MIX_CHUNK] - mu) * inv * g1_ref[:, c:c + MIX_CHUNK]
                  + b1_ref[:, c:c + MIX_CHUNK])
            for k in range(MIX_CHUNK // LANES):
                x1t_ref[pl.ds(c // LANES + k, TM, stride=ROW_PITCH), :] = x1[:, k * LANES:(k + 1) * LANES]
            x_hi = x1.astype(BF16)
            x_lo = (x1 - x_hi.astype(F32)).astype(BF16)
            part = (jnp.dot(x_hi, wr_ref[c:c + MIX_CHUNK, :], preferred_element_type=F32)
                    + jnp.dot(x_lo, wr_ref[c:c + MIX_CHUNK, :], preferred_element_type=F32))
            r = part if r is None else r + part
            yield
        logits = r[:, :LANES] + r[:, LANES:] + br_ref[...]
        c_lo, c_hi, bucket = _route(logits.T)
        routed = jnp.concatenate([c_lo, c_hi, bucket, jnp.zeros((5, TM), F32)], axis=0)
        rt_ref[...] = routed
        padded = jnp.concatenate([routed, jnp.zeros((LANES - 8, TM), F32)], axis=0)
        x1t_ref[pl.ds(X_ROWS, TM, stride=ROW_PITCH), :] = padded.T
        for k in range(X_ROWS + 1, ROW_PITCH):
            x1t_ref[pl.ds(k, TM, stride=ROW_PITCH), :] = jnp.zeros((TM, LANES), F32)
        yield

    n_chunks = D_MODEL // MIX_CHUNK
    def emit(parity):
        nonlocal cur
        cur = parity
        _emit_interleaved((stage_b(), [1] + [512] * (2 * D_MODEL // PROJ_CHUNK)),
                          (stage_a(), [100] + ([600] * N_Q_HEADS + [500]) * nblk),
                          (stage_c(), [1000, 1000] + [550] * n_chunks + [1000]))

    for parity in range(2):
        pl.when(i % 2 == parity)(functools.partial(emit, parity))


def _mix_call(sink, xp, xs, g0, b0, q, k, v, gu, vgn, sa, sb, ws, bs, wa, wb, wo, g1, b1, wr, br):
    n_p, n_s = xp.shape[0] // TM, xs.shape[0] // TM
    t_all = xp.shape[0] + xs.shape[0]
    nblk = TM // BLOCK
    blk_p, blk_all = xp.shape[0] // BLOCK, t_all // BLOCK
    n_tiles = n_p + n_s
    front = lambda i: jnp.minimum(i, n_tiles - 1)
    back = lambda i: jnp.clip(i - 1, 0, n_tiles - 1)
    last = lambda i: jnp.maximum(i - 2, 0)
    xp_spec = pl.BlockSpec((TM, D_MODEL), lambda i: (jnp.minimum(back(i), n_p - 1), 0))
    xs_spec = pl.BlockSpec((TM, D_MODEL), lambda i: (jnp.maximum(back(i) - n_p, 0), 0))

    def tok(width, which):
        return pl.BlockSpec((TM, width), lambda i: (which(i), 0))

    prev_spec = pl.BlockSpec((BLOCK, KV_WIDTH), lambda i: (jnp.maximum(front(i) * nblk - 1, 0), 0))
    next_spec = pl.BlockSpec((BLOCK, KV_WIDTH),
                             lambda i: (jnp.minimum((front(i) + 1) * nblk, blk_all - 1), 0))
    kv_spec = tok(KV_WIDTH, front)
    kernel = functools.partial(_mix_kernel, n_p, n_tiles, (0, blk_p), (blk_p - 1, blk_all - 1))
    return pl.pallas_call(
        kernel,
        out_shape=[jax.ShapeDtypeStruct((t_all * ROW_PITCH, LANES), F32), jax.ShapeDtypeStruct((8, t_all), F32)],
        grid=(n_tiles + 2,),
        in_specs=[pl.BlockSpec(memory_space=pltpu.SMEM),
                  xp_spec, xs_spec, _const_spec((1, D_MODEL)), _const_spec((1, D_MODEL)),
                  tok(ATTN_WIDTH, front), prev_spec, kv_spec, next_spec, prev_spec, kv_spec, next_spec,
                  tok(GMLP_WIDTH, front), tok(GMLP_WIDTH, front), tok(D_MODEL, back), tok(D_MODEL, back),
                  _const_spec((GMLP_GROUPS, BLOCK, BLOCK)), _const_spec((GMLP_GROUPS, BLOCK, BLOCK)),
                  _const_spec((ATTN_WIDTH, D_MODEL)), _const_spec((GMLP_WIDTH, D_MODEL)),
                  _const_spec((D_MODEL, D_MODEL)), _const_spec((1, D_MODEL)), _const_spec((1, D_MODEL)),
                  _const_spec((D_MODEL, 2 * LANES)), _const_spec((1, LANES))],
        out_specs=[pl.BlockSpec((TM * ROW_PITCH, LANES), lambda i: (last(i), 0)),
                   pl.BlockSpec((8, TM), lambda i: (0, last(i)))],
        scratch_shapes=[pltpu.VMEM((TM, D_MODEL), F32),
                        pltpu.VMEM((TM + 2 * BLOCK, KV_WIDTH), BF16),
                        pltpu.VMEM((TM + 2 * BLOCK, KV_WIDTH), BF16),
                        pltpu.VMEM((2, TM, ATTN_WIDTH), BF16),
                        pltpu.VMEM((2, TM, GMLP_WIDTH), BF16),
                        pltpu.VMEM((TM, D_MODEL), BF16),
                        pltpu.VMEM((2, TM, D_MODEL), F32)],
        compiler_params=pltpu.CompilerParams(dimension_semantics=("arbitrary",),
                                             vmem_limit_bytes=VMEM_LIMIT),
        name="mix",
    )(sink, xp, xs, g0, b0, q, k, k, k, v, v, v, gu, vgn, sa, sb, ws, bs, wa, wb, wo, g1, b1, wr, br)


def _gather_copy(src_hbm, dst_buf, sem, src_row, dst_row, n_rows):
    return pltpu.make_async_copy(src_hbm.at[pl.ds(src_row, n_rows)], dst_buf.at[pl.ds(dst_row, n_rows)], sem)


def _start_gather(idx_ref, src_hbm, dst_buf, sem, n_rows):
    def issue(r, carry):
        src_row = pl.multiple_of(idx_ref[0, r] * ROW_PITCH, 8)
        dst_row = pl.multiple_of(r * ROW_PITCH, 8)
        _gather_copy(src_hbm, dst_buf, sem, src_row, dst_row, n_rows).start()
        return carry

    lax.fori_loop(0, TM, issue, 0, unroll=8)


def _wait_gather(src_hbm, dst_buf, sem, n_rows):
    _gather_copy(src_hbm, dst_buf, sem, 0, 0, TM * n_rows).wait()


def _idx_specs(n_tiles):
    first = pl.BlockSpec((None, 1, TM), lambda i, *_: (0, 0, 0), memory_space=pltpu.SMEM)
    nxt = pl.BlockSpec((None, 1, TM), lambda i, *_: (jnp.minimum(i + 1, n_tiles - 1), 0, 0),
                       memory_space=pltpu.SMEM)
    return first, nxt


def _unsort_kernel(idx0_ref, idxn_ref, ys_hbm, out_ref, buf, sem):
    i = pl.program_id(0)
    slot = i % 2

    @pl.when(i == 0)
    def _():
        _start_gather(idx0_ref, ys_hbm, buf.at[0], sem.at[0], X_ROWS)

    @pl.when(i + 1 < pl.num_programs(0))
    def _():
        _start_gather(idxn_ref, ys_hbm, buf.at[1 - slot], sem.at[1 - slot], X_ROWS)

    cur = buf.at[slot]
    _wait_gather(ys_hbm, cur, sem.at[slot], X_ROWS)
    for r in range(X_ROWS):
        out_ref[:, r * LANES:(r + 1) * LANES] = _load_token_major(cur, r)


def _unsort_call(idx, ysorted, name):
    n_tiles = idx.shape[0] // TM
    idx = idx.reshape(n_tiles, 1, TM)
    first, nxt = _idx_specs(n_tiles)
    return pl.pallas_call(
        _unsort_kernel,
        out_shape=jax.ShapeDtypeStruct((n_tiles * TM, D_MODEL), F32),
        grid=(n_tiles,),
        in_specs=[first, nxt, pl.BlockSpec(memory_space=pl.ANY)],
        out_specs=pl.BlockSpec((TM, D_MODEL), lambda i: (i, 0)),
        scratch_shapes=[pltpu.VMEM((2, TM * ROW_PITCH, LANES), F32), pltpu.SemaphoreType.DMA((2,))],
        compiler_params=pltpu.CompilerParams(dimension_semantics=("arbitrary",)),
        name=name,
    )(idx, idx, ysorted)


def _moe_kernel(elo_ref, ehi_ref, nvalid_ref, idx0_ref, idxn_ref, x1t_hbm,
                wg_lo, wu_lo, wd_lo, wg_hi, wu_hi, wd_hi, g2_ref, b2_ref, out_ref,
                buf, sem, xb_scr, z_scr):
    del elo_ref, ehi_ref
    i = pl.program_id(0)
    n = pl.num_programs(0)
    slot = i % 2

    @pl.when(jnp.logical_and(i == 0, nvalid_ref[0] > 0))
    def _():
        _start_gather(idx0_ref, x1t_hbm, buf.at[0], sem.at[0], ROW_PITCH)

    @pl.when(jnp.logical_and(i + 1 < n, nvalid_ref[jnp.minimum(i + 1, n - 1)] > 0))
    def _():
        _start_gather(idxn_ref, x1t_hbm, buf.at[1 - slot], sem.at[1 - slot], ROW_PITCH)

    @pl.when(nvalid_ref[i] > 0)
    def _():
        cur = buf.at[slot]
        _wait_gather(x1t_hbm, cur, sem.at[slot], ROW_PITCH)
        for r in range(X_ROWS):
            xb_scr[:, r * LANES:(r + 1) * LANES] = _load_token_major(cur, r).astype(BF16)
        route = _load_token_major(cur, X_ROWS)
        xb = xb_scr[...]

        def hidden(wg, wu, c):
            gate = jnp.dot(xb, wg[...], preferred_element_type=F32)
            up = jnp.dot(xb, wu[...], preferred_element_type=F32)
            return ((jax.nn.silu(gate) * up) * c).astype(BF16)

        h_lo = hidden(wg_lo, wu_lo, route[:, 0:1])
        h_hi = hidden(wg_hi, wu_hi, route[:, 1:2])
        for c in range(0, D_MODEL, COL_CHUNK):
            y = (jnp.dot(h_lo, wd_lo[:, c:c + COL_CHUNK], preferred_element_type=F32)
                 + jnp.dot(h_hi, wd_hi[:, c:c + COL_CHUNK], preferred_element_type=F32))
            for r in range(c // LANES, (c + COL_CHUNK) // LANES):
                z_scr[:, r * LANES:(r + 1) * LANES] = (DEEPNORM_ALPHA * _load_token_major(cur, r)
                                                       + y[:, r * LANES - c:(r + 1) * LANES - c])
        _store_token_major(out_ref, _layer_norm(z_scr[...], g2_ref[...], b2_ref[...]), X_ROWS)

    @pl.when(nvalid_ref[i] == 0)
    def _():
        out_ref[...] = jnp.zeros(out_ref.shape, out_ref.dtype)


def _moe_call(e_lo, e_hi, nvalid, src, x1t, wg, wu, wd, g2, b2):
    n_tiles = src.shape[0] // TM
    src = src.reshape(n_tiles, 1, TM)
    first, nxt = _idx_specs(n_tiles)

    def w_spec(shape, which):
        def imap(i, elo, ehi, nv):
            return ((elo, ehi)[which][i], 0, 0)
        return pl.BlockSpec((None,) + shape, imap)

    up_shape, down_shape = (D_MODEL, EXPERT_FF), (EXPERT_FF, D_MODEL)
    grid_spec = pltpu.PrefetchScalarGridSpec(
        num_scalar_prefetch=3,
        grid=(n_tiles,),
        in_specs=[first, nxt, pl.BlockSpec(memory_space=pl.ANY),
                  w_spec(up_shape, 0), w_spec(up_shape, 0), w_spec(down_shape, 0),
                  w_spec(up_shape, 1), w_spec(up_shape, 1), w_spec(down_shape, 1),
                  _const_spec((1, D_MODEL)), _const_spec((1, D_MODEL))],
        out_specs=pl.BlockSpec((TM * ROW_PITCH, LANES), lambda i, *_: (i, 0)),
        scratch_shapes=[pltpu.VMEM((2, TM * ROW_PITCH, LANES), F32), pltpu.SemaphoreType.DMA((2,)),
                        pltpu.VMEM((TM, D_MODEL), BF16), pltpu.VMEM((TM, D_MODEL), F32)],
    )
    return pl.pallas_call(
        _moe_kernel,
        out_shape=jax.ShapeDtypeStruct((n_tiles * TM * ROW_PITCH, LANES), F32),
        grid_spec=grid_spec,
        compiler_params=pltpu.CompilerParams(dimension_semantics=("arbitrary",),
                                             vmem_limit_bytes=VMEM_LIMIT),
        name="moe",
    )(e_lo, e_hi, nvalid, src, src, x1t, wg, wu, wd, wg, wu, wd, g2, b2)


_PAIR_LO = np.array([0, 0, 0, 1, 1, 2], np.int32)
_PAIR_HI = np.array([1, 2, 3, 2, 3, 3], np.int32)


def _bucket_layout(bucket, n_slots):
    t_all = bucket.shape[0]
    n_tiles = n_slots // TM
    onehot = (bucket[:, None] == jnp.arange(N_BUCKETS, dtype=jnp.int32)[None, :]).astype(jnp.int32)
    counts = jnp.sum(onehot, axis=0)
    rank = jnp.sum((jnp.cumsum(onehot, axis=0) - onehot) * onehot, axis=1)
    padded = ((counts + TM - 1) // TM) * TM
    ends = jnp.cumsum(padded)
    starts = ends - padded
    dest = starts[bucket] + rank
    src = jnp.zeros((n_slots,), jnp.int32).at[dest].set(jnp.arange(t_all, dtype=jnp.int32))
    tile_start = jnp.arange(n_tiles, dtype=jnp.int32) * TM
    last_start = jnp.maximum(ends[-1] - TM, 0)
    tile_bucket = jnp.sum((jnp.minimum(tile_start, last_start)[:, None] >= ends[None, :]).astype(jnp.int32), axis=1)
    tile_bucket = jnp.minimum(tile_bucket, N_BUCKETS - 1)
    nvalid = jnp.clip(counts[tile_bucket] - (tile_start - starts[tile_bucket]), 0, TM)
    nvalid = jnp.where(tile_start < ends[-1], nvalid, 0).astype(jnp.int32)
    group, pair = tile_bucket // N_PAIRS, tile_bucket % N_PAIRS
    e_lo = group * EXPERTS_PER_GROUP + jnp.asarray(_PAIR_LO)[pair]
    e_hi = group * EXPERTS_PER_GROUP + jnp.asarray(_PAIR_HI)[pair]
    return src, dest, e_lo.astype(jnp.int32), e_hi.astype(jnp.int32), nvalid


def _layer(xp, xs, in_ln_g, in_ln_b, w_in, attn_sink, gmlp_w_s, gmlp_b_s, gmlp_ln_g, gmlp_ln_b,
           w_attn_branch, w_gmlp_branch, w_out, ln1_g, ln1_b,
           router_w_group, router_b_group, router_w_expert, router_b_expert,
           w_expert_gate, w_expert_up, w_expert_down, ln2_g, ln2_b):
    t_p, t_s = xp.shape[0], xs.shape[0]
    t_all = t_p + t_s
    row = lambda p: p.reshape(1, -1).astype(F32)

    q, k, v, gu, vgn, sa, sb = _proj_call(xp, xs, row(in_ln_g), row(in_ln_b), w_in.astype(BF16),
                                          gmlp_ln_g.astype(F32), gmlp_ln_b.astype(F32))

    wr = jnp.concatenate([router_w_group, router_w_expert], axis=1).astype(F32)
    wr = jnp.pad(wr, ((0, 0), (0, LANES - wr.shape[1])))
    wr_hi = wr.astype(BF16)
    wr_lo = (wr - wr_hi.astype(F32)).astype(BF16)
    br = jnp.pad(jnp.concatenate([router_b_group, router_b_expert]).astype(F32),
                 (0, LANES - N_EXPERT_GROUPS - N_EXPERTS)).reshape(1, LANES)
    bs = jnp.broadcast_to(gmlp_b_s.astype(F32)[:, :, None], (GMLP_GROUPS, BLOCK, BLOCK))

    x1t, routed = _mix_call(attn_sink.astype(F32), xp, xs, row(in_ln_g), row(in_ln_b),
                            q, k, v, gu, vgn, sa, sb, gmlp_w_s.astype(BF16), bs,
                            w_attn_branch.astype(BF16), w_gmlp_branch.astype(BF16), w_out.astype(BF16),
                            row(ln1_g), row(ln1_b), jnp.concatenate([wr_hi, wr_lo], axis=1), br)

    n_slots = t_all + N_BUCKETS * TM
    bucket = routed[2].astype(jnp.int32)
    src, dest, e_lo, e_hi, nvalid = _bucket_layout(bucket, n_slots)
    ysorted = _moe_call(e_lo, e_hi, nvalid, src, x1t,
                        w_expert_gate.astype(BF16), w_expert_up.astype(BF16), w_expert_down.astype(BF16),
                        row(ln2_g), row(ln2_b))
    return _unsort_call(dest[:t_p], ysorted, "unsort_prompt"), _unsort_call(dest[t_p:], ysorted, "unsort_sample")


def kernel(x_prompt, x_sample, in_ln_g, in_ln_b, w_in, attn_sink, gmlp_w_s, gmlp_b_s, gmlp_ln_g, gmlp_ln_b,
           w_attn_branch, w_gmlp_branch, w_out, ln1_g, ln1_b,
           router_w_group, router_b_group, router_w_expert, router_b_expert,
           w_expert_gate, w_expert_up, w_expert_down, ln2_g, ln2_b):
    bp, sp, d = x_prompt.shape
    bs, ss, _ = x_sample.shape
    assert bp == 1 and bs == 1 and d == D_MODEL and sp % TM == 0 and ss % TM == 0
    assert w_in.shape[0] == 1, "one layer"
    yp, ys = _layer(x_prompt.reshape(sp, d), x_sample.reshape(ss, d), in_ln_g, in_ln_b, w_in[0], attn_sink[0],
                    gmlp_w_s[0], gmlp_b_s[0], gmlp_ln_g[0], gmlp_ln_b[0],
                    w_attn_branch[0], w_gmlp_branch[0], w_out[0], ln1_g[0], ln1_b[0],
                    router_w_group[0], router_b_group[0], router_w_expert[0], router_b_expert[0],
                    w_expert_gate[0], w_expert_up[0], w_expert_down[0], ln2_g[0], ln2_b[0])
    return yp.reshape(1, sp, d), ys.reshape(1, ss, d)
```

```python
import functools

import numpy as np
import jax
import jax.numpy as jnp
from jax import lax
from jax.experimental import pallas as pl
from jax.experimental.pallas import tpu as pltpu

F32 = jnp.float32
BF16 = jnp.bfloat16

D_MODEL = 2048
HEAD_DIM = 128
N_Q_HEADS = 8
N_KV_HEADS = 2
Q_PER_KV = N_Q_HEADS // N_KV_HEADS
ATTN_WIDTH = N_Q_HEADS * HEAD_DIM
KV_WIDTH = N_KV_HEADS * HEAD_DIM
WINDOW = 128
BLOCK = 128
GMLP_WIDTH = D_MODEL // 2
GMLP_GROUPS = 8
GMLP_GROUP_DIM = GMLP_WIDTH // GMLP_GROUPS
N_EXPERT_GROUPS = 4
EXPERTS_PER_GROUP = 4
N_EXPERTS = N_EXPERT_GROUPS * EXPERTS_PER_GROUP
EXPERT_FF = 512
LN_EPS = 1e-5
DEEPNORM_ALPHA = 2.0 ** 0.25
NEG_INF = -1e9
ATTN_SCALE = HEAD_DIM ** -0.5

_C_Q = 0
_C_K = _C_Q + ATTN_WIDTH
_C_V = _C_K + KV_WIDTH
_C_U = _C_V + KV_WIDTH
_C_VG = _C_U + GMLP_WIDTH
_C_GA = _C_VG + GMLP_WIDTH
_C_GB = _C_GA + D_MODEL
IN_COLS = _C_GB + D_MODEL

LANES = 128
N_PAIRS = 6
N_BUCKETS = N_EXPERT_GROUPS * N_PAIRS
X_ROWS = D_MODEL // LANES
ROW_PITCH = 24
TM = 256
COL_CHUNK = 512
MIX_CHUNK = 256
PROJ_CHUNK = 256
VMEM_LIMIT = 56 * 1024 * 1024

_SLOPES = [float(2.0 ** (-8.0 * (h + 1) / N_Q_HEADS)) for h in range(N_Q_HEADS)]


def _layer_norm(x, g, b):
    mu = jnp.mean(x, axis=-1, keepdims=True)
    xc = x - mu
    var = jnp.mean(xc * xc, axis=-1, keepdims=True)
    return xc * lax.rsqrt(var + LN_EPS) * g + b


def _store_token_major(ref, x, pad_from):
    for r in range(X_ROWS):
        ref[pl.ds(r, TM, stride=ROW_PITCH), :] = x[:, r * LANES:(r + 1) * LANES]
    for r in range(pad_from, ROW_PITCH):
        ref[pl.ds(r, TM, stride=ROW_PITCH), :] = jnp.zeros((TM, LANES), ref.dtype)


def _load_token_major(ref, r):
    return ref[pl.ds(r, TM, stride=ROW_PITCH), :]


def _emit_interleaved(*streams):
    order = []
    for s, (_, costs) in enumerate(streams):
        done = 0.0
        for c in costs:
            order.append(((done + c / 2) / sum(costs), s))
            done += c
    for _, s in sorted(order):
        next(streams[s][0])
    for gen, _ in streams:
        assert next(gen, "done") == "done", "stream has more units than declared"


def _const_spec(shape):
    nd = len(shape)
    return pl.BlockSpec(shape, lambda i, *_: (0,) * nd, pipeline_mode=pl.Buffered(1))


def _proj_kernel(n_prompt_tiles, n_tiles, xp_ref, xs_ref, g0_ref, b0_ref, w_ref, lg_ref, lb_ref,
                 q_ref, k_ref, v_ref, gu_ref, vgn_ref, sa_ref, sb_ref, xn_scr):
    i = pl.program_id(0)
    cur = i % 2
    from_prompt = jnp.minimum(i, n_tiles - 1) < n_prompt_tiles

    @pl.when(i == 0)
    def _():
        xn_scr[1] = jnp.zeros(xn_scr.shape[1:], xn_scr.dtype)

    def stage_a():
        x_cols = lambda c: jnp.where(from_prompt, xp_ref[:, c:c + COL_CHUNK], xs_ref[:, c:c + COL_CHUNK])
        mu = jnp.mean(jnp.where(from_prompt, xp_ref[...], xs_ref[...]), axis=-1, keepdims=True)
        yield
        sq = None
        for c in range(0, D_MODEL, COL_CHUNK):
            xc = x_cols(c) - mu
            part = jnp.sum(xc * xc, axis=-1, keepdims=True)
            sq = part if sq is None else sq + part
        inv = lax.rsqrt(sq / D_MODEL + LN_EPS)
        yield
        for c in range(0, D_MODEL, COL_CHUNK):
            y = (x_cols(c) - mu) * inv * g0_ref[:, c:c + COL_CHUNK] + b0_ref[:, c:c + COL_CHUNK]
            xn_scr[cur, :, c:c + COL_CHUNK] = y.astype(BF16)
            yield

    def stage_b():
        xn = xn_scr[1 - cur]

        def proj(c0, width):
            return jnp.dot(xn, w_ref[:, c0:c0 + width], preferred_element_type=F32)

        for c in range(0, ATTN_WIDTH, COL_CHUNK):
            q_ref[:, c:c + COL_CHUNK] = proj(_C_Q + c, COL_CHUNK).astype(BF16)
            yield
        kv = proj(_C_K, 2 * KV_WIDTH)
        k_ref[...] = kv[:, :KV_WIDTH].astype(BF16)
        v_ref[...] = kv[:, KV_WIDTH:].astype(BF16)
        yield
        for c in range(0, GMLP_WIDTH, COL_CHUNK):
            gu_ref[:, c:c + COL_CHUNK] = jax.nn.gelu(proj(_C_U + c, COL_CHUNK)).astype(BF16)
            yield
        for c in range(0, GMLP_WIDTH, COL_CHUNK):
            vg = jax.nn.gelu(proj(_C_VG + c, COL_CHUNK))
            for j in range(COL_CHUNK // GMLP_GROUP_DIM):
                grp = c // GMLP_GROUP_DIM + j
                blk = vg[:, j * GMLP_GROUP_DIM:(j + 1) * GMLP_GROUP_DIM]
                y = _layer_norm(blk, lg_ref[grp:grp + 1, :], lb_ref[grp:grp + 1, :])
                vgn_ref[:, grp * GMLP_GROUP_DIM:(grp + 1) * GMLP_GROUP_DIM] = y.astype(BF16)
            yield
        for c in range(0, D_MODEL, COL_CHUNK):
            sa_ref[:, c:c + COL_CHUNK] = jax.nn.sigmoid(proj(_C_GA + c, COL_CHUNK)).astype(BF16)
            yield
        for c in range(0, D_MODEL, COL_CHUNK):
            sb_ref[:, c:c + COL_CHUNK] = jax.nn.sigmoid(proj(_C_GB + c, COL_CHUNK)).astype(BF16)
            yield

    n_chunks = D_MODEL // COL_CHUNK
    _emit_interleaved((stage_b(), [1] * (IN_COLS // COL_CHUNK)),
                      (stage_a(), [2, 2] + [1] * n_chunks))


def _proj_call(xp, xs, g0, b0, w_in, lg, lb):
    n_p, n_s = xp.shape[0] // TM, xs.shape[0] // TM
    n_tiles = n_p + n_s
    t_all = xp.shape[0] + xs.shape[0]
    front = lambda i: jnp.minimum(i, n_tiles - 1)
    xp_spec = pl.BlockSpec((TM, D_MODEL), lambda i: (jnp.minimum(front(i), n_p - 1), 0))
    xs_spec = pl.BlockSpec((TM, D_MODEL), lambda i: (jnp.maximum(front(i) - n_p, 0), 0))

    def tok(width):
        return pl.BlockSpec((TM, width), lambda i: (jnp.maximum(i - 1, 0), 0))

    widths = (ATTN_WIDTH, KV_WIDTH, KV_WIDTH, GMLP_WIDTH, GMLP_WIDTH, D_MODEL, D_MODEL)
    return pl.pallas_call(
        functools.partial(_proj_kernel, n_p, n_tiles),
        out_shape=[jax.ShapeDtypeStruct((t_all, w), BF16) for w in widths],
        grid=(n_tiles + 1,),
        in_specs=[xp_spec, xs_spec, _const_spec((1, D_MODEL)), _const_spec((1, D_MODEL)),
                  _const_spec((D_MODEL, IN_COLS)),
                  _const_spec((GMLP_GROUPS, GMLP_GROUP_DIM)), _const_spec((GMLP_GROUPS, GMLP_GROUP_DIM))],
        out_specs=[tok(w) for w in widths],
        scratch_shapes=[pltpu.VMEM((2, TM, D_MODEL), BF16)],
        compiler_params=pltpu.CompilerParams(dimension_semantics=("arbitrary",),
                                             vmem_limit_bytes=VMEM_LIMIT),
        name="proj",
    )(xp, xs, g0, b0, w_in, lg, lb)


def _route(logits_t):
    row = lambda j: logits_t[j:j + 1, :]
    gl = [row(j) for j in range(N_EXPERT_GROUPS)]
    gmax, gidx = gl[0], jnp.zeros(gl[0].shape, jnp.int32)
    for j in range(1, N_EXPERT_GROUPS):
        better = gl[j] > gmax
        gmax = jnp.where(better, gl[j], gmax)
        gidx = jnp.where(better, j, gidx)
    gsum = jnp.exp(gl[0] - gmax)
    for j in range(1, N_EXPERT_GROUPS):
        gsum = gsum + jnp.exp(gl[j] - gmax)
    p_group = 1.0 / gsum

    ig = []
    for e in range(EXPERTS_PER_GROUP):
        v = row(N_EXPERT_GROUPS + (N_EXPERT_GROUPS - 1) * EXPERTS_PER_GROUP + e)
        for g in range(N_EXPERT_GROUPS - 2, -1, -1):
            v = jnp.where(gidx == g, row(N_EXPERT_GROUPS + g * EXPERTS_PER_GROUP + e), v)
        ig.append(v)
    v1, i1 = ig[0], jnp.zeros(ig[0].shape, jnp.int32)
    for e in range(1, EXPERTS_PER_GROUP):
        better = ig[e] > v1
        v1 = jnp.where(better, ig[e], v1)
        i1 = jnp.where(better, e, i1)
    v2 = jnp.where(i1 == 0, ig[1], ig[0])
    i2 = jnp.where(i1 == 0, 1, 0).astype(jnp.int32)
    for e in range(1, EXPERTS_PER_GROUP):
        better = jnp.logical_and(i1 != e, ig[e] > v2)
        v2 = jnp.where(better, ig[e], v2)
        i2 = jnp.where(better, e, i2)
    ev = jnp.exp(v2 - v1)
    ssum = 1.0 + ev
    w1 = 1.0 / ssum
    w2 = ev / ssum
    first_lo = i1 < i2
    lo = jnp.minimum(i1, i2)
    hi = jnp.maximum(i1, i2)
    c_lo = jnp.where(first_lo, w1, w2) * p_group
    c_hi = jnp.where(first_lo, w2, w1) * p_group
    pair = jnp.where(lo == 0, hi - 1, jnp.where(lo == 1, hi + 1, N_PAIRS - 1))
    bucket = gidx * N_PAIRS + pair
    return c_lo, c_hi, bucket.astype(F32)


def _mix_kernel(n_prompt_tiles, n_tiles, seq_edges_first, seq_edges_last,
                sink_ref, xp_ref, xs_ref, g0_ref, b0_ref, q_ref,
                kp_ref, km_ref, kn_ref, vp_ref, vm_ref, vn_ref,
                gu_ref, vgn_ref, sa_ref, sb_ref, ws_ref, bs_ref,
                wa_ref, wb_ref, wo_ref, g1_ref, b1_ref, wr_ref, br_ref,
                x1t_ref, rt_ref,
                xn_scr, kf_scr, vf_scr, a_scr, sg_scr, mg_scr, mix_scr):
    i = pl.program_id(0)
    nblk = TM // BLOCK
    cur = 0

    @pl.when(i == 0)
    def _():
        a_scr[1] = jnp.zeros(a_scr.shape[1:], a_scr.dtype)
        sg_scr[1] = jnp.zeros(sg_scr.shape[1:], sg_scr.dtype)
        mix_scr[0] = jnp.zeros(mix_scr.shape[1:], mix_scr.dtype)

    def stage_a():
        tile = jnp.minimum(i, n_tiles - 1)
        kf_scr[0:BLOCK, :] = kp_ref[...]
        kf_scr[BLOCK:BLOCK + TM, :] = km_ref[...]
        kf_scr[BLOCK + TM:, :] = kn_ref[...]
        vf_scr[0:BLOCK, :] = vp_ref[...]
        vf_scr[BLOCK:BLOCK + TM, :] = vm_ref[...]
        vf_scr[BLOCK + TM:, :] = vn_ref[...]
        first_blk = tile * nblk
        last_blk = tile * nblk + nblk - 1
        has_prev = jnp.logical_and(*[first_blk != e for e in seq_edges_first])
        has_next = jnp.logical_and(*[last_blk != e for e in seq_edges_last])
        kj = lax.broadcasted_iota(jnp.int32, (BLOCK, 3 * BLOCK), 1)
        qi = lax.broadcasted_iota(jnp.int32, (BLOCK, 3 * BLOCK), 0)
        dist = jnp.abs(kj - BLOCK - qi)
        in_window = dist <= WINDOW
        dist_f = dist.astype(F32)
        lo_key = jnp.where(has_prev, 0, BLOCK)
        hi_key = jnp.where(has_next, 3 * BLOCK, 2 * BLOCK)
        yield
        for j in range(nblk):
            mask = in_window
            if j == 0:
                mask = jnp.logical_and(mask, kj >= lo_key)
            if j == nblk - 1:
                mask = jnp.logical_and(mask, kj < hi_key)
            r0 = j * BLOCK
            for kvh in range(N_KV_HEADS):
                c0 = kvh * HEAD_DIM
                kb = kf_scr[r0:r0 + 3 * BLOCK, c0:c0 + HEAD_DIM]
                vb = vf_scr[r0:r0 + 3 * BLOCK, c0:c0 + HEAD_DIM]
                heads = [kvh * Q_PER_KV + g for g in range(Q_PER_KV)]
                qs = jnp.concatenate(
                    [q_ref[r0:r0 + BLOCK, h * HEAD_DIM:(h + 1) * HEAD_DIM] for h in heads], axis=0)
                s_all = lax.dot_general(qs, kb, (((1,), (1,)), ((), ())), preferred_element_type=F32)
                for g, h in enumerate(heads):
                    s = s_all[g * BLOCK:(g + 1) * BLOCK, :] * ATTN_SCALE
                    s = jnp.where(mask, s + dist_f * (-_SLOPES[h]), NEG_INF)
                    sink = sink_ref[h]
                    m = jnp.maximum(jnp.max(s, axis=-1, keepdims=True), sink)
                    p = jnp.exp(s - m)
                    denom = jnp.sum(p, axis=-1, keepdims=True) + jnp.exp(sink - m)
                    pn = (p * (1.0 / denom)).astype(BF16)
                    o = jnp.dot(pn, vb, preferred_element_type=F32)
                    a_scr[cur, r0:r0 + BLOCK, h * HEAD_DIM:(h + 1) * HEAD_DIM] = o.astype(BF16)
                    yield
            for grp in range(GMLP_GROUPS):
                c0 = grp * GMLP_GROUP_DIM
                sp = jnp.dot(ws_ref[grp], vgn_ref[r0:r0 + BLOCK, c0:c0 + GMLP_GROUP_DIM],
                             preferred_element_type=F32) + bs_ref[grp]
                u = gu_ref[r0:r0 + BLOCK, c0:c0 + GMLP_GROUP_DIM].astype(F32)
                sg_scr[cur, r0:r0 + BLOCK, c0:c0 + GMLP_GROUP_DIM] = (u * sp).astype(BF16)
            yield

    def stage_b():
        x = jnp.where(i - 1 < n_prompt_tiles, xp_ref[...], xs_ref[...])
        mu0 = jnp.mean(x, axis=-1, keepdims=True)
        xc = x - mu0
        xn_scr[...] = xc
        inv0 = lax.rsqrt(jnp.mean(xc * xc, axis=-1, keepdims=True) + LN_EPS)
        yield
        a = a_scr[1 - cur]
        sg = sg_scr[1 - cur]
        for c in range(0, D_MODEL, PROJ_CHUNK):
            ma = jnp.dot(a, wa_ref[:, c:c + PROJ_CHUNK], preferred_element_type=F32)
            mb = jnp.dot(sg, wb_ref[:, c:c + PROJ_CHUNK], preferred_element_type=F32)
            merged = (sa_ref[:, c:c + PROJ_CHUNK].astype(F32) * ma
                      + sb_ref[:, c:c + PROJ_CHUNK].astype(F32) * mb)
            mg_scr[:, c:c + PROJ_CHUNK] = merged.astype(BF16)
            yield
        mg = mg_scr[...]
        for c in range(0, D_MODEL, PROJ_CHUNK):
            mix = jnp.dot(mg, wo_ref[:, c:c + PROJ_CHUNK], preferred_element_type=F32)
            xn = xn_scr[:, c:c + PROJ_CHUNK] * inv0 * g0_ref[:, c:c + PROJ_CHUNK] + b0_ref[:, c:c + PROJ_CHUNK]
            mix_scr[1 - cur, :, c:c + PROJ_CHUNK] = DEEPNORM_ALPHA * xn + mix
            yield

    def stage_c():
        mu = jnp.mean(mix_scr[cur], axis=-1, keepdims=True)
        yield
        zc = mix_scr[cur] - mu
        inv = lax.rsqrt(jnp.mean(zc * zc, axis=-1, keepdims=True) + LN_EPS)
        yield
        r = None
        for c in range(0, D_MODEL, MIX_CHUNK):
            x1 = ((mix_scr[cur, :, c:c + MIX_CHUNK] - mu) * inv * g1_ref[:, c:c + MIX_CHUNK]
                  + b1_ref[:, c:c + MIX_CHUNK])
            for k in range(MIX_CHUNK // LANES):
                x1t_ref[pl.ds(c // LANES + k, TM, stride=ROW_PITCH), :] = x1[:, k * LANES:(k + 1) * LANES]
            x_hi = x1.astype(BF16)
            x_lo = (x1 - x_hi.astype(F32)).astype(BF16)
            part = (jnp.dot(x_hi, wr_ref[c:c + MIX_CHUNK, :], preferred_element_type=F32)
                    + jnp.dot(x_lo, wr_ref[c:c + MIX_CHUNK, :], preferred_element_type=F32))
            r = part if r is None else r + part
            yield
        logits = r[:, :LANES] + r[:, LANES:] + br_ref[...]
        c_lo, c_hi, bucket = _route(logits.T)
        routed = jnp.concatenate([c_lo, c_hi, bucket, jnp.zeros((5, TM), F32)], axis=0)
        rt_ref[...] = routed
        padded = jnp.concatenate([routed, jnp.zeros((LANES - 8, TM), F32)], axis=0)
        x1t_ref[pl.ds(X_ROWS, TM, stride=ROW_PITCH), :] = padded.T
        for k in range(X_ROWS + 1, ROW_PITCH):
            x1t_ref[pl.ds(k, TM, stride=ROW_PITCH), :] = jnp.zeros((TM, LANES), F32)
        yield

    n_chunks = D_MODEL // MIX_CHUNK
    def emit(parity):
        nonlocal cur
        cur = parity
        _emit_interleaved((stage_b(), [1] + [512] * (2 * D_MODEL // PROJ_CHUNK)),
                          (stage_a(), [100] + ([600] * N_Q_HEADS + [500]) * nblk),
                          (stage_c(), [1000, 1000] + [550] * n_chunks + [1000]))

    for parity in range(2):
        pl.when(i % 2 == parity)(functools.partial(emit, parity))


def _mix_call(sink, xp, xs, g0, b0, q, k, v, gu, vgn, sa, sb, ws, bs, wa, wb, wo, g1, b1, wr, br):
    n_p, n_s = xp.shape[0] // TM, xs.shape[0] // TM
    t_all = xp.shape[0] + xs.shape[0]
    nblk = TM // BLOCK
    blk_p, blk_all = xp.shape[0] // BLOCK, t_all // BLOCK
    n_tiles = n_p + n_s
    front = lambda i: jnp.minimum(i, n_tiles - 1)
    back = lambda i: jnp.clip(i - 1, 0, n_tiles - 1)
    last = lambda i: jnp.maximum(i - 2, 0)
    xp_spec = pl.BlockSpec((TM, D_MODEL), lambda i: (jnp.minimum(back(i), n_p - 1), 0))
    xs_spec = pl.BlockSpec((TM, D_MODEL), lambda i: (jnp.maximum(back(i) - n_p, 0), 0))

    def tok(width, which):
        return pl.BlockSpec((TM, width), lambda i: (which(i), 0))

    prev_spec = pl.BlockSpec((BLOCK, KV_WIDTH), lambda i: (jnp.maximum(front(i) * nblk - 1, 0), 0))
    next_spec = pl.BlockSpec((BLOCK, KV_WIDTH),
                             lambda i: (jnp.minimum((front(i) + 1) * nblk, blk_all - 1), 0))
    kv_spec = tok(KV_WIDTH, front)
    kernel = functools.partial(_mix_kernel, n_p, n_tiles, (0, blk_p), (blk_p - 1, blk_all - 1))
    return pl.pallas_call(
        kernel,
        out_shape=[jax.ShapeDtypeStruct((t_all * ROW_PITCH, LANES), F32), jax.ShapeDtypeStruct((8, t_all), F32)],
        grid=(n_tiles + 2,),
        in_specs=[pl.BlockSpec(memory_space=pltpu.SMEM),
                  xp_spec, xs_spec, _const_spec((1, D_MODEL)), _const_spec((1, D_MODEL)),
                  tok(ATTN_WIDTH, front), prev_spec, kv_spec, next_spec, prev_spec, kv_spec, next_spec,
                  tok(GMLP_WIDTH, front), tok(GMLP_WIDTH, front), tok(D_MODEL, back), tok(D_MODEL, back),
                  _const_spec((GMLP_GROUPS, BLOCK, BLOCK)), _const_spec((GMLP_GROUPS, BLOCK, BLOCK)),
                  _const_spec((ATTN_WIDTH, D_MODEL)), _const_spec((GMLP_WIDTH, D_MODEL)),
                  _const_spec((D_MODEL, D_MODEL)), _const_spec((1, D_MODEL)), _const_spec((1, D_MODEL)),
                  _const_spec((D_MODEL, 2 * LANES)), _const_spec((1, LANES))],
        out_specs=[pl.BlockSpec((TM * ROW_PITCH, LANES), lambda i: (last(i), 0)),
                   pl.BlockSpec((8, TM), lambda i: (0, last(i)))],
        scratch_shapes=[pltpu.VMEM((TM, D_MODEL), F32),
                        pltpu.VMEM((TM + 2 * BLOCK, KV_WIDTH), BF16),
                        pltpu.VMEM((TM + 2 * BLOCK, KV_WIDTH), BF16),
                        pltpu.VMEM((2, TM, ATTN_WIDTH), BF16),
                        pltpu.VMEM((2, TM, GMLP_WIDTH), BF16),
                        pltpu.VMEM((TM, D_MODEL), BF16),
                        pltpu.VMEM((2, TM, D_MODEL), F32)],
        compiler_params=pltpu.CompilerParams(dimension_semantics=("arbitrary",),
                                             vmem_limit_bytes=VMEM_LIMIT),
        name="mix",
    )(sink, xp, xs, g0, b0, q, k, k, k, v, v, v, gu, vgn, sa, sb, ws, bs, wa, wb, wo, g1, b1, wr, br)


def _gather_copy(src_hbm, dst_buf, sem, src_row, dst_row, n_rows):
    return pltpu.make_async_copy(src_hbm.at[pl.ds(src_row, n_rows)], dst_buf.at[pl.ds(dst_row, n_rows)], sem)


def _start_gather(idx_ref, src_hbm, dst_buf, sem, n_rows):
    def issue(r, carry):
        src_row = pl.multiple_of(idx_ref[0, r] * ROW_PITCH, 8)
        dst_row = pl.multiple_of(r * ROW_PITCH, 8)
        _gather_copy(src_hbm, dst_buf, sem, src_row, dst_row, n_rows).start()
        return carry

    lax.fori_loop(0, TM, issue, 0, unroll=8)


def _wait_gather(src_hbm, dst_buf, sem, n_rows):
    _gather_copy(src_hbm, dst_buf, sem, 0, 0, TM * n_rows).wait()


def _idx_specs(n_tiles):
    first = pl.BlockSpec((None, 1, TM), lambda i, *_: (0, 0, 0), memory_space=pltpu.SMEM)
    nxt = pl.BlockSpec((None, 1, TM), lambda i, *_: (jnp.minimum(i + 1, n_tiles - 1), 0, 0),
                       memory_space=pltpu.SMEM)
    return first, nxt


def _unsort_kernel(idx0_ref, idxn_ref, ys_hbm, g2_ref, b2_ref, out_ref, buf, sem):
    i = pl.program_id(0)
    slot = i % 2

    @pl.when(i == 0)
    def _():
        _start_gather(idx0_ref, ys_hbm, buf.at[0], sem.at[0], X_ROWS)

    @pl.when(i + 1 < pl.num_programs(0))
    def _():
        _start_gather(idxn_ref, ys_hbm, buf.at[1 - slot], sem.at[1 - slot], X_ROWS)

    cur = buf.at[slot]
    _wait_gather(ys_hbm, cur, sem.at[slot], X_ROWS)
    for r in range(X_ROWS):
        out_ref[:, r * LANES:(r + 1) * LANES] = _load_token_major(cur, r)
    out_ref[...] = _layer_norm(out_ref[...], g2_ref[...], b2_ref[...])


def _unsort_call(idx, ysorted, g2, b2, name):
    n_tiles = idx.shape[0] // TM
    idx = idx.reshape(n_tiles, 1, TM)
    first, nxt = _idx_specs(n_tiles)
    return pl.pallas_call(
        _unsort_kernel,
        out_shape=jax.ShapeDtypeStruct((n_tiles * TM, D_MODEL), F32),
        grid=(n_tiles,),
        in_specs=[first, nxt, pl.BlockSpec(memory_space=pl.ANY),
                  _const_spec((1, D_MODEL)), _const_spec((1, D_MODEL))],
        out_specs=pl.BlockSpec((TM, D_MODEL), lambda i: (i, 0)),
        scratch_shapes=[pltpu.VMEM((2, TM * ROW_PITCH, LANES), F32), pltpu.SemaphoreType.DMA((2,))],
        compiler_params=pltpu.CompilerParams(dimension_semantics=("arbitrary",)),
        name=name,
    )(idx, idx, ysorted, g2, b2)


def _moe_kernel(elo_ref, ehi_ref, nvalid_ref, idx0_ref, idxn_ref, x1t_hbm,
                wg_lo, wu_lo, wd_lo, wg_hi, wu_hi, wd_hi, out_ref,
                buf, sem, xb_scr):
    del elo_ref, ehi_ref
    i = pl.program_id(0)
    n = pl.num_programs(0)
    slot = i % 2

    @pl.when(jnp.logical_and(i == 0, nvalid_ref[0] > 0))
    def _():
        _start_gather(idx0_ref, x1t_hbm, buf.at[0], sem.at[0], ROW_PITCH)

    @pl.when(jnp.logical_and(i + 1 < n, nvalid_ref[jnp.minimum(i + 1, n - 1)] > 0))
    def _():
        _start_gather(idxn_ref, x1t_hbm, buf.at[1 - slot], sem.at[1 - slot], ROW_PITCH)

    @pl.when(nvalid_ref[i] > 0)
    def _():
        cur = buf.at[slot]
        _wait_gather(x1t_hbm, cur, sem.at[slot], ROW_PITCH)
        for r in range(X_ROWS):
            xb_scr[:, r * LANES:(r + 1) * LANES] = _load_token_major(cur, r).astype(BF16)
        route = _load_token_major(cur, X_ROWS)
        xb = xb_scr[...]

        def hidden(wg, wu, c):
            gate = jnp.dot(xb, wg[...], preferred_element_type=F32)
            up = jnp.dot(xb, wu[...], preferred_element_type=F32)
            return ((jax.nn.silu(gate) * up) * c).astype(BF16)

        h_lo = hidden(wg_lo, wu_lo, route[:, 0:1])
        h_hi = hidden(wg_hi, wu_hi, route[:, 1:2])
        for c in range(0, D_MODEL, COL_CHUNK):
            y = (jnp.dot(h_lo, wd_lo[:, c:c + COL_CHUNK], preferred_element_type=F32)
                 + jnp.dot(h_hi, wd_hi[:, c:c + COL_CHUNK], preferred_element_type=F32))
            for r in range(c // LANES, (c + COL_CHUNK) // LANES):
                out_ref[pl.ds(r, TM, stride=ROW_PITCH), :] = (DEEPNORM_ALPHA * _load_token_major(cur, r)
                                                              + y[:, r * LANES - c:(r + 1) * LANES - c])
        for r in range(X_ROWS, ROW_PITCH):
            out_ref[pl.ds(r, TM, stride=ROW_PITCH), :] = jnp.zeros((TM, LANES), F32)

    @pl.when(nvalid_ref[i] == 0)
    def _():
        out_ref[...] = jnp.zeros(out_ref.shape, out_ref.dtype)


def _moe_call(e_lo, e_hi, nvalid, src, x1t, wg, wu, wd):
    n_tiles = src.shape[0] // TM
    src = src.reshape(n_tiles, 1, TM)
    first, nxt = _idx_specs(n_tiles)

    def w_spec(shape, which):
        def imap(i, elo, ehi, nv):
            return ((elo, ehi)[which][i], 0, 0)
        return pl.BlockSpec((None,) + shape, imap)

    up_shape, down_shape = (D_MODEL, EXPERT_FF), (EXPERT_FF, D_MODEL)
    grid_spec = pltpu.PrefetchScalarGridSpec(
        num_scalar_prefetch=3,
        grid=(n_tiles,),
        in_specs=[first, nxt, pl.BlockSpec(memory_space=pl.ANY),
                  w_spec(up_shape, 0), w_spec(up_shape, 0), w_spec(down_shape, 0),
                  w_spec(up_shape, 1), w_spec(up_shape, 1), w_spec(down_shape, 1)],
        out_specs=pl.BlockSpec((TM * ROW_PITCH, LANES), lambda i, *_: (i, 0)),
        scratch_shapes=[pltpu.VMEM((2, TM * ROW_PITCH, LANES), F32), pltpu.SemaphoreType.DMA((2,)),
                        pltpu.VMEM((TM, D_MODEL), BF16)],
    )
    return pl.pallas_call(
        _moe_kernel,
        out_shape=jax.ShapeDtypeStruct((n_tiles * TM * ROW_PITCH, LANES), F32),
        grid_spec=grid_spec,
        compiler_params=pltpu.CompilerParams(dimension_semantics=("arbitrary",),
                                             vmem_limit_bytes=VMEM_LIMIT),
        name="moe",
    )(e_lo, e_hi, nvalid, src, src, x1t, wg, wu, wd, wg, wu, wd)


_PAIR_LO = np.array([0, 0, 0, 1, 1, 2], np.int32)
_PAIR_HI = np.array([1, 2, 3, 2, 3, 3], np.int32)


def _bucket_layout(bucket, n_slots):
    t_all = bucket.shape[0]
    n_tiles = n_slots // TM
    onehot = (bucket[:, None] == jnp.arange(N_BUCKETS, dtype=jnp.int32)[None, :]).astype(jnp.int32)
    counts = jnp.sum(onehot, axis=0)
    rank = jnp.sum((jnp.cumsum(onehot, axis=0) - onehot) * onehot, axis=1)
    padded = ((counts + TM - 1) // TM) * TM
    ends = jnp.cumsum(padded)
    starts = ends - padded
    dest = starts[bucket] + rank
    src = jnp.zeros((n_slots,), jnp.int32).at[dest].set(jnp.arange(t_all, dtype=jnp.int32))
    tile_start = jnp.arange(n_tiles, dtype=jnp.int32) * TM
    last_start = jnp.maximum(ends[-1] - TM, 0)
    tile_bucket = jnp.sum((jnp.minimum(tile_start, last_start)[:, None] >= ends[None, :]).astype(jnp.int32), axis=1)
    tile_bucket = jnp.minimum(tile_bucket, N_BUCKETS - 1)
    nvalid = jnp.clip(counts[tile_bucket] - (tile_start - starts[tile_bucket]), 0, TM)
    nvalid = jnp.where(tile_start < ends[-1], nvalid, 0).astype(jnp.int32)
    group, pair = tile_bucket // N_PAIRS, tile_bucket % N_PAIRS
    e_lo = group * EXPERTS_PER_GROUP + jnp.asarray(_PAIR_LO)[pair]
    e_hi = group * EXPERTS_PER_GROUP + jnp.asarray(_PAIR_HI)[pair]
    return src, dest, e_lo.astype(jnp.int32), e_hi.astype(jnp.int32), nvalid


def _layer(xp, xs, in_ln_g, in_ln_b, w_in, attn_sink, gmlp_w_s, gmlp_b_s, gmlp_ln_g, gmlp_ln_b,
           w_attn_branch, w_gmlp_branch, w_out, ln1_g, ln1_b,
           router_w_group, router_b_group, router_w_expert, router_b_expert,
           w_expert_gate, w_expert_up, w_expert_down, ln2_g, ln2_b):
    t_p, t_s = xp.shape[0], xs.shape[0]
    t_all = t_p + t_s
    row = lambda p: p.reshape(1, -1).astype(F32)

    q, k, v, gu, vgn, sa, sb = _proj_call(xp, xs, row(in_ln_g), row(in_ln_b), w_in.astype(BF16),
                                          gmlp_ln_g.astype(F32), gmlp_ln_b.astype(F32))

    wr = jnp.concatenate([router_w_group, router_w_expert], axis=1).astype(F32)
    wr = jnp.pad(wr, ((0, 0), (0, LANES - wr.shape[1])))
    wr_hi = wr.astype(BF16)
    wr_lo = (wr - wr_hi.astype(F32)).astype(BF16)
    br = jnp.pad(jnp.concatenate([router_b_group, router_b_expert]).astype(F32),
                 (0, LANES - N_EXPERT_GROUPS - N_EXPERTS)).reshape(1, LANES)
    bs = jnp.broadcast_to(gmlp_b_s.astype(F32)[:, :, None], (GMLP_GROUPS, BLOCK, BLOCK))

    x1t, routed = _mix_call(attn_sink.astype(F32), xp, xs, row(in_ln_g), row(in_ln_b),
                            q, k, v, gu, vgn, sa, sb, gmlp_w_s.astype(BF16), bs,
                            w_attn_branch.astype(BF16), w_gmlp_branch.astype(BF16), w_out.astype(BF16),
                            row(ln1_g), row(ln1_b), jnp.concatenate([wr_hi, wr_lo], axis=1), br)

    n_slots = t_all + N_BUCKETS * TM
    bucket = routed[2].astype(jnp.int32)
    src, dest, e_lo, e_hi, nvalid = _bucket_layout(bucket, n_slots)
    zsorted = _moe_call(e_lo, e_hi, nvalid, src, x1t,
                        w_expert_gate.astype(BF16), w_expert_up.astype(BF16), w_expert_down.astype(BF16))
    g2, b2 = row(ln2_g), row(ln2_b)
    return (_unsort_call(dest[:t_p], zsorted, g2, b2, "unsort_prompt"),
            _unsort_call(dest[t_p:], zsorted, g2, b2, "unsort_sample"))


def kernel(x_prompt, x_sample, in_ln_g, in_ln_b, w_in, attn_sink, gmlp_w_s, gmlp_b_s, gmlp_ln_g, gmlp_ln_b,
           w_attn_branch, w_gmlp_branch, w_out, ln1_g, ln1_b,
           router_w_group, router_b_group, router_w_expert, router_b_expert,
           w_expert_gate, w_expert_up, w_expert_down, ln2_g, ln2_b):
    bp, sp, d = x_prompt.shape
    bs, ss, _ = x_sample.shape
    assert bp == 1 and bs == 1 and d == D_MODEL and sp % TM == 0 and ss % TM == 0
    assert w_in.shape[0] == 1, "one layer"
    yp, ys = _layer(x_prompt.reshape(sp, d), x_sample.reshape(ss, d), in_ln_g, in_ln_b, w_in[0], attn_sink[0],
                    gmlp_w_s[0], gmlp_b_s[0], gmlp_ln_g[0], gmlp_ln_b[0],
                    w_attn_branch[0], w_gmlp_branch[0], w_out[0], ln1_g[0], ln1_b[0],
                    router_w_group[0], router_b_group[0], router_w_expert[0], router_b_expert[0],
                    w_expert_gate[0], w_expert_up[0], w_expert_down[0], ln2_g[0], ln2_b[0])
    return yp.reshape(1, sp, d), ys.reshape(1, ss, d)
```

```python
import functools

import numpy as np
import jax
import jax.numpy as jnp
from jax import lax
from jax.experimental import pallas as pl
from jax.experimental.pallas import tpu as pltpu

F32 = jnp.float32
BF16 = jnp.bfloat16

D_MODEL = 2048
HEAD_DIM = 128
N_Q_HEADS = 8
N_KV_HEADS = 2
Q_PER_KV = N_Q_HEADS // N_KV_HEADS
ATTN_WIDTH = N_Q_HEADS * HEAD_DIM
KV_WIDTH = N_KV_HEADS * HEAD_DIM
WINDOW = 128
BLOCK = 128
GMLP_WIDTH = D_MODEL // 2
GMLP_GROUPS = 8
GMLP_GROUP_DIM = GMLP_WIDTH // GMLP_GROUPS
N_EXPERT_GROUPS = 4
EXPERTS_PER_GROUP = 4
N_EXPERTS = N_EXPERT_GROUPS * EXPERTS_PER_GROUP
EXPERT_FF = 512
LN_EPS = 1e-5
DEEPNORM_ALPHA = 2.0 ** 0.25
NEG_INF = -1e9
ATTN_SCALE = HEAD_DIM ** -0.5

_C_Q = 0
_C_K = _C_Q + ATTN_WIDTH
_C_V = _C_K + KV_WIDTH
_C_U = _C_V + KV_WIDTH
_C_VG = _C_U + GMLP_WIDTH
_C_GA = _C_VG + GMLP_WIDTH
_C_GB = _C_GA + D_MODEL
IN_COLS = _C_GB + D_MODEL

LANES = 128
N_PAIRS = 6
N_BUCKETS = N_EXPERT_GROUPS * N_PAIRS
X_ROWS = D_MODEL // LANES
ROW_PITCH = 24
TM = 256
COL_CHUNK = 512
MIX_CHUNK = 256
PROJ_CHUNK = 256
VMEM_LIMIT = 56 * 1024 * 1024

_SLOPES = [float(2.0 ** (-8.0 * (h + 1) / N_Q_HEADS)) for h in range(N_Q_HEADS)]


def _layer_norm(x, g, b):
    mu = jnp.mean(x, axis=-1, keepdims=True)
    xc = x - mu
    var = jnp.mean(xc * xc, axis=-1, keepdims=True)
    return xc * lax.rsqrt(var + LN_EPS) * g + b


def _store_token_major(ref, x, pad_from):
    for r in range(X_ROWS):
        ref[pl.ds(r, TM, stride=ROW_PITCH), :] = x[:, r * LANES:(r + 1) * LANES]
    for r in range(pad_from, ROW_PITCH):
        ref[pl.ds(r, TM, stride=ROW_PITCH), :] = jnp.zeros((TM, LANES), ref.dtype)


def _load_token_major(ref, r):
    return ref[pl.ds(r, TM, stride=ROW_PITCH), :]


def _emit_interleaved(*streams):
    order = []
    for s, (_, costs) in enumerate(streams):
        done = 0.0
        for c in costs:
            order.append(((done + c / 2) / sum(costs), s))
            done += c
    for _, s in sorted(order):
        next(streams[s][0])
    for gen, _ in streams:
        assert next(gen, "done") == "done", "stream has more units than declared"


def _const_spec(shape):
    nd = len(shape)
    return pl.BlockSpec(shape, lambda i, *_: (0,) * nd, pipeline_mode=pl.Buffered(1))


def _proj_kernel(n_prompt_tiles, xp_ref, xs_ref, g0_ref, b0_ref, w_ref, lg_ref, lb_ref,
                 q_ref, k_ref, v_ref, gu_ref, vgn_ref, sa_ref, sb_ref, xn_scr):
    i = pl.program_id(0)

    @pl.when(i < n_prompt_tiles)
    def _():
        xn_scr[...] = _layer_norm(xp_ref[...], g0_ref[...], b0_ref[...]).astype(BF16)

    @pl.when(i >= n_prompt_tiles)
    def _():
        xn_scr[...] = _layer_norm(xs_ref[...], g0_ref[...], b0_ref[...]).astype(BF16)

    xn = xn_scr[...]

    def proj(c0, width):
        return jnp.dot(xn, w_ref[:, c0:c0 + width], preferred_element_type=F32)

    for c in range(0, ATTN_WIDTH, COL_CHUNK):
        q_ref[:, c:c + COL_CHUNK] = proj(_C_Q + c, COL_CHUNK).astype(BF16)
    kv = proj(_C_K, 2 * KV_WIDTH)
    k_ref[...] = kv[:, :KV_WIDTH].astype(BF16)
    v_ref[...] = kv[:, KV_WIDTH:].astype(BF16)
    for c in range(0, GMLP_WIDTH, COL_CHUNK):
        gu_ref[:, c:c + COL_CHUNK] = jax.nn.gelu(proj(_C_U + c, COL_CHUNK)).astype(BF16)
    for c in range(0, GMLP_WIDTH, COL_CHUNK):
        vg = jax.nn.gelu(proj(_C_VG + c, COL_CHUNK))
        for j in range(COL_CHUNK // GMLP_GROUP_DIM):
            grp = c // GMLP_GROUP_DIM + j
            blk = vg[:, j * GMLP_GROUP_DIM:(j + 1) * GMLP_GROUP_DIM]
            y = _layer_norm(blk, lg_ref[grp:grp + 1, :], lb_ref[grp:grp + 1, :])
            vgn_ref[:, grp * GMLP_GROUP_DIM:(grp + 1) * GMLP_GROUP_DIM] = y.astype(BF16)
    for c in range(0, D_MODEL, COL_CHUNK):
        sa_ref[:, c:c + COL_CHUNK] = jax.nn.sigmoid(proj(_C_GA + c, COL_CHUNK)).astype(BF16)
    for c in range(0, D_MODEL, COL_CHUNK):
        sb_ref[:, c:c + COL_CHUNK] = jax.nn.sigmoid(proj(_C_GB + c, COL_CHUNK)).astype(BF16)


def _proj_call(xp, xs, g0, b0, w_in, lg, lb):
    n_p, n_s = xp.shape[0] // TM, xs.shape[0] // TM
    t_all = xp.shape[0] + xs.shape[0]
    xp_spec = pl.BlockSpec((TM, D_MODEL), lambda i: (jnp.minimum(i, n_p - 1), 0))
    xs_spec = pl.BlockSpec((TM, D_MODEL), lambda i: (jnp.maximum(i - n_p, 0), 0))

    def tok(width):
        return pl.BlockSpec((TM, width), lambda i: (i, 0))

    widths = (ATTN_WIDTH, KV_WIDTH, KV_WIDTH, GMLP_WIDTH, GMLP_WIDTH, D_MODEL, D_MODEL)
    return pl.pallas_call(
        functools.partial(_proj_kernel, n_p),
        out_shape=[jax.ShapeDtypeStruct((t_all, w), BF16) for w in widths],
        grid=(n_p + n_s,),
        in_specs=[xp_spec, xs_spec, _const_spec((1, D_MODEL)), _const_spec((1, D_MODEL)),
                  _const_spec((D_MODEL, IN_COLS)),
                  _const_spec((GMLP_GROUPS, GMLP_GROUP_DIM)), _const_spec((GMLP_GROUPS, GMLP_GROUP_DIM))],
        out_specs=[tok(w) for w in widths],
        scratch_shapes=[pltpu.VMEM((TM, D_MODEL), BF16)],
        compiler_params=pltpu.CompilerParams(dimension_semantics=("arbitrary",),
                                             vmem_limit_bytes=VMEM_LIMIT),
        name="proj",
    )(xp, xs, g0, b0, w_in, lg, lb)


def _route(logits_t):
    row = lambda j: logits_t[j:j + 1, :]
    gl = [row(j) for j in range(N_EXPERT_GROUPS)]
    gmax, gidx = gl[0], jnp.zeros(gl[0].shape, jnp.int32)
    for j in range(1, N_EXPERT_GROUPS):
        better = gl[j] > gmax
        gmax = jnp.where(better, gl[j], gmax)
        gidx = jnp.where(better, j, gidx)
    gsum = jnp.exp(gl[0] - gmax)
    for j in range(1, N_EXPERT_GROUPS):
        gsum = gsum + jnp.exp(gl[j] - gmax)
    p_group = 1.0 / gsum

    ig = []
    for e in range(EXPERTS_PER_GROUP):
        v = row(N_EXPERT_GROUPS + (N_EXPERT_GROUPS - 1) * EXPERTS_PER_GROUP + e)
        for g in range(N_EXPERT_GROUPS - 2, -1, -1):
            v = jnp.where(gidx == g, row(N_EXPERT_GROUPS + g * EXPERTS_PER_GROUP + e), v)
        ig.append(v)
    v1, i1 = ig[0], jnp.zeros(ig[0].shape, jnp.int32)
    for e in range(1, EXPERTS_PER_GROUP):
        better = ig[e] > v1
        v1 = jnp.where(better, ig[e], v1)
        i1 = jnp.where(better, e, i1)
    v2 = jnp.where(i1 == 0, ig[1], ig[0])
    i2 = jnp.where(i1 == 0, 1, 0).astype(jnp.int32)
    for e in range(1, EXPERTS_PER_GROUP):
        better = jnp.logical_and(i1 != e, ig[e] > v2)
        v2 = jnp.where(better, ig[e], v2)
        i2 = jnp.where(better, e, i2)
    ev = jnp.exp(v2 - v1)
    ssum = 1.0 + ev
    w1 = 1.0 / ssum
    w2 = ev / ssum
    first_lo = i1 < i2
    lo = jnp.minimum(i1, i2)
    hi = jnp.maximum(i1, i2)
    c_lo = jnp.where(first_lo, w1, w2) * p_group
    c_hi = jnp.where(first_lo, w2, w1) * p_group
    pair = jnp.where(lo == 0, hi - 1, jnp.where(lo == 1, hi + 1, N_PAIRS - 1))
    bucket = gidx * N_PAIRS + pair
    return c_lo, c_hi, bucket.astype(F32)


def _mix_kernel(n_prompt_tiles, n_tiles, seq_edges_first, seq_edges_last,
                sink_ref, xp_ref, xs_ref, g0_ref, b0_ref, q_ref,
                kp_ref, km_ref, kn_ref, vp_ref, vm_ref, vn_ref,
                gu_ref, vgn_ref, sa_ref, sb_ref, ws_ref, bs_ref,
                wa_ref, wb_ref, wo_ref, g1_ref, b1_ref, wr_ref, br_ref,
                x1t_ref, rt_ref,
                xn_scr, kf_scr, vf_scr, a_scr, sg_scr, mg_scr, mix_scr):
    i = pl.program_id(0)
    nblk = TM // BLOCK
    cur = 0

    @pl.when(i == 0)
    def _():
        a_scr[1] = jnp.zeros(a_scr.shape[1:], a_scr.dtype)
        sg_scr[1] = jnp.zeros(sg_scr.shape[1:], sg_scr.dtype)
        mix_scr[0] = jnp.zeros(mix_scr.shape[1:], mix_scr.dtype)

    def stage_a():
        tile = jnp.minimum(i, n_tiles - 1)
        kf_scr[0:BLOCK, :] = kp_ref[...]
        kf_scr[BLOCK:BLOCK + TM, :] = km_ref[...]
        kf_scr[BLOCK + TM:, :] = kn_ref[...]
        vf_scr[0:BLOCK, :] = vp_ref[...]
        vf_scr[BLOCK:BLOCK + TM, :] = vm_ref[...]
        vf_scr[BLOCK + TM:, :] = vn_ref[...]
        first_blk = tile * nblk
        last_blk = tile * nblk + nblk - 1
        has_prev = jnp.logical_and(*[first_blk != e for e in seq_edges_first])
        has_next = jnp.logical_and(*[last_blk != e for e in seq_edges_last])
        kj = lax.broadcasted_iota(jnp.int32, (BLOCK, 3 * BLOCK), 1)
        qi = lax.broadcasted_iota(jnp.int32, (BLOCK, 3 * BLOCK), 0)
        dist = jnp.abs(kj - BLOCK - qi)
        in_window = dist <= WINDOW
        dist_f = dist.astype(F32)
        lo_key = jnp.where(has_prev, 0, BLOCK)
        hi_key = jnp.where(has_next, 3 * BLOCK, 2 * BLOCK)
        yield
        for j in range(nblk):
            mask = in_window
            if j == 0:
                mask = jnp.logical_and(mask, kj >= lo_key)
            if j == nblk - 1:
                mask = jnp.logical_and(mask, kj < hi_key)
            r0 = j * BLOCK
            for kvh in range(N_KV_HEADS):
                c0 = kvh * HEAD_DIM
                kb = kf_scr[r0:r0 + 3 * BLOCK, c0:c0 + HEAD_DIM]
                vb = vf_scr[r0:r0 + 3 * BLOCK, c0:c0 + HEAD_DIM]
                heads = [kvh * Q_PER_KV + g for g in range(Q_PER_KV)]
                qs = jnp.concatenate(
                    [q_ref[r0:r0 + BLOCK, h * HEAD_DIM:(h + 1) * HEAD_DIM] for h in heads], axis=0)
                s_all = lax.dot_general(qs, kb, (((1,), (1,)), ((), ())), preferred_element_type=F32)
                for g, h in enumerate(heads):
                    s = s_all[g * BLOCK:(g + 1) * BLOCK, :] * ATTN_SCALE
                    s = jnp.where(mask, s + dist_f * (-_SLOPES[h]), NEG_INF)
                    sink = sink_ref[h]
                    m = jnp.maximum(jnp.max(s, axis=-1, keepdims=True), sink)
                    p = jnp.exp(s - m)
                    denom = jnp.sum(p, axis=-1, keepdims=True) + jnp.exp(sink - m)
                    pn = (p * (1.0 / denom)).astype(BF16)
                    o = jnp.dot(pn, vb, preferred_element_type=F32)
                    a_scr[cur, r0:r0 + BLOCK, h * HEAD_DIM:(h + 1) * HEAD_DIM] = o.astype(BF16)
                    yield
            for grp in range(GMLP_GROUPS):
                c0 = grp * GMLP_GROUP_DIM
                sp = jnp.dot(ws_ref[grp], vgn_ref[r0:r0 + BLOCK, c0:c0 + GMLP_GROUP_DIM],
                             preferred_element_type=F32) + bs_ref[grp]
                u = gu_ref[r0:r0 + BLOCK, c0:c0 + GMLP_GROUP_DIM].astype(F32)
                sg_scr[cur, r0:r0 + BLOCK, c0:c0 + GMLP_GROUP_DIM] = (u * sp).astype(BF16)
            yield

    def stage_b():
        x = jnp.where(i - 1 < n_prompt_tiles, xp_ref[...], xs_ref[...])
        mu0 = jnp.mean(x, axis=-1, keepdims=True)
        xc = x - mu0
        xn_scr[...] = xc
        inv0 = lax.rsqrt(jnp.mean(xc * xc, axis=-1, keepdims=True) + LN_EPS)
        yield
        a = a_scr[1 - cur]
        sg = sg_scr[1 - cur]
        for c in range(0, D_MODEL, PROJ_CHUNK):
            ma = jnp.dot(a, wa_ref[:, c:c + PROJ_CHUNK], preferred_element_type=F32)
            mb = jnp.dot(sg, wb_ref[:, c:c + PROJ_CHUNK], preferred_element_type=F32)
            merged = (sa_ref[:, c:c + PROJ_CHUNK].astype(F32) * ma
                      + sb_ref[:, c:c + PROJ_CHUNK].astype(F32) * mb)
            mg_scr[:, c:c + PROJ_CHUNK] = merged.astype(BF16)
            yield
        mg = mg_scr[...]
        for c in range(0, D_MODEL, PROJ_CHUNK):
            mix = jnp.dot(mg, wo_ref[:, c:c + PROJ_CHUNK], preferred_element_type=F32)
            xn = xn_scr[:, c:c + PROJ_CHUNK] * inv0 * g0_ref[:, c:c + PROJ_CHUNK] + b0_ref[:, c:c + PROJ_CHUNK]
            mix_scr[1 - cur, :, c:c + PROJ_CHUNK] = DEEPNORM_ALPHA * xn + mix
            yield

    def stage_c():
        mu = jnp.mean(mix_scr[cur], axis=-1, keepdims=True)
        yield
        zc = mix_scr[cur] - mu
        inv = lax.rsqrt(jnp.mean(zc * zc, axis=-1, keepdims=True) + LN_EPS)
        yield
        r = None
        for c in range(0, D_MODEL, MIX_CHUNK):
            x1 = ((mix_scr[cur, :, c:c + MIX_CHUNK] - mu) * inv * g1_ref[:, c:c + MIX_CHUNK]
                  + b1_ref[:, c:c + MIX_CHUNK])
            for k in range(MIX_CHUNK // LANES):
                x1t_ref[pl.ds(c // LANES + k, TM, stride=ROW_PITCH), :] = x1[:, k * LANES:(k + 1) * LANES]
            x_hi = x1.astype(BF16)
            x_lo = (x1 - x_hi.astype(F32)).astype(BF16)
            part = (jnp.dot(x_hi, wr_ref[c:c + MIX_CHUNK, :], preferred_element_type=F32)
                    + jnp.dot(x_lo, wr_ref[c:c + MIX_CHUNK, :], preferred_element_type=F32))
            r = part if r is None else r + part
            yield
        logits = r[:, :LANES] + r[:, LANES:] + br_ref[...]
        c_lo, c_hi, bucket = _route(logits.T)
        routed = jnp.concatenate([c_lo, c_hi, bucket, jnp.zeros((5, TM), F32)], axis=0)
        rt_ref[...] = routed
        padded = jnp.concatenate([routed, jnp.zeros((LANES - 8, TM), F32)], axis=0)
        x1t_ref[pl.ds(X_ROWS, TM, stride=ROW_PITCH), :] = padded.T
        for k in range(X_ROWS + 1, ROW_PITCH):
            x1t_ref[pl.ds(k, TM, stride=ROW_PITCH), :] = jnp.zeros((TM, LANES), F32)
        yield

    n_chunks = D_MODEL // MIX_CHUNK
    def emit(parity):
        nonlocal cur
        cur = parity
        _emit_interleaved((stage_b(), [1] + [512] * (2 * D_MODEL // PROJ_CHUNK)),
                          (stage_a(), [100] + ([600] * N_Q_HEADS + [500]) * nblk),
                          (stage_c(), [1000, 1000] + [550] * n_chunks + [1000]))

    for parity in range(2):
        pl.when(i % 2 == parity)(functools.partial(emit, parity))


def _mix_call(sink, xp, xs, g0, b0, q, k, v, gu, vgn, sa, sb, ws, bs, wa, wb, wo, g1, b1, wr, br):
    n_p, n_s = xp.shape[0] // TM, xs.shape[0] // TM
    t_all = xp.shape[0] + xs.shape[0]
    nblk = TM // BLOCK
    blk_p, blk_all = xp.shape[0] // BLOCK, t_all // BLOCK
    n_tiles = n_p + n_s
    front = lambda i: jnp.minimum(i, n_tiles - 1)
    back = lambda i: jnp.clip(i - 1, 0, n_tiles - 1)
    last = lambda i: jnp.maximum(i - 2, 0)
    xp_spec = pl.BlockSpec((TM, D_MODEL), lambda i: (jnp.minimum(back(i), n_p - 1), 0))
    xs_spec = pl.BlockSpec((TM, D_MODEL), lambda i: (jnp.maximum(back(i) - n_p, 0), 0))

    def tok(width, which):
        return pl.BlockSpec((TM, width), lambda i: (which(i), 0))

    prev_spec = pl.BlockSpec((BLOCK, KV_WIDTH), lambda i: (jnp.maximum(front(i) * nblk - 1, 0), 0))
    next_spec = pl.BlockSpec((BLOCK, KV_WIDTH),
                             lambda i: (jnp.minimum((front(i) + 1) * nblk, blk_all - 1), 0))
    kv_spec = tok(KV_WIDTH, front)
    kernel = functools.partial(_mix_kernel, n_p, n_tiles, (0, blk_p), (blk_p - 1, blk_all - 1))
    return pl.pallas_call(
        kernel,
        out_shape=[jax.ShapeDtypeStruct((t_all * ROW_PITCH, LANES), F32), jax.ShapeDtypeStruct((8, t_all), F32)],
        grid=(n_tiles + 2,),
        in_specs=[pl.BlockSpec(memory_space=pltpu.SMEM),
                  xp_spec, xs_spec, _const_spec((1, D_MODEL)), _const_spec((1, D_MODEL)),
                  tok(ATTN_WIDTH, front), prev_spec, kv_spec, next_spec, prev_spec, kv_spec, next_spec,
                  tok(GMLP_WIDTH, front), tok(GMLP_WIDTH, front), tok(D_MODEL, back), tok(D_MODEL, back),
                  _const_spec((GMLP_GROUPS, BLOCK, BLOCK)), _const_spec((GMLP_GROUPS, BLOCK, BLOCK)),
                  _const_spec((ATTN_WIDTH, D_MODEL)), _const_spec((GMLP_WIDTH, D_MODEL)),
                  _const_spec((D_MODEL, D_MODEL)), _const_spec((1, D_MODEL)), _const_spec((1, D_MODEL)),
                  _const_spec((D_MODEL, 2 * LANES)), _const_spec((1, LANES))],
        out_specs=[pl.BlockSpec((TM * ROW_PITCH, LANES), lambda i: (last(i), 0)),
                   pl.BlockSpec((8, TM), lambda i: (0, last(i)))],
        scratch_shapes=[pltpu.VMEM((TM, D_MODEL), F32),
                        pltpu.VMEM((TM + 2 * BLOCK, KV_WIDTH), BF16),
                        pltpu.VMEM((TM + 2 * BLOCK, KV_WIDTH), BF16),
                        pltpu.VMEM((2, TM, ATTN_WIDTH), BF16),
                        pltpu.VMEM((2, TM, GMLP_WIDTH), BF16),
                        pltpu.VMEM((TM, D_MODEL), BF16),
                        pltpu.VMEM((2, TM, D_MODEL), F32)],
        compiler_params=pltpu.CompilerParams(dimension_semantics=("arbitrary",),
                                             vmem_limit_bytes=VMEM_LIMIT),
        name="mix",
    )(sink, xp, xs, g0, b0, q, k, k, k, v, v, v, gu, vgn, sa, sb, ws, bs, wa, wb, wo, g1, b1, wr, br)


def _gather_copy(src_hbm, dst_buf, sem, src_row, dst_row, n_rows):
    return pltpu.make_async_copy(src_hbm.at[pl.ds(src_row, n_rows)], dst_buf.at[pl.ds(dst_row, n_rows)], sem)


def _start_gather(idx_ref, src_hbm, dst_buf, sem, n_rows):
    def issue(r, carry):
        src_row = pl.multiple_of(idx_ref[0, r] * ROW_PITCH, 8)
        dst_row = pl.multiple_of(r * ROW_PITCH, 8)
        _gather_copy(src_hbm, dst_buf, sem, src_row, dst_row, n_rows).start()
        return carry

    lax.fori_loop(0, TM, issue, 0, unroll=8)


def _wait_gather(src_hbm, dst_buf, sem, n_rows):
    _gather_copy(src_hbm, dst_buf, sem, 0, 0, TM * n_rows).wait()


def _idx_specs(n_tiles):
    first = pl.BlockSpec((None, 1, TM), lambda i, *_: (0, 0, 0), memory_space=pltpu.SMEM)
    nxt = pl.BlockSpec((None, 1, TM), lambda i, *_: (jnp.minimum(i + 1, n_tiles - 1), 0, 0),
                       memory_space=pltpu.SMEM)
    return first, nxt


def _unsort_kernel(idx0_ref, idxn_ref, ys_hbm, g2_ref, b2_ref, out_ref, buf, sem):
    i = pl.program_id(0)
    slot = i % 2

    @pl.when(i == 0)
    def _():
        _start_gather(idx0_ref, ys_hbm, buf.at[0], sem.at[0], X_ROWS)

    @pl.when(i + 1 < pl.num_programs(0))
    def _():
        _start_gather(idxn_ref, ys_hbm, buf.at[1 - slot], sem.at[1 - slot], X_ROWS)

    cur = buf.at[slot]
    _wait_gather(ys_hbm, cur, sem.at[slot], X_ROWS)
    for r in range(X_ROWS):
        out_ref[:, r * LANES:(r + 1) * LANES] = _load_token_major(cur, r)
    out_ref[...] = _layer_norm(out_ref[...], g2_ref[...], b2_ref[...])


def _unsort_call(idx, ysorted, g2, b2, name):
    n_tiles = idx.shape[0] // TM
    idx = idx.reshape(n_tiles, 1, TM)
    first, nxt = _idx_specs(n_tiles)
    return pl.pallas_call(
        _unsort_kernel,
        out_shape=jax.ShapeDtypeStruct((n_tiles * TM, D_MODEL), F32),
        grid=(n_tiles,),
        in_specs=[first, nxt, pl.BlockSpec(memory_space=pl.ANY),
                  _const_spec((1, D_MODEL)), _const_spec((1, D_MODEL))],
        out_specs=pl.BlockSpec((TM, D_MODEL), lambda i: (i, 0)),
        scratch_shapes=[pltpu.VMEM((2, TM * ROW_PITCH, LANES), F32), pltpu.SemaphoreType.DMA((2,))],
        compiler_params=pltpu.CompilerParams(dimension_semantics=("arbitrary",)),
        name=name,
    )(idx, idx, ysorted, g2, b2)


def _moe_kernel(elo_ref, ehi_ref, nvalid_ref, idx0_ref, idxn_ref, x1t_hbm,
                wg_lo, wu_lo, wd_lo, wg_hi, wu_hi, wd_hi, out_ref,
                buf, sem, xb_scr):
    del elo_ref, ehi_ref
    i = pl.program_id(0)
    n = pl.num_programs(0)
    slot = i % 2

    @pl.when(jnp.logical_and(i == 0, nvalid_ref[0] > 0))
    def _():
        _start_gather(idx0_ref, x1t_hbm, buf.at[0], sem.at[0], ROW_PITCH)

    @pl.when(jnp.logical_and(i + 1 < n, nvalid_ref[jnp.minimum(i + 1, n - 1)] > 0))
    def _():
        _start_gather(idxn_ref, x1t_hbm, buf.at[1 - slot], sem.at[1 - slot], ROW_PITCH)

    @pl.when(nvalid_ref[i] > 0)
    def _():
        cur = buf.at[slot]
        _wait_gather(x1t_hbm, cur, sem.at[slot], ROW_PITCH)
        for r in range(X_ROWS):
            xb_scr[:, r * LANES:(r + 1) * LANES] = _load_token_major(cur, r).astype(BF16)
        route = _load_token_major(cur, X_ROWS)
        xb = xb_scr[...]

        def hidden(wg, wu, c):
            gate = jnp.dot(xb, wg[...], preferred_element_type=F32)
            up = jnp.dot(xb, wu[...], preferred_element_type=F32)
            return ((jax.nn.silu(gate) * up) * c).astype(BF16)

        h_lo = hidden(wg_lo, wu_lo, route[:, 0:1])
        h_hi = hidden(wg_hi, wu_hi, route[:, 1:2])
        for c in range(0, D_MODEL, COL_CHUNK):
            y = (jnp.dot(h_lo, wd_lo[:, c:c + COL_CHUNK], preferred_element_type=F32)
                 + jnp.dot(h_hi, wd_hi[:, c:c + COL_CHUNK], preferred_element_type=F32))
            for r in range(c // LANES, (c + COL_CHUNK) // LANES):
                out_ref[pl.ds(r, TM, stride=ROW_PITCH), :] = (DEEPNORM_ALPHA * _load_token_major(cur, r)
                                                              + y[:, r * LANES - c:(r + 1) * LANES - c])
        for r in range(X_ROWS, ROW_PITCH):
            out_ref[pl.ds(r, TM, stride=ROW_PITCH), :] = jnp.zeros((TM, LANES), F32)

    @pl.when(nvalid_ref[i] == 0)
    def _():
        out_ref[...] = jnp.zeros(out_ref.shape, out_ref.dtype)


def _moe_call(e_lo, e_hi, nvalid, src, x1t, wg, wu, wd):
    n_tiles = src.shape[0] // TM
    src = src.reshape(n_tiles, 1, TM)
    first, nxt = _idx_specs(n_tiles)

    def w_spec(shape, which):
        def imap(i, elo, ehi, nv):
            return ((elo, ehi)[which][i], 0, 0)
        return pl.BlockSpec((None,) + shape, imap)

    up_shape, down_shape = (D_MODEL, EXPERT_FF), (EXPERT_FF, D_MODEL)
    grid_spec = pltpu.PrefetchScalarGridSpec(
        num_scalar_prefetch=3,
        grid=(n_tiles,),
        in_specs=[first, nxt, pl.BlockSpec(memory_space=pl.ANY),
                  w_spec(up_shape, 0), w_spec(up_shape, 0), w_spec(down_shape, 0),
                  w_spec(up_shape, 1), w_spec(up_shape, 1), w_spec(down_shape, 1)],
        out_specs=pl.BlockSpec((TM * ROW_PITCH, LANES), lambda i, *_: (i, 0)),
        scratch_shapes=[pltpu.VMEM((2, TM * ROW_PITCH, LANES), F32), pltpu.SemaphoreType.DMA((2,)),
                        pltpu.VMEM((TM, D_MODEL), BF16)],
    )
    return pl.pallas_call(
        _moe_kernel,
        out_shape=jax.ShapeDtypeStruct((n_tiles * TM * ROW_PITCH, LANES), F32),
        grid_spec=grid_spec,
        compiler_params=pltpu.CompilerParams(dimension_semantics=("arbitrary",),
                                             vmem_limit_bytes=VMEM_LIMIT),
        name="moe",
    )(e_lo, e_hi, nvalid, src, src, x1t, wg, wu, wd, wg, wu, wd)


_PAIR_LO = np.array([0, 0, 0, 1, 1, 2], np.int32)
_PAIR_HI = np.array([1, 2, 3, 2, 3, 3], np.int32)


def _bucket_layout(bucket, n_slots):
    t_all = bucket.shape[0]
    n_tiles = n_slots // TM
    rows = t_all // LANES
    onehot = (bucket.reshape(rows, LANES, 1) == jnp.arange(N_BUCKETS, dtype=jnp.int32)).astype(F32)
    earlier = (jnp.arange(LANES)[:, None] > jnp.arange(LANES)[None, :]).astype(F32)
    within = jnp.einsum("ts,rsb->rtb", earlier, onehot)
    row_total = jnp.sum(onehot, axis=1)
    row_start = jnp.cumsum(row_total, axis=0) - row_total
    rank = jnp.sum((within + row_start[:, None, :]) * onehot, axis=-1).reshape(t_all).astype(jnp.int32)
    counts = jnp.sum(row_total, axis=0).astype(jnp.int32)
    padded = ((counts + TM - 1) // TM) * TM
    ends = jnp.cumsum(padded)
    starts = ends - padded
    dest = starts[bucket] + rank
    src = jnp.zeros((n_slots,), jnp.int32).at[dest].set(jnp.arange(t_all, dtype=jnp.int32),
                                                       unique_indices=True, mode="promise_in_bounds")
    tile_start = jnp.arange(n_tiles, dtype=jnp.int32) * TM
    last_start = jnp.maximum(ends[-1] - TM, 0)
    tile_bucket = jnp.sum((jnp.minimum(tile_start, last_start)[:, None] >= ends[None, :]).astype(jnp.int32), axis=1)
    tile_bucket = jnp.minimum(tile_bucket, N_BUCKETS - 1)
    nvalid = jnp.clip(counts[tile_bucket] - (tile_start - starts[tile_bucket]), 0, TM)
    nvalid = jnp.where(tile_start < ends[-1], nvalid, 0).astype(jnp.int32)
    group, pair = tile_bucket // N_PAIRS, tile_bucket % N_PAIRS
    e_lo = group * EXPERTS_PER_GROUP + jnp.asarray(_PAIR_LO)[pair]
    e_hi = group * EXPERTS_PER_GROUP + jnp.asarray(_PAIR_HI)[pair]
    return src, dest, e_lo.astype(jnp.int32), e_hi.astype(jnp.int32), nvalid


def _layer(xp, xs, in_ln_g, in_ln_b, w_in, attn_sink, gmlp_w_s, gmlp_b_s, gmlp_ln_g, gmlp_ln_b,
           w_attn_branch, w_gmlp_branch, w_out, ln1_g, ln1_b,
           router_w_group, router_b_group, router_w_expert, router_b_expert,
           w_expert_gate, w_expert_up, w_expert_down, ln2_g, ln2_b):
    t_p, t_s = xp.shape[0], xs.shape[0]
    t_all = t_p + t_s
    row = lambda p: p.reshape(1, -1).astype(F32)

    q, k, v, gu, vgn, sa, sb = _proj_call(xp, xs, row(in_ln_g), row(in_ln_b), w_in.astype(BF16),
                                          gmlp_ln_g.astype(F32), gmlp_ln_b.astype(F32))

    wr = jnp.concatenate([router_w_group, router_w_expert], axis=1).astype(F32)
    wr = jnp.pad(wr, ((0, 0), (0, LANES - wr.shape[1])))
    wr_hi = wr.astype(BF16)
    wr_lo = (wr - wr_hi.astype(F32)).astype(BF16)
    br = jnp.pad(jnp.concatenate([router_b_group, router_b_expert]).astype(F32),
                 (0, LANES - N_EXPERT_GROUPS - N_EXPERTS)).reshape(1, LANES)
    bs = jnp.broadcast_to(gmlp_b_s.astype(F32)[:, :, None], (GMLP_GROUPS, BLOCK, BLOCK))

    x1t, routed = _mix_call(attn_sink.astype(F32), xp, xs, row(in_ln_g), row(in_ln_b),
                            q, k, v, gu, vgn, sa, sb, gmlp_w_s.astype(BF16), bs,
                            w_attn_branch.astype(BF16), w_gmlp_branch.astype(BF16), w_out.astype(BF16),
                            row(ln1_g), row(ln1_b), jnp.concatenate([wr_hi, wr_lo], axis=1), br)

    n_slots = t_all + N_BUCKETS * TM
    bucket = routed[2].astype(jnp.int32)
    src, dest, e_lo, e_hi, nvalid = _bucket_layout(bucket, n_slots)
    zsorted = _moe_call(e_lo, e_hi, nvalid, src, x1t,
                        w_expert_gate.astype(BF16), w_expert_up.astype(BF16), w_expert_down.astype(BF16))
    g2, b2 = row(ln2_g), row(ln2_b)
    return (_unsort_call(dest[:t_p], zsorted, g2, b2, "unsort_prompt"),
            _unsort_call(dest[t_p:], zsorted, g2, b2, "unsort_sample"))


def kernel(x_prompt, x_sample, in_ln_g, in_ln_b, w_in, attn_sink, gmlp_w_s, gmlp_b_s, gmlp_ln_g, gmlp_ln_b,
           w_attn_branch, w_gmlp_branch, w_out, ln1_g, ln1_b,
           router_w_group, router_b_group, router_w_expert, router_b_expert,
           w_expert_gate, w_expert_up, w_expert_down, ln2_g, ln2_b):
    bp, sp, d = x_prompt.shape
    bs, ss, _ = x_sample.shape
    assert bp == 1 and bs == 1 and d == D_MODEL and sp % TM == 0 and ss % TM == 0
    assert w_in.shape[0] == 1, "one layer"
    yp, ys = _layer(x_prompt.reshape(sp, d), x_sample.reshape(ss, d), in_ln_g, in_ln_b, w_in[0], attn_sink[0],
                    gmlp_w_s[0], gmlp_b_s[0], gmlp_ln_g[0], gmlp_ln_b[0],
                    w_attn_branch[0], w_gmlp_branch[0], w_out[0], ln1_g[0], ln1_b[0],
                    router_w_group[0], router_b_group[0], router_w_expert[0], router_b_expert[0],
                    w_expert_gate[0], w_expert_up[0], w_expert_down[0], ln2_g[0], ln2_b[0])
    return yp.reshape(1, sp, d), ys.reshape(1, ss, d)
```

```python
import functools

import numpy as np
import jax
import jax.numpy as jnp
from jax import lax
from jax.experimental import pallas as pl
from jax.experimental.pallas import tpu as pltpu

F32 = jnp.float32
BF16 = jnp.bfloat16

D_MODEL = 2048
HEAD_DIM = 128
N_Q_HEADS = 8
N_KV_HEADS = 2
Q_PER_KV = N_Q_HEADS // N_KV_HEADS
ATTN_WIDTH = N_Q_HEADS * HEAD_DIM
KV_WIDTH = N_KV_HEADS * HEAD_DIM
WINDOW = 128
BLOCK = 128
GMLP_WIDTH = D_MODEL // 2
GMLP_GROUPS = 8
GMLP_GROUP_DIM = GMLP_WIDTH // GMLP_GROUPS
N_EXPERT_GROUPS = 4
EXPERTS_PER_GROUP = 4
N_EXPERTS = N_EXPERT_GROUPS * EXPERTS_PER_GROUP
EXPERT_FF = 512
LN_EPS = 1e-5
DEEPNORM_ALPHA = 2.0 ** 0.25
NEG_INF = -1e9
ATTN_SCALE = HEAD_DIM ** -0.5

_C_Q = 0
_C_K = _C_Q + ATTN_WIDTH
_C_V = _C_K + KV_WIDTH
_C_U = _C_V + KV_WIDTH
_C_VG = _C_U + GMLP_WIDTH
_C_GA = _C_VG + GMLP_WIDTH
_C_GB = _C_GA + D_MODEL
IN_COLS = _C_GB + D_MODEL

LANES = 128
N_PAIRS = 6
N_BUCKETS = N_EXPERT_GROUPS * N_PAIRS
X_ROWS = D_MODEL // LANES
ROW_PITCH = 24
TM = 256
COL_CHUNK = 512
MIX_CHUNK = 256
PROJ_CHUNK = 256
VMEM_LIMIT = 56 * 1024 * 1024

_SLOPES = [float(2.0 ** (-8.0 * (h + 1) / N_Q_HEADS)) for h in range(N_Q_HEADS)]


def _layer_norm(x, g, b):
    mu = jnp.mean(x, axis=-1, keepdims=True)
    xc = x - mu
    var = jnp.mean(xc * xc, axis=-1, keepdims=True)
    return xc * lax.rsqrt(var + LN_EPS) * g + b


def _store_token_major(ref, x, pad_from):
    for r in range(X_ROWS):
        ref[pl.ds(r, TM, stride=ROW_PITCH), :] = x[:, r * LANES:(r + 1) * LANES]
    for r in range(pad_from, ROW_PITCH):
        ref[pl.ds(r, TM, stride=ROW_PITCH), :] = jnp.zeros((TM, LANES), ref.dtype)


def _load_token_major(ref, r):
    return ref[pl.ds(r, TM, stride=ROW_PITCH), :]


def _emit_interleaved(*streams):
    order = []
    for s, (_, costs) in enumerate(streams):
        done = 0.0
        for c in costs:
            order.append(((done + c / 2) / sum(costs), s))
            done += c
    for _, s in sorted(order):
        next(streams[s][0])
    for gen, _ in streams:
        assert next(gen, "done") == "done", "stream has more units than declared"


def _const_spec(shape):
    nd = len(shape)
    return pl.BlockSpec(shape, lambda i, *_: (0,) * nd, pipeline_mode=pl.Buffered(1))


def _proj_kernel(n_prompt_tiles, xp_ref, xs_ref, g0_ref, b0_ref, w_ref, lg_ref, lb_ref,
                 q_ref, k_ref, v_ref, gu_ref, vgn_ref, sa_ref, sb_ref, xn_scr):
    i = pl.program_id(0)

    @pl.when(i < n_prompt_tiles)
    def _():
        xn_scr[...] = _layer_norm(xp_ref[...], g0_ref[...], b0_ref[...]).astype(BF16)

    @pl.when(i >= n_prompt_tiles)
    def _():
        xn_scr[...] = _layer_norm(xs_ref[...], g0_ref[...], b0_ref[...]).astype(BF16)

    xn = xn_scr[...]

    def proj(c0, width):
        return jnp.dot(xn, w_ref[:, c0:c0 + width], preferred_element_type=F32)

    for c in range(0, ATTN_WIDTH, COL_CHUNK):
        q_ref[:, c:c + COL_CHUNK] = proj(_C_Q + c, COL_CHUNK).astype(BF16)
    kv = proj(_C_K, 2 * KV_WIDTH)
    k_ref[...] = kv[:, :KV_WIDTH].astype(BF16)
    v_ref[...] = kv[:, KV_WIDTH:].astype(BF16)
    for c in range(0, GMLP_WIDTH, COL_CHUNK):
        gu_ref[:, c:c + COL_CHUNK] = jax.nn.gelu(proj(_C_U + c, COL_CHUNK)).astype(BF16)
    for c in range(0, GMLP_WIDTH, COL_CHUNK):
        vg = jax.nn.gelu(proj(_C_VG + c, COL_CHUNK))
        for j in range(COL_CHUNK // GMLP_GROUP_DIM):
            grp = c // GMLP_GROUP_DIM + j
            blk = vg[:, j * GMLP_GROUP_DIM:(j + 1) * GMLP_GROUP_DIM]
            y = _layer_norm(blk, lg_ref[grp:grp + 1, :], lb_ref[grp:grp + 1, :])
            vgn_ref[:, grp * GMLP_GROUP_DIM:(grp + 1) * GMLP_GROUP_DIM] = y.astype(BF16)
    for c in range(0, D_MODEL, COL_CHUNK):
        sa_ref[:, c:c + COL_CHUNK] = jax.nn.sigmoid(proj(_C_GA + c, COL_CHUNK)).astype(BF16)
    for c in range(0, D_MODEL, COL_CHUNK):
        sb_ref[:, c:c + COL_CHUNK] = jax.nn.sigmoid(proj(_C_GB + c, COL_CHUNK)).astype(BF16)


def _proj_call(xp, xs, g0, b0, w_in, lg, lb):
    n_p, n_s = xp.shape[0] // TM, xs.shape[0] // TM
    t_all = xp.shape[0] + xs.shape[0]
    xp_spec = pl.BlockSpec((TM, D_MODEL), lambda i: (jnp.minimum(i, n_p - 1), 0))
    xs_spec = pl.BlockSpec((TM, D_MODEL), lambda i: (jnp.maximum(i - n_p, 0), 0))

    def tok(width):
        return pl.BlockSpec((TM, width), lambda i: (i, 0))

    widths = (ATTN_WIDTH, KV_WIDTH, KV_WIDTH, GMLP_WIDTH, GMLP_WIDTH, D_MODEL, D_MODEL)
    return pl.pallas_call(
        functools.partial(_proj_kernel, n_p),
        out_shape=[jax.ShapeDtypeStruct((t_all, w), BF16) for w in widths],
        grid=(n_p + n_s,),
        in_specs=[xp_spec, xs_spec, _const_spec((1, D_MODEL)), _const_spec((1, D_MODEL)),
                  _const_spec((D_MODEL, IN_COLS)),
                  _const_spec((GMLP_GROUPS, GMLP_GROUP_DIM)), _const_spec((GMLP_GROUPS, GMLP_GROUP_DIM))],
        out_specs=[tok(w) for w in widths],
        scratch_shapes=[pltpu.VMEM((TM, D_MODEL), BF16)],
        compiler_params=pltpu.CompilerParams(dimension_semantics=("arbitrary",),
                                             vmem_limit_bytes=VMEM_LIMIT),
        name="proj",
    )(xp, xs, g0, b0, w_in, lg, lb)


def _route(logits_t):
    row = lambda j: logits_t[j:j + 1, :]
    gl = [row(j) for j in range(N_EXPERT_GROUPS)]
    gmax, gidx = gl[0], jnp.zeros(gl[0].shape, jnp.int32)
    for j in range(1, N_EXPERT_GROUPS):
        better = gl[j] > gmax
        gmax = jnp.where(better, gl[j], gmax)
        gidx = jnp.where(better, j, gidx)
    gsum = jnp.exp(gl[0] - gmax)
    for j in range(1, N_EXPERT_GROUPS):
        gsum = gsum + jnp.exp(gl[j] - gmax)
    p_group = 1.0 / gsum

    ig = []
    for e in range(EXPERTS_PER_GROUP):
        v = row(N_EXPERT_GROUPS + (N_EXPERT_GROUPS - 1) * EXPERTS_PER_GROUP + e)
        for g in range(N_EXPERT_GROUPS - 2, -1, -1):
            v = jnp.where(gidx == g, row(N_EXPERT_GROUPS + g * EXPERTS_PER_GROUP + e), v)
        ig.append(v)
    v1, i1 = ig[0], jnp.zeros(ig[0].shape, jnp.int32)
    for e in range(1, EXPERTS_PER_GROUP):
        better = ig[e] > v1
        v1 = jnp.where(better, ig[e], v1)
        i1 = jnp.where(better, e, i1)
    v2 = jnp.where(i1 == 0, ig[1], ig[0])
    i2 = jnp.where(i1 == 0, 1, 0).astype(jnp.int32)
    for e in range(1, EXPERTS_PER_GROUP):
        better = jnp.logical_and(i1 != e, ig[e] > v2)
        v2 = jnp.where(better, ig[e], v2)
        i2 = jnp.where(better, e, i2)
    ev = jnp.exp(v2 - v1)
    ssum = 1.0 + ev
    w1 = 1.0 / ssum
    w2 = ev / ssum
    first_lo = i1 < i2
    lo = jnp.minimum(i1, i2)
    hi = jnp.maximum(i1, i2)
    c_lo = jnp.where(first_lo, w1, w2) * p_group
    c_hi = jnp.where(first_lo, w2, w1) * p_group
    pair = jnp.where(lo == 0, hi - 1, jnp.where(lo == 1, hi + 1, N_PAIRS - 1))
    bucket = gidx * N_PAIRS + pair
    return c_lo, c_hi, bucket.astype(F32)


def _mix_kernel(n_prompt_tiles, n_tiles, seq_edges_first, seq_edges_last,
                sink_ref, xp_ref, xs_ref, g0_ref, b0_ref, q_ref,
                kp_ref, km_ref, kn_ref, vp_ref, vm_ref, vn_ref,
                gu_ref, vgn_ref, sa_ref, sb_ref, ws_ref, bs_ref,
                wa_ref, wb_ref, wo_ref, g1_ref, b1_ref, wr_ref, br_ref,
                x1t_ref, rt_ref,
                xn_scr, kf_scr, vf_scr, a_scr, sg_scr, mg_scr, mix_scr):
    i = pl.program_id(0)
    nblk = TM // BLOCK
    cur = i % 2

    @pl.when(i == 0)
    def _():
        a_scr[1] = jnp.zeros(a_scr.shape[1:], a_scr.dtype)
        sg_scr[1] = jnp.zeros(sg_scr.shape[1:], sg_scr.dtype)
        mix_scr[0] = jnp.zeros(mix_scr.shape[1:], mix_scr.dtype)

    def stage_a():
        tile = jnp.minimum(i, n_tiles - 1)
        kf_scr[0:BLOCK, :] = kp_ref[...]
        kf_scr[BLOCK:BLOCK + TM, :] = km_ref[...]
        kf_scr[BLOCK + TM:, :] = kn_ref[...]
        vf_scr[0:BLOCK, :] = vp_ref[...]
        vf_scr[BLOCK:BLOCK + TM, :] = vm_ref[...]
        vf_scr[BLOCK + TM:, :] = vn_ref[...]
        first_blk = tile * nblk
        last_blk = tile * nblk + nblk - 1
        has_prev = jnp.logical_and(*[first_blk != e for e in seq_edges_first])
        has_next = jnp.logical_and(*[last_blk != e for e in seq_edges_last])
        kj = lax.broadcasted_iota(jnp.int32, (BLOCK, 3 * BLOCK), 1)
        qi = lax.broadcasted_iota(jnp.int32, (BLOCK, 3 * BLOCK), 0)
        dist = jnp.abs(kj - BLOCK - qi)
        in_window = dist <= WINDOW
        dist_f = dist.astype(F32)
        lo_key = jnp.where(has_prev, 0, BLOCK)
        hi_key = jnp.where(has_next, 3 * BLOCK, 2 * BLOCK)
        yield
        for j in range(nblk):
            mask = in_window
            if j == 0:
                mask = jnp.logical_and(mask, kj >= lo_key)
            if j == nblk - 1:
                mask = jnp.logical_and(mask, kj < hi_key)
            r0 = j * BLOCK
            for kvh in range(N_KV_HEADS):
                c0 = kvh * HEAD_DIM
                kb = kf_scr[r0:r0 + 3 * BLOCK, c0:c0 + HEAD_DIM]
                vb = vf_scr[r0:r0 + 3 * BLOCK, c0:c0 + HEAD_DIM]
                heads = [kvh * Q_PER_KV + g for g in range(Q_PER_KV)]
                qs = jnp.concatenate(
                    [q_ref[r0:r0 + BLOCK, h * HEAD_DIM:(h + 1) * HEAD_DIM] for h in heads], axis=0)
                s_all = lax.dot_general(qs, kb, (((1,), (1,)), ((), ())), preferred_element_type=F32)
                for g, h in enumerate(heads):
                    s = s_all[g * BLOCK:(g + 1) * BLOCK, :] * ATTN_SCALE
                    s = jnp.where(mask, s + dist_f * (-_SLOPES[h]), NEG_INF)
                    sink = sink_ref[h]
                    m = jnp.maximum(jnp.max(s, axis=-1, keepdims=True), sink)
                    p = jnp.exp(s - m)
                    denom = jnp.sum(p, axis=-1, keepdims=True) + jnp.exp(sink - m)
                    pn = (p * (1.0 / denom)).astype(BF16)
                    o = jnp.dot(pn, vb, preferred_element_type=F32)
                    a_scr[cur, r0:r0 + BLOCK, h * HEAD_DIM:(h + 1) * HEAD_DIM] = o.astype(BF16)
                    yield
            for grp in range(GMLP_GROUPS):
                c0 = grp * GMLP_GROUP_DIM
                sp = jnp.dot(ws_ref[grp], vgn_ref[r0:r0 + BLOCK, c0:c0 + GMLP_GROUP_DIM],
                             preferred_element_type=F32) + bs_ref[grp]
                u = gu_ref[r0:r0 + BLOCK, c0:c0 + GMLP_GROUP_DIM].astype(F32)
                sg_scr[cur, r0:r0 + BLOCK, c0:c0 + GMLP_GROUP_DIM] = (u * sp).astype(BF16)
            yield

    def stage_b():
        x = jnp.where(i - 1 < n_prompt_tiles, xp_ref[...], xs_ref[...])
        mu0 = jnp.mean(x, axis=-1, keepdims=True)
        xc = x - mu0
        xn_scr[...] = xc
        inv0 = lax.rsqrt(jnp.mean(xc * xc, axis=-1, keepdims=True) + LN_EPS)
        yield
        a = a_scr[1 - cur]
        sg = sg_scr[1 - cur]
        for c in range(0, D_MODEL, PROJ_CHUNK):
            ma = jnp.dot(a, wa_ref[:, c:c + PROJ_CHUNK], preferred_element_type=F32)
            mb = jnp.dot(sg, wb_ref[:, c:c + PROJ_CHUNK], preferred_element_type=F32)
            merged = (sa_ref[:, c:c + PROJ_CHUNK].astype(F32) * ma
                      + sb_ref[:, c:c + PROJ_CHUNK].astype(F32) * mb)
            mg_scr[:, c:c + PROJ_CHUNK] = merged.astype(BF16)
            yield
        mg = mg_scr[...]
        for c in range(0, D_MODEL, PROJ_CHUNK):
            mix = jnp.dot(mg, wo_ref[:, c:c + PROJ_CHUNK], preferred_element_type=F32)
            xn = xn_scr[:, c:c + PROJ_CHUNK] * inv0 * g0_ref[:, c:c + PROJ_CHUNK] + b0_ref[:, c:c + PROJ_CHUNK]
            mix_scr[1 - cur, :, c:c + PROJ_CHUNK] = DEEPNORM_ALPHA * xn + mix
            yield

    def stage_c():
        mu = jnp.mean(mix_scr[cur], axis=-1, keepdims=True)
        yield
        zc = mix_scr[cur] - mu
        inv = lax.rsqrt(jnp.mean(zc * zc, axis=-1, keepdims=True) + LN_EPS)
        yield
        r = None
        for c in range(0, D_MODEL, MIX_CHUNK):
            x1 = ((mix_scr[cur, :, c:c + MIX_CHUNK] - mu) * inv * g1_ref[:, c:c + MIX_CHUNK]
                  + b1_ref[:, c:c + MIX_CHUNK])
            for k in range(MIX_CHUNK // LANES):
                x1t_ref[pl.ds(c // LANES + k, TM, stride=ROW_PITCH), :] = x1[:, k * LANES:(k + 1) * LANES]
            x_hi = x1.astype(BF16)
            x_lo = (x1 - x_hi.astype(F32)).astype(BF16)
            part = (jnp.dot(x_hi, wr_ref[c:c + MIX_CHUNK, :], preferred_element_type=F32)
                    + jnp.dot(x_lo, wr_ref[c:c + MIX_CHUNK, :], preferred_element_type=F32))
            r = part if r is None else r + part
            yield
        logits = r[:, :LANES] + r[:, LANES:] + br_ref[...]
        c_lo, c_hi, bucket = _route(logits.T)
        routed = jnp.concatenate([c_lo, c_hi, bucket, jnp.zeros((5, TM), F32)], axis=0)
        rt_ref[...] = routed
        padded = jnp.concatenate([routed, jnp.zeros((LANES - 8, TM), F32)], axis=0)
        x1t_ref[pl.ds(X_ROWS, TM, stride=ROW_PITCH), :] = padded.T
        for k in range(X_ROWS + 1, ROW_PITCH):
            x1t_ref[pl.ds(k, TM, stride=ROW_PITCH), :] = jnp.zeros((TM, LANES), F32)
        yield

    n_chunks = D_MODEL // MIX_CHUNK
    _emit_interleaved((stage_b(), [1] + [512] * (2 * D_MODEL // PROJ_CHUNK)),
                      (stage_a(), [100] + ([600] * N_Q_HEADS + [500]) * nblk),
                      (stage_c(), [1000, 1000] + [550] * n_chunks + [1000]))


def _mix_call(sink, xp, xs, g0, b0, q, k, v, gu, vgn, sa, sb, ws, bs, wa, wb, wo, g1, b1, wr, br):
    n_p, n_s = xp.shape[0] // TM, xs.shape[0] // TM
    t_all = xp.shape[0] + xs.shape[0]
    nblk = TM // BLOCK
    blk_p, blk_all = xp.shape[0] // BLOCK, t_all // BLOCK
    n_tiles = n_p + n_s
    front = lambda i: jnp.minimum(i, n_tiles - 1)
    back = lambda i: jnp.clip(i - 1, 0, n_tiles - 1)
    last = lambda i: jnp.maximum(i - 2, 0)
    xp_spec = pl.BlockSpec((TM, D_MODEL), lambda i: (jnp.minimum(back(i), n_p - 1), 0))
    xs_spec = pl.BlockSpec((TM, D_MODEL), lambda i: (jnp.maximum(back(i) - n_p, 0), 0))

    def tok(width, which):
        return pl.BlockSpec((TM, width), lambda i: (which(i), 0))

    prev_spec = pl.BlockSpec((BLOCK, KV_WIDTH), lambda i: (jnp.maximum(front(i) * nblk - 1, 0), 0))
    next_spec = pl.BlockSpec((BLOCK, KV_WIDTH),
                             lambda i: (jnp.minimum((front(i) + 1) * nblk, blk_all - 1), 0))
    kv_spec = tok(KV_WIDTH, front)
    kernel = functools.partial(_mix_kernel, n_p, n_tiles, (0, blk_p), (blk_p - 1, blk_all - 1))
    return pl.pallas_call(
        kernel,
        out_shape=[jax.ShapeDtypeStruct((t_all * ROW_PITCH, LANES), F32), jax.ShapeDtypeStruct((8, t_all), F32)],
        grid=(n_tiles + 2,),
        in_specs=[pl.BlockSpec(memory_space=pltpu.SMEM),
                  xp_spec, xs_spec, _const_spec((1, D_MODEL)), _const_spec((1, D_MODEL)),
                  tok(ATTN_WIDTH, front), prev_spec, kv_spec, next_spec, prev_spec, kv_spec, next_spec,
                  tok(GMLP_WIDTH, front), tok(GMLP_WIDTH, front), tok(D_MODEL, back), tok(D_MODEL, back),
                  _const_spec((GMLP_GROUPS, BLOCK, BLOCK)), _const_spec((GMLP_GROUPS, BLOCK, BLOCK)),
                  _const_spec((ATTN_WIDTH, D_MODEL)), _const_spec((GMLP_WIDTH, D_MODEL)),
                  _const_spec((D_MODEL, D_MODEL)), _const_spec((1, D_MODEL)), _const_spec((1, D_MODEL)),
                  _const_spec((D_MODEL, 2 * LANES)), _const_spec((1, LANES))],
        out_specs=[pl.BlockSpec((TM * ROW_PITCH, LANES), lambda i: (last(i), 0)),
                   pl.BlockSpec((8, TM), lambda i: (0, last(i)))],
        scratch_shapes=[pltpu.VMEM((TM, D_MODEL), F32),
                        pltpu.VMEM((TM + 2 * BLOCK, KV_WIDTH), BF16),
                        pltpu.VMEM((TM + 2 * BLOCK, KV_WIDTH), BF16),
                        pltpu.VMEM((2, TM, ATTN_WIDTH), BF16),
                        pltpu.VMEM((2, TM, GMLP_WIDTH), BF16),
                        pltpu.VMEM((TM, D_MODEL), BF16),
                        pltpu.VMEM((2, TM, D_MODEL), F32)],
        compiler_params=pltpu.CompilerParams(dimension_semantics=("arbitrary",),
                                             vmem_limit_bytes=VMEM_LIMIT),
        name="mix",
    )(sink, xp, xs, g0, b0, q, k, k, k, v, v, v, gu, vgn, sa, sb, ws, bs, wa, wb, wo, g1, b1, wr, br)


def _gather_copy(src_hbm, dst_buf, sem, src_row, dst_row, n_rows):
    return pltpu.make_async_copy(src_hbm.at[pl.ds(src_row, n_rows)], dst_buf.at[pl.ds(dst_row, n_rows)], sem)


def _start_gather(idx_ref, src_hbm, dst_buf, sem, n_rows):
    def issue(r, carry):
        src_row = pl.multiple_of(idx_ref[0, r] * ROW_PITCH, 8)
        dst_row = pl.multiple_of(r * ROW_PITCH, 8)
        _gather_copy(src_hbm, dst_buf, sem, src_row, dst_row, n_rows).start()
        return carry

    lax.fori_loop(0, TM, issue, 0, unroll=8)


def _wait_gather(src_hbm, dst_buf, sem, n_rows):
    _gather_copy(src_hbm, dst_buf, sem, 0, 0, TM * n_rows).wait()


def _idx_specs(n_tiles):
    first = pl.BlockSpec((None, 1, TM), lambda i, *_: (0, 0, 0), memory_space=pltpu.SMEM)
    nxt = pl.BlockSpec((None, 1, TM), lambda i, *_: (jnp.minimum(i + 1, n_tiles - 1), 0, 0),
                       memory_space=pltpu.SMEM)
    return first, nxt


def _unsort_kernel(idx0_ref, idxn_ref, ys_hbm, g2_ref, b2_ref, out_ref, buf, sem):
    i = pl.program_id(0)
    slot = i % 2

    @pl.when(i == 0)
    def _():
        _start_gather(idx0_ref, ys_hbm, buf.at[0], sem.at[0], X_ROWS)

    @pl.when(i + 1 < pl.num_programs(0))
    def _():
        _start_gather(idxn_ref, ys_hbm, buf.at[1 - slot], sem.at[1 - slot], X_ROWS)

    cur = buf.at[slot]
    _wait_gather(ys_hbm, cur, sem.at[slot], X_ROWS)
    for r in range(X_ROWS):
        out_ref[:, r * LANES:(r + 1) * LANES] = _load_token_major(cur, r)
    out_ref[...] = _layer_norm(out_ref[...], g2_ref[...], b2_ref[...])


def _unsort_call(idx, ysorted, g2, b2, name):
    n_tiles = idx.shape[0] // TM
    idx = idx.reshape(n_tiles, 1, TM)
    first, nxt = _idx_specs(n_tiles)
    return pl.pallas_call(
        _unsort_kernel,
        out_shape=jax.ShapeDtypeStruct((n_tiles * TM, D_MODEL), F32),
        grid=(n_tiles,),
        in_specs=[first, nxt, pl.BlockSpec(memory_space=pl.ANY),
                  _const_spec((1, D_MODEL)), _const_spec((1, D_MODEL))],
        out_specs=pl.BlockSpec((TM, D_MODEL), lambda i: (i, 0)),
        scratch_shapes=[pltpu.VMEM((2, TM * ROW_PITCH, LANES), F32), pltpu.SemaphoreType.DMA((2,))],
        compiler_params=pltpu.CompilerParams(dimension_semantics=("arbitrary",)),
        name=name,
    )(idx, idx, ysorted, g2, b2)


def _moe_kernel(elo_ref, ehi_ref, nvalid_ref, idx0_ref, idxn_ref, x1t_hbm,
                wg_lo, wu_lo, wd_lo, wg_hi, wu_hi, wd_hi, out_ref,
                buf, sem, xb_scr):
    del elo_ref, ehi_ref
    i = pl.program_id(0)
    n = pl.num_programs(0)
    slot = i % 2

    @pl.when(jnp.logical_and(i == 0, nvalid_ref[0] > 0))
    def _():
        _start_gather(idx0_ref, x1t_hbm, buf.at[0], sem.at[0], ROW_PITCH)

    @pl.when(jnp.logical_and(i + 1 < n, nvalid_ref[jnp.minimum(i + 1, n - 1)] > 0))
    def _():
        _start_gather(idxn_ref, x1t_hbm, buf.at[1 - slot], sem.at[1 - slot], ROW_PITCH)

    @pl.when(nvalid_ref[i] > 0)
    def _():
        cur = buf.at[slot]
        _wait_gather(x1t_hbm, cur, sem.at[slot], ROW_PITCH)
        for r in range(X_ROWS):
            xb_scr[:, r * LANES:(r + 1) * LANES] = _load_token_major(cur, r).astype(BF16)
        route = _load_token_major(cur, X_ROWS)
        xb = xb_scr[...]

        def hidden(wg, wu, c):
            gate = jnp.dot(xb, wg[...], preferred_element_type=F32)
            up = jnp.dot(xb, wu[...], preferred_element_type=F32)
            return ((jax.nn.silu(gate) * up) * c).astype(BF16)

        h_lo = hidden(wg_lo, wu_lo, route[:, 0:1])
        h_hi = hidden(wg_hi, wu_hi, route[:, 1:2])
        for c in range(0, D_MODEL, COL_CHUNK):
            y = (jnp.dot(h_lo, wd_lo[:, c:c + COL_CHUNK], preferred_element_type=F32)
                 + jnp.dot(h_hi, wd_hi[:, c:c + COL_CHUNK], preferred_element_type=F32))
            for r in range(c // LANES, (c + COL_CHUNK) // LANES):
                out_ref[pl.ds(r, TM, stride=ROW_PITCH), :] = (DEEPNORM_ALPHA * _load_token_major(cur, r)
                                                              + y[:, r * LANES - c:(r + 1) * LANES - c])
        for r in range(X_ROWS, ROW_PITCH):
            out_ref[pl.ds(r, TM, stride=ROW_PITCH), :] = jnp.zeros((TM, LANES), F32)

    @pl.when(nvalid_ref[i] == 0)
    def _():
        out_ref[...] = jnp.zeros(out_ref.shape, out_ref.dtype)


def _moe_call(e_lo, e_hi, nvalid, src, x1t, wg, wu, wd):
    n_tiles = src.shape[0] // TM
    src = src.reshape(n_tiles, 1, TM)
    first, nxt = _idx_specs(n_tiles)

    def w_spec(shape, which):
        def imap(i, elo, ehi, nv):
            return ((elo, ehi)[which][i], 0, 0)
        return pl.BlockSpec((None,) + shape, imap)

    up_shape, down_shape = (D_MODEL, EXPERT_FF), (EXPERT_FF, D_MODEL)
    grid_spec = pltpu.PrefetchScalarGridSpec(
        num_scalar_prefetch=3,
        grid=(n_tiles,),
        in_specs=[first, nxt, pl.BlockSpec(memory_space=pl.ANY),
                  w_spec(up_shape, 0), w_spec(up_shape, 0), w_spec(down_shape, 0),
                  w_spec(up_shape, 1), w_spec(up_shape, 1), w_spec(down_shape, 1)],
        out_specs=pl.BlockSpec((TM * ROW_PITCH, LANES), lambda i, *_: (i, 0)),
        scratch_shapes=[pltpu.VMEM((2, TM * ROW_PITCH, LANES), F32), pltpu.SemaphoreType.DMA((2,)),
                        pltpu.VMEM((TM, D_MODEL), BF16)],
    )
    return pl.pallas_call(
        _moe_kernel,
        out_shape=jax.ShapeDtypeStruct((n_tiles * TM * ROW_PITCH, LANES), F32),
        grid_spec=grid_spec,
        compiler_params=pltpu.CompilerParams(dimension_semantics=("arbitrary",),
                                             vmem_limit_bytes=VMEM_LIMIT),
        name="moe",
    )(e_lo, e_hi, nvalid, src, src, x1t, wg, wu, wd, wg, wu, wd)


_PAIR_LO = np.array([0, 0, 0, 1, 1, 2], np.int32)
_PAIR_HI = np.array([1, 2, 3, 2, 3, 3], np.int32)


def _bucket_layout(bucket, n_slots):
    t_all = bucket.shape[0]
    n_tiles = n_slots // TM
    rows = t_all // LANES
    onehot = (bucket.reshape(rows, LANES, 1) == jnp.arange(N_BUCKETS, dtype=jnp.int32)).astype(F32)
    earlier = (jnp.arange(LANES)[:, None] > jnp.arange(LANES)[None, :]).astype(F32)
    within = jnp.einsum("ts,rsb->rtb", earlier, onehot)
    row_total = jnp.sum(onehot, axis=1)
    row_start = jnp.cumsum(row_total, axis=0) - row_total
    counts = jnp.sum(row_total, axis=0).astype(jnp.int32)
    padded = ((counts + TM - 1) // TM) * TM
    ends = jnp.cumsum(padded)
    starts = ends - padded
    slot = within + (row_start + starts.astype(F32))[:, None, :]
    dest = jnp.sum(slot * onehot, axis=-1).reshape(t_all).astype(jnp.int32)
    src = jnp.zeros((n_slots,), jnp.int32).at[dest].set(jnp.arange(t_all, dtype=jnp.int32),
                                                       unique_indices=True, mode="promise_in_bounds")
    tile_start = jnp.arange(n_tiles, dtype=jnp.int32) * TM
    owner_start = jnp.minimum(tile_start, jnp.maximum(ends[-1:] - TM, 0))[:, None]
    owner = jnp.logical_and(owner_start >= starts[None, :], owner_start < ends[None, :]).astype(jnp.int32)
    remaining = counts[None, :] - (tile_start[:, None] - starts[None, :])
    nvalid = jnp.sum(owner * jnp.clip(remaining, 0, TM), axis=1)
    bucket_ids = np.arange(N_BUCKETS)
    first_expert = (bucket_ids // N_PAIRS) * EXPERTS_PER_GROUP
    e_lo = jnp.sum(owner * jnp.asarray(first_expert + _PAIR_LO[bucket_ids % N_PAIRS], jnp.int32)[None, :], axis=1)
    e_hi = jnp.sum(owner * jnp.asarray(first_expert + _PAIR_HI[bucket_ids % N_PAIRS], jnp.int32)[None, :], axis=1)
    return src, dest, e_lo.astype(jnp.int32), e_hi.astype(jnp.int32), nvalid.astype(jnp.int32)


def _layer(xp, xs, in_ln_g, in_ln_b, w_in, attn_sink, gmlp_w_s, gmlp_b_s, gmlp_ln_g, gmlp_ln_b,
           w_attn_branch, w_gmlp_branch, w_out, ln1_g, ln1_b,
           router_w_group, router_b_group, router_w_expert, router_b_expert,
           w_expert_gate, w_expert_up, w_expert_down, ln2_g, ln2_b):
    t_p, t_s = xp.shape[0], xs.shape[0]
    t_all = t_p + t_s
    row = lambda p: p.reshape(1, -1).astype(F32)

    q, k, v, gu, vgn, sa, sb = _proj_call(xp, xs, row(in_ln_g), row(in_ln_b), w_in.astype(BF16),
                                          gmlp_ln_g.astype(F32), gmlp_ln_b.astype(F32))

    wr = jnp.concatenate([router_w_group, router_w_expert], axis=1).astype(F32)
    wr = jnp.pad(wr, ((0, 0), (0, LANES - wr.shape[1])))
    wr_hi = wr.astype(BF16)
    wr_lo = (wr - wr_hi.astype(F32)).astype(BF16)
    br = jnp.pad(jnp.concatenate([router_b_group, router_b_expert]).astype(F32),
                 (0, LANES - N_EXPERT_GROUPS - N_EXPERTS)).reshape(1, LANES)
    bs = jnp.broadcast_to(gmlp_b_s.astype(F32)[:, :, None], (GMLP_GROUPS, BLOCK, BLOCK))

    x1t, routed = _mix_call(attn_sink.astype(F32), xp, xs, row(in_ln_g), row(in_ln_b),
                            q, k, v, gu, vgn, sa, sb, gmlp_w_s.astype(BF16), bs,
                            w_attn_branch.astype(BF16), w_gmlp_branch.astype(BF16), w_out.astype(BF16),
                            row(ln1_g), row(ln1_b), jnp.concatenate([wr_hi, wr_lo], axis=1), br)

    n_slots = t_all + N_BUCKETS * TM
    bucket = routed[2].astype(jnp.int32)
    src, dest, e_lo, e_hi, nvalid = _bucket_layout(bucket, n_slots)
    zsorted = _moe_call(e_lo, e_hi, nvalid, src, x1t,
                        w_expert_gate.astype(BF16), w_expert_up.astype(BF16), w_expert_down.astype(BF16))
    g2, b2 = row(ln2_g), row(ln2_b)
    return (_unsort_call(dest[:t_p], zsorted, g2, b2, "unsort_prompt"),
            _unsort_call(dest[t_p:], zsorted, g2, b2, "unsort_sample"))


def kernel(x_prompt, x_sample, in_ln_g, in_ln_b, w_in, attn_sink, gmlp_w_s, gmlp_b_s, gmlp_ln_g, gmlp_ln_b,
           w_attn_branch, w_gmlp_branch, w_out, ln1_g, ln1_b,
           router_w_group, router_b_group, router_w_expert, router_b_expert,
           w_expert_gate, w_expert_up, w_expert_down, ln2_g, ln2_b):
    bp, sp, d = x_prompt.shape
    bs, ss, _ = x_sample.shape
    assert bp == 1 and bs == 1 and d == D_MODEL and sp % TM == 0 and ss % TM == 0
    assert w_in.shape[0] == 1, "one layer"
    yp, ys = _layer(x_prompt.reshape(sp, d), x_sample.reshape(ss, d), in_ln_g, in_ln_b, w_in[0], attn_sink[0],
                    gmlp_w_s[0], gmlp_b_s[0], gmlp_ln_g[0], gmlp_ln_b[0],
                    w_attn_branch[0], w_gmlp_branch[0], w_out[0], ln1_g[0], ln1_b[0],
                    router_w_group[0], router_b_group[0], router_w_expert[0], router_b_expert[0],
                    w_expert_gate[0], w_expert_up[0], w_expert_down[0], ln2_g[0], ln2_b[0])
    return yp.reshape(1, sp, d), ys.reshape(1, ss, d)
```

```python
import functools

import numpy as np
import jax
import jax.numpy as jnp
from jax import lax
from jax.experimental import pallas as pl
from jax.experimental.pallas import tpu as pltpu

F32 = jnp.float32
BF16 = jnp.bfloat16

D_MODEL = 2048
HEAD_DIM = 128
N_Q_HEADS = 8
N_KV_HEADS = 2
Q_PER_KV = N_Q_HEADS // N_KV_HEADS
ATTN_WIDTH = N_Q_HEADS * HEAD_DIM
KV_WIDTH = N_KV_HEADS * HEAD_DIM
WINDOW = 128
BLOCK = 128
GMLP_WIDTH = D_MODEL // 2
GMLP_GROUPS = 8
GMLP_GROUP_DIM = GMLP_WIDTH // GMLP_GROUPS
N_EXPERT_GROUPS = 4
EXPERTS_PER_GROUP = 4
N_EXPERTS = N_EXPERT_GROUPS * EXPERTS_PER_GROUP
EXPERT_FF = 512
LN_EPS = 1e-5
DEEPNORM_ALPHA = 2.0 ** 0.25
NEG_INF = -1e9
ATTN_SCALE = HEAD_DIM ** -0.5

_C_Q = 0
_C_K = _C_Q + ATTN_WIDTH
_C_V = _C_K + KV_WIDTH
_C_U = _C_V + KV_WIDTH
_C_VG = _C_U + GMLP_WIDTH
_C_GA = _C_VG + GMLP_WIDTH
_C_GB = _C_GA + D_MODEL
IN_COLS = _C_GB + D_MODEL

LANES = 128
N_PAIRS = 6
N_BUCKETS = N_EXPERT_GROUPS * N_PAIRS
X_ROWS = D_MODEL // LANES
ROW_PITCH = 24
TM = 256
COL_CHUNK = 512
MIX_CHUNK = 256
PROJ_CHUNK = 256
VMEM_LIMIT = 56 * 1024 * 1024

_SLOPES = [float(2.0 ** (-8.0 * (h + 1) / N_Q_HEADS)) for h in range(N_Q_HEADS)]


def _layer_norm(x, g, b):
    mu = jnp.mean(x, axis=-1, keepdims=True)
    xc = x - mu
    var = jnp.mean(xc * xc, axis=-1, keepdims=True)
    return xc * lax.rsqrt(var + LN_EPS) * g + b


def _store_token_major(ref, x, pad_from):
    for r in range(X_ROWS):
        ref[pl.ds(r, TM, stride=ROW_PITCH), :] = x[:, r * LANES:(r + 1) * LANES]
    for r in range(pad_from, ROW_PITCH):
        ref[pl.ds(r, TM, stride=ROW_PITCH), :] = jnp.zeros((TM, LANES), ref.dtype)


def _load_token_major(ref, r):
    return ref[pl.ds(r, TM, stride=ROW_PITCH), :]


def _emit_interleaved(*streams):
    order = []
    for s, (_, costs) in enumerate(streams):
        done = 0.0
        for c in costs:
            order.append(((done + c / 2) / sum(costs), s))
            done += c
    for _, s in sorted(order):
        next(streams[s][0])
    for gen, _ in streams:
        assert next(gen, "done") == "done", "stream has more units than declared"


def _const_spec(shape):
    nd = len(shape)
    return pl.BlockSpec(shape, lambda i, *_: (0,) * nd, pipeline_mode=pl.Buffered(1))


def _proj_kernel(n_prompt_tiles, xp_ref, xs_ref, g0_ref, b0_ref, w_ref, lg_ref, lb_ref,
                 xn_ref, q_ref, k_ref, v_ref, gu_ref, vgn_ref, sa_ref, sb_ref, xn_scr):
    i = pl.program_id(0)

    @pl.when(i < n_prompt_tiles)
    def _():
        xn_ref[...] = _layer_norm(xp_ref[...], g0_ref[...], b0_ref[...])

    @pl.when(i >= n_prompt_tiles)
    def _():
        xn_ref[...] = _layer_norm(xs_ref[...], g0_ref[...], b0_ref[...])

    xn_scr[...] = xn_ref[...].astype(BF16)
    xn = xn_scr[...]

    def proj(c0, width):
        return jnp.dot(xn, w_ref[:, c0:c0 + width], preferred_element_type=F32)

    for c in range(0, ATTN_WIDTH, COL_CHUNK):
        q_ref[:, c:c + COL_CHUNK] = proj(_C_Q + c, COL_CHUNK).astype(BF16)
    kv = proj(_C_K, 2 * KV_WIDTH)
    k_ref[...] = kv[:, :KV_WIDTH].astype(BF16)
    v_ref[...] = kv[:, KV_WIDTH:].astype(BF16)
    for c in range(0, GMLP_WIDTH, COL_CHUNK):
        gu_ref[:, c:c + COL_CHUNK] = jax.nn.gelu(proj(_C_U + c, COL_CHUNK)).astype(BF16)
    for c in range(0, GMLP_WIDTH, COL_CHUNK):
        vg = jax.nn.gelu(proj(_C_VG + c, COL_CHUNK))
        for j in range(COL_CHUNK // GMLP_GROUP_DIM):
            grp = c // GMLP_GROUP_DIM + j
            blk = vg[:, j * GMLP_GROUP_DIM:(j + 1) * GMLP_GROUP_DIM]
            y = _layer_norm(blk, lg_ref[grp:grp + 1, :], lb_ref[grp:grp + 1, :])
            vgn_ref[:, grp * GMLP_GROUP_DIM:(grp + 1) * GMLP_GROUP_DIM] = y.astype(BF16)
    for c in range(0, D_MODEL, COL_CHUNK):
        sa_ref[:, c:c + COL_CHUNK] = jax.nn.sigmoid(proj(_C_GA + c, COL_CHUNK)).astype(BF16)
    for c in range(0, D_MODEL, COL_CHUNK):
        sb_ref[:, c:c + COL_CHUNK] = jax.nn.sigmoid(proj(_C_GB + c, COL_CHUNK)).astype(BF16)


def _proj_call(xp, xs, g0, b0, w_in, lg, lb):
    n_p, n_s = xp.shape[0] // TM, xs.shape[0] // TM
    t_all = xp.shape[0] + xs.shape[0]
    xp_spec = pl.BlockSpec((TM, D_MODEL), lambda i: (jnp.minimum(i, n_p - 1), 0))
    xs_spec = pl.BlockSpec((TM, D_MODEL), lambda i: (jnp.maximum(i - n_p, 0), 0))

    def tok(width):
        return pl.BlockSpec((TM, width), lambda i: (i, 0))

    widths = (ATTN_WIDTH, KV_WIDTH, KV_WIDTH, GMLP_WIDTH, GMLP_WIDTH, D_MODEL, D_MODEL)
    return pl.pallas_call(
        functools.partial(_proj_kernel, n_p),
        out_shape=([jax.ShapeDtypeStruct((t_all, D_MODEL), F32)]
                   + [jax.ShapeDtypeStruct((t_all, w), BF16) for w in widths]),
        grid=(n_p + n_s,),
        in_specs=[xp_spec, xs_spec, _const_spec((1, D_MODEL)), _const_spec((1, D_MODEL)),
                  _const_spec((D_MODEL, IN_COLS)),
                  _const_spec((GMLP_GROUPS, GMLP_GROUP_DIM)), _const_spec((GMLP_GROUPS, GMLP_GROUP_DIM))],
        out_specs=[tok(D_MODEL)] + [tok(w) for w in widths],
        scratch_shapes=[pltpu.VMEM((TM, D_MODEL), BF16)],
        compiler_params=pltpu.CompilerParams(dimension_semantics=("arbitrary",),
                                             vmem_limit_bytes=VMEM_LIMIT),
        name="proj",
    )(xp, xs, g0, b0, w_in, lg, lb)


def _route(logits_t):
    row = lambda j: logits_t[j:j + 1, :]
    gl = [row(j) for j in range(N_EXPERT_GROUPS)]
    gmax, gidx = gl[0], jnp.zeros(gl[0].shape, jnp.int32)
    for j in range(1, N_EXPERT_GROUPS):
        better = gl[j] > gmax
        gmax = jnp.where(better, gl[j], gmax)
        gidx = jnp.where(better, j, gidx)
    gsum = jnp.exp(gl[0] - gmax)
    for j in range(1, N_EXPERT_GROUPS):
        gsum = gsum + jnp.exp(gl[j] - gmax)
    p_group = 1.0 / gsum

    ig = []
    for e in range(EXPERTS_PER_GROUP):
        v = row(N_EXPERT_GROUPS + (N_EXPERT_GROUPS - 1) * EXPERTS_PER_GROUP + e)
        for g in range(N_EXPERT_GROUPS - 2, -1, -1):
            v = jnp.where(gidx == g, row(N_EXPERT_GROUPS + g * EXPERTS_PER_GROUP + e), v)
        ig.append(v)
    v1, i1 = ig[0], jnp.zeros(ig[0].shape, jnp.int32)
    for e in range(1, EXPERTS_PER_GROUP):
        better = ig[e] > v1
        v1 = jnp.where(better, ig[e], v1)
        i1 = jnp.where(better, e, i1)
    v2 = jnp.where(i1 == 0, ig[1], ig[0])
    i2 = jnp.where(i1 == 0, 1, 0).astype(jnp.int32)
    for e in range(1, EXPERTS_PER_GROUP):
        better = jnp.logical_and(i1 != e, ig[e] > v2)
        v2 = jnp.where(better, ig[e], v2)
        i2 = jnp.where(better, e, i2)
    ev = jnp.exp(v2 - v1)
    ssum = 1.0 + ev
    w1 = 1.0 / ssum
    w2 = ev / ssum
    first_lo = i1 < i2
    lo = jnp.minimum(i1, i2)
    hi = jnp.maximum(i1, i2)
    c_lo = jnp.where(first_lo, w1, w2) * p_group
    c_hi = jnp.where(first_lo, w2, w1) * p_group
    pair = jnp.where(lo == 0, hi - 1, jnp.where(lo == 1, hi + 1, N_PAIRS - 1))
    bucket = gidx * N_PAIRS + pair
    return c_lo, c_hi, bucket.astype(F32)


def _mix_kernel(n_tiles, seq_edges_first, seq_edges_last,
                sink_ref, xn_ref, q_ref,
                kp_ref, km_ref, kn_ref, vp_ref, vm_ref, vn_ref,
                gu_ref, vgn_ref, sa_ref, sb_ref, ws_ref, bs_ref,
                wa_ref, wb_ref, wo_ref, g1_ref, b1_ref, wr_ref, br_ref,
                x1t_ref, rt_ref,
                kf_scr, vf_scr, a_scr, sg_scr, mg_scr, mix_scr):
    i = pl.program_id(0)
    nblk = TM // BLOCK
    cur = i % 2

    @pl.when(i == 0)
    def _():
        a_scr[1] = jnp.zeros(a_scr.shape[1:], a_scr.dtype)
        sg_scr[1] = jnp.zeros(sg_scr.shape[1:], sg_scr.dtype)
        mix_scr[0] = jnp.zeros(mix_scr.shape[1:], mix_scr.dtype)

    def stage_a():
        tile = jnp.minimum(i, n_tiles - 1)
        kf_scr[0:BLOCK, :] = kp_ref[...]
        kf_scr[BLOCK:BLOCK + TM, :] = km_ref[...]
        kf_scr[BLOCK + TM:, :] = kn_ref[...]
        vf_scr[0:BLOCK, :] = vp_ref[...]
        vf_scr[BLOCK:BLOCK + TM, :] = vm_ref[...]
        vf_scr[BLOCK + TM:, :] = vn_ref[...]
        first_blk = tile * nblk
        last_blk = tile * nblk + nblk - 1
        has_prev = jnp.logical_and(*[first_blk != e for e in seq_edges_first])
        has_next = jnp.logical_and(*[last_blk != e for e in seq_edges_last])
        kj = lax.broadcasted_iota(jnp.int32, (BLOCK, 3 * BLOCK), 1)
        qi = lax.broadcasted_iota(jnp.int32, (BLOCK, 3 * BLOCK), 0)
        dist = jnp.abs(kj - BLOCK - qi)
        in_window = dist <= WINDOW
        dist_f = dist.astype(F32)
        lo_key = jnp.where(has_prev, 0, BLOCK)
        hi_key = jnp.where(has_next, 3 * BLOCK, 2 * BLOCK)
        yield
        for j in range(nblk):
            mask = in_window
            if j == 0:
                mask = jnp.logical_and(mask, kj >= lo_key)
            if j == nblk - 1:
                mask = jnp.logical_and(mask, kj < hi_key)
            r0 = j * BLOCK
            for kvh in range(N_KV_HEADS):
                c0 = kvh * HEAD_DIM
                kb = kf_scr[r0:r0 + 3 * BLOCK, c0:c0 + HEAD_DIM]
                vb = vf_scr[r0:r0 + 3 * BLOCK, c0:c0 + HEAD_DIM]
                heads = [kvh * Q_PER_KV + g for g in range(Q_PER_KV)]
                qs = jnp.concatenate(
                    [q_ref[r0:r0 + BLOCK, h * HEAD_DIM:(h + 1) * HEAD_DIM] for h in heads], axis=0)
                s_all = lax.dot_general(qs, kb, (((1,), (1,)), ((), ())), preferred_element_type=F32)
                for g, h in enumerate(heads):
                    s = s_all[g * BLOCK:(g + 1) * BLOCK, :] * ATTN_SCALE
                    s = jnp.where(mask, s + dist_f * (-_SLOPES[h]), NEG_INF)
                    sink = sink_ref[h]
                    m = jnp.maximum(jnp.max(s, axis=-1, keepdims=True), sink)
                    p = jnp.exp(s - m)
                    denom = jnp.sum(p, axis=-1, keepdims=True) + jnp.exp(sink - m)
                    pn = (p * (1.0 / denom)).astype(BF16)
                    o = jnp.dot(pn, vb, preferred_element_type=F32)
                    a_scr[cur, r0:r0 + BLOCK, h * HEAD_DIM:(h + 1) * HEAD_DIM] = o.astype(BF16)
                    yield
            for grp in range(GMLP_GROUPS):
                c0 = grp * GMLP_GROUP_DIM
                sp = jnp.dot(ws_ref[grp], vgn_ref[r0:r0 + BLOCK, c0:c0 + GMLP_GROUP_DIM],
                             preferred_element_type=F32) + bs_ref[grp]
                u = gu_ref[r0:r0 + BLOCK, c0:c0 + GMLP_GROUP_DIM].astype(F32)
                sg_scr[cur, r0:r0 + BLOCK, c0:c0 + GMLP_GROUP_DIM] = (u * sp).astype(BF16)
            yield

    def stage_b():
        a = a_scr[1 - cur]
        sg = sg_scr[1 - cur]
        for c in range(0, D_MODEL, PROJ_CHUNK):
            ma = jnp.dot(a, wa_ref[:, c:c + PROJ_CHUNK], preferred_element_type=F32)
            mb = jnp.dot(sg, wb_ref[:, c:c + PROJ_CHUNK], preferred_element_type=F32)
            merged = (sa_ref[:, c:c + PROJ_CHUNK].astype(F32) * ma
                      + sb_ref[:, c:c + PROJ_CHUNK].astype(F32) * mb)
            mg_scr[:, c:c + PROJ_CHUNK] = merged.astype(BF16)
            yield
        mg = mg_scr[...]
        for c in range(0, D_MODEL, PROJ_CHUNK):
            mix = jnp.dot(mg, wo_ref[:, c:c + PROJ_CHUNK], preferred_element_type=F32)
            mix_scr[1 - cur, :, c:c + PROJ_CHUNK] = DEEPNORM_ALPHA * xn_ref[:, c:c + PROJ_CHUNK] + mix
            yield

    def stage_c():
        mu = jnp.mean(mix_scr[cur], axis=-1, keepdims=True)
        yield
        zc = mix_scr[cur] - mu
        inv = lax.rsqrt(jnp.mean(zc * zc, axis=-1, keepdims=True) + LN_EPS)
        yield
        r = None
        for c in range(0, D_MODEL, MIX_CHUNK):
            x1 = ((mix_scr[cur, :, c:c + MIX_CHUNK] - mu) * inv * g1_ref[:, c:c + MIX_CHUNK]
                  + b1_ref[:, c:c + MIX_CHUNK])
            for k in range(MIX_CHUNK // LANES):
                x1t_ref[pl.ds(c // LANES + k, TM, stride=ROW_PITCH), :] = x1[:, k * LANES:(k + 1) * LANES]
            x_hi = x1.astype(BF16)
            x_lo = (x1 - x_hi.astype(F32)).astype(BF16)
            part = (jnp.dot(x_hi, wr_ref[c:c + MIX_CHUNK, :], preferred_element_type=F32)
                    + jnp.dot(x_lo, wr_ref[c:c + MIX_CHUNK, :], preferred_element_type=F32))
            r = part if r is None else r + part
            yield
        logits = r[:, :LANES] + r[:, LANES:] + br_ref[...]
        c_lo, c_hi, bucket = _route(logits.T)
        routed = jnp.concatenate([c_lo, c_hi, bucket, jnp.zeros((5, TM), F32)], axis=0)
        rt_ref[...] = routed
        padded = jnp.concatenate([routed, jnp.zeros((LANES - 8, TM), F32)], axis=0)
        x1t_ref[pl.ds(X_ROWS, TM, stride=ROW_PITCH), :] = padded.T
        for k in range(X_ROWS + 1, ROW_PITCH):
            x1t_ref[pl.ds(k, TM, stride=ROW_PITCH), :] = jnp.zeros((TM, LANES), F32)
        yield

    n_chunks = D_MODEL // MIX_CHUNK
    _emit_interleaved((stage_b(), [512] * (2 * D_MODEL // PROJ_CHUNK)),
                      (stage_a(), [100] + ([600] * N_Q_HEADS + [500]) * nblk),
                      (stage_c(), [1000, 1000] + [550] * n_chunks + [1000]))


def _mix_call(t_prompt, sink, xn, q, k, v, gu, vgn, sa, sb, ws, bs, wa, wb, wo, g1, b1, wr, br):
    t_all = xn.shape[0]
    nblk = TM // BLOCK
    blk_p, blk_all = t_prompt // BLOCK, t_all // BLOCK
    n_tiles = t_all // TM
    front = lambda i: jnp.minimum(i, n_tiles - 1)
    back = lambda i: jnp.clip(i - 1, 0, n_tiles - 1)
    last = lambda i: jnp.maximum(i - 2, 0)

    def tok(width, which):
        return pl.BlockSpec((TM, width), lambda i: (which(i), 0))

    prev_spec = pl.BlockSpec((BLOCK, KV_WIDTH), lambda i: (jnp.maximum(front(i) * nblk - 1, 0), 0))
    next_spec = pl.BlockSpec((BLOCK, KV_WIDTH),
                             lambda i: (jnp.minimum((front(i) + 1) * nblk, blk_all - 1), 0))
    kv_spec = tok(KV_WIDTH, front)
    kernel = functools.partial(_mix_kernel, n_tiles, (0, blk_p), (blk_p - 1, blk_all - 1))
    return pl.pallas_call(
        kernel,
        out_shape=[jax.ShapeDtypeStruct((t_all * ROW_PITCH, LANES), F32), jax.ShapeDtypeStruct((8, t_all), F32)],
        grid=(n_tiles + 2,),
        in_specs=[pl.BlockSpec(memory_space=pltpu.SMEM),
                  tok(D_MODEL, back), tok(ATTN_WIDTH, front), prev_spec, kv_spec, next_spec, prev_spec, kv_spec, next_spec,
                  tok(GMLP_WIDTH, front), tok(GMLP_WIDTH, front), tok(D_MODEL, back), tok(D_MODEL, back),
                  _const_spec((GMLP_GROUPS, BLOCK, BLOCK)), _const_spec((GMLP_GROUPS, BLOCK, BLOCK)),
                  _const_spec((ATTN_WIDTH, D_MODEL)), _const_spec((GMLP_WIDTH, D_MODEL)),
                  _const_spec((D_MODEL, D_MODEL)), _const_spec((1, D_MODEL)), _const_spec((1, D_MODEL)),
                  _const_spec((D_MODEL, 2 * LANES)), _const_spec((1, LANES))],
        out_specs=[pl.BlockSpec((TM * ROW_PITCH, LANES), lambda i: (last(i), 0)),
                   pl.BlockSpec((8, TM), lambda i: (0, last(i)))],
        scratch_shapes=[pltpu.VMEM((TM + 2 * BLOCK, KV_WIDTH), BF16),
                        pltpu.VMEM((TM + 2 * BLOCK, KV_WIDTH), BF16),
                        pltpu.VMEM((2, TM, ATTN_WIDTH), BF16),
                        pltpu.VMEM((2, TM, GMLP_WIDTH), BF16),
                        pltpu.VMEM((TM, D_MODEL), BF16),
                        pltpu.VMEM((2, TM, D_MODEL), F32)],
        compiler_params=pltpu.CompilerParams(dimension_semantics=("arbitrary",),
                                             vmem_limit_bytes=VMEM_LIMIT),
        name="mix",
    )(sink, xn, q, k, k, k, v, v, v, gu, vgn, sa, sb, ws, bs, wa, wb, wo, g1, b1, wr, br)


def _gather_copy(src_hbm, dst_buf, sem, src_row, dst_row, n_rows):
    return pltpu.make_async_copy(src_hbm.at[pl.ds(src_row, n_rows)], dst_buf.at[pl.ds(dst_row, n_rows)], sem)


def _start_gather(idx_ref, src_hbm, dst_buf, sem, n_rows):
    def issue(r, carry):
        src_row = pl.multiple_of(idx_ref[0, r] * ROW_PITCH, 8)
        dst_row = pl.multiple_of(r * ROW_PITCH, 8)
        _gather_copy(src_hbm, dst_buf, sem, src_row, dst_row, n_rows).start()
        return carry

    lax.fori_loop(0, TM, issue, 0, unroll=8)


def _wait_gather(src_hbm, dst_buf, sem, n_rows):
    _gather_copy(src_hbm, dst_buf, sem, 0, 0, TM * n_rows).wait()


def _idx_specs(n_tiles):
    first = pl.BlockSpec((None, 1, TM), lambda i, *_: (0, 0, 0), memory_space=pltpu.SMEM)
    nxt = pl.BlockSpec((None, 1, TM), lambda i, *_: (jnp.minimum(i + 1, n_tiles - 1), 0, 0),
                       memory_space=pltpu.SMEM)
    return first, nxt


def _unsort_kernel(idx0_ref, idxn_ref, ys_hbm, g2_ref, b2_ref, out_ref, buf, sem):
    i = pl.program_id(0)
    slot = i % 2

    @pl.when(i == 0)
    def _():
        _start_gather(idx0_ref, ys_hbm, buf.at[0], sem.at[0], X_ROWS)

    @pl.when(i + 1 < pl.num_programs(0))
    def _():
        _start_gather(idxn_ref, ys_hbm, buf.at[1 - slot], sem.at[1 - slot], X_ROWS)

    cur = buf.at[slot]
    _wait_gather(ys_hbm, cur, sem.at[slot], X_ROWS)
    for r in range(X_ROWS):
        out_ref[:, r * LANES:(r + 1) * LANES] = _load_token_major(cur, r)
    out_ref[...] = _layer_norm(out_ref[...], g2_ref[...], b2_ref[...])


def _unsort_call(idx, ysorted, g2, b2, name):
    n_tiles = idx.shape[0] // TM
    idx = idx.reshape(n_tiles, 1, TM)
    first, nxt = _idx_specs(n_tiles)
    return pl.pallas_call(
        _unsort_kernel,
        out_shape=jax.ShapeDtypeStruct((n_tiles * TM, D_MODEL), F32),
        grid=(n_tiles,),
        in_specs=[first, nxt, pl.BlockSpec(memory_space=pl.ANY),
                  _const_spec((1, D_MODEL)), _const_spec((1, D_MODEL))],
        out_specs=pl.BlockSpec((TM, D_MODEL), lambda i: (i, 0)),
        scratch_shapes=[pltpu.VMEM((2, TM * ROW_PITCH, LANES), F32), pltpu.SemaphoreType.DMA((2,))],
        compiler_params=pltpu.CompilerParams(dimension_semantics=("arbitrary",)),
        name=name,
    )(idx, idx, ysorted, g2, b2)


def _moe_kernel(elo_ref, ehi_ref, nvalid_ref, idx0_ref, idxn_ref, x1t_hbm,
                wg_lo, wu_lo, wd_lo, wg_hi, wu_hi, wd_hi, out_ref,
                buf, sem, xb_scr):
    del elo_ref, ehi_ref
    i = pl.program_id(0)
    n = pl.num_programs(0)
    slot = i % 2

    @pl.when(jnp.logical_and(i == 0, nvalid_ref[0] > 0))
    def _():
        _start_gather(idx0_ref, x1t_hbm, buf.at[0], sem.at[0], ROW_PITCH)

    @pl.when(jnp.logical_and(i + 1 < n, nvalid_ref[jnp.minimum(i + 1, n - 1)] > 0))
    def _():
        _start_gather(idxn_ref, x1t_hbm, buf.at[1 - slot], sem.at[1 - slot], ROW_PITCH)

    @pl.when(nvalid_ref[i] > 0)
    def _():
        cur = buf.at[slot]
        _wait_gather(x1t_hbm, cur, sem.at[slot], ROW_PITCH)
        for r in range(X_ROWS):
            xb_scr[:, r * LANES:(r + 1) * LANES] = _load_token_major(cur, r).astype(BF16)
        route = _load_token_major(cur, X_ROWS)
        xb = xb_scr[...]

        def hidden(wg, wu, c):
            gate = jnp.dot(xb, wg[...], preferred_element_type=F32)
            up = jnp.dot(xb, wu[...], preferred_element_type=F32)
            return ((jax.nn.silu(gate) * up) * c).astype(BF16)

        h_lo = hidden(wg_lo, wu_lo, route[:, 0:1])
        h_hi = hidden(wg_hi, wu_hi, route[:, 1:2])
        for c in range(0, D_MODEL, COL_CHUNK):
            y = (jnp.dot(h_lo, wd_lo[:, c:c + COL_CHUNK], preferred_element_type=F32)
                 + jnp.dot(h_hi, wd_hi[:, c:c + COL_CHUNK], preferred_element_type=F32))
            for r in range(c // LANES, (c + COL_CHUNK) // LANES):
                out_ref[pl.ds(r, TM, stride=ROW_PITCH), :] = (DEEPNORM_ALPHA * _load_token_major(cur, r)
                                                              + y[:, r * LANES - c:(r + 1) * LANES - c])
        for r in range(X_ROWS, ROW_PITCH):
            out_ref[pl.ds(r, TM, stride=ROW_PITCH), :] = jnp.zeros((TM, LANES), F32)

    @pl.when(nvalid_ref[i] == 0)
    def _():
        out_ref[...] = jnp.zeros(out_ref.shape, out_ref.dtype)


def _moe_call(e_lo, e_hi, nvalid, src, x1t, wg, wu, wd):
    n_tiles = src.shape[0] // TM
    src = src.reshape(n_tiles, 1, TM)
    first, nxt = _idx_specs(n_tiles)

    def w_spec(shape, which):
        def imap(i, elo, ehi, nv):
            return ((elo, ehi)[which][i], 0, 0)
        return pl.BlockSpec((None,) + shape, imap)

    up_shape, down_shape = (D_MODEL, EXPERT_FF), (EXPERT_FF, D_MODEL)
    grid_spec = pltpu.PrefetchScalarGridSpec(
        num_scalar_prefetch=3,
        grid=(n_tiles,),
        in_specs=[first, nxt, pl.BlockSpec(memory_space=pl.ANY),
                  w_spec(up_shape, 0), w_spec(up_shape, 0), w_spec(down_shape, 0),
                  w_spec(up_shape, 1), w_spec(up_shape, 1), w_spec(down_shape, 1)],
        out_specs=pl.BlockSpec((TM * ROW_PITCH, LANES), lambda i, *_: (i, 0)),
        scratch_shapes=[pltpu.VMEM((2, TM * ROW_PITCH, LANES), F32), pltpu.SemaphoreType.DMA((2,)),
                        pltpu.VMEM((TM, D_MODEL), BF16)],
    )
    return pl.pallas_call(
        _moe_kernel,
        out_shape=jax.ShapeDtypeStruct((n_tiles * TM * ROW_PITCH, LANES), F32),
        grid_spec=grid_spec,
        compiler_params=pltpu.CompilerParams(dimension_semantics=("arbitrary",),
                                             vmem_limit_bytes=VMEM_LIMIT),
        name="moe",
    )(e_lo, e_hi, nvalid, src, src, x1t, wg, wu, wd, wg, wu, wd)


_PAIR_LO = np.array([0, 0, 0, 1, 1, 2], np.int32)
_PAIR_HI = np.array([1, 2, 3, 2, 3, 3], np.int32)


def _bucket_layout(bucket, n_slots):
    t_all = bucket.shape[0]
    n_tiles = n_slots // TM
    rows = t_all // LANES
    onehot = (bucket.reshape(rows, LANES, 1) == jnp.arange(N_BUCKETS, dtype=jnp.int32)).astype(F32)
    earlier = (jnp.arange(LANES)[:, None] > jnp.arange(LANES)[None, :]).astype(F32)
    within = jnp.einsum("ts,rsb->rtb", earlier, onehot)
    row_total = jnp.sum(onehot, axis=1)
    row_start = jnp.cumsum(row_total, axis=0) - row_total
    counts = jnp.sum(row_total, axis=0).astype(jnp.int32)
    padded = ((counts + TM - 1) // TM) * TM
    ends = jnp.cumsum(padded)
    starts = ends - padded
    slot = within + (row_start + starts.astype(F32))[:, None, :]
    dest = jnp.sum(slot * onehot, axis=-1).reshape(t_all).astype(jnp.int32)
    src = jnp.zeros((n_slots,), jnp.int32).at[dest].set(jnp.arange(t_all, dtype=jnp.int32),
                                                       unique_indices=True, mode="promise_in_bounds")
    tile_start = jnp.arange(n_tiles, dtype=jnp.int32) * TM
    owner_start = jnp.minimum(tile_start, jnp.maximum(ends[-1:] - TM, 0))[:, None]
    owner = jnp.logical_and(owner_start >= starts[None, :], owner_start < ends[None, :]).astype(jnp.int32)
    remaining = counts[None, :] - (tile_start[:, None] - starts[None, :])
    nvalid = jnp.sum(owner * jnp.clip(remaining, 0, TM), axis=1)
    bucket_ids = np.arange(N_BUCKETS)
    first_expert = (bucket_ids // N_PAIRS) * EXPERTS_PER_GROUP
    e_lo = jnp.sum(owner * jnp.asarray(first_expert + _PAIR_LO[bucket_ids % N_PAIRS], jnp.int32)[None, :], axis=1)
    e_hi = jnp.sum(owner * jnp.asarray(first_expert + _PAIR_HI[bucket_ids % N_PAIRS], jnp.int32)[None, :], axis=1)
    return src, dest, e_lo.astype(jnp.int32), e_hi.astype(jnp.int32), nvalid.astype(jnp.int32)


def _layer(xp, xs, in_ln_g, in_ln_b, w_in, attn_sink, gmlp_w_s, gmlp_b_s, gmlp_ln_g, gmlp_ln_b,
           w_attn_branch, w_gmlp_branch, w_out, ln1_g, ln1_b,
           router_w_group, router_b_group, router_w_expert, router_b_expert,
           w_expert_gate, w_expert_up, w_expert_down, ln2_g, ln2_b):
    t_p, t_s = xp.shape[0], xs.shape[0]
    t_all = t_p + t_s
    row = lambda p: p.reshape(1, -1).astype(F32)

    xn, q, k, v, gu, vgn, sa, sb = _proj_call(xp, xs, row(in_ln_g), row(in_ln_b), w_in.astype(BF16),
                                              gmlp_ln_g.astype(F32), gmlp_ln_b.astype(F32))

    wr = jnp.concatenate([router_w_group, router_w_expert], axis=1).astype(F32)
    wr = jnp.pad(wr, ((0, 0), (0, LANES - wr.shape[1])))
    wr_hi = wr.astype(BF16)
    wr_lo = (wr - wr_hi.astype(F32)).astype(BF16)
    br = jnp.pad(jnp.concatenate([router_b_group, router_b_expert]).astype(F32),
                 (0, LANES - N_EXPERT_GROUPS - N_EXPERTS)).reshape(1, LANES)
    bs = jnp.broadcast_to(gmlp_b_s.astype(F32)[:, :, None], (GMLP_GROUPS, BLOCK, BLOCK))

    x1t, routed = _mix_call(t_p, attn_sink.astype(F32), xn,
                            q, k, v, gu, vgn, sa, sb, gmlp_w_s.astype(BF16), bs,
                            w_attn_branch.astype(BF16), w_gmlp_branch.astype(BF16), w_out.astype(BF16),
                            row(ln1_g), row(ln1_b), jnp.concatenate([wr_hi, wr_lo], axis=1), br)

    n_slots = t_all + N_BUCKETS * TM
    bucket = routed[2].astype(jnp.int32)
    src, dest, e_lo, e_hi, nvalid = _bucket_layout(bucket, n_slots)
    zsorted = _moe_call(e_lo, e_hi, nvalid, src, x1t,
                        w_expert_gate.astype(BF16), w_expert_up.astype(BF16), w_expert_down.astype(BF16))
    g2, b2 = row(ln2_g), row(ln2_b)
    return (_unsort_call(dest[:t_p], zsorted, g2, b2, "unsort_prompt"),
            _unsort_call(dest[t_p:], zsorted, g2, b2, "unsort_sample"))


def kernel(x_prompt, x_sample, in_ln_g, in_ln_b, w_in, attn_sink, gmlp_w_s, gmlp_b_s, gmlp_ln_g, gmlp_ln_b,
           w_attn_branch, w_gmlp_branch, w_out, ln1_g, ln1_b,
           router_w_group, router_b_group, router_w_expert, router_b_expert,
           w_expert_gate, w_expert_up, w_expert_down, ln2_g, ln2_b):
    bp, sp, d = x_prompt.shape
    bs, ss, _ = x_sample.shape
    assert bp == 1 and bs == 1 and d == D_MODEL and sp % TM == 0 and ss % TM == 0
    assert w_in.shape[0] == 1, "one layer"
    yp, ys = _layer(x_prompt.reshape(sp, d), x_sample.reshape(ss, d), in_ln_g, in_ln_b, w_in[0], attn_sink[0],
                    gmlp_w_s[0], gmlp_b_s[0], gmlp_ln_g[0], gmlp_ln_b[0],
                    w_attn_branch[0], w_gmlp_branch[0], w_out[0], ln1_g[0], ln1_b[0],
                    router_w_group[0], router_b_group[0], router_w_expert[0], router_b_expert[0],
                    w_expert_gate[0], w_expert_up[0], w_expert_down[0], ln2_g[0], ln2_b[0])
    return yp.reshape(1, sp, d), ys.reshape(1, ss, d)
```

```python
import functools

import numpy as np
import jax
import jax.numpy as jnp
from jax import lax
from jax.experimental import pallas as pl
from jax.experimental.pallas import tpu as pltpu

F32 = jnp.float32
BF16 = jnp.bfloat16

D_MODEL = 2048
HEAD_DIM = 128
N_Q_HEADS = 8
N_KV_HEADS = 2
Q_PER_KV = N_Q_HEADS // N_KV_HEADS
ATTN_WIDTH = N_Q_HEADS * HEAD_DIM
KV_WIDTH = N_KV_HEADS * HEAD_DIM
WINDOW = 128
BLOCK = 128
GMLP_WIDTH = D_MODEL // 2
GMLP_GROUPS = 8
GMLP_GROUP_DIM = GMLP_WIDTH // GMLP_GROUPS
N_EXPERT_GROUPS = 4
EXPERTS_PER_GROUP = 4
N_EXPERTS = N_EXPERT_GROUPS * EXPERTS_PER_GROUP
EXPERT_FF = 512
LN_EPS = 1e-5
DEEPNORM_ALPHA = 2.0 ** 0.25
NEG_INF = -1e9
ATTN_SCALE = HEAD_DIM ** -0.5

_C_Q = 0
_C_K = _C_Q + ATTN_WIDTH
_C_V = _C_K + KV_WIDTH
_C_U = _C_V + KV_WIDTH
_C_VG = _C_U + GMLP_WIDTH
_C_GA = _C_VG + GMLP_WIDTH
_C_GB = _C_GA + D_MODEL
IN_COLS = _C_GB + D_MODEL

LANES = 128
N_PAIRS = 6
N_BUCKETS = N_EXPERT_GROUPS * N_PAIRS
X_ROWS = D_MODEL // LANES
ROW_PITCH = 24
TM = 256
COL_CHUNK = 512
MIX_CHUNK = 256
PROJ_CHUNK = 256
MOE_LOOKAHEAD = 2
MOE_BUFFERS = MOE_LOOKAHEAD + 1
VMEM_LIMIT = 56 * 1024 * 1024

_SLOPES = [float(2.0 ** (-8.0 * (h + 1) / N_Q_HEADS)) for h in range(N_Q_HEADS)]


def _layer_norm(x, g, b):
    mu = jnp.mean(x, axis=-1, keepdims=True)
    xc = x - mu
    var = jnp.mean(xc * xc, axis=-1, keepdims=True)
    return xc * lax.rsqrt(var + LN_EPS) * g + b


def _store_token_major(ref, x, pad_from):
    for r in range(X_ROWS):
        ref[pl.ds(r, TM, stride=ROW_PITCH), :] = x[:, r * LANES:(r + 1) * LANES]
    for r in range(pad_from, ROW_PITCH):
        ref[pl.ds(r, TM, stride=ROW_PITCH), :] = jnp.zeros((TM, LANES), ref.dtype)


def _load_token_major(ref, r):
    return ref[pl.ds(r, TM, stride=ROW_PITCH), :]


def _emit_interleaved(*streams):
    order = []
    for s, (_, costs) in enumerate(streams):
        done = 0.0
        for c in costs:
            order.append(((done + c / 2) / sum(costs), s))
            done += c
    for _, s in sorted(order):
        next(streams[s][0])
    for gen, _ in streams:
        assert next(gen, "done") == "done", "stream has more units than declared"


def _const_spec(shape):
    nd = len(shape)
    return pl.BlockSpec(shape, lambda i, *_: (0,) * nd, pipeline_mode=pl.Buffered(1))


def _proj_kernel(n_prompt_tiles, xp_ref, xs_ref, g0_ref, b0_ref, w_ref, lg_ref, lb_ref,
                 xn_ref, q_ref, k_ref, v_ref, gu_ref, vgn_ref, sa_ref, sb_ref, xn_scr):
    i = pl.program_id(0)

    @pl.when(i < n_prompt_tiles)
    def _():
        xn_ref[...] = _layer_norm(xp_ref[...], g0_ref[...], b0_ref[...])

    @pl.when(i >= n_prompt_tiles)
    def _():
        xn_ref[...] = _layer_norm(xs_ref[...], g0_ref[...], b0_ref[...])

    xn_scr[...] = xn_ref[...].astype(BF16)
    xn = xn_scr[...]

    def proj(c0, width):
        return jnp.dot(xn, w_ref[:, c0:c0 + width], preferred_element_type=F32)

    for c in range(0, ATTN_WIDTH, COL_CHUNK):
        q_ref[:, c:c + COL_CHUNK] = proj(_C_Q + c, COL_CHUNK).astype(BF16)
    kv = proj(_C_K, 2 * KV_WIDTH)
    k_ref[...] = kv[:, :KV_WIDTH].astype(BF16)
    v_ref[...] = kv[:, KV_WIDTH:].astype(BF16)
    for c in range(0, GMLP_WIDTH, COL_CHUNK):
        gu_ref[:, c:c + COL_CHUNK] = jax.nn.gelu(proj(_C_U + c, COL_CHUNK)).astype(BF16)
    for c in range(0, GMLP_WIDTH, COL_CHUNK):
        vg = jax.nn.gelu(proj(_C_VG + c, COL_CHUNK))
        for j in range(COL_CHUNK // GMLP_GROUP_DIM):
            grp = c // GMLP_GROUP_DIM + j
            blk = vg[:, j * GMLP_GROUP_DIM:(j + 1) * GMLP_GROUP_DIM]
            y = _layer_norm(blk, lg_ref[grp:grp + 1, :], lb_ref[grp:grp + 1, :])
            vgn_ref[:, grp * GMLP_GROUP_DIM:(grp + 1) * GMLP_GROUP_DIM] = y.astype(BF16)
    for c in range(0, D_MODEL, COL_CHUNK):
        sa_ref[:, c:c + COL_CHUNK] = jax.nn.sigmoid(proj(_C_GA + c, COL_CHUNK)).astype(BF16)
    for c in range(0, D_MODEL, COL_CHUNK):
        sb_ref[:, c:c + COL_CHUNK] = jax.nn.sigmoid(proj(_C_GB + c, COL_CHUNK)).astype(BF16)


def _proj_call(xp, xs, g0, b0, w_in, lg, lb):
    n_p, n_s = xp.shape[0] // TM, xs.shape[0] // TM
    t_all = xp.shape[0] + xs.shape[0]
    xp_spec = pl.BlockSpec((TM, D_MODEL), lambda i: (jnp.minimum(i, n_p - 1), 0))
    xs_spec = pl.BlockSpec((TM, D_MODEL), lambda i: (jnp.maximum(i - n_p, 0), 0))

    def tok(width):
        return pl.BlockSpec((TM, width), lambda i: (i, 0))

    widths = (ATTN_WIDTH, KV_WIDTH, KV_WIDTH, GMLP_WIDTH, GMLP_WIDTH, D_MODEL, D_MODEL)
    return pl.pallas_call(
        functools.partial(_proj_kernel, n_p),
        out_shape=([jax.ShapeDtypeStruct((t_all, D_MODEL), F32)]
                   + [jax.ShapeDtypeStruct((t_all, w), BF16) for w in widths]),
        grid=(n_p + n_s,),
        in_specs=[xp_spec, xs_spec, _const_spec((1, D_MODEL)), _const_spec((1, D_MODEL)),
                  _const_spec((D_MODEL, IN_COLS)),
                  _const_spec((GMLP_GROUPS, GMLP_GROUP_DIM)), _const_spec((GMLP_GROUPS, GMLP_GROUP_DIM))],
        out_specs=[tok(D_MODEL)] + [tok(w) for w in widths],
        scratch_shapes=[pltpu.VMEM((TM, D_MODEL), BF16)],
        compiler_params=pltpu.CompilerParams(dimension_semantics=("arbitrary",),
                                             vmem_limit_bytes=VMEM_LIMIT),
        name="proj",
    )(xp, xs, g0, b0, w_in, lg, lb)


def _route(logits_t):
    row = lambda j: logits_t[j:j + 1, :]
    gl = [row(j) for j in range(N_EXPERT_GROUPS)]
    gmax, gidx = gl[0], jnp.zeros(gl[0].shape, jnp.int32)
    for j in range(1, N_EXPERT_GROUPS):
        better = gl[j] > gmax
        gmax = jnp.where(better, gl[j], gmax)
        gidx = jnp.where(better, j, gidx)
    gsum = jnp.exp(gl[0] - gmax)
    for j in range(1, N_EXPERT_GROUPS):
        gsum = gsum + jnp.exp(gl[j] - gmax)
    p_group = 1.0 / gsum

    ig = []
    for e in range(EXPERTS_PER_GROUP):
        v = row(N_EXPERT_GROUPS + (N_EXPERT_GROUPS - 1) * EXPERTS_PER_GROUP + e)
        for g in range(N_EXPERT_GROUPS - 2, -1, -1):
            v = jnp.where(gidx == g, row(N_EXPERT_GROUPS + g * EXPERTS_PER_GROUP + e), v)
        ig.append(v)
    v1, i1 = ig[0], jnp.zeros(ig[0].shape, jnp.int32)
    for e in range(1, EXPERTS_PER_GROUP):
        better = ig[e] > v1
        v1 = jnp.where(better, ig[e], v1)
        i1 = jnp.where(better, e, i1)
    v2 = jnp.where(i1 == 0, ig[1], ig[0])
    i2 = jnp.where(i1 == 0, 1, 0).astype(jnp.int32)
    for e in range(1, EXPERTS_PER_GROUP):
        better = jnp.logical_and(i1 != e, ig[e] > v2)
        v2 = jnp.where(better, ig[e], v2)
        i2 = jnp.where(better, e, i2)
    ev = jnp.exp(v2 - v1)
    ssum = 1.0 + ev
    w1 = 1.0 / ssum
    w2 = ev / ssum
    first_lo = i1 < i2
    lo = jnp.minimum(i1, i2)
    hi = jnp.maximum(i1, i2)
    c_lo = jnp.where(first_lo, w1, w2) * p_group
    c_hi = jnp.where(first_lo, w2, w1) * p_group
    pair = jnp.where(lo == 0, hi - 1, jnp.where(lo == 1, hi + 1, N_PAIRS - 1))
    bucket = gidx * N_PAIRS + pair
    return c_lo, c_hi, bucket.astype(F32)


def _mix_kernel(n_tiles, seq_edges_first, seq_edges_last,
                sink_ref, xn_ref, q_ref,
                kp_ref, km_ref, kn_ref, vp_ref, vm_ref, vn_ref,
                gu_ref, vgn_ref, sa_ref, sb_ref, ws_ref, bs_ref,
                wa_ref, wb_ref, wo_ref, g1_ref, b1_ref, wr_ref, br_ref,
                x1t_ref, rt_ref,
                kf_scr, vf_scr, a_scr, sg_scr, mg_scr, mix_scr):
    i = pl.program_id(0)
    nblk = TM // BLOCK
    cur = i % 2

    @pl.when(i == 0)
    def _():
        a_scr[1] = jnp.zeros(a_scr.shape[1:], a_scr.dtype)
        sg_scr[1] = jnp.zeros(sg_scr.shape[1:], sg_scr.dtype)
        mix_scr[0] = jnp.zeros(mix_scr.shape[1:], mix_scr.dtype)

    def stage_a():
        tile = jnp.minimum(i, n_tiles - 1)
        kf_scr[0:BLOCK, :] = kp_ref[...]
        kf_scr[BLOCK:BLOCK + TM, :] = km_ref[...]
        kf_scr[BLOCK + TM:, :] = kn_ref[...]
        vf_scr[0:BLOCK, :] = vp_ref[...]
        vf_scr[BLOCK:BLOCK + TM, :] = vm_ref[...]
        vf_scr[BLOCK + TM:, :] = vn_ref[...]
        first_blk = tile * nblk
        last_blk = tile * nblk + nblk - 1
        has_prev = jnp.logical_and(*[first_blk != e for e in seq_edges_first])
        has_next = jnp.logical_and(*[last_blk != e for e in seq_edges_last])
        kj = lax.broadcasted_iota(jnp.int32, (BLOCK, 3 * BLOCK), 1)
        qi = lax.broadcasted_iota(jnp.int32, (BLOCK, 3 * BLOCK), 0)
        dist = jnp.abs(kj - BLOCK - qi)
        in_window = dist <= WINDOW
        dist_f = dist.astype(F32)
        lo_key = jnp.where(has_prev, 0, BLOCK)
        hi_key = jnp.where(has_next, 3 * BLOCK, 2 * BLOCK)
        yield
        for j in range(nblk):
            mask = in_window
            if j == 0:
                mask = jnp.logical_and(mask, kj >= lo_key)
            if j == nblk - 1:
                mask = jnp.logical_and(mask, kj < hi_key)
            r0 = j * BLOCK
            for kvh in range(N_KV_HEADS):
                c0 = kvh * HEAD_DIM
                kb = kf_scr[r0:r0 + 3 * BLOCK, c0:c0 + HEAD_DIM]
                vb = vf_scr[r0:r0 + 3 * BLOCK, c0:c0 + HEAD_DIM]
                heads = [kvh * Q_PER_KV + g for g in range(Q_PER_KV)]
                qs = jnp.concatenate(
                    [q_ref[r0:r0 + BLOCK, h * HEAD_DIM:(h + 1) * HEAD_DIM] for h in heads], axis=0)
                s_all = lax.dot_general(qs, kb, (((1,), (1,)), ((), ())), preferred_element_type=F32)
                for g, h in enumerate(heads):
                    s = s_all[g * BLOCK:(g + 1) * BLOCK, :] * ATTN_SCALE
                    s = jnp.where(mask, s + dist_f * (-_SLOPES[h]), NEG_INF)
                    sink = sink_ref[h]
                    m = jnp.maximum(jnp.max(s, axis=-1, keepdims=True), sink)
                    p = jnp.exp(s - m)
                    denom = jnp.sum(p, axis=-1, keepdims=True) + jnp.exp(sink - m)
                    pn = (p * (1.0 / denom)).astype(BF16)
                    o = jnp.dot(pn, vb, preferred_element_type=F32)
                    a_scr[cur, r0:r0 + BLOCK, h * HEAD_DIM:(h + 1) * HEAD_DIM] = o.astype(BF16)
                    yield
            for grp in range(GMLP_GROUPS):
                c0 = grp * GMLP_GROUP_DIM
                sp = jnp.dot(ws_ref[grp], vgn_ref[r0:r0 + BLOCK, c0:c0 + GMLP_GROUP_DIM],
                             preferred_element_type=F32) + bs_ref[grp]
                u = gu_ref[r0:r0 + BLOCK, c0:c0 + GMLP_GROUP_DIM].astype(F32)
                sg_scr[cur, r0:r0 + BLOCK, c0:c0 + GMLP_GROUP_DIM] = (u * sp).astype(BF16)
            yield

    def stage_b():
        a = a_scr[1 - cur]
        sg = sg_scr[1 - cur]
        for c in range(0, D_MODEL, PROJ_CHUNK):
            ma = jnp.dot(a, wa_ref[:, c:c + PROJ_CHUNK], preferred_element_type=F32)
            mb = jnp.dot(sg, wb_ref[:, c:c + PROJ_CHUNK], preferred_element_type=F32)
            merged = (sa_ref[:, c:c + PROJ_CHUNK].astype(F32) * ma
                      + sb_ref[:, c:c + PROJ_CHUNK].astype(F32) * mb)
            mg_scr[:, c:c + PROJ_CHUNK] = merged.astype(BF16)
            yield
        mg = mg_scr[...]
        for c in range(0, D_MODEL, PROJ_CHUNK):
            mix = jnp.dot(mg, wo_ref[:, c:c + PROJ_CHUNK], preferred_element_type=F32)
            mix_scr[1 - cur, :, c:c + PROJ_CHUNK] = DEEPNORM_ALPHA * xn_ref[:, c:c + PROJ_CHUNK] + mix
            yield

    def stage_c():
        mu = jnp.mean(mix_scr[cur], axis=-1, keepdims=True)
        yield
        zc = mix_scr[cur] - mu
        inv = lax.rsqrt(jnp.mean(zc * zc, axis=-1, keepdims=True) + LN_EPS)
        yield
        r = None
        for c in range(0, D_MODEL, MIX_CHUNK):
            x1 = ((mix_scr[cur, :, c:c + MIX_CHUNK] - mu) * inv * g1_ref[:, c:c + MIX_CHUNK]
                  + b1_ref[:, c:c + MIX_CHUNK])
            for k in range(MIX_CHUNK // LANES):
                x1t_ref[pl.ds(c // LANES + k, TM, stride=ROW_PITCH), :] = x1[:, k * LANES:(k + 1) * LANES]
            x_hi = x1.astype(BF16)
            x_lo = (x1 - x_hi.astype(F32)).astype(BF16)
            part = (jnp.dot(x_hi, wr_ref[c:c + MIX_CHUNK, :], preferred_element_type=F32)
                    + jnp.dot(x_lo, wr_ref[c:c + MIX_CHUNK, :], preferred_element_type=F32))
            r = part if r is None else r + part
            yield
        logits = r[:, :LANES] + r[:, LANES:] + br_ref[...]
        c_lo, c_hi, bucket = _route(logits.T)
        routed = jnp.concatenate([c_lo, c_hi, bucket, jnp.zeros((5, TM), F32)], axis=0)
        rt_ref[...] = routed
        padded = jnp.concatenate([routed, jnp.zeros((LANES - 8, TM), F32)], axis=0)
        x1t_ref[pl.ds(X_ROWS, TM, stride=ROW_PITCH), :] = padded.T
        for k in range(X_ROWS + 1, ROW_PITCH):
            x1t_ref[pl.ds(k, TM, stride=ROW_PITCH), :] = jnp.zeros((TM, LANES), F32)
        yield

    n_chunks = D_MODEL // MIX_CHUNK
    _emit_interleaved((stage_b(), [512] * (2 * D_MODEL // PROJ_CHUNK)),
                      (stage_a(), [100] + ([600] * N_Q_HEADS + [500]) * nblk),
                      (stage_c(), [1000, 1000] + [550] * n_chunks + [1000]))


def _mix_call(t_prompt, sink, xn, q, k, v, gu, vgn, sa, sb, ws, bs, wa, wb, wo, g1, b1, wr, br):
    t_all = xn.shape[0]
    nblk = TM // BLOCK
    blk_p, blk_all = t_prompt // BLOCK, t_all // BLOCK
    n_tiles = t_all // TM
    front = lambda i: jnp.minimum(i, n_tiles - 1)
    back = lambda i: jnp.clip(i - 1, 0, n_tiles - 1)
    last = lambda i: jnp.maximum(i - 2, 0)

    def tok(width, which):
        return pl.BlockSpec((TM, width), lambda i: (which(i), 0))

    prev_spec = pl.BlockSpec((BLOCK, KV_WIDTH), lambda i: (jnp.maximum(front(i) * nblk - 1, 0), 0))
    next_spec = pl.BlockSpec((BLOCK, KV_WIDTH),
                             lambda i: (jnp.minimum((front(i) + 1) * nblk, blk_all - 1), 0))
    kv_spec = tok(KV_WIDTH, front)
    kernel = functools.partial(_mix_kernel, n_tiles, (0, blk_p), (blk_p - 1, blk_all - 1))
    return pl.pallas_call(
        kernel,
        out_shape=[jax.ShapeDtypeStruct((t_all * ROW_PITCH, LANES), F32), jax.ShapeDtypeStruct((8, t_all), F32)],
        grid=(n_tiles + 2,),
        in_specs=[pl.BlockSpec(memory_space=pltpu.SMEM),
                  tok(D_MODEL, back), tok(ATTN_WIDTH, front), prev_spec, kv_spec, next_spec, prev_spec, kv_spec, next_spec,
                  tok(GMLP_WIDTH, front), tok(GMLP_WIDTH, front), tok(D_MODEL, back), tok(D_MODEL, back),
                  _const_spec((GMLP_GROUPS, BLOCK, BLOCK)), _const_spec((GMLP_GROUPS, BLOCK, BLOCK)),
                  _const_spec((ATTN_WIDTH, D_MODEL)), _const_spec((GMLP_WIDTH, D_MODEL)),
                  _const_spec((D_MODEL, D_MODEL)), _const_spec((1, D_MODEL)), _const_spec((1, D_MODEL)),
                  _const_spec((D_MODEL, 2 * LANES)), _const_spec((1, LANES))],
        out_specs=[pl.BlockSpec((TM * ROW_PITCH, LANES), lambda i: (last(i), 0)),
                   pl.BlockSpec((8, TM), lambda i: (0, last(i)))],
        scratch_shapes=[pltpu.VMEM((TM + 2 * BLOCK, KV_WIDTH), BF16),
                        pltpu.VMEM((TM + 2 * BLOCK, KV_WIDTH), BF16),
                        pltpu.VMEM((2, TM, ATTN_WIDTH), BF16),
                        pltpu.VMEM((2, TM, GMLP_WIDTH), BF16),
                        pltpu.VMEM((TM, D_MODEL), BF16),
                        pltpu.VMEM((2, TM, D_MODEL), F32)],
        compiler_params=pltpu.CompilerParams(dimension_semantics=("arbitrary",),
                                             vmem_limit_bytes=VMEM_LIMIT),
        name="mix",
    )(sink, xn, q, k, k, k, v, v, v, gu, vgn, sa, sb, ws, bs, wa, wb, wo, g1, b1, wr, br)


def _gather_copy(src_hbm, dst_buf, sem, src_row, dst_row, n_rows):
    return pltpu.make_async_copy(src_hbm.at[pl.ds(src_row, n_rows)], dst_buf.at[pl.ds(dst_row, n_rows)], sem)


def _start_gather(idx_ref, src_hbm, dst_buf, sem, n_rows):
    def issue(r, carry):
        src_row = pl.multiple_of(idx_ref[0, r] * ROW_PITCH, 8)
        dst_row = pl.multiple_of(r * ROW_PITCH, 8)
        _gather_copy(src_hbm, dst_buf, sem, src_row, dst_row, n_rows).start()
        return carry

    lax.fori_loop(0, TM, issue, 0, unroll=8)


def _wait_gather(src_hbm, dst_buf, sem, n_rows):
    _gather_copy(src_hbm, dst_buf, sem, 0, 0, TM * n_rows).wait()


def _idx_specs(n_tiles, lookahead=1):
    def spec(tile_of_step):
        return pl.BlockSpec((None, 1, TM), lambda i, *_: (jnp.minimum(tile_of_step(i), n_tiles - 1), 0, 0),
                            memory_space=pltpu.SMEM)

    return [spec(lambda i, t=t: t) for t in range(lookahead)] + [spec(lambda i: i + lookahead)]


def _unsort_kernel(idx0_ref, idxn_ref, ys_hbm, g2_ref, b2_ref, out_ref, buf, sem):
    i = pl.program_id(0)
    slot = i % 2

    @pl.when(i == 0)
    def _():
        _start_gather(idx0_ref, ys_hbm, buf.at[0], sem.at[0], X_ROWS)

    @pl.when(i + 1 < pl.num_programs(0))
    def _():
        _start_gather(idxn_ref, ys_hbm, buf.at[1 - slot], sem.at[1 - slot], X_ROWS)

    cur = buf.at[slot]
    _wait_gather(ys_hbm, cur, sem.at[slot], X_ROWS)
    for r in range(X_ROWS):
        out_ref[:, r * LANES:(r + 1) * LANES] = _load_token_major(cur, r)
    out_ref[...] = _layer_norm(out_ref[...], g2_ref[...], b2_ref[...])


def _unsort_call(idx, ysorted, g2, b2, name):
    n_tiles = idx.shape[0] // TM
    idx = idx.reshape(n_tiles, 1, TM)
    first, nxt = _idx_specs(n_tiles)
    return pl.pallas_call(
        _unsort_kernel,
        out_shape=jax.ShapeDtypeStruct((n_tiles * TM, D_MODEL), F32),
        grid=(n_tiles,),
        in_specs=[first, nxt, pl.BlockSpec(memory_space=pl.ANY),
                  _const_spec((1, D_MODEL)), _const_spec((1, D_MODEL))],
        out_specs=pl.BlockSpec((TM, D_MODEL), lambda i: (i, 0)),
        scratch_shapes=[pltpu.VMEM((2, TM * ROW_PITCH, LANES), F32), pltpu.SemaphoreType.DMA((2,))],
        compiler_params=pltpu.CompilerParams(dimension_semantics=("arbitrary",)),
        name=name,
    )(idx, idx, ysorted, g2, b2)


def _moe_kernel(elo_ref, ehi_ref, nvalid_ref, idx0_ref, idx1_ref, idxn_ref, x1t_hbm,
                wg_lo, wu_lo, wd_lo, wg_hi, wu_hi, wd_hi, out_ref,
                buf, sem, xb_scr):
    del elo_ref, ehi_ref
    i = pl.program_id(0)
    n = pl.num_programs(0)
    slot = i % MOE_BUFFERS

    def gather_if_valid(tile, idx_ref, when):
        dst = tile % MOE_BUFFERS

        @pl.when(jnp.logical_and(when, nvalid_ref[jnp.minimum(tile, n - 1)] > 0))
        def _():
            _start_gather(idx_ref, x1t_hbm, buf.at[dst], sem.at[dst], ROW_PITCH)

    assert MOE_LOOKAHEAD == 2
    gather_if_valid(0, idx0_ref, i == 0)
    gather_if_valid(1, idx1_ref, jnp.logical_and(i == 0, n > 1))
    gather_if_valid(i + MOE_LOOKAHEAD, idxn_ref, i + MOE_LOOKAHEAD < n)

    @pl.when(nvalid_ref[i] > 0)
    def _():
        cur = buf.at[slot]
        _wait_gather(x1t_hbm, cur, sem.at[slot], ROW_PITCH)
        for r in range(X_ROWS):
            xb_scr[:, r * LANES:(r + 1) * LANES] = _load_token_major(cur, r).astype(BF16)
        route = _load_token_major(cur, X_ROWS)
        xb = xb_scr[...]

        def hidden(wg, wu, c):
            gate = jnp.dot(xb, wg[...], preferred_element_type=F32)
            up = jnp.dot(xb, wu[...], preferred_element_type=F32)
            return ((jax.nn.silu(gate) * up) * c).astype(BF16)

        h_lo = hidden(wg_lo, wu_lo, route[:, 0:1])
        h_hi = hidden(wg_hi, wu_hi, route[:, 1:2])
        for c in range(0, D_MODEL, COL_CHUNK):
            y = (jnp.dot(h_lo, wd_lo[:, c:c + COL_CHUNK], preferred_element_type=F32)
                 + jnp.dot(h_hi, wd_hi[:, c:c + COL_CHUNK], preferred_element_type=F32))
            for r in range(c // LANES, (c + COL_CHUNK) // LANES):
                out_ref[pl.ds(r, TM, stride=ROW_PITCH), :] = (DEEPNORM_ALPHA * _load_token_major(cur, r)
                                                              + y[:, r * LANES - c:(r + 1) * LANES - c])
        for r in range(X_ROWS, ROW_PITCH):
            out_ref[pl.ds(r, TM, stride=ROW_PITCH), :] = jnp.zeros((TM, LANES), F32)

    @pl.when(nvalid_ref[i] == 0)
    def _():
        out_ref[...] = jnp.zeros(out_ref.shape, out_ref.dtype)


def _moe_call(e_lo, e_hi, nvalid, src, x1t, wg, wu, wd):
    n_tiles = src.shape[0] // TM
    src = src.reshape(n_tiles, 1, TM)
    idx_specs = _idx_specs(n_tiles, MOE_LOOKAHEAD)

    def w_spec(shape, which):
        def imap(i, elo, ehi, nv):
            return ((elo, ehi)[which][i], 0, 0)
        return pl.BlockSpec((None,) + shape, imap)

    up_shape, down_shape = (D_MODEL, EXPERT_FF), (EXPERT_FF, D_MODEL)
    grid_spec = pltpu.PrefetchScalarGridSpec(
        num_scalar_prefetch=3,
        grid=(n_tiles,),
        in_specs=idx_specs + [pl.BlockSpec(memory_space=pl.ANY),
                              w_spec(up_shape, 0), w_spec(up_shape, 0), w_spec(down_shape, 0),
                              w_spec(up_shape, 1), w_spec(up_shape, 1), w_spec(down_shape, 1)],
        out_specs=pl.BlockSpec((TM * ROW_PITCH, LANES), lambda i, *_: (i, 0)),
        scratch_shapes=[pltpu.VMEM((MOE_BUFFERS, TM * ROW_PITCH, LANES), F32),
                        pltpu.SemaphoreType.DMA((MOE_BUFFERS,)),
                        pltpu.VMEM((TM, D_MODEL), BF16)],
    )
    return pl.pallas_call(
        _moe_kernel,
        out_shape=jax.ShapeDtypeStruct((n_tiles * TM * ROW_PITCH, LANES), F32),
        grid_spec=grid_spec,
        compiler_params=pltpu.CompilerParams(dimension_semantics=("arbitrary",),
                                             vmem_limit_bytes=VMEM_LIMIT),
        name="moe",
    )(e_lo, e_hi, nvalid, *([src] * len(idx_specs)), x1t, wg, wu, wd, wg, wu, wd)


_PAIR_LO = np.array([0, 0, 0, 1, 1, 2], np.int32)
_PAIR_HI = np.array([1, 2, 3, 2, 3, 3], np.int32)


def _bucket_layout(bucket, n_slots):
    t_all = bucket.shape[0]
    n_tiles = n_slots // TM
    rows = t_all // LANES
    onehot = (bucket.reshape(rows, LANES, 1) == jnp.arange(N_BUCKETS, dtype=jnp.int32)).astype(F32)
    earlier = (jnp.arange(LANES)[:, None] > jnp.arange(LANES)[None, :]).astype(F32)
    within = jnp.einsum("ts,rsb->rtb", earlier, onehot)
    row_total = jnp.sum(onehot, axis=1)
    row_start = jnp.cumsum(row_total, axis=0) - row_total
    counts = jnp.sum(row_total, axis=0).astype(jnp.int32)
    padded = ((counts + TM - 1) // TM) * TM
    ends = jnp.cumsum(padded)
    starts = ends - padded
    slot = within + (row_start + starts.astype(F32))[:, None, :]
    dest = jnp.sum(slot * onehot, axis=-1).reshape(t_all).astype(jnp.int32)
    src = jnp.zeros((n_slots,), jnp.int32).at[dest].set(jnp.arange(t_all, dtype=jnp.int32),
                                                       unique_indices=True, mode="promise_in_bounds")
    tile_start = jnp.arange(n_tiles, dtype=jnp.int32) * TM
    owner_start = jnp.minimum(tile_start, jnp.maximum(ends[-1:] - TM, 0))[:, None]
    owner = jnp.logical_and(owner_start >= starts[None, :], owner_start < ends[None, :]).astype(jnp.int32)
    remaining = counts[None, :] - (tile_start[:, None] - starts[None, :])
    nvalid = jnp.sum(owner * jnp.clip(remaining, 0, TM), axis=1)
    bucket_ids = np.arange(N_BUCKETS)
    first_expert = (bucket_ids // N_PAIRS) * EXPERTS_PER_GROUP
    e_lo = jnp.sum(owner * jnp.asarray(first_expert + _PAIR_LO[bucket_ids % N_PAIRS], jnp.int32)[None, :], axis=1)
    e_hi = jnp.sum(owner * jnp.asarray(first_expert + _PAIR_HI[bucket_ids % N_PAIRS], jnp.int32)[None, :], axis=1)
    return src, dest, e_lo.astype(jnp.int32), e_hi.astype(jnp.int32), nvalid.astype(jnp.int32)


def _layer(xp, xs, in_ln_g, in_ln_b, w_in, attn_sink, gmlp_w_s, gmlp_b_s, gmlp_ln_g, gmlp_ln_b,
           w_attn_branch, w_gmlp_branch, w_out, ln1_g, ln1_b,
           router_w_group, router_b_group, router_w_expert, router_b_expert,
           w_expert_gate, w_expert_up, w_expert_down, ln2_g, ln2_b):
    t_p, t_s = xp.shape[0], xs.shape[0]
    t_all = t_p + t_s
    row = lambda p: p.reshape(1, -1).astype(F32)

    xn, q, k, v, gu, vgn, sa, sb = _proj_call(xp, xs, row(in_ln_g), row(in_ln_b), w_in.astype(BF16),
                                              gmlp_ln_g.astype(F32), gmlp_ln_b.astype(F32))

    wr = jnp.concatenate([router_w_group, router_w_expert], axis=1).astype(F32)
    wr = jnp.pad(wr, ((0, 0), (0, LANES - wr.shape[1])))
    wr_hi = wr.astype(BF16)
    wr_lo = (wr - wr_hi.astype(F32)).astype(BF16)
    br = jnp.pad(jnp.concatenate([router_b_group, router_b_expert]).astype(F32),
                 (0, LANES - N_EXPERT_GROUPS - N_EXPERTS)).reshape(1, LANES)
    bs = jnp.broadcast_to(gmlp_b_s.astype(F32)[:, :, None], (GMLP_GROUPS, BLOCK, BLOCK))

    x1t, routed = _mix_call(t_p, attn_sink.astype(F32), xn,
                            q, k, v, gu, vgn, sa, sb, gmlp_w_s.astype(BF16), bs,
                            w_attn_branch.astype(BF16), w_gmlp_branch.astype(BF16), w_out.astype(BF16),
                            row(ln1_g), row(ln1_b), jnp.concatenate([wr_hi, wr_lo], axis=1), br)

    n_slots = t_all + N_BUCKETS * TM
    bucket = routed[2].astype(jnp.int32)
    src, dest, e_lo, e_hi, nvalid = _bucket_layout(bucket, n_slots)
    zsorted = _moe_call(e_lo, e_hi, nvalid, src, x1t,
                        w_expert_gate.astype(BF16), w_expert_up.astype(BF16), w_expert_down.astype(BF16))
    g2, b2 = row(ln2_g), row(ln2_b)
    return (_unsort_call(dest[:t_p], zsorted, g2, b2, "unsort_prompt"),
            _unsort_call(dest[t_p:], zsorted, g2, b2, "unsort_sample"))


def kernel(x_prompt, x_sample, in_ln_g, in_ln_b, w_in, attn_sink, gmlp_w_s, gmlp_b_s, gmlp_ln_g, gmlp_ln_b,
           w_attn_branch, w_gmlp_branch, w_out, ln1_g, ln1_b,
           router_w_group, router_b_group, router_w_expert, router_b_expert,
           w_expert_gate, w_expert_up, w_expert_down, ln2_g, ln2_b):
    bp, sp, d = x_prompt.shape
    bs, ss, _ = x_sample.shape
    assert bp == 1 and bs == 1 and d == D_MODEL and sp % TM == 0 and ss % TM == 0
    assert w_in.shape[0] == 1, "one layer"
    yp, ys = _layer(x_prompt.reshape(sp, d), x_sample.reshape(ss, d), in_ln_g, in_ln_b, w_in[0], attn_sink[0],
                    gmlp_w_s[0], gmlp_b_s[0], gmlp_ln_g[0], gmlp_ln_b[0],
                    w_attn_branch[0], w_gmlp_branch[0], w_out[0], ln1_g[0], ln1_b[0],
                    router_w_group[0], router_b_group[0], router_w_expert[0], router_b_expert[0],
                    w_expert_gate[0], w_expert_up[0], w_expert_down[0], ln2_g[0], ln2_b[0])
    return yp.reshape(1, sp, d), ys.reshape(1, ss, d)
```

```python
import functools

import numpy as np
import jax
import jax.numpy as jnp
from jax import lax
from jax.experimental import pallas as pl
from jax.experimental.pallas import tpu as pltpu

F32 = jnp.float32
BF16 = jnp.bfloat16

D_MODEL = 2048
HEAD_DIM = 128
N_Q_HEADS = 8
N_KV_HEADS = 2
Q_PER_KV = N_Q_HEADS // N_KV_HEADS
ATTN_WIDTH = N_Q_HEADS * HEAD_DIM
KV_WIDTH = N_KV_HEADS * HEAD_DIM
WINDOW = 128
BLOCK = 128
GMLP_WIDTH = D_MODEL // 2
GMLP_GROUPS = 8
GMLP_GROUP_DIM = GMLP_WIDTH // GMLP_GROUPS
N_EXPERT_GROUPS = 4
EXPERTS_PER_GROUP = 4
N_EXPERTS = N_EXPERT_GROUPS * EXPERTS_PER_GROUP
EXPERT_FF = 512
LN_EPS = 1e-5
DEEPNORM_ALPHA = 2.0 ** 0.25
NEG_INF = -1e9
ATTN_SCALE = HEAD_DIM ** -0.5

_C_Q = 0
_C_K = _C_Q + ATTN_WIDTH
_C_V = _C_K + KV_WIDTH
_C_U = _C_V + KV_WIDTH
_C_VG = _C_U + GMLP_WIDTH
_C_GA = _C_VG + GMLP_WIDTH
_C_GB = _C_GA + D_MODEL
IN_COLS = _C_GB + D_MODEL

LANES = 128
N_PAIRS = 6
N_BUCKETS = N_EXPERT_GROUPS * N_PAIRS
X_ROWS = D_MODEL // LANES
ROW_PITCH = 24
TM = 256
COL_CHUNK = 512
MIX_CHUNK = 256
PROJ_CHUNK = 256
DMA_PRIORITIES = (0, 1)
MOE_LOOKAHEAD = 2
MOE_BUFFERS = MOE_LOOKAHEAD + 1
VMEM_LIMIT = 56 * 1024 * 1024

_SLOPES = [float(2.0 ** (-8.0 * (h + 1) / N_Q_HEADS)) for h in range(N_Q_HEADS)]


def _layer_norm(x, g, b):
    mu = jnp.mean(x, axis=-1, keepdims=True)
    xc = x - mu
    var = jnp.mean(xc * xc, axis=-1, keepdims=True)
    return xc * lax.rsqrt(var + LN_EPS) * g + b


def _store_token_major(ref, x, pad_from):
    for r in range(X_ROWS):
        ref[pl.ds(r, TM, stride=ROW_PITCH), :] = x[:, r * LANES:(r + 1) * LANES]
    for r in range(pad_from, ROW_PITCH):
        ref[pl.ds(r, TM, stride=ROW_PITCH), :] = jnp.zeros((TM, LANES), ref.dtype)


def _load_token_major(ref, r):
    return ref[pl.ds(r, TM, stride=ROW_PITCH), :]


def _emit_interleaved(*streams):
    order = []
    for s, (_, costs) in enumerate(streams):
        done = 0.0
        for c in costs:
            order.append(((done + c / 2) / sum(costs), s))
            done += c
    for _, s in sorted(order):
        next(streams[s][0])
    for gen, _ in streams:
        assert next(gen, "done") == "done", "stream has more units than declared"


def _const_spec(shape):
    nd = len(shape)
    return pl.BlockSpec(shape, lambda i, *_: (0,) * nd, pipeline_mode=pl.Buffered(1))


def _proj_kernel(n_prompt_tiles, xp_ref, xs_ref, g0_ref, b0_ref, w_ref, lg_ref, lb_ref,
                 xn_ref, q_ref, k_ref, v_ref, gu_ref, vgn_ref, sa_ref, sb_ref, xn_scr):
    i = pl.program_id(0)

    @pl.when(i < n_prompt_tiles)
    def _():
        xn_ref[...] = _layer_norm(xp_ref[...], g0_ref[...], b0_ref[...])

    @pl.when(i >= n_prompt_tiles)
    def _():
        xn_ref[...] = _layer_norm(xs_ref[...], g0_ref[...], b0_ref[...])

    xn_scr[...] = xn_ref[...].astype(BF16)
    xn = xn_scr[...]

    def proj(c0, width):
        return jnp.dot(xn, w_ref[:, c0:c0 + width], preferred_element_type=F32)

    for c in range(0, ATTN_WIDTH, COL_CHUNK):
        q_ref[:, c:c + COL_CHUNK] = proj(_C_Q + c, COL_CHUNK).astype(BF16)
    kv = proj(_C_K, 2 * KV_WIDTH)
    k_ref[...] = kv[:, :KV_WIDTH].astype(BF16)
    v_ref[...] = kv[:, KV_WIDTH:].astype(BF16)
    for c in range(0, GMLP_WIDTH, COL_CHUNK):
        gu_ref[:, c:c + COL_CHUNK] = jax.nn.gelu(proj(_C_U + c, COL_CHUNK)).astype(BF16)
    for c in range(0, GMLP_WIDTH, COL_CHUNK):
        vg = jax.nn.gelu(proj(_C_VG + c, COL_CHUNK))
        for j in range(COL_CHUNK // GMLP_GROUP_DIM):
            grp = c // GMLP_GROUP_DIM + j
            blk = vg[:, j * GMLP_GROUP_DIM:(j + 1) * GMLP_GROUP_DIM]
            y = _layer_norm(blk, lg_ref[grp:grp + 1, :], lb_ref[grp:grp + 1, :])
            vgn_ref[:, grp * GMLP_GROUP_DIM:(grp + 1) * GMLP_GROUP_DIM] = y.astype(BF16)
    for c in range(0, D_MODEL, COL_CHUNK):
        sa_ref[:, c:c + COL_CHUNK] = jax.nn.sigmoid(proj(_C_GA + c, COL_CHUNK)).astype(BF16)
    for c in range(0, D_MODEL, COL_CHUNK):
        sb_ref[:, c:c + COL_CHUNK] = jax.nn.sigmoid(proj(_C_GB + c, COL_CHUNK)).astype(BF16)


def _proj_call(xp, xs, g0, b0, w_in, lg, lb):
    n_p, n_s = xp.shape[0] // TM, xs.shape[0] // TM
    t_all = xp.shape[0] + xs.shape[0]
    xp_spec = pl.BlockSpec((TM, D_MODEL), lambda i: (jnp.minimum(i, n_p - 1), 0))
    xs_spec = pl.BlockSpec((TM, D_MODEL), lambda i: (jnp.maximum(i - n_p, 0), 0))

    def tok(width):
        return pl.BlockSpec((TM, width), lambda i: (i, 0))

    widths = (ATTN_WIDTH, KV_WIDTH, KV_WIDTH, GMLP_WIDTH, GMLP_WIDTH, D_MODEL, D_MODEL)
    return pl.pallas_call(
        functools.partial(_proj_kernel, n_p),
        out_shape=([jax.ShapeDtypeStruct((t_all, D_MODEL), F32)]
                   + [jax.ShapeDtypeStruct((t_all, w), BF16) for w in widths]),
        grid=(n_p + n_s,),
        in_specs=[xp_spec, xs_spec, _const_spec((1, D_MODEL)), _const_spec((1, D_MODEL)),
                  _const_spec((D_MODEL, IN_COLS)),
                  _const_spec((GMLP_GROUPS, GMLP_GROUP_DIM)), _const_spec((GMLP_GROUPS, GMLP_GROUP_DIM))],
        out_specs=[tok(D_MODEL)] + [tok(w) for w in widths],
        scratch_shapes=[pltpu.VMEM((TM, D_MODEL), BF16)],
        compiler_params=pltpu.CompilerParams(dimension_semantics=("arbitrary",),
                                             vmem_limit_bytes=VMEM_LIMIT),
        name="proj",
    )(xp, xs, g0, b0, w_in, lg, lb)


def _route(logits_t):
    row = lambda j: logits_t[j:j + 1, :]
    gl = [row(j) for j in range(N_EXPERT_GROUPS)]
    gmax, gidx = gl[0], jnp.zeros(gl[0].shape, jnp.int32)
    for j in range(1, N_EXPERT_GROUPS):
        better = gl[j] > gmax
        gmax = jnp.where(better, gl[j], gmax)
        gidx = jnp.where(better, j, gidx)
    gsum = jnp.exp(gl[0] - gmax)
    for j in range(1, N_EXPERT_GROUPS):
        gsum = gsum + jnp.exp(gl[j] - gmax)
    p_group = 1.0 / gsum

    ig = []
    for e in range(EXPERTS_PER_GROUP):
        v = row(N_EXPERT_GROUPS + (N_EXPERT_GROUPS - 1) * EXPERTS_PER_GROUP + e)
        for g in range(N_EXPERT_GROUPS - 2, -1, -1):
            v = jnp.where(gidx == g, row(N_EXPERT_GROUPS + g * EXPERTS_PER_GROUP + e), v)
        ig.append(v)
    v1, i1 = ig[0], jnp.zeros(ig[0].shape, jnp.int32)
    for e in range(1, EXPERTS_PER_GROUP):
        better = ig[e] > v1
        v1 = jnp.where(better, ig[e], v1)
        i1 = jnp.where(better, e, i1)
    v2 = jnp.where(i1 == 0, ig[1], ig[0])
    i2 = jnp.where(i1 == 0, 1, 0).astype(jnp.int32)
    for e in range(1, EXPERTS_PER_GROUP):
        better = jnp.logical_and(i1 != e, ig[e] > v2)
        v2 = jnp.where(better, ig[e], v2)
        i2 = jnp.where(better, e, i2)
    ev = jnp.exp(v2 - v1)
    ssum = 1.0 + ev
    w1 = 1.0 / ssum
    w2 = ev / ssum
    first_lo = i1 < i2
    lo = jnp.minimum(i1, i2)
    hi = jnp.maximum(i1, i2)
    c_lo = jnp.where(first_lo, w1, w2) * p_group
    c_hi = jnp.where(first_lo, w2, w1) * p_group
    pair = jnp.where(lo == 0, hi - 1, jnp.where(lo == 1, hi + 1, N_PAIRS - 1))
    bucket = gidx * N_PAIRS + pair
    return c_lo, c_hi, bucket.astype(F32)


def _mix_kernel(n_tiles, seq_edges_first, seq_edges_last,
                sink_ref, xn_ref, q_ref,
                kp_ref, km_ref, kn_ref, vp_ref, vm_ref, vn_ref,
                gu_ref, vgn_ref, sa_ref, sb_ref, ws_ref, bs_ref,
                wa_ref, wb_ref, wo_ref, g1_ref, b1_ref, wr_ref, br_ref,
                x1t_ref, rt_ref,
                kf_scr, vf_scr, a_scr, sg_scr, mg_scr, mix_scr):
    i = pl.program_id(0)
    nblk = TM // BLOCK
    cur = i % 2

    @pl.when(i == 0)
    def _():
        a_scr[1] = jnp.zeros(a_scr.shape[1:], a_scr.dtype)
        sg_scr[1] = jnp.zeros(sg_scr.shape[1:], sg_scr.dtype)
        mix_scr[0] = jnp.zeros(mix_scr.shape[1:], mix_scr.dtype)

    def stage_a():
        tile = jnp.minimum(i, n_tiles - 1)
        kf_scr[0:BLOCK, :] = kp_ref[...]
        kf_scr[BLOCK:BLOCK + TM, :] = km_ref[...]
        kf_scr[BLOCK + TM:, :] = kn_ref[...]
        vf_scr[0:BLOCK, :] = vp_ref[...]
        vf_scr[BLOCK:BLOCK + TM, :] = vm_ref[...]
        vf_scr[BLOCK + TM:, :] = vn_ref[...]
        first_blk = tile * nblk
        last_blk = tile * nblk + nblk - 1
        has_prev = jnp.logical_and(*[first_blk != e for e in seq_edges_first])
        has_next = jnp.logical_and(*[last_blk != e for e in seq_edges_last])
        kj = lax.broadcasted_iota(jnp.int32, (BLOCK, 3 * BLOCK), 1)
        qi = lax.broadcasted_iota(jnp.int32, (BLOCK, 3 * BLOCK), 0)
        dist = jnp.abs(kj - BLOCK - qi)
        in_window = dist <= WINDOW
        dist_f = dist.astype(F32)
        lo_key = jnp.where(has_prev, 0, BLOCK)
        hi_key = jnp.where(has_next, 3 * BLOCK, 2 * BLOCK)
        yield
        for j in range(nblk):
            mask = in_window
            if j == 0:
                mask = jnp.logical_and(mask, kj >= lo_key)
            if j == nblk - 1:
                mask = jnp.logical_and(mask, kj < hi_key)
            r0 = j * BLOCK
            for kvh in range(N_KV_HEADS):
                c0 = kvh * HEAD_DIM
                kb = kf_scr[r0:r0 + 3 * BLOCK, c0:c0 + HEAD_DIM]
                vb = vf_scr[r0:r0 + 3 * BLOCK, c0:c0 + HEAD_DIM]
                heads = [kvh * Q_PER_KV + g for g in range(Q_PER_KV)]
                qs = jnp.concatenate(
                    [q_ref[r0:r0 + BLOCK, h * HEAD_DIM:(h + 1) * HEAD_DIM] for h in heads], axis=0)
                s_all = lax.dot_general(qs, kb, (((1,), (1,)), ((), ())), preferred_element_type=F32)
                for g, h in enumerate(heads):
                    s = s_all[g * BLOCK:(g + 1) * BLOCK, :] * ATTN_SCALE
                    s = jnp.where(mask, s + dist_f * (-_SLOPES[h]), NEG_INF)
                    sink = sink_ref[h]
                    m = jnp.maximum(jnp.max(s, axis=-1, keepdims=True), sink)
                    p = jnp.exp(s - m)
                    denom = jnp.sum(p, axis=-1, keepdims=True) + jnp.exp(sink - m)
                    pn = (p * (1.0 / denom)).astype(BF16)
                    o = jnp.dot(pn, vb, preferred_element_type=F32)
                    a_scr[cur, r0:r0 + BLOCK, h * HEAD_DIM:(h + 1) * HEAD_DIM] = o.astype(BF16)
                    yield
            for grp in range(GMLP_GROUPS):
                c0 = grp * GMLP_GROUP_DIM
                sp = jnp.dot(ws_ref[grp], vgn_ref[r0:r0 + BLOCK, c0:c0 + GMLP_GROUP_DIM],
                             preferred_element_type=F32) + bs_ref[grp]
                u = gu_ref[r0:r0 + BLOCK, c0:c0 + GMLP_GROUP_DIM].astype(F32)
                sg_scr[cur, r0:r0 + BLOCK, c0:c0 + GMLP_GROUP_DIM] = (u * sp).astype(BF16)
            yield

    def stage_b():
        a = a_scr[1 - cur]
        sg = sg_scr[1 - cur]
        for c in range(0, D_MODEL, PROJ_CHUNK):
            ma = jnp.dot(a, wa_ref[:, c:c + PROJ_CHUNK], preferred_element_type=F32)
            mb = jnp.dot(sg, wb_ref[:, c:c + PROJ_CHUNK], preferred_element_type=F32)
            merged = (sa_ref[:, c:c + PROJ_CHUNK].astype(F32) * ma
                      + sb_ref[:, c:c + PROJ_CHUNK].astype(F32) * mb)
            mg_scr[:, c:c + PROJ_CHUNK] = merged.astype(BF16)
            yield
        mg = mg_scr[...]
        for c in range(0, D_MODEL, PROJ_CHUNK):
            mix = jnp.dot(mg, wo_ref[:, c:c + PROJ_CHUNK], preferred_element_type=F32)
            mix_scr[1 - cur, :, c:c + PROJ_CHUNK] = DEEPNORM_ALPHA * xn_ref[:, c:c + PROJ_CHUNK] + mix
            yield

    def stage_c():
        mu = jnp.mean(mix_scr[cur], axis=-1, keepdims=True)
        yield
        zc = mix_scr[cur] - mu
        inv = lax.rsqrt(jnp.mean(zc * zc, axis=-1, keepdims=True) + LN_EPS)
        yield
        r = None
        for c in range(0, D_MODEL, MIX_CHUNK):
            x1 = ((mix_scr[cur, :, c:c + MIX_CHUNK] - mu) * inv * g1_ref[:, c:c + MIX_CHUNK]
                  + b1_ref[:, c:c + MIX_CHUNK])
            for k in range(MIX_CHUNK // LANES):
                x1t_ref[pl.ds(c // LANES + k, TM, stride=ROW_PITCH), :] = x1[:, k * LANES:(k + 1) * LANES]
            x_hi = x1.astype(BF16)
            x_lo = (x1 - x_hi.astype(F32)).astype(BF16)
            part = (jnp.dot(x_hi, wr_ref[c:c + MIX_CHUNK, :], preferred_element_type=F32)
                    + jnp.dot(x_lo, wr_ref[c:c + MIX_CHUNK, :], preferred_element_type=F32))
            r = part if r is None else r + part
            yield
        logits = r[:, :LANES] + r[:, LANES:] + br_ref[...]
        c_lo, c_hi, bucket = _route(logits.T)
        routed = jnp.concatenate([c_lo, c_hi, bucket, jnp.zeros((5, TM), F32)], axis=0)
        rt_ref[...] = routed
        padded = jnp.concatenate([routed, jnp.zeros((LANES - 8, TM), F32)], axis=0)
        x1t_ref[pl.ds(X_ROWS, TM, stride=ROW_PITCH), :] = padded.T
        for k in range(X_ROWS + 1, ROW_PITCH):
            x1t_ref[pl.ds(k, TM, stride=ROW_PITCH), :] = jnp.zeros((TM, LANES), F32)
        yield

    n_chunks = D_MODEL // MIX_CHUNK
    _emit_interleaved((stage_b(), [512] * (2 * D_MODEL // PROJ_CHUNK)),
                      (stage_a(), [100] + ([600] * N_Q_HEADS + [500]) * nblk),
                      (stage_c(), [1000, 1000] + [550] * n_chunks + [1000]))


def _mix_call(t_prompt, sink, xn, q, k, v, gu, vgn, sa, sb, ws, bs, wa, wb, wo, g1, b1, wr, br):
    t_all = xn.shape[0]
    nblk = TM // BLOCK
    blk_p, blk_all = t_prompt // BLOCK, t_all // BLOCK
    n_tiles = t_all // TM
    front = lambda i: jnp.minimum(i, n_tiles - 1)
    back = lambda i: jnp.clip(i - 1, 0, n_tiles - 1)
    last = lambda i: jnp.maximum(i - 2, 0)

    def tok(width, which):
        return pl.BlockSpec((TM, width), lambda i: (which(i), 0))

    prev_spec = pl.BlockSpec((BLOCK, KV_WIDTH), lambda i: (jnp.maximum(front(i) * nblk - 1, 0), 0))
    next_spec = pl.BlockSpec((BLOCK, KV_WIDTH),
                             lambda i: (jnp.minimum((front(i) + 1) * nblk, blk_all - 1), 0))
    kv_spec = tok(KV_WIDTH, front)
    kernel = functools.partial(_mix_kernel, n_tiles, (0, blk_p), (blk_p - 1, blk_all - 1))
    return pl.pallas_call(
        kernel,
        out_shape=[jax.ShapeDtypeStruct((t_all * ROW_PITCH, LANES), F32), jax.ShapeDtypeStruct((8, t_all), F32)],
        grid=(n_tiles + 2,),
        in_specs=[pl.BlockSpec(memory_space=pltpu.SMEM),
                  tok(D_MODEL, back), tok(ATTN_WIDTH, front), prev_spec, kv_spec, next_spec, prev_spec, kv_spec, next_spec,
                  tok(GMLP_WIDTH, front), tok(GMLP_WIDTH, front), tok(D_MODEL, back), tok(D_MODEL, back),
                  _const_spec((GMLP_GROUPS, BLOCK, BLOCK)), _const_spec((GMLP_GROUPS, BLOCK, BLOCK)),
                  _const_spec((ATTN_WIDTH, D_MODEL)), _const_spec((GMLP_WIDTH, D_MODEL)),
                  _const_spec((D_MODEL, D_MODEL)), _const_spec((1, D_MODEL)), _const_spec((1, D_MODEL)),
                  _const_spec((D_MODEL, 2 * LANES)), _const_spec((1, LANES))],
        out_specs=[pl.BlockSpec((TM * ROW_PITCH, LANES), lambda i: (last(i), 0)),
                   pl.BlockSpec((8, TM), lambda i: (0, last(i)))],
        scratch_shapes=[pltpu.VMEM((TM + 2 * BLOCK, KV_WIDTH), BF16),
                        pltpu.VMEM((TM + 2 * BLOCK, KV_WIDTH), BF16),
                        pltpu.VMEM((2, TM, ATTN_WIDTH), BF16),
                        pltpu.VMEM((2, TM, GMLP_WIDTH), BF16),
                        pltpu.VMEM((TM, D_MODEL), BF16),
                        pltpu.VMEM((2, TM, D_MODEL), F32)],
        compiler_params=pltpu.CompilerParams(dimension_semantics=("arbitrary",),
                                             vmem_limit_bytes=VMEM_LIMIT),
        name="mix",
    )(sink, xn, q, k, k, k, v, v, v, gu, vgn, sa, sb, ws, bs, wa, wb, wo, g1, b1, wr, br)


def _gather_copy(src_hbm, dst_buf, sem, src_row, dst_row, n_rows):
    return pltpu.make_async_copy(src_hbm.at[pl.ds(src_row, n_rows)], dst_buf.at[pl.ds(dst_row, n_rows)], sem)


def _start_gather(idx_ref, src_hbm, dst_buf, sem, n_rows, priorities=(0,)):
    def issue(k, carry):
        for j, priority in enumerate(priorities):
            r = k * len(priorities) + j
            src_row = pl.multiple_of(idx_ref[0, r] * ROW_PITCH, 8)
            dst_row = pl.multiple_of(r * ROW_PITCH, 8)
            _gather_copy(src_hbm, dst_buf, sem, src_row, dst_row, n_rows).start(priority=priority)
        return carry

    lax.fori_loop(0, TM // len(priorities), issue, 0, unroll=8 // len(priorities))


def _wait_gather(src_hbm, dst_buf, sem, n_rows):
    _gather_copy(src_hbm, dst_buf, sem, 0, 0, TM * n_rows).wait()


def _idx_specs(n_tiles, lookahead=1):
    def spec(tile_of_step):
        return pl.BlockSpec((None, 1, TM), lambda i, *_: (jnp.minimum(tile_of_step(i), n_tiles - 1), 0, 0),
                            memory_space=pltpu.SMEM)

    return [spec(lambda i, t=t: t) for t in range(lookahead)] + [spec(lambda i: i + lookahead)]


def _unsort_kernel(idx0_ref, idxn_ref, ys_hbm, g2_ref, b2_ref, out_ref, buf, sem):
    i = pl.program_id(0)
    slot = i % 2

    @pl.when(i == 0)
    def _():
        _start_gather(idx0_ref, ys_hbm, buf.at[0], sem.at[0], X_ROWS, DMA_PRIORITIES)

    @pl.when(i + 1 < pl.num_programs(0))
    def _():
        _start_gather(idxn_ref, ys_hbm, buf.at[1 - slot], sem.at[1 - slot], X_ROWS, DMA_PRIORITIES)

    cur = buf.at[slot]
    _wait_gather(ys_hbm, cur, sem.at[slot], X_ROWS)
    for r in range(X_ROWS):
        out_ref[:, r * LANES:(r + 1) * LANES] = _load_token_major(cur, r)
    out_ref[...] = _layer_norm(out_ref[...], g2_ref[...], b2_ref[...])


def _unsort_call(idx, ysorted, g2, b2, name):
    n_tiles = idx.shape[0] // TM
    idx = idx.reshape(n_tiles, 1, TM)
    first, nxt = _idx_specs(n_tiles)
    return pl.pallas_call(
        _unsort_kernel,
        out_shape=jax.ShapeDtypeStruct((n_tiles * TM, D_MODEL), F32),
        grid=(n_tiles,),
        in_specs=[first, nxt, pl.BlockSpec(memory_space=pl.ANY),
                  _const_spec((1, D_MODEL)), _const_spec((1, D_MODEL))],
        out_specs=pl.BlockSpec((TM, D_MODEL), lambda i: (i, 0)),
        scratch_shapes=[pltpu.VMEM((2, TM * ROW_PITCH, LANES), F32), pltpu.SemaphoreType.DMA((2,))],
        compiler_params=pltpu.CompilerParams(dimension_semantics=("arbitrary",)),
        name=name,
    )(idx, idx, ysorted, g2, b2)


def _moe_kernel(elo_ref, ehi_ref, nvalid_ref, idx0_ref, idx1_ref, idxn_ref, x1t_hbm,
                wg_lo, wu_lo, wd_lo, wg_hi, wu_hi, wd_hi, out_ref,
                buf, sem, xb_scr):
    del elo_ref, ehi_ref
    i = pl.program_id(0)
    n = pl.num_programs(0)
    slot = i % MOE_BUFFERS

    def gather_if_valid(tile, idx_ref, when):
        dst = tile % MOE_BUFFERS

        @pl.when(jnp.logical_and(when, nvalid_ref[jnp.minimum(tile, n - 1)] > 0))
        def _():
            _start_gather(idx_ref, x1t_hbm, buf.at[dst], sem.at[dst], ROW_PITCH)

    assert MOE_LOOKAHEAD == 2
    gather_if_valid(0, idx0_ref, i == 0)
    gather_if_valid(1, idx1_ref, jnp.logical_and(i == 0, n > 1))
    gather_if_valid(i + MOE_LOOKAHEAD, idxn_ref, i + MOE_LOOKAHEAD < n)

    @pl.when(nvalid_ref[i] > 0)
    def _():
        cur = buf.at[slot]
        _wait_gather(x1t_hbm, cur, sem.at[slot], ROW_PITCH)
        for r in range(X_ROWS):
            xb_scr[:, r * LANES:(r + 1) * LANES] = _load_token_major(cur, r).astype(BF16)
        route = _load_token_major(cur, X_ROWS)
        xb = xb_scr[...]

        def hidden(wg, wu, c):
            gate = jnp.dot(xb, wg[...], preferred_element_type=F32)
            up = jnp.dot(xb, wu[...], preferred_element_type=F32)
            return ((jax.nn.silu(gate) * up) * c).astype(BF16)

        h_lo = hidden(wg_lo, wu_lo, route[:, 0:1])
        h_hi = hidden(wg_hi, wu_hi, route[:, 1:2])
        for c in range(0, D_MODEL, COL_CHUNK):
            y = (jnp.dot(h_lo, wd_lo[:, c:c + COL_CHUNK], preferred_element_type=F32)
                 + jnp.dot(h_hi, wd_hi[:, c:c + COL_CHUNK], preferred_element_type=F32))
            for r in range(c // LANES, (c + COL_CHUNK) // LANES):
                out_ref[pl.ds(r, TM, stride=ROW_PITCH), :] = (DEEPNORM_ALPHA * _load_token_major(cur, r)
                                                              + y[:, r * LANES - c:(r + 1) * LANES - c])
        for r in range(X_ROWS, ROW_PITCH):
            out_ref[pl.ds(r, TM, stride=ROW_PITCH), :] = jnp.zeros((TM, LANES), F32)

    @pl.when(nvalid_ref[i] == 0)
    def _():
        out_ref[...] = jnp.zeros(out_ref.shape, out_ref.dtype)


def _moe_call(e_lo, e_hi, nvalid, src, x1t, wg, wu, wd):
    n_tiles = src.shape[0] // TM
    src = src.reshape(n_tiles, 1, TM)
    idx_specs = _idx_specs(n_tiles, MOE_LOOKAHEAD)

    def w_spec(shape, which):
        def imap(i, elo, ehi, nv):
            return ((elo, ehi)[which][i], 0, 0)
        return pl.BlockSpec((None,) + shape, imap)

    up_shape, down_shape = (D_MODEL, EXPERT_FF), (EXPERT_FF, D_MODEL)
    grid_spec = pltpu.PrefetchScalarGridSpec(
        num_scalar_prefetch=3,
        grid=(n_tiles,),
        in_specs=idx_specs + [pl.BlockSpec(memory_space=pl.ANY),
                              w_spec(up_shape, 0), w_spec(up_shape, 0), w_spec(down_shape, 0),
                              w_spec(up_shape, 1), w_spec(up_shape, 1), w_spec(down_shape, 1)],
        out_specs=pl.BlockSpec((TM * ROW_PITCH, LANES), lambda i, *_: (i, 0)),
        scratch_shapes=[pltpu.VMEM((MOE_BUFFERS, TM * ROW_PITCH, LANES), F32),
                        pltpu.SemaphoreType.DMA((MOE_BUFFERS,)),
                        pltpu.VMEM((TM, D_MODEL), BF16)],
    )
    return pl.pallas_call(
        _moe_kernel,
        out_shape=jax.ShapeDtypeStruct((n_tiles * TM * ROW_PITCH, LANES), F32),
        grid_spec=grid_spec,
        compiler_params=pltpu.CompilerParams(dimension_semantics=("arbitrary",),
                                             vmem_limit_bytes=VMEM_LIMIT),
        name="moe",
    )(e_lo, e_hi, nvalid, *([src] * len(idx_specs)), x1t, wg, wu, wd, wg, wu, wd)


_PAIR_LO = np.array([0, 0, 0, 1, 1, 2], np.int32)
_PAIR_HI = np.array([1, 2, 3, 2, 3, 3], np.int32)


def _bucket_layout(bucket, n_slots):
    t_all = bucket.shape[0]
    n_tiles = n_slots // TM
    rows = t_all // LANES
    onehot = (bucket.reshape(rows, LANES, 1) == jnp.arange(N_BUCKETS, dtype=jnp.int32)).astype(F32)
    earlier = (jnp.arange(LANES)[:, None] > jnp.arange(LANES)[None, :]).astype(F32)
    within = jnp.einsum("ts,rsb->rtb", earlier, onehot)
    row_total = jnp.sum(onehot, axis=1)
    row_start = jnp.cumsum(row_total, axis=0) - row_total
    counts = jnp.sum(row_total, axis=0).astype(jnp.int32)
    padded = ((counts + TM - 1) // TM) * TM
    ends = jnp.cumsum(padded)
    starts = ends - padded
    slot = within + (row_start + starts.astype(F32))[:, None, :]
    dest = jnp.sum(slot * onehot, axis=-1).reshape(t_all).astype(jnp.int32)
    src = jnp.zeros((n_slots,), jnp.int32).at[dest].set(jnp.arange(t_all, dtype=jnp.int32),
                                                       unique_indices=True, mode="promise_in_bounds")
    tile_start = jnp.arange(n_tiles, dtype=jnp.int32) * TM
    owner_start = jnp.minimum(tile_start, jnp.maximum(ends[-1:] - TM, 0))[:, None]
    owner = jnp.logical_and(owner_start >= starts[None, :], owner_start < ends[None, :]).astype(jnp.int32)
    remaining = counts[None, :] - (tile_start[:, None] - starts[None, :])
    nvalid = jnp.sum(owner * jnp.clip(remaining, 0, TM), axis=1)
    bucket_ids = np.arange(N_BUCKETS)
    first_expert = (bucket_ids // N_PAIRS) * EXPERTS_PER_GROUP
    e_lo = jnp.sum(owner * jnp.asarray(first_expert + _PAIR_LO[bucket_ids % N_PAIRS], jnp.int32)[None, :], axis=1)
    e_hi = jnp.sum(owner * jnp.asarray(first_expert + _PAIR_HI[bucket_ids % N_PAIRS], jnp.int32)[None, :], axis=1)
    return src, dest, e_lo.astype(jnp.int32), e_hi.astype(jnp.int32), nvalid.astype(jnp.int32)


def _layer(xp, xs, in_ln_g, in_ln_b, w_in, attn_sink, gmlp_w_s, gmlp_b_s, gmlp_ln_g, gmlp_ln_b,
           w_attn_branch, w_gmlp_branch, w_out, ln1_g, ln1_b,
           router_w_group, router_b_group, router_w_expert, router_b_expert,
           w_expert_gate, w_expert_up, w_expert_down, ln2_g, ln2_b):
    t_p, t_s = xp.shape[0], xs.shape[0]
    t_all = t_p + t_s
    row = lambda p: p.reshape(1, -1).astype(F32)

    xn, q, k, v, gu, vgn, sa, sb = _proj_call(xp, xs, row(in_ln_g), row(in_ln_b), w_in.astype(BF16),
                                              gmlp_ln_g.astype(F32), gmlp_ln_b.astype(F32))

    wr = jnp.concatenate([router_w_group, router_w_expert], axis=1).astype(F32)
    wr = jnp.pad(wr, ((0, 0), (0, LANES - wr.shape[1])))
    wr_hi = wr.astype(BF16)
    wr_lo = (wr - wr_hi.astype(F32)).astype(BF16)
    br = jnp.pad(jnp.concatenate([router_b_group, router_b_expert]).astype(F32),
                 (0, LANES - N_EXPERT_GROUPS - N_EXPERTS)).reshape(1, LANES)
    bs = jnp.broadcast_to(gmlp_b_s.astype(F32)[:, :, None], (GMLP_GROUPS, BLOCK, BLOCK))

    x1t, routed = _mix_call(t_p, attn_sink.astype(F32), xn,
                            q, k, v, gu, vgn, sa, sb, gmlp_w_s.astype(BF16), bs,
                            w_attn_branch.astype(BF16), w_gmlp_branch.astype(BF16), w_out.astype(BF16),
                            row(ln1_g), row(ln1_b), jnp.concatenate([wr_hi, wr_lo], axis=1), br)

    n_slots = t_all + N_BUCKETS * TM
    bucket = routed[2].astype(jnp.int32)
    src, dest, e_lo, e_hi, nvalid = _bucket_layout(bucket, n_slots)
    zsorted = _moe_call(e_lo, e_hi, nvalid, src, x1t,
                        w_expert_gate.astype(BF16), w_expert_up.astype(BF16), w_expert_down.astype(BF16))
    g2, b2 = row(ln2_g), row(ln2_b)
    return (_unsort_call(dest[:t_p], zsorted, g2, b2, "unsort_prompt"),
            _unsort_call(dest[t_p:], zsorted, g2, b2, "unsort_sample"))


def kernel(x_prompt, x_sample, in_ln_g, in_ln_b, w_in, attn_sink, gmlp_w_s, gmlp_b_s, gmlp_ln_g, gmlp_ln_b,
           w_attn_branch, w_gmlp_branch, w_out, ln1_g, ln1_b,
           router_w_group, router_b_group, router_w_expert, router_b_expert,
           w_expert_gate, w_expert_up, w_expert_down, ln2_g, ln2_b):
    bp, sp, d = x_prompt.shape
    bs, ss, _ = x_sample.shape
    assert bp == 1 and bs == 1 and d == D_MODEL and sp % TM == 0 and ss % TM == 0
    assert w_in.shape[0] == 1, "one layer"
    yp, ys = _layer(x_prompt.reshape(sp, d), x_sample.reshape(ss, d), in_ln_g, in_ln_b, w_in[0], attn_sink[0],
                    gmlp_w_s[0], gmlp_b_s[0], gmlp_ln_g[0], gmlp_ln_b[0],
                    w_attn_branch[0], w_gmlp_branch[0], w_out[0], ln1_g[0], ln1_b[0],
                    router_w_group[0], router_b_group[0], router_w_expert[0], router_b_expert[0],
                    w_expert_gate[0], w_expert_up[0], w_expert_down[0], ln2_g[0], ln2_b[0])
    return yp.reshape(1, sp, d), ys.reshape(1, ss, d)
```

```python
import functools

import numpy as np
import jax
import jax.numpy as jnp
from jax import lax
from jax.experimental import pallas as pl
from jax.experimental.pallas import tpu as pltpu

F32 = jnp.float32
BF16 = jnp.bfloat16

D_MODEL = 2048
HEAD_DIM = 128
N_Q_HEADS = 8
N_KV_HEADS = 2
Q_PER_KV = N_Q_HEADS // N_KV_HEADS
ATTN_WIDTH = N_Q_HEADS * HEAD_DIM
KV_WIDTH = N_KV_HEADS * HEAD_DIM
WINDOW = 128
BLOCK = 128
GMLP_WIDTH = D_MODEL // 2
GMLP_GROUPS = 8
GMLP_GROUP_DIM = GMLP_WIDTH // GMLP_GROUPS
N_EXPERT_GROUPS = 4
EXPERTS_PER_GROUP = 4
N_EXPERTS = N_EXPERT_GROUPS * EXPERTS_PER_GROUP
EXPERT_FF = 512
LN_EPS = 1e-5
DEEPNORM_ALPHA = 2.0 ** 0.25
NEG_INF = -1e9
ATTN_SCALE = HEAD_DIM ** -0.5

_C_Q = 0
_C_K = _C_Q + ATTN_WIDTH
_C_V = _C_K + KV_WIDTH
_C_U = _C_V + KV_WIDTH
_C_VG = _C_U + GMLP_WIDTH
_C_GA = _C_VG + GMLP_WIDTH
_C_GB = _C_GA + D_MODEL
IN_COLS = _C_GB + D_MODEL

LANES = 128
N_PAIRS = 6
N_BUCKETS = N_EXPERT_GROUPS * N_PAIRS
X_ROWS = D_MODEL // LANES
ROW_PITCH = 24
TM = 256
COL_CHUNK = 512
MIX_CHUNK = 256
PROJ_CHUNK = 256
MOE_LOOKAHEAD = 3
MOE_BUFFERS = MOE_LOOKAHEAD + 1
VMEM_LIMIT = 60 * 1024 * 1024

_SLOPES = [float(2.0 ** (-8.0 * (h + 1) / N_Q_HEADS)) for h in range(N_Q_HEADS)]


def _layer_norm(x, g, b):
    mu = jnp.mean(x, axis=-1, keepdims=True)
    xc = x - mu
    var = jnp.mean(xc * xc, axis=-1, keepdims=True)
    return xc * lax.rsqrt(var + LN_EPS) * g + b


def _store_token_major(ref, x, pad_from):
    for r in range(X_ROWS):
        ref[pl.ds(r, TM, stride=ROW_PITCH), :] = x[:, r * LANES:(r + 1) * LANES]
    for r in range(pad_from, ROW_PITCH):
        ref[pl.ds(r, TM, stride=ROW_PITCH), :] = jnp.zeros((TM, LANES), ref.dtype)


def _load_token_major(ref, r):
    return ref[pl.ds(r, TM, stride=ROW_PITCH), :]


def _emit_interleaved(*streams):
    order = []
    for s, (_, costs) in enumerate(streams):
        done = 0.0
        for c in costs:
            order.append(((done + c / 2) / sum(costs), s))
            done += c
    for _, s in sorted(order):
        next(streams[s][0])
    for gen, _ in streams:
        assert next(gen, "done") == "done", "stream has more units than declared"


def _cast_plan(weights, n_steps):
    steps = 1 << (n_steps.bit_length() - 1)
    steps = min([steps] + [w.shape[0] // 16 for w in weights])
    specs = [pl.BlockSpec((w.shape[0] // steps, w.shape[1]), lambda i, *_: (jnp.minimum(i, steps - 1), 0))
             for w in weights]
    return steps, specs


def _cast_side_job(steps, src_refs, dst_refs):
    @pl.when(pl.program_id(0) < steps)
    def _():
        for src, dst in zip(src_refs, dst_refs):
            dst[...] = src[...].astype(dst.dtype)


def _const_spec(shape):
    nd = len(shape)
    return pl.BlockSpec(shape, lambda i, *_: (0,) * nd, pipeline_mode=pl.Buffered(1))


def _proj_kernel(n_prompt_tiles, cast_steps, n_cast, *refs):
    xp_ref, xs_ref, g0_ref, b0_ref, w_ref, lg_ref, lb_ref = refs[:7]
    cast_src = refs[7:7 + n_cast]
    xn_ref, q_ref, k_ref, v_ref, gu_ref, vgn_ref, sa_ref, sb_ref = refs[7 + n_cast:15 + n_cast]
    cast_dst = refs[15 + n_cast:15 + 2 * n_cast]
    xn_scr = refs[15 + 2 * n_cast]
    i = pl.program_id(0)
    _cast_side_job(cast_steps, cast_src, cast_dst)

    @pl.when(i < n_prompt_tiles)
    def _():
        xn_ref[...] = _layer_norm(xp_ref[...], g0_ref[...], b0_ref[...])

    @pl.when(i >= n_prompt_tiles)
    def _():
        xn_ref[...] = _layer_norm(xs_ref[...], g0_ref[...], b0_ref[...])

    xn_scr[...] = xn_ref[...].astype(BF16)
    xn = xn_scr[...]

    def proj(c0, width):
        return jnp.dot(xn, w_ref[:, c0:c0 + width], preferred_element_type=F32)

    for c in range(0, ATTN_WIDTH, COL_CHUNK):
        q_ref[:, c:c + COL_CHUNK] = proj(_C_Q + c, COL_CHUNK).astype(BF16)
    kv = proj(_C_K, 2 * KV_WIDTH)
    k_ref[...] = kv[:, :KV_WIDTH].astype(BF16)
    v_ref[...] = kv[:, KV_WIDTH:].astype(BF16)
    for c in range(0, GMLP_WIDTH, COL_CHUNK):
        gu_ref[:, c:c + COL_CHUNK] = jax.nn.gelu(proj(_C_U + c, COL_CHUNK)).astype(BF16)
    for c in range(0, GMLP_WIDTH, COL_CHUNK):
        vg = jax.nn.gelu(proj(_C_VG + c, COL_CHUNK))
        for j in range(COL_CHUNK // GMLP_GROUP_DIM):
            grp = c // GMLP_GROUP_DIM + j
            blk = vg[:, j * GMLP_GROUP_DIM:(j + 1) * GMLP_GROUP_DIM]
            y = _layer_norm(blk, lg_ref[grp:grp + 1, :], lb_ref[grp:grp + 1, :])
            vgn_ref[:, grp * GMLP_GROUP_DIM:(grp + 1) * GMLP_GROUP_DIM] = y.astype(BF16)
    for c in range(0, D_MODEL, COL_CHUNK):
        sa_ref[:, c:c + COL_CHUNK] = jax.nn.sigmoid(proj(_C_GA + c, COL_CHUNK)).astype(BF16)
    for c in range(0, D_MODEL, COL_CHUNK):
        sb_ref[:, c:c + COL_CHUNK] = jax.nn.sigmoid(proj(_C_GB + c, COL_CHUNK)).astype(BF16)


def _proj_call(xp, xs, g0, b0, w_in, lg, lb, later_weights):
    n_p, n_s = xp.shape[0] // TM, xs.shape[0] // TM
    t_all = xp.shape[0] + xs.shape[0]
    cast_steps, cast_specs = _cast_plan(later_weights, n_p + n_s)
    _, cast_out_specs = _cast_plan(later_weights, n_p + n_s)
    xp_spec = pl.BlockSpec((TM, D_MODEL), lambda i: (jnp.minimum(i, n_p - 1), 0))
    xs_spec = pl.BlockSpec((TM, D_MODEL), lambda i: (jnp.maximum(i - n_p, 0), 0))

    def tok(width):
        return pl.BlockSpec((TM, width), lambda i: (i, 0))

    widths = (ATTN_WIDTH, KV_WIDTH, KV_WIDTH, GMLP_WIDTH, GMLP_WIDTH, D_MODEL, D_MODEL)
    return pl.pallas_call(
        functools.partial(_proj_kernel, n_p, cast_steps, len(later_weights)),
        out_shape=([jax.ShapeDtypeStruct((t_all, D_MODEL), F32)]
                   + [jax.ShapeDtypeStruct((t_all, w), BF16) for w in widths]
                   + [jax.ShapeDtypeStruct(w.shape, BF16) for w in later_weights]),
        grid=(n_p + n_s,),
        in_specs=[xp_spec, xs_spec, _const_spec((1, D_MODEL)), _const_spec((1, D_MODEL)),
                  _const_spec((D_MODEL, IN_COLS)),
                  _const_spec((GMLP_GROUPS, GMLP_GROUP_DIM)), _const_spec((GMLP_GROUPS, GMLP_GROUP_DIM))]
        + cast_specs,
        out_specs=[tok(D_MODEL)] + [tok(w) for w in widths] + cast_out_specs,
        scratch_shapes=[pltpu.VMEM((TM, D_MODEL), BF16)],
        compiler_params=pltpu.CompilerParams(dimension_semantics=("arbitrary",),
                                             vmem_limit_bytes=VMEM_LIMIT),
        name="proj",
    )(xp, xs, g0, b0, w_in, lg, lb, *later_weights)


def _route(logits_t):
    row = lambda j: logits_t[j:j + 1, :]
    gl = [row(j) for j in range(N_EXPERT_GROUPS)]
    gmax, gidx = gl[0], jnp.zeros(gl[0].shape, jnp.int32)
    for j in range(1, N_EXPERT_GROUPS):
        better = gl[j] > gmax
        gmax = jnp.where(better, gl[j], gmax)
        gidx = jnp.where(better, j, gidx)
    gsum = jnp.exp(gl[0] - gmax)
    for j in range(1, N_EXPERT_GROUPS):
        gsum = gsum + jnp.exp(gl[j] - gmax)
    p_group = 1.0 / gsum

    ig = []
    for e in range(EXPERTS_PER_GROUP):
        v = row(N_EXPERT_GROUPS + (N_EXPERT_GROUPS - 1) * EXPERTS_PER_GROUP + e)
        for g in range(N_EXPERT_GROUPS - 2, -1, -1):
            v = jnp.where(gidx == g, row(N_EXPERT_GROUPS + g * EXPERTS_PER_GROUP + e), v)
        ig.append(v)
    v1, i1 = ig[0], jnp.zeros(ig[0].shape, jnp.int32)
    for e in range(1, EXPERTS_PER_GROUP):
        better = ig[e] > v1
        v1 = jnp.where(better, ig[e], v1)
        i1 = jnp.where(better, e, i1)
    v2 = jnp.where(i1 == 0, ig[1], ig[0])
    i2 = jnp.where(i1 == 0, 1, 0).astype(jnp.int32)
    for e in range(1, EXPERTS_PER_GROUP):
        better = jnp.logical_and(i1 != e, ig[e] > v2)
        v2 = jnp.where(better, ig[e], v2)
        i2 = jnp.where(better, e, i2)
    ev = jnp.exp(v2 - v1)
    ssum = 1.0 + ev
    w1 = 1.0 / ssum
    w2 = ev / ssum
    first_lo = i1 < i2
    lo = jnp.minimum(i1, i2)
    hi = jnp.maximum(i1, i2)
    c_lo = jnp.where(first_lo, w1, w2) * p_group
    c_hi = jnp.where(first_lo, w2, w1) * p_group
    pair = jnp.where(lo == 0, hi - 1, jnp.where(lo == 1, hi + 1, N_PAIRS - 1))
    bucket = gidx * N_PAIRS + pair
    return c_lo, c_hi, bucket.astype(F32)


def _mix_kernel(n_tiles, seq_edges_first, seq_edges_last, cast_steps, n_cast, *refs):
    (sink_ref, xn_ref, q_ref, kp_ref, km_ref, kn_ref, vp_ref, vm_ref, vn_ref,
     gu_ref, vgn_ref, sa_ref, sb_ref, ws_ref, bs_ref,
     wa_ref, wb_ref, wo_ref, g1_ref, b1_ref, wr_ref, br_ref) = refs[:22]
    cast_src = refs[22:22 + n_cast]
    x1t_ref, rt_ref = refs[22 + n_cast:24 + n_cast]
    cast_dst = refs[24 + n_cast:24 + 2 * n_cast]
    kf_scr, vf_scr, a_scr, sg_scr, mg_scr, mix_scr = refs[24 + 2 * n_cast:]
    _cast_side_job(cast_steps, cast_src, cast_dst)
    i = pl.program_id(0)
    nblk = TM // BLOCK
    cur = i % 2

    @pl.when(i == 0)
    def _():
        a_scr[1] = jnp.zeros(a_scr.shape[1:], a_scr.dtype)
        sg_scr[1] = jnp.zeros(sg_scr.shape[1:], sg_scr.dtype)
        mix_scr[0] = jnp.zeros(mix_scr.shape[1:], mix_scr.dtype)

    def stage_a():
        tile = jnp.minimum(i, n_tiles - 1)
        kf_scr[0:BLOCK, :] = kp_ref[...]
        kf_scr[BLOCK:BLOCK + TM, :] = km_ref[...]
        kf_scr[BLOCK + TM:, :] = kn_ref[...]
        vf_scr[0:BLOCK, :] = vp_ref[...]
        vf_scr[BLOCK:BLOCK + TM, :] = vm_ref[...]
        vf_scr[BLOCK + TM:, :] = vn_ref[...]
        first_blk = tile * nblk
        last_blk = tile * nblk + nblk - 1
        has_prev = jnp.logical_and(*[first_blk != e for e in seq_edges_first])
        has_next = jnp.logical_and(*[last_blk != e for e in seq_edges_last])
        kj = lax.broadcasted_iota(jnp.int32, (BLOCK, 3 * BLOCK), 1)
        qi = lax.broadcasted_iota(jnp.int32, (BLOCK, 3 * BLOCK), 0)
        dist = jnp.abs(kj - BLOCK - qi)
        in_window = dist <= WINDOW
        dist_f = dist.astype(F32)
        lo_key = jnp.where(has_prev, 0, BLOCK)
        hi_key = jnp.where(has_next, 3 * BLOCK, 2 * BLOCK)
        yield
        for j in range(nblk):
            mask = in_window
            if j == 0:
                mask = jnp.logical_and(mask, kj >= lo_key)
            if j == nblk - 1:
                mask = jnp.logical_and(mask, kj < hi_key)
            r0 = j * BLOCK
            for kvh in range(N_KV_HEADS):
                c0 = kvh * HEAD_DIM
                kb = kf_scr[r0:r0 + 3 * BLOCK, c0:c0 + HEAD_DIM]
                vb = vf_scr[r0:r0 + 3 * BLOCK, c0:c0 + HEAD_DIM]
                heads = [kvh * Q_PER_KV + g for g in range(Q_PER_KV)]
                qs = jnp.concatenate(
                    [q_ref[r0:r0 + BLOCK, h * HEAD_DIM:(h + 1) * HEAD_DIM] for h in heads], axis=0)
                s_all = lax.dot_general(qs, kb, (((1,), (1,)), ((), ())), preferred_element_type=F32)
                for g, h in enumerate(heads):
                    s = s_all[g * BLOCK:(g + 1) * BLOCK, :] * ATTN_SCALE
                    s = jnp.where(mask, s + dist_f * (-_SLOPES[h]), NEG_INF)
                    sink = sink_ref[h]
                    m = jnp.maximum(jnp.max(s, axis=-1, keepdims=True), sink)
                    p = jnp.exp(s - m)
                    denom = jnp.sum(p, axis=-1, keepdims=True) + jnp.exp(sink - m)
                    pn = (p * (1.0 / denom)).astype(BF16)
                    o = jnp.dot(pn, vb, preferred_element_type=F32)
                    a_scr[cur, r0:r0 + BLOCK, h * HEAD_DIM:(h + 1) * HEAD_DIM] = o.astype(BF16)
                    yield
            for grp in range(GMLP_GROUPS):
                c0 = grp * GMLP_GROUP_DIM
                sp = jnp.dot(ws_ref[grp], vgn_ref[r0:r0 + BLOCK, c0:c0 + GMLP_GROUP_DIM],
                             preferred_element_type=F32) + bs_ref[grp]
                u = gu_ref[r0:r0 + BLOCK, c0:c0 + GMLP_GROUP_DIM].astype(F32)
                sg_scr[cur, r0:r0 + BLOCK, c0:c0 + GMLP_GROUP_DIM] = (u * sp).astype(BF16)
            yield

    def stage_b():
        a = a_scr[1 - cur]
        sg = sg_scr[1 - cur]
        for c in range(0, D_MODEL, PROJ_CHUNK):
            ma = jnp.dot(a, wa_ref[:, c:c + PROJ_CHUNK], preferred_element_type=F32)
            mb = jnp.dot(sg, wb_ref[:, c:c + PROJ_CHUNK], preferred_element_type=F32)
            merged = (sa_ref[:, c:c + PROJ_CHUNK].astype(F32) * ma
                      + sb_ref[:, c:c + PROJ_CHUNK].astype(F32) * mb)
            mg_scr[:, c:c + PROJ_CHUNK] = merged.astype(BF16)
            yield
        mg = mg_scr[...]
        for c in range(0, D_MODEL, PROJ_CHUNK):
            mix = jnp.dot(mg, wo_ref[:, c:c + PROJ_CHUNK], preferred_element_type=F32)
            mix_scr[1 - cur, :, c:c + PROJ_CHUNK] = DEEPNORM_ALPHA * xn_ref[:, c:c + PROJ_CHUNK] + mix
            yield

    def stage_c():
        mu = jnp.mean(mix_scr[cur], axis=-1, keepdims=True)
        yield
        zc = mix_scr[cur] - mu
        inv = lax.rsqrt(jnp.mean(zc * zc, axis=-1, keepdims=True) + LN_EPS)
        yield
        r = None
        for c in range(0, D_MODEL, MIX_CHUNK):
            x1 = ((mix_scr[cur, :, c:c + MIX_CHUNK] - mu) * inv * g1_ref[:, c:c + MIX_CHUNK]
                  + b1_ref[:, c:c + MIX_CHUNK])
            for k in range(MIX_CHUNK // LANES):
                x1t_ref[pl.ds(c // LANES + k, TM, stride=ROW_PITCH), :] = x1[:, k * LANES:(k + 1) * LANES]
            x_hi = x1.astype(BF16)
            x_lo = (x1 - x_hi.astype(F32)).astype(BF16)
            part = (jnp.dot(x_hi, wr_ref[c:c + MIX_CHUNK, :], preferred_element_type=F32)
                    + jnp.dot(x_lo, wr_ref[c:c + MIX_CHUNK, :], preferred_element_type=F32))
            r = part if r is None else r + part
            yield
        logits = r[:, :LANES] + r[:, LANES:] + br_ref[...]
        c_lo, c_hi, bucket = _route(logits.T)
        routed = jnp.concatenate([c_lo, c_hi, bucket, jnp.zeros((5, TM), F32)], axis=0)
        rt_ref[...] = routed
        padded = jnp.concatenate([routed, jnp.zeros((LANES - 8, TM), F32)], axis=0)
        x1t_ref[pl.ds(X_ROWS, TM, stride=ROW_PITCH), :] = padded.T
        for k in range(X_ROWS + 1, ROW_PITCH):
            x1t_ref[pl.ds(k, TM, stride=ROW_PITCH), :] = jnp.zeros((TM, LANES), F32)
        yield

    n_chunks = D_MODEL // MIX_CHUNK
    _emit_interleaved((stage_b(), [512] * (2 * D_MODEL // PROJ_CHUNK)),
                      (stage_a(), [100] + ([600] * N_Q_HEADS + [500]) * nblk),
                      (stage_c(), [1000, 1000] + [550] * n_chunks + [1000]))


def _mix_call(t_prompt, sink, xn, q, k, v, gu, vgn, sa, sb, ws, bs, wa, wb, wo, g1, b1, wr, br, later_weights):
    t_all = xn.shape[0]
    nblk = TM // BLOCK
    blk_p, blk_all = t_prompt // BLOCK, t_all // BLOCK
    n_tiles = t_all // TM
    front = lambda i: jnp.minimum(i, n_tiles - 1)
    back = lambda i: jnp.clip(i - 1, 0, n_tiles - 1)
    last = lambda i: jnp.maximum(i - 2, 0)

    def tok(width, which):
        return pl.BlockSpec((TM, width), lambda i: (which(i), 0))

    prev_spec = pl.BlockSpec((BLOCK, KV_WIDTH), lambda i: (jnp.maximum(front(i) * nblk - 1, 0), 0))
    next_spec = pl.BlockSpec((BLOCK, KV_WIDTH),
                             lambda i: (jnp.minimum((front(i) + 1) * nblk, blk_all - 1), 0))
    kv_spec = tok(KV_WIDTH, front)
    cast_steps, cast_specs = _cast_plan(later_weights, n_tiles + 2)
    _, cast_out_specs = _cast_plan(later_weights, n_tiles + 2)
    kernel = functools.partial(_mix_kernel, n_tiles, (0, blk_p), (blk_p - 1, blk_all - 1),
                               cast_steps, len(later_weights))
    return pl.pallas_call(
        kernel,
        out_shape=[jax.ShapeDtypeStruct((t_all * ROW_PITCH, LANES), F32), jax.ShapeDtypeStruct((8, t_all), F32)]
        + [jax.ShapeDtypeStruct(w.shape, BF16) for w in later_weights],
        grid=(n_tiles + 2,),
        in_specs=[pl.BlockSpec(memory_space=pltpu.SMEM),
                  tok(D_MODEL, back), tok(ATTN_WIDTH, front), prev_spec, kv_spec, next_spec, prev_spec, kv_spec, next_spec,
                  tok(GMLP_WIDTH, front), tok(GMLP_WIDTH, front), tok(D_MODEL, back), tok(D_MODEL, back),
                  _const_spec((GMLP_GROUPS, BLOCK, BLOCK)), _const_spec((GMLP_GROUPS, BLOCK, BLOCK)),
                  _const_spec((ATTN_WIDTH, D_MODEL)), _const_spec((GMLP_WIDTH, D_MODEL)),
                  _const_spec((D_MODEL, D_MODEL)), _const_spec((1, D_MODEL)), _const_spec((1, D_MODEL)),
                  _const_spec((D_MODEL, 2 * LANES)), _const_spec((1, LANES))] + cast_specs,
        out_specs=[pl.BlockSpec((TM * ROW_PITCH, LANES), lambda i: (last(i), 0)),
                   pl.BlockSpec((8, TM), lambda i: (0, last(i)))] + cast_out_specs,
        scratch_shapes=[pltpu.VMEM((TM + 2 * BLOCK, KV_WIDTH), BF16),
                        pltpu.VMEM((TM + 2 * BLOCK, KV_WIDTH), BF16),
                        pltpu.VMEM((2, TM, ATTN_WIDTH), BF16),
                        pltpu.VMEM((2, TM, GMLP_WIDTH), BF16),
                        pltpu.VMEM((TM, D_MODEL), BF16),
                        pltpu.VMEM((2, TM, D_MODEL), F32)],
        compiler_params=pltpu.CompilerParams(dimension_semantics=("arbitrary",),
                                             vmem_limit_bytes=VMEM_LIMIT),
        name="mix",
    )(sink, xn, q, k, k, k, v, v, v, gu, vgn, sa, sb, ws, bs, wa, wb, wo, g1, b1, wr, br, *later_weights)


def _gather_copy(src_hbm, dst_buf, sem, src_row, dst_row, n_rows):
    return pltpu.make_async_copy(src_hbm.at[pl.ds(src_row, n_rows)], dst_buf.at[pl.ds(dst_row, n_rows)], sem)


def _start_gather(idx_ref, src_hbm, dst_buf, sem, n_rows):
    def issue(r, carry):
        src_row = pl.multiple_of(idx_ref[0, r] * ROW_PITCH, 8)
        dst_row = pl.multiple_of(r * ROW_PITCH, 8)
        _gather_copy(src_hbm, dst_buf, sem, src_row, dst_row, n_rows).start()
        return carry

    lax.fori_loop(0, TM, issue, 0, unroll=8)


def _wait_gather(src_hbm, dst_buf, sem, n_rows):
    _gather_copy(src_hbm, dst_buf, sem, 0, 0, TM * n_rows).wait()


def _idx_specs(n_tiles, lookahead=1):
    def spec(tile_of_step):
        return pl.BlockSpec((None, 1, TM), lambda i, *_: (jnp.minimum(tile_of_step(i), n_tiles - 1), 0, 0),
                            memory_space=pltpu.SMEM)

    return [spec(lambda i, t=t: t) for t in range(lookahead)] + [spec(lambda i: i + lookahead)]


def _unsort_kernel(idx0_ref, idxn_ref, ys_hbm, g2_ref, b2_ref, out_ref, buf, sem):
    i = pl.program_id(0)
    slot = i % 2

    @pl.when(i == 0)
    def _():
        _start_gather(idx0_ref, ys_hbm, buf.at[0], sem.at[0], X_ROWS)

    @pl.when(i + 1 < pl.num_programs(0))
    def _():
        _start_gather(idxn_ref, ys_hbm, buf.at[1 - slot], sem.at[1 - slot], X_ROWS)

    cur = buf.at[slot]
    _wait_gather(ys_hbm, cur, sem.at[slot], X_ROWS)
    for r in range(X_ROWS):
        out_ref[:, r * LANES:(r + 1) * LANES] = _load_token_major(cur, r)
    out_ref[...] = _layer_norm(out_ref[...], g2_ref[...], b2_ref[...])


def _unsort_call(idx, ysorted, g2, b2, name):
    n_tiles = idx.shape[0] // TM
    idx = idx.reshape(n_tiles, 1, TM)
    first, nxt = _idx_specs(n_tiles)
    return pl.pallas_call(
        _unsort_kernel,
        out_shape=jax.ShapeDtypeStruct((n_tiles * TM, D_MODEL), F32),
        grid=(n_tiles,),
        in_specs=[first, nxt, pl.BlockSpec(memory_space=pl.ANY),
                  _const_spec((1, D_MODEL)), _const_spec((1, D_MODEL))],
        out_specs=pl.BlockSpec((TM, D_MODEL), lambda i: (i, 0)),
        scratch_shapes=[pltpu.VMEM((2, TM * ROW_PITCH, LANES), F32), pltpu.SemaphoreType.DMA((2,))],
        compiler_params=pltpu.CompilerParams(dimension_semantics=("arbitrary",)),
        name=name,
    )(idx, idx, ysorted, g2, b2)


def _moe_kernel(elo_ref, ehi_ref, nvalid_ref, *refs):
    del elo_ref, ehi_ref
    first_idx_refs, idxn_ref = refs[:MOE_LOOKAHEAD], refs[MOE_LOOKAHEAD]
    x1t_hbm, wg_lo, wu_lo, wd_lo, wg_hi, wu_hi, wd_hi, out_ref, buf, sem, xb_scr = refs[MOE_LOOKAHEAD + 1:]
    i = pl.program_id(0)
    n = pl.num_programs(0)
    slot = i % MOE_BUFFERS

    def gather_if_valid(tile, idx_ref, when):
        dst = tile % MOE_BUFFERS

        @pl.when(jnp.logical_and(when, nvalid_ref[jnp.minimum(tile, n - 1)] > 0))
        def _():
            _start_gather(idx_ref, x1t_hbm, buf.at[dst], sem.at[dst], ROW_PITCH)

    for tile, idx_ref in enumerate(first_idx_refs):
        gather_if_valid(tile, idx_ref, jnp.logical_and(i == 0, tile < n))
    gather_if_valid(i + MOE_LOOKAHEAD, idxn_ref, i + MOE_LOOKAHEAD < n)

    @pl.when(nvalid_ref[i] > 0)
    def _():
        cur = buf.at[slot]
        _wait_gather(x1t_hbm, cur, sem.at[slot], ROW_PITCH)
        for r in range(X_ROWS):
            xb_scr[:, r * LANES:(r + 1) * LANES] = _load_token_major(cur, r).astype(BF16)
        route = _load_token_major(cur, X_ROWS)
        xb = xb_scr[...]

        def hidden(wg, wu, c):
            gate = jnp.dot(xb, wg[...], preferred_element_type=F32)
            up = jnp.dot(xb, wu[...], preferred_element_type=F32)
            return ((jax.nn.silu(gate) * up) * c).astype(BF16)

        h_lo = hidden(wg_lo, wu_lo, route[:, 0:1])
        h_hi = hidden(wg_hi, wu_hi, route[:, 1:2])
        for c in range(0, D_MODEL, COL_CHUNK):
            y = (jnp.dot(h_lo, wd_lo[:, c:c + COL_CHUNK], preferred_element_type=F32)
                 + jnp.dot(h_hi, wd_hi[:, c:c + COL_CHUNK], preferred_element_type=F32))
            for r in range(c // LANES, (c + COL_CHUNK) // LANES):
                out_ref[pl.ds(r, TM, stride=ROW_PITCH), :] = (DEEPNORM_ALPHA * _load_token_major(cur, r)
                                                              + y[:, r * LANES - c:(r + 1) * LANES - c])
        for r in range(X_ROWS, ROW_PITCH):
            out_ref[pl.ds(r, TM, stride=ROW_PITCH), :] = jnp.zeros((TM, LANES), F32)

    @pl.when(nvalid_ref[i] == 0)
    def _():
        out_ref[...] = jnp.zeros(out_ref.shape, out_ref.dtype)


def _moe_call(e_lo, e_hi, nvalid, src, x1t, wg, wu, wd):
    n_tiles = src.shape[0] // TM
    src = src.reshape(n_tiles, 1, TM)
    idx_specs = _idx_specs(n_tiles, MOE_LOOKAHEAD)

    def w_spec(shape, which):
        def imap(i, elo, ehi, nv):
            return ((elo, ehi)[which][i], 0, 0)
        return pl.BlockSpec((None,) + shape, imap)

    up_shape, down_shape = (D_MODEL, EXPERT_FF), (EXPERT_FF, D_MODEL)
    grid_spec = pltpu.PrefetchScalarGridSpec(
        num_scalar_prefetch=3,
        grid=(n_tiles,),
        in_specs=idx_specs + [pl.BlockSpec(memory_space=pl.ANY),
                              w_spec(up_shape, 0), w_spec(up_shape, 0), w_spec(down_shape, 0),
                              w_spec(up_shape, 1), w_spec(up_shape, 1), w_spec(down_shape, 1)],
        out_specs=pl.BlockSpec((TM * ROW_PITCH, LANES), lambda i, *_: (i, 0)),
        scratch_shapes=[pltpu.VMEM((MOE_BUFFERS, TM * ROW_PITCH, LANES), F32),
                        pltpu.SemaphoreType.DMA((MOE_BUFFERS,)),
                        pltpu.VMEM((TM, D_MODEL), BF16)],
    )
    return pl.pallas_call(
        _moe_kernel,
        out_shape=jax.ShapeDtypeStruct((n_tiles * TM * ROW_PITCH, LANES), F32),
        grid_spec=grid_spec,
        compiler_params=pltpu.CompilerParams(dimension_semantics=("arbitrary",),
                                             vmem_limit_bytes=VMEM_LIMIT),
        name="moe",
    )(e_lo, e_hi, nvalid, *([src] * len(idx_specs)), x1t, wg, wu, wd, wg, wu, wd)


_PAIR_LO = np.array([0, 0, 0, 1, 1, 2], np.int32)
_PAIR_HI = np.array([1, 2, 3, 2, 3, 3], np.int32)


def _bucket_layout(bucket, n_slots):
    t_all = bucket.shape[0]
    n_tiles = n_slots // TM
    rows = t_all // LANES
    onehot = (bucket.reshape(rows, LANES, 1) == jnp.arange(N_BUCKETS, dtype=jnp.int32)).astype(F32)
    earlier = (jnp.arange(LANES)[:, None] > jnp.arange(LANES)[None, :]).astype(F32)
    within = jnp.einsum("ts,rsb->rtb", earlier, onehot)
    row_total = jnp.sum(onehot, axis=1)
    row_start = jnp.cumsum(row_total, axis=0) - row_total
    counts = jnp.sum(row_total, axis=0).astype(jnp.int32)
    padded = ((counts + TM - 1) // TM) * TM
    ends = jnp.cumsum(padded)
    starts = ends - padded
    slot = within + (row_start + starts.astype(F32))[:, None, :]
    dest = jnp.sum(slot * onehot, axis=-1).reshape(t_all).astype(jnp.int32)
    src = jnp.zeros((n_slots,), jnp.int32).at[dest].set(jnp.arange(t_all, dtype=jnp.int32),
                                                       unique_indices=True, mode="promise_in_bounds")
    tile_start = jnp.arange(n_tiles, dtype=jnp.int32) * TM
    owner_start = jnp.minimum(tile_start, jnp.maximum(ends[-1:] - TM, 0))[:, None]
    owner = jnp.logical_and(owner_start >= starts[None, :], owner_start < ends[None, :]).astype(jnp.int32)
    remaining = counts[None, :] - (tile_start[:, None] - starts[None, :])
    nvalid = jnp.sum(owner * jnp.clip(remaining, 0, TM), axis=1)
    bucket_ids = np.arange(N_BUCKETS)
    first_expert = (bucket_ids // N_PAIRS) * EXPERTS_PER_GROUP
    e_lo = jnp.sum(owner * jnp.asarray(first_expert + _PAIR_LO[bucket_ids % N_PAIRS], jnp.int32)[None, :], axis=1)
    e_hi = jnp.sum(owner * jnp.asarray(first_expert + _PAIR_HI[bucket_ids % N_PAIRS], jnp.int32)[None, :], axis=1)
    return src, dest, e_lo.astype(jnp.int32), e_hi.astype(jnp.int32), nvalid.astype(jnp.int32)


def _layer(xp, xs, in_ln_g, in_ln_b, w_in, attn_sink, gmlp_w_s, gmlp_b_s, gmlp_ln_g, gmlp_ln_b,
           w_attn_branch, w_gmlp_branch, w_out, ln1_g, ln1_b,
           router_w_group, router_b_group, router_w_expert, router_b_expert,
           w_expert_gate, w_expert_up, w_expert_down, ln2_g, ln2_b):
    t_p, t_s = xp.shape[0], xs.shape[0]
    t_all = t_p + t_s
    row = lambda p: p.reshape(1, -1).astype(F32)

    rows_of = lambda w: w.astype(F32).reshape(-1, w.shape[-1])
    xn, q, k, v, gu, vgn, sa, sb, wa, wb, wo, wd = _proj_call(
        xp, xs, row(in_ln_g), row(in_ln_b), w_in.astype(BF16), gmlp_ln_g.astype(F32), gmlp_ln_b.astype(F32),
        [rows_of(w_attn_branch), rows_of(w_gmlp_branch), rows_of(w_out), rows_of(w_expert_down)])

    wr = jnp.concatenate([router_w_group, router_w_expert], axis=1).astype(F32)
    wr = jnp.pad(wr, ((0, 0), (0, LANES - wr.shape[1])))
    wr_hi = wr.astype(BF16)
    wr_lo = (wr - wr_hi.astype(F32)).astype(BF16)
    br = jnp.pad(jnp.concatenate([router_b_group, router_b_expert]).astype(F32),
                 (0, LANES - N_EXPERT_GROUPS - N_EXPERTS)).reshape(1, LANES)
    bs = jnp.broadcast_to(gmlp_b_s.astype(F32)[:, :, None], (GMLP_GROUPS, BLOCK, BLOCK))

    x1t, routed, wg, wu = _mix_call(t_p, attn_sink.astype(F32), xn,
                                    q, k, v, gu, vgn, sa, sb, gmlp_w_s.astype(BF16), bs, wa, wb, wo,
                                    row(ln1_g), row(ln1_b), jnp.concatenate([wr_hi, wr_lo], axis=1), br,
                                    [rows_of(w_expert_gate), rows_of(w_expert_up)])

    n_slots = t_all + N_BUCKETS * TM
    bucket = routed[2].astype(jnp.int32)
    src, dest, e_lo, e_hi, nvalid = _bucket_layout(bucket, n_slots)
    zsorted = _moe_call(e_lo, e_hi, nvalid, src, x1t, wg.reshape(w_expert_gate.shape),
                        wu.reshape(w_expert_up.shape), wd.reshape(w_expert_down.shape))
    g2, b2 = row(ln2_g), row(ln2_b)
    return (_unsort_call(dest[:t_p], zsorted, g2, b2, "unsort_prompt"),
            _unsort_call(dest[t_p:], zsorted, g2, b2, "unsort_sample"))


def kernel(x_prompt, x_sample, in_ln_g, in_ln_b, w_in, attn_sink, gmlp_w_s, gmlp_b_s, gmlp_ln_g, gmlp_ln_b,
           w_attn_branch, w_gmlp_branch, w_out, ln1_g, ln1_b,
           router_w_group, router_b_group, router_w_expert, router_b_expert,
           w_expert_gate, w_expert_up, w_expert_down, ln2_g, ln2_b):
    bp, sp, d = x_prompt.shape
    bs, ss, _ = x_sample.shape
    assert bp == 1 and bs == 1 and d == D_MODEL and sp % TM == 0 and ss % TM == 0
    assert w_in.shape[0] == 1, "one layer"
    yp, ys = _layer(x_prompt.reshape(sp, d), x_sample.reshape(ss, d), in_ln_g, in_ln_b, w_in[0], attn_sink[0],
                    gmlp_w_s[0], gmlp_b_s[0], gmlp_ln_g[0], gmlp_ln_b[0],
                    w_attn_branch[0], w_gmlp_branch[0], w_out[0], ln1_g[0], ln1_b[0],
                    router_w_group[0], router_b_group[0], router_w_expert[0], router_b_expert[0],
                    w_expert_gate[0], w_expert_up[0], w_expert_down[0], ln2_g[0], ln2_b[0])
    return yp.reshape(1, sp, d), ys.reshape(1, ss, d)
```

```python
import functools

import numpy as np
import jax
import jax.numpy as jnp
from jax import lax
from jax.experimental import pallas as pl
from jax.experimental.pallas import tpu as pltpu

F32 = jnp.float32
BF16 = jnp.bfloat16

D_MODEL = 2048
HEAD_DIM = 128
N_Q_HEADS = 8
N_KV_HEADS = 2
Q_PER_KV = N_Q_HEADS // N_KV_HEADS
ATTN_WIDTH = N_Q_HEADS * HEAD_DIM
KV_WIDTH = N_KV_HEADS * HEAD_DIM
WINDOW = 128
BLOCK = 128
GMLP_WIDTH = D_MODEL // 2
GMLP_GROUPS = 8
GMLP_GROUP_DIM = GMLP_WIDTH // GMLP_GROUPS
N_EXPERT_GROUPS = 4
EXPERTS_PER_GROUP = 4
N_EXPERTS = N_EXPERT_GROUPS * EXPERTS_PER_GROUP
EXPERT_FF = 512
LN_EPS = 1e-5
DEEPNORM_ALPHA = 2.0 ** 0.25
NEG_INF = -1e9
ATTN_SCALE = HEAD_DIM ** -0.5

_C_Q = 0
_C_K = _C_Q + ATTN_WIDTH
_C_V = _C_K + KV_WIDTH
_C_U = _C_V + KV_WIDTH
_C_VG = _C_U + GMLP_WIDTH
_C_GA = _C_VG + GMLP_WIDTH
_C_GB = _C_GA + D_MODEL
IN_COLS = _C_GB + D_MODEL

LANES = 128
N_PAIRS = 6
N_BUCKETS = N_EXPERT_GROUPS * N_PAIRS
X_ROWS = D_MODEL // LANES
ROW_PITCH = 24
TM = 256
COL_CHUNK = 512
MIX_CHUNK = 256
PROJ_CHUNK = 256
MOE_LOOKAHEAD = 3
MOE_BUFFERS = MOE_LOOKAHEAD + 1
VMEM_LIMIT = 60 * 1024 * 1024

_SLOPES = [float(2.0 ** (-8.0 * (h + 1) / N_Q_HEADS)) for h in range(N_Q_HEADS)]


def _layer_norm(x, g, b):
    mu = jnp.mean(x, axis=-1, keepdims=True)
    xc = x - mu
    var = jnp.mean(xc * xc, axis=-1, keepdims=True)
    return xc * lax.rsqrt(var + LN_EPS) * g + b


def _store_token_major(ref, x, pad_from):
    for r in range(X_ROWS):
        ref[pl.ds(r, TM, stride=ROW_PITCH), :] = x[:, r * LANES:(r + 1) * LANES]
    for r in range(pad_from, ROW_PITCH):
        ref[pl.ds(r, TM, stride=ROW_PITCH), :] = jnp.zeros((TM, LANES), ref.dtype)


def _load_token_major(ref, r):
    return ref[pl.ds(r, TM, stride=ROW_PITCH), :]


def _emit_interleaved(*streams):
    order = []
    for s, (_, costs) in enumerate(streams):
        done = 0.0
        for c in costs:
            order.append(((done + c / 2) / sum(costs), s))
            done += c
    for _, s in sorted(order):
        next(streams[s][0])
    for gen, _ in streams:
        assert next(gen, "done") == "done", "stream has more units than declared"


def _cast_plan(weights, n_steps):
    steps = 1 << (n_steps.bit_length() - 1)
    steps = min([steps] + [w.shape[0] // 16 for w in weights])
    specs = [pl.BlockSpec((w.shape[0] // steps, w.shape[1]), lambda i, *_: (jnp.minimum(i, steps - 1), 0))
             for w in weights]
    return steps, specs


def _cast_side_job(steps, src_refs, dst_refs):
    @pl.when(pl.program_id(0) < steps)
    def _():
        for src, dst in zip(src_refs, dst_refs):
            dst[...] = src[...].astype(dst.dtype)


def _const_spec(shape):
    nd = len(shape)
    return pl.BlockSpec(shape, lambda i, *_: (0,) * nd, pipeline_mode=pl.Buffered(1))


def _proj_kernel(n_prompt_tiles, cast_steps, n_cast, *refs):
    xp_ref, xs_ref, g0_ref, b0_ref, w_ref, lg_ref, lb_ref = refs[:7]
    cast_src = refs[7:7 + n_cast]
    xn_ref, q_ref, k_ref, v_ref, gu_ref, vgn_ref, sa_ref, sb_ref = refs[7 + n_cast:15 + n_cast]
    cast_dst = refs[15 + n_cast:15 + 2 * n_cast]
    xn_scr = refs[15 + 2 * n_cast]
    i = pl.program_id(0)
    _cast_side_job(cast_steps, cast_src, cast_dst)

    @pl.when(i < n_prompt_tiles)
    def _():
        xn_ref[...] = _layer_norm(xp_ref[...], g0_ref[...], b0_ref[...])

    @pl.when(i >= n_prompt_tiles)
    def _():
        xn_ref[...] = _layer_norm(xs_ref[...], g0_ref[...], b0_ref[...])

    xn_scr[...] = xn_ref[...].astype(BF16)
    xn = xn_scr[...]

    def proj(c0, width):
        return jnp.dot(xn, w_ref[:, c0:c0 + width], preferred_element_type=F32)

    for c in range(0, ATTN_WIDTH, COL_CHUNK):
        q_ref[:, c:c + COL_CHUNK] = proj(_C_Q + c, COL_CHUNK).astype(BF16)
    kv = proj(_C_K, 2 * KV_WIDTH)
    k_ref[...] = kv[:, :KV_WIDTH].astype(BF16)
    v_ref[...] = kv[:, KV_WIDTH:].astype(BF16)
    for c in range(0, GMLP_WIDTH, COL_CHUNK):
        gu_ref[:, c:c + COL_CHUNK] = jax.nn.gelu(proj(_C_U + c, COL_CHUNK)).astype(BF16)
    for c in range(0, GMLP_WIDTH, COL_CHUNK):
        vg = jax.nn.gelu(proj(_C_VG + c, COL_CHUNK))
        for j in range(COL_CHUNK // GMLP_GROUP_DIM):
            grp = c // GMLP_GROUP_DIM + j
            blk = vg[:, j * GMLP_GROUP_DIM:(j + 1) * GMLP_GROUP_DIM]
            y = _layer_norm(blk, lg_ref[grp:grp + 1, :], lb_ref[grp:grp + 1, :])
            vgn_ref[:, grp * GMLP_GROUP_DIM:(grp + 1) * GMLP_GROUP_DIM] = y.astype(BF16)
    for c in range(0, D_MODEL, COL_CHUNK):
        sa_ref[:, c:c + COL_CHUNK] = jax.nn.sigmoid(proj(_C_GA + c, COL_CHUNK)).astype(BF16)
    for c in range(0, D_MODEL, COL_CHUNK):
        sb_ref[:, c:c + COL_CHUNK] = jax.nn.sigmoid(proj(_C_GB + c, COL_CHUNK)).astype(BF16)


def _proj_call(xp, xs, g0, b0, w_in, lg, lb, later_weights):
    n_p, n_s = xp.shape[0] // TM, xs.shape[0] // TM
    t_all = xp.shape[0] + xs.shape[0]
    cast_steps, cast_specs = _cast_plan(later_weights, n_p + n_s)
    _, cast_out_specs = _cast_plan(later_weights, n_p + n_s)
    xp_spec = pl.BlockSpec((TM, D_MODEL), lambda i: (jnp.minimum(i, n_p - 1), 0))
    xs_spec = pl.BlockSpec((TM, D_MODEL), lambda i: (jnp.maximum(i - n_p, 0), 0))

    def tok(width):
        return pl.BlockSpec((TM, width), lambda i: (i, 0))

    widths = (ATTN_WIDTH, KV_WIDTH, KV_WIDTH, GMLP_WIDTH, GMLP_WIDTH, D_MODEL, D_MODEL)
    return pl.pallas_call(
        functools.partial(_proj_kernel, n_p, cast_steps, len(later_weights)),
        out_shape=([jax.ShapeDtypeStruct((t_all, D_MODEL), F32)]
                   + [jax.ShapeDtypeStruct((t_all, w), BF16) for w in widths]
                   + [jax.ShapeDtypeStruct(w.shape, BF16) for w in later_weights]),
        grid=(n_p + n_s,),
        in_specs=[xp_spec, xs_spec, _const_spec((1, D_MODEL)), _const_spec((1, D_MODEL)),
                  _const_spec((D_MODEL, IN_COLS)),
                  _const_spec((GMLP_GROUPS, GMLP_GROUP_DIM)), _const_spec((GMLP_GROUPS, GMLP_GROUP_DIM))]
        + cast_specs,
        out_specs=[tok(D_MODEL)] + [tok(w) for w in widths] + cast_out_specs,
        scratch_shapes=[pltpu.VMEM((TM, D_MODEL), BF16)],
        compiler_params=pltpu.CompilerParams(dimension_semantics=("arbitrary",),
                                             vmem_limit_bytes=VMEM_LIMIT),
        name="proj",
    )(xp, xs, g0, b0, w_in, lg, lb, *later_weights)


def _route(logits_t):
    row = lambda j: logits_t[j:j + 1, :]
    gl = [row(j) for j in range(N_EXPERT_GROUPS)]
    gmax, gidx = gl[0], jnp.zeros(gl[0].shape, jnp.int32)
    for j in range(1, N_EXPERT_GROUPS):
        better = gl[j] > gmax
        gmax = jnp.where(better, gl[j], gmax)
        gidx = jnp.where(better, j, gidx)
    gsum = jnp.exp(gl[0] - gmax)
    for j in range(1, N_EXPERT_GROUPS):
        gsum = gsum + jnp.exp(gl[j] - gmax)
    p_group = 1.0 / gsum

    ig = []
    for e in range(EXPERTS_PER_GROUP):
        v = row(N_EXPERT_GROUPS + (N_EXPERT_GROUPS - 1) * EXPERTS_PER_GROUP + e)
        for g in range(N_EXPERT_GROUPS - 2, -1, -1):
            v = jnp.where(gidx == g, row(N_EXPERT_GROUPS + g * EXPERTS_PER_GROUP + e), v)
        ig.append(v)
    v1, i1 = ig[0], jnp.zeros(ig[0].shape, jnp.int32)
    for e in range(1, EXPERTS_PER_GROUP):
        better = ig[e] > v1
        v1 = jnp.where(better, ig[e], v1)
        i1 = jnp.where(better, e, i1)
    v2 = jnp.where(i1 == 0, ig[1], ig[0])
    i2 = jnp.where(i1 == 0, 1, 0).astype(jnp.int32)
    for e in range(1, EXPERTS_PER_GROUP):
        better = jnp.logical_and(i1 != e, ig[e] > v2)
        v2 = jnp.where(better, ig[e], v2)
        i2 = jnp.where(better, e, i2)
    ev = jnp.exp(v2 - v1)
    ssum = 1.0 + ev
    w1 = 1.0 / ssum
    w2 = ev / ssum
    first_lo = i1 < i2
    lo = jnp.minimum(i1, i2)
    hi = jnp.maximum(i1, i2)
    c_lo = jnp.where(first_lo, w1, w2) * p_group
    c_hi = jnp.where(first_lo, w2, w1) * p_group
    pair = jnp.where(lo == 0, hi - 1, jnp.where(lo == 1, hi + 1, N_PAIRS - 1))
    bucket = gidx * N_PAIRS + pair
    return c_lo, c_hi, bucket.astype(F32)


def _mix_kernel(n_tiles, seq_edges_first, seq_edges_last, cast_steps, n_cast, *refs):
    (sink_ref, xn_ref, q_ref, kp_ref, km_ref, kn_ref, vp_ref, vm_ref, vn_ref,
     gu_ref, vgn_ref, sa_ref, sb_ref, ws_ref, bs_ref,
     wa_ref, wb_ref, wo_ref, g1_ref, b1_ref, wr_ref, br_ref) = refs[:22]
    cast_src = refs[22:22 + n_cast]
    x1t_ref, rt_ref = refs[22 + n_cast:24 + n_cast]
    cast_dst = refs[24 + n_cast:24 + 2 * n_cast]
    kf_scr, vf_scr, a_scr, sg_scr, mg_scr, mix_scr = refs[24 + 2 * n_cast:]
    _cast_side_job(cast_steps, cast_src, cast_dst)
    i = pl.program_id(0)
    nblk = TM // BLOCK
    cur = i % 2

    @pl.when(i == 0)
    def _():
        a_scr[1] = jnp.zeros(a_scr.shape[1:], a_scr.dtype)
        sg_scr[1] = jnp.zeros(sg_scr.shape[1:], sg_scr.dtype)
        mix_scr[0] = jnp.zeros(mix_scr.shape[1:], mix_scr.dtype)

    def stage_a():
        tile = jnp.minimum(i, n_tiles - 1)
        kf_scr[0:BLOCK, :] = kp_ref[...]
        kf_scr[BLOCK:BLOCK + TM, :] = km_ref[...]
        kf_scr[BLOCK + TM:, :] = kn_ref[...]
        vf_scr[0:BLOCK, :] = vp_ref[...]
        vf_scr[BLOCK:BLOCK + TM, :] = vm_ref[...]
        vf_scr[BLOCK + TM:, :] = vn_ref[...]
        first_blk = tile * nblk
        last_blk = tile * nblk + nblk - 1
        has_prev = jnp.logical_and(*[first_blk != e for e in seq_edges_first])
        has_next = jnp.logical_and(*[last_blk != e for e in seq_edges_last])
        kj = lax.broadcasted_iota(jnp.int32, (BLOCK, 3 * BLOCK), 1)
        qi = lax.broadcasted_iota(jnp.int32, (BLOCK, 3 * BLOCK), 0)
        dist = jnp.abs(kj - BLOCK - qi)
        in_window = dist <= WINDOW
        dist_f = dist.astype(F32)
        lo_key = jnp.where(has_prev, 0, BLOCK)
        hi_key = jnp.where(has_next, 3 * BLOCK, 2 * BLOCK)
        yield
        for j in range(nblk):
            mask = in_window
            if j == 0:
                mask = jnp.logical_and(mask, kj >= lo_key)
            if j == nblk - 1:
                mask = jnp.logical_and(mask, kj < hi_key)
            r0 = j * BLOCK
            for kvh in range(N_KV_HEADS):
                c0 = kvh * HEAD_DIM
                kb = kf_scr[r0:r0 + 3 * BLOCK, c0:c0 + HEAD_DIM]
                vb = vf_scr[r0:r0 + 3 * BLOCK, c0:c0 + HEAD_DIM]
                heads = [kvh * Q_PER_KV + g for g in range(Q_PER_KV)]
                qs = jnp.concatenate(
                    [q_ref[r0:r0 + BLOCK, h * HEAD_DIM:(h + 1) * HEAD_DIM] for h in heads], axis=0)
                s_all = lax.dot_general(qs, kb, (((1,), (1,)), ((), ())), preferred_element_type=F32)
                for g, h in enumerate(heads):
                    s = s_all[g * BLOCK:(g + 1) * BLOCK, :] * ATTN_SCALE
                    s = jnp.where(mask, s + dist_f * (-_SLOPES[h]), NEG_INF)
                    sink = sink_ref[h]
                    m = jnp.maximum(jnp.max(s, axis=-1, keepdims=True), sink)
                    p = jnp.exp(s - m)
                    denom = jnp.sum(p, axis=-1, keepdims=True) + jnp.exp(sink - m)
                    pn = (p * (1.0 / denom)).astype(BF16)
                    o = jnp.dot(pn, vb, preferred_element_type=F32)
                    a_scr[cur, r0:r0 + BLOCK, h * HEAD_DIM:(h + 1) * HEAD_DIM] = o.astype(BF16)
                    yield
            for grp in range(GMLP_GROUPS):
                c0 = grp * GMLP_GROUP_DIM
                sp = jnp.dot(ws_ref[grp], vgn_ref[r0:r0 + BLOCK, c0:c0 + GMLP_GROUP_DIM],
                             preferred_element_type=F32) + bs_ref[grp]
                u = gu_ref[r0:r0 + BLOCK, c0:c0 + GMLP_GROUP_DIM].astype(F32)
                sg_scr[cur, r0:r0 + BLOCK, c0:c0 + GMLP_GROUP_DIM] = (u * sp).astype(BF16)
            yield

    def stage_b():
        a = a_scr[1 - cur]
        sg = sg_scr[1 - cur]
        for c in range(0, D_MODEL, PROJ_CHUNK):
            ma = jnp.dot(a, wa_ref[:, c:c + PROJ_CHUNK], preferred_element_type=F32)
            mb = jnp.dot(sg, wb_ref[:, c:c + PROJ_CHUNK], preferred_element_type=F32)
            merged = (sa_ref[:, c:c + PROJ_CHUNK].astype(F32) * ma
                      + sb_ref[:, c:c + PROJ_CHUNK].astype(F32) * mb)
            mg_scr[:, c:c + PROJ_CHUNK] = merged.astype(BF16)
            yield
        mg = mg_scr[...]
        for c in range(0, D_MODEL, PROJ_CHUNK):
            mix = jnp.dot(mg, wo_ref[:, c:c + PROJ_CHUNK], preferred_element_type=F32)
            mix_scr[1 - cur, :, c:c + PROJ_CHUNK] = DEEPNORM_ALPHA * xn_ref[:, c:c + PROJ_CHUNK] + mix
            yield

    def stage_c():
        mu = jnp.mean(mix_scr[cur], axis=-1, keepdims=True)
        yield
        zc = mix_scr[cur] - mu
        inv = lax.rsqrt(jnp.mean(zc * zc, axis=-1, keepdims=True) + LN_EPS)
        yield
        r = None
        for c in range(0, D_MODEL, MIX_CHUNK):
            x1 = ((mix_scr[cur, :, c:c + MIX_CHUNK] - mu) * inv * g1_ref[:, c:c + MIX_CHUNK]
                  + b1_ref[:, c:c + MIX_CHUNK])
            for k in range(MIX_CHUNK // LANES):
                x1t_ref[pl.ds(c // LANES + k, TM, stride=ROW_PITCH), :] = x1[:, k * LANES:(k + 1) * LANES]
            x_hi = x1.astype(BF16)
            x_lo = (x1 - x_hi.astype(F32)).astype(BF16)
            part = (jnp.dot(x_hi, wr_ref[c:c + MIX_CHUNK, :], preferred_element_type=F32)
                    + jnp.dot(x_lo, wr_ref[c:c + MIX_CHUNK, :], preferred_element_type=F32))
            r = part if r is None else r + part
            yield
        logits = r[:, :LANES] + r[:, LANES:] + br_ref[...]
        c_lo, c_hi, bucket = _route(logits.T)
        routed = jnp.concatenate([c_lo, c_hi, bucket, jnp.zeros((5, TM), F32)], axis=0)
        rt_ref[...] = routed
        padded = jnp.concatenate([routed, jnp.zeros((LANES - 8, TM), F32)], axis=0)
        x1t_ref[pl.ds(X_ROWS, TM, stride=ROW_PITCH), :] = padded.T
        for k in range(X_ROWS + 1, ROW_PITCH):
            x1t_ref[pl.ds(k, TM, stride=ROW_PITCH), :] = jnp.zeros((TM, LANES), F32)
        yield

    n_chunks = D_MODEL // MIX_CHUNK
    _emit_interleaved((stage_b(), [512] * (2 * D_MODEL // PROJ_CHUNK)),
                      (stage_a(), [100] + ([600] * N_Q_HEADS + [500]) * nblk),
                      (stage_c(), [1000, 1000] + [550] * n_chunks + [1000]))


def _mix_call(t_prompt, sink, xn, q, k, v, gu, vgn, sa, sb, ws, bs, wa, wb, wo, g1, b1, wr, br, later_weights):
    t_all = xn.shape[0]
    nblk = TM // BLOCK
    blk_p, blk_all = t_prompt // BLOCK, t_all // BLOCK
    n_tiles = t_all // TM
    front = lambda i: jnp.minimum(i, n_tiles - 1)
    back = lambda i: jnp.clip(i - 1, 0, n_tiles - 1)
    last = lambda i: jnp.maximum(i - 2, 0)

    def tok(width, which):
        return pl.BlockSpec((TM, width), lambda i: (which(i), 0))

    prev_spec = pl.BlockSpec((BLOCK, KV_WIDTH), lambda i: (jnp.maximum(front(i) * nblk - 1, 0), 0))
    next_spec = pl.BlockSpec((BLOCK, KV_WIDTH),
                             lambda i: (jnp.minimum((front(i) + 1) * nblk, blk_all - 1), 0))
    kv_spec = tok(KV_WIDTH, front)
    cast_steps, cast_specs = _cast_plan(later_weights, n_tiles + 2)
    _, cast_out_specs = _cast_plan(later_weights, n_tiles + 2)
    kernel = functools.partial(_mix_kernel, n_tiles, (0, blk_p), (blk_p - 1, blk_all - 1),
                               cast_steps, len(later_weights))
    return pl.pallas_call(
        kernel,
        out_shape=[jax.ShapeDtypeStruct((t_all * ROW_PITCH, LANES), F32), jax.ShapeDtypeStruct((8, t_all), F32)]
        + [jax.ShapeDtypeStruct(w.shape, BF16) for w in later_weights],
        grid=(n_tiles + 2,),
        in_specs=[pl.BlockSpec(memory_space=pltpu.SMEM),
                  tok(D_MODEL, back), tok(ATTN_WIDTH, front), prev_spec, kv_spec, next_spec, prev_spec, kv_spec, next_spec,
                  tok(GMLP_WIDTH, front), tok(GMLP_WIDTH, front), tok(D_MODEL, back), tok(D_MODEL, back),
                  _const_spec((GMLP_GROUPS, BLOCK, BLOCK)), _const_spec((GMLP_GROUPS, BLOCK, BLOCK)),
                  _const_spec((ATTN_WIDTH, D_MODEL)), _const_spec((GMLP_WIDTH, D_MODEL)),
                  _const_spec((D_MODEL, D_MODEL)), _const_spec((1, D_MODEL)), _const_spec((1, D_MODEL)),
                  _const_spec((D_MODEL, 2 * LANES)), _const_spec((1, LANES))] + cast_specs,
        out_specs=[pl.BlockSpec((TM * ROW_PITCH, LANES), lambda i: (last(i), 0)),
                   pl.BlockSpec((8, TM), lambda i: (0, last(i)))] + cast_out_specs,
        scratch_shapes=[pltpu.VMEM((TM + 2 * BLOCK, KV_WIDTH), BF16),
                        pltpu.VMEM((TM + 2 * BLOCK, KV_WIDTH), BF16),
                        pltpu.VMEM((2, TM, ATTN_WIDTH), BF16),
                        pltpu.VMEM((2, TM, GMLP_WIDTH), BF16),
                        pltpu.VMEM((TM, D_MODEL), BF16),
                        pltpu.VMEM((2, TM, D_MODEL), F32)],
        compiler_params=pltpu.CompilerParams(dimension_semantics=("arbitrary",),
                                             vmem_limit_bytes=VMEM_LIMIT),
        name="mix",
    )(sink, xn, q, k, k, k, v, v, v, gu, vgn, sa, sb, ws, bs, wa, wb, wo, g1, b1, wr, br, *later_weights)


def _gather_copy(src_hbm, dst_buf, sem, src_row, dst_row, n_rows):
    return pltpu.make_async_copy(src_hbm.at[pl.ds(src_row, n_rows)], dst_buf.at[pl.ds(dst_row, n_rows)], sem)


def _start_gather(idx_ref, src_hbm, dst_buf, sem, n_rows):
    def issue(r, carry):
        src_row = pl.multiple_of(idx_ref[0, r] * ROW_PITCH, 8)
        dst_row = pl.multiple_of(r * ROW_PITCH, 8)
        _gather_copy(src_hbm, dst_buf, sem, src_row, dst_row, n_rows).start()
        return carry

    lax.fori_loop(0, TM, issue, 0, unroll=8)


def _wait_gather(src_hbm, dst_buf, sem, n_rows):
    _gather_copy(src_hbm, dst_buf, sem, 0, 0, TM * n_rows).wait()


def _idx_specs(n_tiles, lookahead=1):
    def spec(tile_of_step):
        return pl.BlockSpec((None, 1, TM), lambda i, *_: (jnp.minimum(tile_of_step(i), n_tiles - 1), 0, 0),
                            memory_space=pltpu.SMEM)

    return [spec(lambda i, t=t: t) for t in range(lookahead)] + [spec(lambda i: i + lookahead)]


def _unsort_kernel(idx0_ref, idxn_ref, ys_hbm, g2_ref, b2_ref, out_ref, buf, sem):
    i = pl.program_id(0)
    slot = i % 2

    @pl.when(i == 0)
    def _():
        _start_gather(idx0_ref, ys_hbm, buf.at[0], sem.at[0], X_ROWS)

    @pl.when(i + 1 < pl.num_programs(0))
    def _():
        _start_gather(idxn_ref, ys_hbm, buf.at[1 - slot], sem.at[1 - slot], X_ROWS)

    cur = buf.at[slot]
    _wait_gather(ys_hbm, cur, sem.at[slot], X_ROWS)
    for r in range(X_ROWS):
        out_ref[:, r * LANES:(r + 1) * LANES] = _load_token_major(cur, r)
    out_ref[...] = _layer_norm(out_ref[...], g2_ref[...], b2_ref[...])


def _unsort_call(idx, ysorted, g2, b2, name):
    n_tiles = idx.shape[0] // TM
    idx = idx.reshape(n_tiles, 1, TM)
    first, nxt = _idx_specs(n_tiles)
    return pl.pallas_call(
        _unsort_kernel,
        out_shape=jax.ShapeDtypeStruct((n_tiles * TM, D_MODEL), F32),
        grid=(n_tiles,),
        in_specs=[first, nxt, pl.BlockSpec(memory_space=pl.ANY),
                  _const_spec((1, D_MODEL)), _const_spec((1, D_MODEL))],
        out_specs=pl.BlockSpec((TM, D_MODEL), lambda i: (i, 0)),
        scratch_shapes=[pltpu.VMEM((2, TM * ROW_PITCH, LANES), F32), pltpu.SemaphoreType.DMA((2,))],
        compiler_params=pltpu.CompilerParams(dimension_semantics=("arbitrary",)),
        name=name,
    )(idx, idx, ysorted, g2, b2)


def _moe_kernel(elo_ref, ehi_ref, nvalid_ref, *refs):
    del elo_ref, ehi_ref
    first_idx_refs, idxn_ref = refs[:MOE_LOOKAHEAD], refs[MOE_LOOKAHEAD]
    x1t_hbm, wg_lo, wu_lo, wd_lo, wg_hi, wu_hi, wd_hi, out_ref, buf, sem, xb_scr = refs[MOE_LOOKAHEAD + 1:]
    i = pl.program_id(0)
    n = pl.num_programs(0)
    slot = i % MOE_BUFFERS

    def gather_if_valid(tile, idx_ref, when):
        dst = tile % MOE_BUFFERS

        @pl.when(jnp.logical_and(when, nvalid_ref[jnp.minimum(tile, n - 1)] > 0))
        def _():
            _start_gather(idx_ref, x1t_hbm, buf.at[dst], sem.at[dst], ROW_PITCH)

    for tile, idx_ref in enumerate(first_idx_refs):
        gather_if_valid(tile, idx_ref, jnp.logical_and(i == 0, tile < n))
    gather_if_valid(i + MOE_LOOKAHEAD, idxn_ref, i + MOE_LOOKAHEAD < n)

    @pl.when(nvalid_ref[i] > 0)
    def _():
        cur = buf.at[slot]
        _wait_gather(x1t_hbm, cur, sem.at[slot], ROW_PITCH)
        for r in range(X_ROWS):
            xb_scr[:, r * LANES:(r + 1) * LANES] = _load_token_major(cur, r).astype(BF16)
        route = _load_token_major(cur, X_ROWS)
        xb = xb_scr[...]

        def hidden(wg, wu, c):
            gate = jnp.dot(xb, wg[...], preferred_element_type=F32)
            up = jnp.dot(xb, wu[...], preferred_element_type=F32)
            return ((jax.nn.silu(gate) * up) * c).astype(BF16)

        h_lo = hidden(wg_lo, wu_lo, route[:, 0:1])
        h_hi = hidden(wg_hi, wu_hi, route[:, 1:2])
        for c in range(0, D_MODEL, COL_CHUNK):
            y = (jnp.dot(h_lo, wd_lo[:, c:c + COL_CHUNK], preferred_element_type=F32)
                 + jnp.dot(h_hi, wd_hi[:, c:c + COL_CHUNK], preferred_element_type=F32))
            for r in range(c // LANES, (c + COL_CHUNK) // LANES):
                out_ref[pl.ds(r, TM, stride=ROW_PITCH), :] = (DEEPNORM_ALPHA * _load_token_major(cur, r)
                                                              + y[:, r * LANES - c:(r + 1) * LANES - c])
        for r in range(X_ROWS, ROW_PITCH):
            out_ref[pl.ds(r, TM, stride=ROW_PITCH), :] = jnp.zeros((TM, LANES), F32)

    @pl.when(nvalid_ref[i] == 0)
    def _():
        out_ref[...] = jnp.zeros(out_ref.shape, out_ref.dtype)


def _moe_call(e_lo, e_hi, nvalid, src, x1t, wg, wu, wd):
    n_tiles = src.shape[0] // TM
    src = src.reshape(n_tiles, 1, TM)
    idx_specs = _idx_specs(n_tiles, MOE_LOOKAHEAD)

    def w_spec(shape, which):
        def imap(i, elo, ehi, nv):
            return ((elo, ehi)[which][i], 0, 0)
        return pl.BlockSpec((None,) + shape, imap)

    up_shape, down_shape = (D_MODEL, EXPERT_FF), (EXPERT_FF, D_MODEL)
    grid_spec = pltpu.PrefetchScalarGridSpec(
        num_scalar_prefetch=3,
        grid=(n_tiles,),
        in_specs=idx_specs + [pl.BlockSpec(memory_space=pl.ANY),
                              w_spec(up_shape, 0), w_spec(up_shape, 0), w_spec(down_shape, 0),
                              w_spec(up_shape, 1), w_spec(up_shape, 1), w_spec(down_shape, 1)],
        out_specs=pl.BlockSpec((TM * ROW_PITCH, LANES), lambda i, *_: (i, 0)),
        scratch_shapes=[pltpu.VMEM((MOE_BUFFERS, TM * ROW_PITCH, LANES), F32),
                        pltpu.SemaphoreType.DMA((MOE_BUFFERS,)),
                        pltpu.VMEM((TM, D_MODEL), BF16)],
    )
    return pl.pallas_call(
        _moe_kernel,
        out_shape=jax.ShapeDtypeStruct((n_tiles * TM * ROW_PITCH, LANES), F32),
        grid_spec=grid_spec,
        compiler_params=pltpu.CompilerParams(dimension_semantics=("arbitrary",),
                                             vmem_limit_bytes=VMEM_LIMIT),
        name="moe",
    )(e_lo, e_hi, nvalid, *([src] * len(idx_specs)), x1t, wg, wu, wd, wg, wu, wd)


_PAIR_LO = np.array([0, 0, 0, 1, 1, 2], np.int32)
_PAIR_HI = np.array([1, 2, 3, 2, 3, 3], np.int32)


def _bucket_layout(bucket, n_slots):
    t_all = bucket.shape[0]
    n_tiles = n_slots // TM
    rows = t_all // LANES
    onehot = (bucket.reshape(rows, LANES, 1) == jnp.arange(N_BUCKETS, dtype=jnp.int32)).astype(F32)
    earlier = (jnp.arange(LANES)[:, None] > jnp.arange(LANES)[None, :]).astype(F32)
    within = jnp.einsum("ts,rsb->rtb", earlier, onehot)
    row_total = jnp.sum(onehot, axis=1)
    row_start = jnp.cumsum(row_total, axis=0) - row_total
    counts = jnp.sum(row_total, axis=0).astype(jnp.int32)
    padded = ((counts + TM - 1) // TM) * TM
    ends = jnp.cumsum(padded)
    starts = ends - padded
    slot = within + (row_start + starts.astype(F32))[:, None, :]
    dest = jnp.sum(slot * onehot, axis=-1).reshape(t_all).astype(jnp.int32)
    order = jnp.argsort(bucket, stable=True).astype(jnp.int32)
    slot_ids = jnp.arange(n_slots, dtype=jnp.int32)[:, None]
    in_bucket = jnp.logical_and(slot_ids >= starts[None, :], slot_ids < ends[None, :]).astype(jnp.int32)
    offset = slot_ids - starts[None, :]
    compact = jnp.sum(in_bucket * (offset + (jnp.cumsum(counts) - counts)[None, :]), axis=1)
    real = jnp.sum(in_bucket * (offset < counts[None, :]).astype(jnp.int32), axis=1)
    src = jnp.where(real > 0, order[jnp.clip(compact, 0, t_all - 1)], 0)
    tile_start = jnp.arange(n_tiles, dtype=jnp.int32) * TM
    owner_start = jnp.minimum(tile_start, jnp.maximum(ends[-1:] - TM, 0))[:, None]
    owner = jnp.logical_and(owner_start >= starts[None, :], owner_start < ends[None, :]).astype(jnp.int32)
    remaining = counts[None, :] - (tile_start[:, None] - starts[None, :])
    nvalid = jnp.sum(owner * jnp.clip(remaining, 0, TM), axis=1)
    bucket_ids = np.arange(N_BUCKETS)
    first_expert = (bucket_ids // N_PAIRS) * EXPERTS_PER_GROUP
    e_lo = jnp.sum(owner * jnp.asarray(first_expert + _PAIR_LO[bucket_ids % N_PAIRS], jnp.int32)[None, :], axis=1)
    e_hi = jnp.sum(owner * jnp.asarray(first_expert + _PAIR_HI[bucket_ids % N_PAIRS], jnp.int32)[None, :], axis=1)
    return src, dest, e_lo.astype(jnp.int32), e_hi.astype(jnp.int32), nvalid.astype(jnp.int32)


def _layer(xp, xs, in_ln_g, in_ln_b, w_in, attn_sink, gmlp_w_s, gmlp_b_s, gmlp_ln_g, gmlp_ln_b,
           w_attn_branch, w_gmlp_branch, w_out, ln1_g, ln1_b,
           router_w_group, router_b_group, router_w_expert, router_b_expert,
           w_expert_gate, w_expert_up, w_expert_down, ln2_g, ln2_b):
    t_p, t_s = xp.shape[0], xs.shape[0]
    t_all = t_p + t_s
    row = lambda p: p.reshape(1, -1).astype(F32)

    rows_of = lambda w: w.astype(F32).reshape(-1, w.shape[-1])
    xn, q, k, v, gu, vgn, sa, sb, wa, wb, wo, wd = _proj_call(
        xp, xs, row(in_ln_g), row(in_ln_b), w_in.astype(BF16), gmlp_ln_g.astype(F32), gmlp_ln_b.astype(F32),
        [rows_of(w_attn_branch), rows_of(w_gmlp_branch), rows_of(w_out), rows_of(w_expert_down)])

    wr = jnp.concatenate([router_w_group, router_w_expert], axis=1).astype(F32)
    wr = jnp.pad(wr, ((0, 0), (0, LANES - wr.shape[1])))
    wr_hi = wr.astype(BF16)
    wr_lo = (wr - wr_hi.astype(F32)).astype(BF16)
    br = jnp.pad(jnp.concatenate([router_b_group, router_b_expert]).astype(F32),
                 (0, LANES - N_EXPERT_GROUPS - N_EXPERTS)).reshape(1, LANES)
    bs = jnp.broadcast_to(gmlp_b_s.astype(F32)[:, :, None], (GMLP_GROUPS, BLOCK, BLOCK))

    x1t, routed, wg, wu = _mix_call(t_p, attn_sink.astype(F32), xn,
                                    q, k, v, gu, vgn, sa, sb, gmlp_w_s.astype(BF16), bs, wa, wb, wo,
                                    row(ln1_g), row(ln1_b), jnp.concatenate([wr_hi, wr_lo], axis=1), br,
                                    [rows_of(w_expert_gate), rows_of(w_expert_up)])

    n_slots = t_all + N_BUCKETS * TM
    bucket = routed[2].astype(jnp.int32)
    src, dest, e_lo, e_hi, nvalid = _bucket_layout(bucket, n_slots)
    zsorted = _moe_call(e_lo, e_hi, nvalid, src, x1t, wg.reshape(w_expert_gate.shape),
                        wu.reshape(w_expert_up.shape), wd.reshape(w_expert_down.shape))
    g2, b2 = row(ln2_g), row(ln2_b)
    return (_unsort_call(dest[:t_p], zsorted, g2, b2, "unsort_prompt"),
            _unsort_call(dest[t_p:], zsorted, g2, b2, "unsort_sample"))


def kernel(x_prompt, x_sample, in_ln_g, in_ln_b, w_in, attn_sink, gmlp_w_s, gmlp_b_s, gmlp_ln_g, gmlp_ln_b,
           w_attn_branch, w_gmlp_branch, w_out, ln1_g, ln1_b,
           router_w_group, router_b_group, router_w_expert, router_b_expert,
           w_expert_gate, w_expert_up, w_expert_down, ln2_g, ln2_b):
    bp, sp, d = x_prompt.shape
    bs, ss, _ = x_sample.shape
    assert bp == 1 and bs == 1 and d == D_MODEL and sp % TM == 0 and ss % TM == 0
    assert w_in.shape[0] == 1, "one layer"
    yp, ys = _layer(x_prompt.reshape(sp, d), x_sample.reshape(ss, d), in_ln_g, in_ln_b, w_in[0], attn_sink[0],
                    gmlp_w_s[0], gmlp_b_s[0], gmlp_ln_g[0], gmlp_ln_b[0],
                    w_attn_branch[0], w_gmlp_branch[0], w_out[0], ln1_g[0], ln1_b[0],
                    router_w_group[0], router_b_group[0], router_w_expert[0], router_b_expert[0],
                    w_expert_gate[0], w_expert_up[0], w_expert_down[0], ln2_g[0], ln2_b[0])
    return yp.reshape(1, sp, d), ys.reshape(1, ss, d)
```

```python
import functools

import numpy as np
import jax
import jax.numpy as jnp
from jax import lax
from jax.experimental import pallas as pl
from jax.experimental.pallas import tpu as pltpu

F32 = jnp.float32
BF16 = jnp.bfloat16

D_MODEL = 2048
HEAD_DIM = 128
N_Q_HEADS = 8
N_KV_HEADS = 2
Q_PER_KV = N_Q_HEADS // N_KV_HEADS
ATTN_WIDTH = N_Q_HEADS * HEAD_DIM
KV_WIDTH = N_KV_HEADS * HEAD_DIM
WINDOW = 128
BLOCK = 128
GMLP_WIDTH = D_MODEL // 2
GMLP_GROUPS = 8
GMLP_GROUP_DIM = GMLP_WIDTH // GMLP_GROUPS
N_EXPERT_GROUPS = 4
EXPERTS_PER_GROUP = 4
N_EXPERTS = N_EXPERT_GROUPS * EXPERTS_PER_GROUP
EXPERT_FF = 512
LN_EPS = 1e-5
DEEPNORM_ALPHA = 2.0 ** 0.25
NEG_INF = -1e9
ATTN_SCALE = HEAD_DIM ** -0.5

_C_Q = 0
_C_K = _C_Q + ATTN_WIDTH
_C_V = _C_K + KV_WIDTH
_C_U = _C_V + KV_WIDTH
_C_VG = _C_U + GMLP_WIDTH
_C_GA = _C_VG + GMLP_WIDTH
_C_GB = _C_GA + D_MODEL
IN_COLS = _C_GB + D_MODEL

LANES = 128
N_PAIRS = 6
N_BUCKETS = N_EXPERT_GROUPS * N_PAIRS
X_ROWS = D_MODEL // LANES
ROW_PITCH = 24
TM = 256
COL_CHUNK = 512
MIX_CHUNK = 256
PROJ_CHUNK = 256
MOE_LOOKAHEAD = 3
MOE_BUFFERS = MOE_LOOKAHEAD + 1
ISSUE_GROUP = 16
UNSORT_LOOKAHEAD = 4
VMEM_LIMIT = 60 * 1024 * 1024

_SLOPES = [float(2.0 ** (-8.0 * (h + 1) / N_Q_HEADS)) for h in range(N_Q_HEADS)]


def _layer_norm(x, g, b):
    mu = jnp.mean(x, axis=-1, keepdims=True)
    xc = x - mu
    var = jnp.mean(xc * xc, axis=-1, keepdims=True)
    return xc * lax.rsqrt(var + LN_EPS) * g + b


def _store_token_major(ref, x, pad_from):
    for r in range(X_ROWS):
        ref[pl.ds(r, TM, stride=ROW_PITCH), :] = x[:, r * LANES:(r + 1) * LANES]
    for r in range(pad_from, ROW_PITCH):
        ref[pl.ds(r, TM, stride=ROW_PITCH), :] = jnp.zeros((TM, LANES), ref.dtype)


def _load_token_major(ref, r):
    return ref[pl.ds(r, TM, stride=ROW_PITCH), :]


def _emit_interleaved(*streams):
    order = []
    for s, (_, costs) in enumerate(streams):
        done = 0.0
        for c in costs:
            order.append(((done + c / 2) / sum(costs), s))
            done += c
    for _, s in sorted(order):
        next(streams[s][0])
    for gen, _ in streams:
        assert next(gen, "done") == "done", "stream has more units than declared"


def _cast_plan(weights, n_steps):
    steps = 1 << (n_steps.bit_length() - 1)
    steps = min([steps] + [w.shape[0] // 16 for w in weights])
    specs = [pl.BlockSpec((w.shape[0] // steps, w.shape[1]), lambda i, *_: (jnp.minimum(i, steps - 1), 0))
             for w in weights]
    return steps, specs


def _cast_side_job(steps, src_refs, dst_refs):
    @pl.when(pl.program_id(0) < steps)
    def _():
        for src, dst in zip(src_refs, dst_refs):
            dst[...] = src[...].astype(dst.dtype)


def _const_spec(shape):
    nd = len(shape)
    return pl.BlockSpec(shape, lambda i, *_: (0,) * nd, pipeline_mode=pl.Buffered(1))


def _proj_kernel(n_prompt_tiles, cast_steps, n_cast, *refs):
    xp_ref, xs_ref, g0_ref, b0_ref, w_ref, lg_ref, lb_ref = refs[:7]
    cast_src = refs[7:7 + n_cast]
    xn_ref, q_ref, k_ref, v_ref, gu_ref, vgn_ref, sa_ref, sb_ref = refs[7 + n_cast:15 + n_cast]
    cast_dst = refs[15 + n_cast:15 + 2 * n_cast]
    xn_scr = refs[15 + 2 * n_cast]
    i = pl.program_id(0)
    _cast_side_job(cast_steps, cast_src, cast_dst)

    @pl.when(i < n_prompt_tiles)
    def _():
        xn_ref[...] = _layer_norm(xp_ref[...], g0_ref[...], b0_ref[...])

    @pl.when(i >= n_prompt_tiles)
    def _():
        xn_ref[...] = _layer_norm(xs_ref[...], g0_ref[...], b0_ref[...])

    xn_scr[...] = xn_ref[...].astype(BF16)
    xn = xn_scr[...]

    def proj(c0, width):
        return jnp.dot(xn, w_ref[:, c0:c0 + width], preferred_element_type=F32)

    for c in range(0, ATTN_WIDTH, COL_CHUNK):
        q_ref[:, c:c + COL_CHUNK] = proj(_C_Q + c, COL_CHUNK).astype(BF16)
    kv = proj(_C_K, 2 * KV_WIDTH)
    k_ref[...] = kv[:, :KV_WIDTH].astype(BF16)
    v_ref[...] = kv[:, KV_WIDTH:].astype(BF16)
    for c in range(0, GMLP_WIDTH, COL_CHUNK):
        gu_ref[:, c:c + COL_CHUNK] = jax.nn.gelu(proj(_C_U + c, COL_CHUNK)).astype(BF16)
    for c in range(0, GMLP_WIDTH, COL_CHUNK):
        vg = jax.nn.gelu(proj(_C_VG + c, COL_CHUNK))
        for j in range(COL_CHUNK // GMLP_GROUP_DIM):
            grp = c // GMLP_GROUP_DIM + j
            blk = vg[:, j * GMLP_GROUP_DIM:(j + 1) * GMLP_GROUP_DIM]
            y = _layer_norm(blk, lg_ref[grp:grp + 1, :], lb_ref[grp:grp + 1, :])
            vgn_ref[:, grp * GMLP_GROUP_DIM:(grp + 1) * GMLP_GROUP_DIM] = y.astype(BF16)
    for c in range(0, D_MODEL, COL_CHUNK):
        sa_ref[:, c:c + COL_CHUNK] = jax.nn.sigmoid(proj(_C_GA + c, COL_CHUNK)).astype(BF16)
    for c in range(0, D_MODEL, COL_CHUNK):
        sb_ref[:, c:c + COL_CHUNK] = jax.nn.sigmoid(proj(_C_GB + c, COL_CHUNK)).astype(BF16)


def _proj_call(xp, xs, g0, b0, w_in, lg, lb, later_weights):
    n_p, n_s = xp.shape[0] // TM, xs.shape[0] // TM
    t_all = xp.shape[0] + xs.shape[0]
    cast_steps, cast_specs = _cast_plan(later_weights, n_p + n_s)
    _, cast_out_specs = _cast_plan(later_weights, n_p + n_s)
    xp_spec = pl.BlockSpec((TM, D_MODEL), lambda i: (jnp.minimum(i, n_p - 1), 0))
    xs_spec = pl.BlockSpec((TM, D_MODEL), lambda i: (jnp.maximum(i - n_p, 0), 0))

    def tok(width):
        return pl.BlockSpec((TM, width), lambda i: (i, 0))

    widths = (ATTN_WIDTH, KV_WIDTH, KV_WIDTH, GMLP_WIDTH, GMLP_WIDTH, D_MODEL, D_MODEL)
    return pl.pallas_call(
        functools.partial(_proj_kernel, n_p, cast_steps, len(later_weights)),
        out_shape=([jax.ShapeDtypeStruct((t_all, D_MODEL), F32)]
                   + [jax.ShapeDtypeStruct((t_all, w), BF16) for w in widths]
                   + [jax.ShapeDtypeStruct(w.shape, BF16) for w in later_weights]),
        grid=(n_p + n_s,),
        in_specs=[xp_spec, xs_spec, _const_spec((1, D_MODEL)), _const_spec((1, D_MODEL)),
                  _const_spec((D_MODEL, IN_COLS)),
                  _const_spec((GMLP_GROUPS, GMLP_GROUP_DIM)), _const_spec((GMLP_GROUPS, GMLP_GROUP_DIM))]
        + cast_specs,
        out_specs=[tok(D_MODEL)] + [tok(w) for w in widths] + cast_out_specs,
        scratch_shapes=[pltpu.VMEM((TM, D_MODEL), BF16)],
        compiler_params=pltpu.CompilerParams(dimension_semantics=("arbitrary",),
                                             vmem_limit_bytes=VMEM_LIMIT),
        name="proj",
    )(xp, xs, g0, b0, w_in, lg, lb, *later_weights)


def _route(logits_t):
    row = lambda j: logits_t[j:j + 1, :]
    gl = [row(j) for j in range(N_EXPERT_GROUPS)]
    gmax, gidx = gl[0], jnp.zeros(gl[0].shape, jnp.int32)
    for j in range(1, N_EXPERT_GROUPS):
        better = gl[j] > gmax
        gmax = jnp.where(better, gl[j], gmax)
        gidx = jnp.where(better, j, gidx)
    gsum = jnp.exp(gl[0] - gmax)
    for j in range(1, N_EXPERT_GROUPS):
        gsum = gsum + jnp.exp(gl[j] - gmax)
    p_group = 1.0 / gsum

    ig = []
    for e in range(EXPERTS_PER_GROUP):
        v = row(N_EXPERT_GROUPS + (N_EXPERT_GROUPS - 1) * EXPERTS_PER_GROUP + e)
        for g in range(N_EXPERT_GROUPS - 2, -1, -1):
            v = jnp.where(gidx == g, row(N_EXPERT_GROUPS + g * EXPERTS_PER_GROUP + e), v)
        ig.append(v)
    v1, i1 = ig[0], jnp.zeros(ig[0].shape, jnp.int32)
    for e in range(1, EXPERTS_PER_GROUP):
        better = ig[e] > v1
        v1 = jnp.where(better, ig[e], v1)
        i1 = jnp.where(better, e, i1)
    v2 = jnp.where(i1 == 0, ig[1], ig[0])
    i2 = jnp.where(i1 == 0, 1, 0).astype(jnp.int32)
    for e in range(1, EXPERTS_PER_GROUP):
        better = jnp.logical_and(i1 != e, ig[e] > v2)
        v2 = jnp.where(better, ig[e], v2)
        i2 = jnp.where(better, e, i2)
    ev = jnp.exp(v2 - v1)
    ssum = 1.0 + ev
    w1 = 1.0 / ssum
    w2 = ev / ssum
    first_lo = i1 < i2
    lo = jnp.minimum(i1, i2)
    hi = jnp.maximum(i1, i2)
    c_lo = jnp.where(first_lo, w1, w2) * p_group
    c_hi = jnp.where(first_lo, w2, w1) * p_group
    pair = jnp.where(lo == 0, hi - 1, jnp.where(lo == 1, hi + 1, N_PAIRS - 1))
    bucket = gidx * N_PAIRS + pair
    return c_lo, c_hi, bucket.astype(F32)


def _mix_kernel(n_tiles, seq_edges_first, seq_edges_last, cast_steps, n_cast, *refs):
    (sink_ref, xn_ref, q_ref, kp_ref, km_ref, kn_ref, vp_ref, vm_ref, vn_ref,
     gu_ref, vgn_ref, sa_ref, sb_ref, ws_ref, bs_ref,
     wa_ref, wb_ref, wo_ref, g1_ref, b1_ref, wr_ref, br_ref) = refs[:22]
    cast_src = refs[22:22 + n_cast]
    x1t_ref, rt_ref = refs[22 + n_cast:24 + n_cast]
    cast_dst = refs[24 + n_cast:24 + 2 * n_cast]
    kf_scr, vf_scr, a_scr, sg_scr, mg_scr, mix_scr = refs[24 + 2 * n_cast:]
    _cast_side_job(cast_steps, cast_src, cast_dst)
    i = pl.program_id(0)
    nblk = TM // BLOCK
    cur = i % 2

    @pl.when(i == 0)
    def _():
        a_scr[1] = jnp.zeros(a_scr.shape[1:], a_scr.dtype)
        sg_scr[1] = jnp.zeros(sg_scr.shape[1:], sg_scr.dtype)
        mix_scr[0] = jnp.zeros(mix_scr.shape[1:], mix_scr.dtype)

    def stage_a():
        tile = jnp.minimum(i, n_tiles - 1)
        kf_scr[0:BLOCK, :] = kp_ref[...]
        kf_scr[BLOCK:BLOCK + TM, :] = km_ref[...]
        kf_scr[BLOCK + TM:, :] = kn_ref[...]
        vf_scr[0:BLOCK, :] = vp_ref[...]
        vf_scr[BLOCK:BLOCK + TM, :] = vm_ref[...]
        vf_scr[BLOCK + TM:, :] = vn_ref[...]
        first_blk = tile * nblk
        last_blk = tile * nblk + nblk - 1
        has_prev = jnp.logical_and(*[first_blk != e for e in seq_edges_first])
        has_next = jnp.logical_and(*[last_blk != e for e in seq_edges_last])
        kj = lax.broadcasted_iota(jnp.int32, (BLOCK, 3 * BLOCK), 1)
        qi = lax.broadcasted_iota(jnp.int32, (BLOCK, 3 * BLOCK), 0)
        dist = jnp.abs(kj - BLOCK - qi)
        in_window = dist <= WINDOW
        dist_f = dist.astype(F32)
        lo_key = jnp.where(has_prev, 0, BLOCK)
        hi_key = jnp.where(has_next, 3 * BLOCK, 2 * BLOCK)
        yield
        for j in range(nblk):
            mask = in_window
            if j == 0:
                mask = jnp.logical_and(mask, kj >= lo_key)
            if j == nblk - 1:
                mask = jnp.logical_and(mask, kj < hi_key)
            r0 = j * BLOCK
            for kvh in range(N_KV_HEADS):
                c0 = kvh * HEAD_DIM
                kb = kf_scr[r0:r0 + 3 * BLOCK, c0:c0 + HEAD_DIM]
                vb = vf_scr[r0:r0 + 3 * BLOCK, c0:c0 + HEAD_DIM]
                heads = [kvh * Q_PER_KV + g for g in range(Q_PER_KV)]
                qs = jnp.concatenate(
                    [q_ref[r0:r0 + BLOCK, h * HEAD_DIM:(h + 1) * HEAD_DIM] for h in heads], axis=0)
                s_all = lax.dot_general(qs, kb, (((1,), (1,)), ((), ())), preferred_element_type=F32)
                for g, h in enumerate(heads):
                    s = s_all[g * BLOCK:(g + 1) * BLOCK, :] * ATTN_SCALE
                    s = jnp.where(mask, s + dist_f * (-_SLOPES[h]), NEG_INF)
                    sink = sink_ref[h]
                    m = jnp.maximum(jnp.max(s, axis=-1, keepdims=True), sink)
                    p = jnp.exp(s - m)
                    denom = jnp.sum(p, axis=-1, keepdims=True) + jnp.exp(sink - m)
                    pn = (p * (1.0 / denom)).astype(BF16)
                    o = jnp.dot(pn, vb, preferred_element_type=F32)
                    a_scr[cur, r0:r0 + BLOCK, h * HEAD_DIM:(h + 1) * HEAD_DIM] = o.astype(BF16)
                    yield
            for grp in range(GMLP_GROUPS):
                c0 = grp * GMLP_GROUP_DIM
                sp = jnp.dot(ws_ref[grp], vgn_ref[r0:r0 + BLOCK, c0:c0 + GMLP_GROUP_DIM],
                             preferred_element_type=F32) + bs_ref[grp]
                u = gu_ref[r0:r0 + BLOCK, c0:c0 + GMLP_GROUP_DIM].astype(F32)
                sg_scr[cur, r0:r0 + BLOCK, c0:c0 + GMLP_GROUP_DIM] = (u * sp).astype(BF16)
            yield

    def stage_b():
        a = a_scr[1 - cur]
        sg = sg_scr[1 - cur]
        for c in range(0, D_MODEL, PROJ_CHUNK):
            ma = jnp.dot(a, wa_ref[:, c:c + PROJ_CHUNK], preferred_element_type=F32)
            mb = jnp.dot(sg, wb_ref[:, c:c + PROJ_CHUNK], preferred_element_type=F32)
            merged = (sa_ref[:, c:c + PROJ_CHUNK].astype(F32) * ma
                      + sb_ref[:, c:c + PROJ_CHUNK].astype(F32) * mb)
            mg_scr[:, c:c + PROJ_CHUNK] = merged.astype(BF16)
            yield
        mg = mg_scr[...]
        for c in range(0, D_MODEL, PROJ_CHUNK):
            mix = jnp.dot(mg, wo_ref[:, c:c + PROJ_CHUNK], preferred_element_type=F32)
            mix_scr[1 - cur, :, c:c + PROJ_CHUNK] = DEEPNORM_ALPHA * xn_ref[:, c:c + PROJ_CHUNK] + mix
            yield

    def stage_c():
        mu = jnp.mean(mix_scr[cur], axis=-1, keepdims=True)
        yield
        zc = mix_scr[cur] - mu
        inv = lax.rsqrt(jnp.mean(zc * zc, axis=-1, keepdims=True) + LN_EPS)
        yield
        r = None
        for c in range(0, D_MODEL, MIX_CHUNK):
            x1 = ((mix_scr[cur, :, c:c + MIX_CHUNK] - mu) * inv * g1_ref[:, c:c + MIX_CHUNK]
                  + b1_ref[:, c:c + MIX_CHUNK])
            for k in range(MIX_CHUNK // LANES):
                x1t_ref[pl.ds(c // LANES + k, TM, stride=ROW_PITCH), :] = x1[:, k * LANES:(k + 1) * LANES]
            x_hi = x1.astype(BF16)
            x_lo = (x1 - x_hi.astype(F32)).astype(BF16)
            part = (jnp.dot(x_hi, wr_ref[c:c + MIX_CHUNK, :], preferred_element_type=F32)
                    + jnp.dot(x_lo, wr_ref[c:c + MIX_CHUNK, :], preferred_element_type=F32))
            r = part if r is None else r + part
            yield
        logits = r[:, :LANES] + r[:, LANES:] + br_ref[...]
        c_lo, c_hi, bucket = _route(logits.T)
        routed = jnp.concatenate([c_lo, c_hi, bucket, jnp.zeros((5, TM), F32)], axis=0)
        rt_ref[...] = routed
        padded = jnp.concatenate([routed, jnp.zeros((LANES - 8, TM), F32)], axis=0)
        x1t_ref[pl.ds(X_ROWS, TM, stride=ROW_PITCH), :] = padded.T
        for k in range(X_ROWS + 1, ROW_PITCH):
            x1t_ref[pl.ds(k, TM, stride=ROW_PITCH), :] = jnp.zeros((TM, LANES), F32)
        yield

    n_chunks = D_MODEL // MIX_CHUNK
    _emit_interleaved((stage_b(), [512] * (2 * D_MODEL // PROJ_CHUNK)),
                      (stage_a(), [100] + ([600] * N_Q_HEADS + [500]) * nblk),
                      (stage_c(), [1000, 1000] + [550] * n_chunks + [1000]))


def _mix_call(t_prompt, sink, xn, q, k, v, gu, vgn, sa, sb, ws, bs, wa, wb, wo, g1, b1, wr, br, later_weights):
    t_all = xn.shape[0]
    nblk = TM // BLOCK
    blk_p, blk_all = t_prompt // BLOCK, t_all // BLOCK
    n_tiles = t_all // TM
    front = lambda i: jnp.minimum(i, n_tiles - 1)
    back = lambda i: jnp.clip(i - 1, 0, n_tiles - 1)
    last = lambda i: jnp.maximum(i - 2, 0)

    def tok(width, which):
        return pl.BlockSpec((TM, width), lambda i: (which(i), 0))

    prev_spec = pl.BlockSpec((BLOCK, KV_WIDTH), lambda i: (jnp.maximum(front(i) * nblk - 1, 0), 0))
    next_spec = pl.BlockSpec((BLOCK, KV_WIDTH),
                             lambda i: (jnp.minimum((front(i) + 1) * nblk, blk_all - 1), 0))
    kv_spec = tok(KV_WIDTH, front)
    cast_steps, cast_specs = _cast_plan(later_weights, n_tiles + 2)
    _, cast_out_specs = _cast_plan(later_weights, n_tiles + 2)
    kernel = functools.partial(_mix_kernel, n_tiles, (0, blk_p), (blk_p - 1, blk_all - 1),
                               cast_steps, len(later_weights))
    return pl.pallas_call(
        kernel,
        out_shape=[jax.ShapeDtypeStruct((t_all * ROW_PITCH, LANES), F32), jax.ShapeDtypeStruct((8, t_all), F32)]
        + [jax.ShapeDtypeStruct(w.shape, BF16) for w in later_weights],
        grid=(n_tiles + 2,),
        in_specs=[pl.BlockSpec(memory_space=pltpu.SMEM),
                  tok(D_MODEL, back), tok(ATTN_WIDTH, front), prev_spec, kv_spec, next_spec, prev_spec, kv_spec, next_spec,
                  tok(GMLP_WIDTH, front), tok(GMLP_WIDTH, front), tok(D_MODEL, back), tok(D_MODEL, back),
                  _const_spec((GMLP_GROUPS, BLOCK, BLOCK)), _const_spec((GMLP_GROUPS, BLOCK, BLOCK)),
                  _const_spec((ATTN_WIDTH, D_MODEL)), _const_spec((GMLP_WIDTH, D_MODEL)),
                  _const_spec((D_MODEL, D_MODEL)), _const_spec((1, D_MODEL)), _const_spec((1, D_MODEL)),
                  _const_spec((D_MODEL, 2 * LANES)), _const_spec((1, LANES))] + cast_specs,
        out_specs=[pl.BlockSpec((TM * ROW_PITCH, LANES), lambda i: (last(i), 0)),
                   pl.BlockSpec((8, TM), lambda i: (0, last(i)))] + cast_out_specs,
        scratch_shapes=[pltpu.VMEM((TM + 2 * BLOCK, KV_WIDTH), BF16),
                        pltpu.VMEM((TM + 2 * BLOCK, KV_WIDTH), BF16),
                        pltpu.VMEM((2, TM, ATTN_WIDTH), BF16),
                        pltpu.VMEM((2, TM, GMLP_WIDTH), BF16),
                        pltpu.VMEM((TM, D_MODEL), BF16),
                        pltpu.VMEM((2, TM, D_MODEL), F32)],
        compiler_params=pltpu.CompilerParams(dimension_semantics=("arbitrary",),
                                             vmem_limit_bytes=VMEM_LIMIT),
        name="mix",
    )(sink, xn, q, k, k, k, v, v, v, gu, vgn, sa, sb, ws, bs, wa, wb, wo, g1, b1, wr, br, *later_weights)


def _gather_copy(src_hbm, dst_buf, sem, src_row, dst_row, n_rows):
    return pltpu.make_async_copy(src_hbm.at[pl.ds(src_row, n_rows)], dst_buf.at[pl.ds(dst_row, n_rows)], sem)


def _start_gather(idx_ref, src_hbm, dst_buf, sem, n_rows):
    def issue(r, carry):
        src_row = pl.multiple_of(idx_ref[0, r] * ROW_PITCH, 8)
        dst_row = pl.multiple_of(r * ROW_PITCH, 8)
        _gather_copy(src_hbm, dst_buf, sem, src_row, dst_row, n_rows).start()
        return carry

    lax.fori_loop(0, TM, issue, 0, unroll=8)


def _wait_gather(src_hbm, dst_buf, sem, n_rows):
    _gather_copy(src_hbm, dst_buf, sem, 0, 0, TM * n_rows).wait()


def _idx_specs(n_tiles, lookahead=1):
    def spec(tile_of_step):
        return pl.BlockSpec((None, 1, TM), lambda i, *_: (jnp.minimum(tile_of_step(i), n_tiles - 1), 0, 0),
                            memory_space=pltpu.SMEM)

    return [spec(lambda i, t=t: t) for t in range(lookahead)] + [spec(lambda i: i + lookahead)]


def _unsort_kernel(*refs):
    first_idx_refs, idxn_ref = refs[:UNSORT_LOOKAHEAD], refs[UNSORT_LOOKAHEAD]
    ys_hbm, g2_ref, b2_ref, out_ref, buf, sem = refs[UNSORT_LOOKAHEAD + 1:]
    i = pl.program_id(0)
    n = pl.num_programs(0)
    n_buf = UNSORT_LOOKAHEAD + 1
    slot = i % n_buf

    def gather(tile, idx_ref, when):
        dst = tile % n_buf

        @pl.when(when)
        def _():
            _start_gather(idx_ref, ys_hbm, buf.at[dst], sem.at[dst], X_ROWS)

    for tile, idx_ref in enumerate(first_idx_refs):
        gather(tile, idx_ref, jnp.logical_and(i == 0, tile < n))
    gather(i + UNSORT_LOOKAHEAD, idxn_ref, i + UNSORT_LOOKAHEAD < n)

    cur = buf.at[slot]
    _wait_gather(ys_hbm, cur, sem.at[slot], X_ROWS)
    for r in range(X_ROWS):
        out_ref[:, r * LANES:(r + 1) * LANES] = _load_token_major(cur, r)
    out_ref[...] = _layer_norm(out_ref[...], g2_ref[...], b2_ref[...])


def _unsort_call(idx, ysorted, g2, b2, name):
    n_tiles = idx.shape[0] // TM
    idx = idx.reshape(n_tiles, 1, TM)
    idx_specs = _idx_specs(n_tiles, UNSORT_LOOKAHEAD)
    n_buf = UNSORT_LOOKAHEAD + 1
    return pl.pallas_call(
        _unsort_kernel,
        out_shape=jax.ShapeDtypeStruct((n_tiles * TM, D_MODEL), F32),
        grid=(n_tiles,),
        in_specs=idx_specs + [pl.BlockSpec(memory_space=pl.ANY),
                              _const_spec((1, D_MODEL)), _const_spec((1, D_MODEL))],
        out_specs=pl.BlockSpec((TM, D_MODEL), lambda i: (i, 0)),
        scratch_shapes=[pltpu.VMEM((n_buf, TM * ROW_PITCH, LANES), F32), pltpu.SemaphoreType.DMA((n_buf,))],
        compiler_params=pltpu.CompilerParams(dimension_semantics=("arbitrary",),
                                             vmem_limit_bytes=VMEM_LIMIT),
        name=name,
    )(*([idx] * len(idx_specs)), ysorted, g2, b2)


def _moe_kernel(elo_ref, ehi_ref, nvalid_ref, *refs):
    del elo_ref, ehi_ref
    first_idx_refs, idxn_ref = refs[:MOE_LOOKAHEAD], refs[MOE_LOOKAHEAD]
    x1t_hbm, wg_lo, wu_lo, wd_lo, wg_hi, wu_hi, wd_hi, out_ref, buf, sem, xb_scr = refs[MOE_LOOKAHEAD + 1:]
    i = pl.program_id(0)
    n = pl.num_programs(0)
    slot = i % MOE_BUFFERS

    def gather_if_valid(tile, idx_ref, when):
        dst = tile % MOE_BUFFERS

        @pl.when(jnp.logical_and(when, nvalid_ref[jnp.minimum(tile, n - 1)] > 0))
        def _():
            _start_gather(idx_ref, x1t_hbm, buf.at[dst], sem.at[dst], ROW_PITCH)

    for tile, idx_ref in enumerate(first_idx_refs):
        gather_if_valid(tile, idx_ref, jnp.logical_and(i == 0, tile < n))

    def experts():
        cur = buf.at[slot]
        _wait_gather(x1t_hbm, cur, sem.at[slot], ROW_PITCH)
        for r in range(X_ROWS):
            xb_scr[:, r * LANES:(r + 1) * LANES] = _load_token_major(cur, r).astype(BF16)
        route = _load_token_major(cur, X_ROWS)
        xb = xb_scr[...]
        yield

        def hidden(wg, wu, c):
            gate = jnp.dot(xb, wg[...], preferred_element_type=F32)
            up = jnp.dot(xb, wu[...], preferred_element_type=F32)
            return ((jax.nn.silu(gate) * up) * c).astype(BF16)

        h_lo = hidden(wg_lo, wu_lo, route[:, 0:1])
        yield
        h_hi = hidden(wg_hi, wu_hi, route[:, 1:2])
        yield
        for c in range(0, D_MODEL, COL_CHUNK):
            y = (jnp.dot(h_lo, wd_lo[:, c:c + COL_CHUNK], preferred_element_type=F32)
                 + jnp.dot(h_hi, wd_hi[:, c:c + COL_CHUNK], preferred_element_type=F32))
            for r in range(c // LANES, (c + COL_CHUNK) // LANES):
                out_ref[pl.ds(r, TM, stride=ROW_PITCH), :] = (DEEPNORM_ALPHA * _load_token_major(cur, r)
                                                              + y[:, r * LANES - c:(r + 1) * LANES - c])
            yield
        for r in range(X_ROWS, ROW_PITCH):
            out_ref[pl.ds(r, TM, stride=ROW_PITCH), :] = jnp.zeros((TM, LANES), F32)
        yield

    def start_later_gather():
        dst = (i + MOE_LOOKAHEAD) % MOE_BUFFERS
        for r in range(TM):
            src_row = pl.multiple_of(idxn_ref[0, r] * ROW_PITCH, 8)
            _gather_copy(x1t_hbm, buf.at[dst], sem.at[dst], src_row, r * ROW_PITCH, ROW_PITCH).start()
            if r % ISSUE_GROUP == ISSUE_GROUP - 1:
                yield

    expert_costs = [1, 4, 4] + [2] * (D_MODEL // COL_CHUNK) + [1]
    later = jnp.minimum(i + MOE_LOOKAHEAD, n - 1)
    gather_later = jnp.logical_and(i + MOE_LOOKAHEAD < n, nvalid_ref[later] > 0)

    @pl.when(jnp.logical_and(nvalid_ref[i] > 0, gather_later))
    def _():
        _emit_interleaved((experts(), expert_costs), (start_later_gather(), [1] * (TM // ISSUE_GROUP)))

    @pl.when(jnp.logical_and(nvalid_ref[i] > 0, jnp.logical_not(gather_later)))
    def _():
        _emit_interleaved((experts(), expert_costs))

    @pl.when(nvalid_ref[i] == 0)
    def _():
        out_ref[...] = jnp.zeros(out_ref.shape, out_ref.dtype)


def _moe_call(e_lo, e_hi, nvalid, src, x1t, wg, wu, wd):
    n_tiles = src.shape[0] // TM
    src = src.reshape(n_tiles, 1, TM)
    idx_specs = _idx_specs(n_tiles, MOE_LOOKAHEAD)

    def w_spec(shape, which):
        def imap(i, elo, ehi, nv):
            return ((elo, ehi)[which][i], 0, 0)
        return pl.BlockSpec((None,) + shape, imap)

    up_shape, down_shape = (D_MODEL, EXPERT_FF), (EXPERT_FF, D_MODEL)
    grid_spec = pltpu.PrefetchScalarGridSpec(
        num_scalar_prefetch=3,
        grid=(n_tiles,),
        in_specs=idx_specs + [pl.BlockSpec(memory_space=pl.ANY),
                              w_spec(up_shape, 0), w_spec(up_shape, 0), w_spec(down_shape, 0),
                              w_spec(up_shape, 1), w_spec(up_shape, 1), w_spec(down_shape, 1)],
        out_specs=pl.BlockSpec((TM * ROW_PITCH, LANES), lambda i, *_: (i, 0)),
        scratch_shapes=[pltpu.VMEM((MOE_BUFFERS, TM * ROW_PITCH, LANES), F32),
                        pltpu.SemaphoreType.DMA((MOE_BUFFERS,)),
                        pltpu.VMEM((TM, D_MODEL), BF16)],
    )
    return pl.pallas_call(
        _moe_kernel,
        out_shape=jax.ShapeDtypeStruct((n_tiles * TM * ROW_PITCH, LANES), F32),
        grid_spec=grid_spec,
        compiler_params=pltpu.CompilerParams(dimension_semantics=("arbitrary",),
                                             vmem_limit_bytes=VMEM_LIMIT),
        name="moe",
    )(e_lo, e_hi, nvalid, *([src] * len(idx_specs)), x1t, wg, wu, wd, wg, wu, wd)


_PAIR_LO = np.array([0, 0, 0, 1, 1, 2], np.int32)
_PAIR_HI = np.array([1, 2, 3, 2, 3, 3], np.int32)


def _bucket_layout(bucket, n_slots):
    t_all = bucket.shape[0]
    n_tiles = n_slots // TM
    rows = t_all // LANES
    onehot = (bucket.reshape(rows, LANES, 1) == jnp.arange(N_BUCKETS, dtype=jnp.int32)).astype(F32)
    earlier = (jnp.arange(LANES)[:, None] > jnp.arange(LANES)[None, :]).astype(F32)
    within = jnp.einsum("ts,rsb->rtb", earlier, onehot)
    row_total = jnp.sum(onehot, axis=1)
    row_start = jnp.cumsum(row_total, axis=0) - row_total
    counts = jnp.sum(row_total, axis=0).astype(jnp.int32)
    padded = ((counts + TM - 1) // TM) * TM
    ends = jnp.cumsum(padded)
    starts = ends - padded
    slot = within + (row_start + starts.astype(F32))[:, None, :]
    dest = jnp.sum(slot * onehot, axis=-1).reshape(t_all).astype(jnp.int32)
    order = jnp.argsort(bucket, stable=True).astype(jnp.int32)
    slot_ids = jnp.arange(n_slots, dtype=jnp.int32)[:, None]
    in_bucket = jnp.logical_and(slot_ids >= starts[None, :], slot_ids < ends[None, :]).astype(jnp.int32)
    offset = slot_ids - starts[None, :]
    compact = jnp.sum(in_bucket * (offset + (jnp.cumsum(counts) - counts)[None, :]), axis=1)
    real = jnp.sum(in_bucket * (offset < counts[None, :]).astype(jnp.int32), axis=1)
    src = jnp.where(real > 0, order[jnp.clip(compact, 0, t_all - 1)], 0)
    tile_start = jnp.arange(n_tiles, dtype=jnp.int32) * TM
    owner_start = jnp.minimum(tile_start, jnp.maximum(ends[-1:] - TM, 0))[:, None]
    owner = jnp.logical_and(owner_start >= starts[None, :], owner_start < ends[None, :]).astype(jnp.int32)
    remaining = counts[None, :] - (tile_start[:, None] - starts[None, :])
    nvalid = jnp.sum(owner * jnp.clip(remaining, 0, TM), axis=1)
    bucket_ids = np.arange(N_BUCKETS)
    first_expert = (bucket_ids // N_PAIRS) * EXPERTS_PER_GROUP
    e_lo = jnp.sum(owner * jnp.asarray(first_expert + _PAIR_LO[bucket_ids % N_PAIRS], jnp.int32)[None, :], axis=1)
    e_hi = jnp.sum(owner * jnp.asarray(first_expert + _PAIR_HI[bucket_ids % N_PAIRS], jnp.int32)[None, :], axis=1)
    return src, dest, e_lo.astype(jnp.int32), e_hi.astype(jnp.int32), nvalid.astype(jnp.int32)


def _layer(xp, xs, in_ln_g, in_ln_b, w_in, attn_sink, gmlp_w_s, gmlp_b_s, gmlp_ln_g, gmlp_ln_b,
           w_attn_branch, w_gmlp_branch, w_out, ln1_g, ln1_b,
           router_w_group, router_b_group, router_w_expert, router_b_expert,
           w_expert_gate, w_expert_up, w_expert_down, ln2_g, ln2_b):
    t_p, t_s = xp.shape[0], xs.shape[0]
    t_all = t_p + t_s
    row = lambda p: p.reshape(1, -1).astype(F32)

    rows_of = lambda w: w.astype(F32).reshape(-1, w.shape[-1])
    xn, q, k, v, gu, vgn, sa, sb, wa, wb, wo, wd = _proj_call(
        xp, xs, row(in_ln_g), row(in_ln_b), w_in.astype(BF16), gmlp_ln_g.astype(F32), gmlp_ln_b.astype(F32),
        [rows_of(w_attn_branch), rows_of(w_gmlp_branch), rows_of(w_out), rows_of(w_expert_down)])

    wr = jnp.concatenate([router_w_group, router_w_expert], axis=1).astype(F32)
    wr = jnp.pad(wr, ((0, 0), (0, LANES - wr.shape[1])))
    wr_hi = wr.astype(BF16)
    wr_lo = (wr - wr_hi.astype(F32)).astype(BF16)
    br = jnp.pad(jnp.concatenate([router_b_group, router_b_expert]).astype(F32),
                 (0, LANES - N_EXPERT_GROUPS - N_EXPERTS)).reshape(1, LANES)
    bs = jnp.broadcast_to(gmlp_b_s.astype(F32)[:, :, None], (GMLP_GROUPS, BLOCK, BLOCK))

    x1t, routed, wg, wu = _mix_call(t_p, attn_sink.astype(F32), xn,
                                    q, k, v, gu, vgn, sa, sb, gmlp_w_s.astype(BF16), bs, wa, wb, wo,
                                    row(ln1_g), row(ln1_b), jnp.concatenate([wr_hi, wr_lo], axis=1), br,
                                    [rows_of(w_expert_gate), rows_of(w_expert_up)])

    n_slots = t_all + N_BUCKETS * TM
    bucket = routed[2].astype(jnp.int32)
    src, dest, e_lo, e_hi, nvalid = _bucket_layout(bucket, n_slots)
    zsorted = _moe_call(e_lo, e_hi, nvalid, src, x1t, wg.reshape(w_expert_gate.shape),
                        wu.reshape(w_expert_up.shape), wd.reshape(w_expert_down.shape))
    g2, b2 = row(ln2_g), row(ln2_b)
    return (_unsort_call(dest[:t_p], zsorted, g2, b2, "unsort_prompt"),
            _unsort_call(dest[t_p:], zsorted, g2, b2, "unsort_sample"))


def kernel(x_prompt, x_sample, in_ln_g, in_ln_b, w_in, attn_sink, gmlp_w_s, gmlp_b_s, gmlp_ln_g, gmlp_ln_b,
           w_attn_branch, w_gmlp_branch, w_out, ln1_g, ln1_b,
           router_w_group, router_b_group, router_w_expert, router_b_expert,
           w_expert_gate, w_expert_up, w_expert_down, ln2_g, ln2_b):
    bp, sp, d = x_prompt.shape
    bs, ss, _ = x_sample.shape
    assert bp == 1 and bs == 1 and d == D_MODEL and sp % TM == 0 and ss % TM == 0
    assert w_in.shape[0] == 1, "one layer"
    yp, ys = _layer(x_prompt.reshape(sp, d), x_sample.reshape(ss, d), in_ln_g, in_ln_b, w_in[0], attn_sink[0],
                    gmlp_w_s[0], gmlp_b_s[0], gmlp_ln_g[0], gmlp_ln_b[0],
                    w_attn_branch[0], w_gmlp_branch[0], w_out[0], ln1_g[0], ln1_b[0],
                    router_w_group[0], router_b_group[0], router_w_expert[0], router_b_expert[0],
                    w_expert_gate[0], w_expert_up[0], w_expert_down[0], ln2_g[0], ln2_b[0])
    return yp.reshape(1, sp, d), ys.reshape(1, ss, d)
```

```python
import functools

import numpy as np
import jax
import jax.numpy as jnp
from jax import lax
from jax.experimental import pallas as pl
from jax.experimental.pallas import tpu as pltpu

F32 = jnp.float32
BF16 = jnp.bfloat16

D_MODEL = 2048
HEAD_DIM = 128
N_Q_HEADS = 8
N_KV_HEADS = 2
Q_PER_KV = N_Q_HEADS // N_KV_HEADS
ATTN_WIDTH = N_Q_HEADS * HEAD_DIM
KV_WIDTH = N_KV_HEADS * HEAD_DIM
WINDOW = 128
BLOCK = 128
GMLP_WIDTH = D_MODEL // 2
GMLP_GROUPS = 8
GMLP_GROUP_DIM = GMLP_WIDTH // GMLP_GROUPS
N_EXPERT_GROUPS = 4
EXPERTS_PER_GROUP = 4
N_EXPERTS = N_EXPERT_GROUPS * EXPERTS_PER_GROUP
EXPERT_FF = 512
LN_EPS = 1e-5
DEEPNORM_ALPHA = 2.0 ** 0.25
NEG_INF = -1e9
ATTN_SCALE = HEAD_DIM ** -0.5

_C_Q = 0
_C_K = _C_Q + ATTN_WIDTH
_C_V = _C_K + KV_WIDTH
_C_U = _C_V + KV_WIDTH
_C_VG = _C_U + GMLP_WIDTH
_C_GA = _C_VG + GMLP_WIDTH
_C_GB = _C_GA + D_MODEL
IN_COLS = _C_GB + D_MODEL

LANES = 128
N_PAIRS = 6
N_BUCKETS = N_EXPERT_GROUPS * N_PAIRS
X_ROWS = D_MODEL // LANES
ROW_PITCH = 24
TM = 256
COL_CHUNK = 512
MIX_CHUNK = 256
PROJ_CHUNK = 256
MOE_LOOKAHEAD = 3
MOE_BUFFERS = MOE_LOOKAHEAD + 1
ISSUE_GROUP = 16
UNSORT_LOOKAHEAD = 4
VMEM_LIMIT = 60 * 1024 * 1024

_SLOPES = [float(2.0 ** (-8.0 * (h + 1) / N_Q_HEADS)) for h in range(N_Q_HEADS)]


def _layer_norm(x, g, b):
    mu = jnp.mean(x, axis=-1, keepdims=True)
    xc = x - mu
    var = jnp.mean(xc * xc, axis=-1, keepdims=True)
    return xc * lax.rsqrt(var + LN_EPS) * g + b


def _store_token_major(ref, x, pad_from):
    for r in range(X_ROWS):
        ref[pl.ds(r, TM, stride=ROW_PITCH), :] = x[:, r * LANES:(r + 1) * LANES]
    for r in range(pad_from, ROW_PITCH):
        ref[pl.ds(r, TM, stride=ROW_PITCH), :] = jnp.zeros((TM, LANES), ref.dtype)


def _load_token_major(ref, r):
    return ref[pl.ds(r, TM, stride=ROW_PITCH), :]


def _emit_interleaved(*streams):
    order = []
    for s, (_, costs) in enumerate(streams):
        done = 0.0
        for c in costs:
            order.append(((done + c / 2) / sum(costs), s))
            done += c
    for _, s in sorted(order):
        next(streams[s][0])
    for gen, _ in streams:
        assert next(gen, "done") == "done", "stream has more units than declared"


def _cast_plan(weights, n_steps):
    steps = 1 << (n_steps.bit_length() - 1)
    steps = min([steps] + [w.shape[0] // 16 for w in weights])
    specs = [pl.BlockSpec((w.shape[0] // steps, w.shape[1]), lambda i, *_: (jnp.minimum(i, steps - 1), 0))
             for w in weights]
    return steps, specs


def _cast_side_job(steps, src_refs, dst_refs):
    @pl.when(pl.program_id(0) < steps)
    def _():
        for src, dst in zip(src_refs, dst_refs):
            dst[...] = src[...].astype(dst.dtype)


def _const_spec(shape):
    nd = len(shape)
    return pl.BlockSpec(shape, lambda i, *_: (0,) * nd, pipeline_mode=pl.Buffered(1))


def _proj_kernel(n_prompt_tiles, cast_steps, n_cast, *refs):
    xp_ref, xs_ref, g0_ref, b0_ref, w_ref, lg_ref, lb_ref = refs[:7]
    cast_src = refs[7:7 + n_cast]
    xn_ref, q_ref, k_ref, v_ref, gu_ref, vgn_ref, sa_ref, sb_ref = refs[7 + n_cast:15 + n_cast]
    cast_dst = refs[15 + n_cast:15 + 2 * n_cast]
    xn_scr = refs[15 + 2 * n_cast]
    i = pl.program_id(0)
    _cast_side_job(cast_steps, cast_src, cast_dst)

    @pl.when(i < n_prompt_tiles)
    def _():
        xn_ref[...] = _layer_norm(xp_ref[...], g0_ref[...], b0_ref[...])

    @pl.when(i >= n_prompt_tiles)
    def _():
        xn_ref[...] = _layer_norm(xs_ref[...], g0_ref[...], b0_ref[...])

    xn_scr[...] = xn_ref[...].astype(BF16)
    xn = xn_scr[...]

    def proj(c0, width):
        return jnp.dot(xn, w_ref[:, c0:c0 + width], preferred_element_type=F32)

    for c in range(0, ATTN_WIDTH, COL_CHUNK):
        q_ref[:, c:c + COL_CHUNK] = proj(_C_Q + c, COL_CHUNK).astype(BF16)
    kv = proj(_C_K, 2 * KV_WIDTH)
    k_ref[...] = kv[:, :KV_WIDTH].astype(BF16)
    v_ref[...] = kv[:, KV_WIDTH:].astype(BF16)
    for c in range(0, GMLP_WIDTH, COL_CHUNK):
        gu_ref[:, c:c + COL_CHUNK] = jax.nn.gelu(proj(_C_U + c, COL_CHUNK)).astype(BF16)
    for c in range(0, GMLP_WIDTH, COL_CHUNK):
        vg = jax.nn.gelu(proj(_C_VG + c, COL_CHUNK))
        for j in range(COL_CHUNK // GMLP_GROUP_DIM):
            grp = c // GMLP_GROUP_DIM + j
            blk = vg[:, j * GMLP_GROUP_DIM:(j + 1) * GMLP_GROUP_DIM]
            y = _layer_norm(blk, lg_ref[grp:grp + 1, :], lb_ref[grp:grp + 1, :])
            vgn_ref[:, grp * GMLP_GROUP_DIM:(grp + 1) * GMLP_GROUP_DIM] = y.astype(BF16)
    for c in range(0, D_MODEL, COL_CHUNK):
        sa_ref[:, c:c + COL_CHUNK] = jax.nn.sigmoid(proj(_C_GA + c, COL_CHUNK)).astype(BF16)
    for c in range(0, D_MODEL, COL_CHUNK):
        sb_ref[:, c:c + COL_CHUNK] = jax.nn.sigmoid(proj(_C_GB + c, COL_CHUNK)).astype(BF16)


def _proj_call(xp, xs, g0, b0, w_in, lg, lb, later_weights):
    n_p, n_s = xp.shape[0] // TM, xs.shape[0] // TM
    t_all = xp.shape[0] + xs.shape[0]
    cast_steps, cast_specs = _cast_plan(later_weights, n_p + n_s)
    _, cast_out_specs = _cast_plan(later_weights, n_p + n_s)
    xp_spec = pl.BlockSpec((TM, D_MODEL), lambda i: (jnp.minimum(i, n_p - 1), 0))
    xs_spec = pl.BlockSpec((TM, D_MODEL), lambda i: (jnp.maximum(i - n_p, 0), 0))

    def tok(width):
        return pl.BlockSpec((TM, width), lambda i: (i, 0))

    widths = (ATTN_WIDTH, KV_WIDTH, KV_WIDTH, GMLP_WIDTH, GMLP_WIDTH, D_MODEL, D_MODEL)
    return pl.pallas_call(
        functools.partial(_proj_kernel, n_p, cast_steps, len(later_weights)),
        out_shape=([jax.ShapeDtypeStruct((t_all, D_MODEL), F32)]
                   + [jax.ShapeDtypeStruct((t_all, w), BF16) for w in widths]
                   + [jax.ShapeDtypeStruct(w.shape, BF16) for w in later_weights]),
        grid=(n_p + n_s,),
        in_specs=[xp_spec, xs_spec, _const_spec((1, D_MODEL)), _const_spec((1, D_MODEL)),
                  _const_spec((D_MODEL, IN_COLS)),
                  _const_spec((GMLP_GROUPS, GMLP_GROUP_DIM)), _const_spec((GMLP_GROUPS, GMLP_GROUP_DIM))]
        + cast_specs,
        out_specs=[tok(D_MODEL)] + [tok(w) for w in widths] + cast_out_specs,
        scratch_shapes=[pltpu.VMEM((TM, D_MODEL), BF16)],
        compiler_params=pltpu.CompilerParams(dimension_semantics=("arbitrary",),
                                             vmem_limit_bytes=VMEM_LIMIT),
        name="proj",
    )(xp, xs, g0, b0, w_in, lg, lb, *later_weights)


def _route(logits_t):
    row = lambda j: logits_t[j:j + 1, :]
    gl = [row(j) for j in range(N_EXPERT_GROUPS)]
    gmax, gidx = gl[0], jnp.zeros(gl[0].shape, jnp.int32)
    for j in range(1, N_EXPERT_GROUPS):
        better = gl[j] > gmax
        gmax = jnp.where(better, gl[j], gmax)
        gidx = jnp.where(better, j, gidx)
    gsum = jnp.exp(gl[0] - gmax)
    for j in range(1, N_EXPERT_GROUPS):
        gsum = gsum + jnp.exp(gl[j] - gmax)
    p_group = 1.0 / gsum

    ig = []
    for e in range(EXPERTS_PER_GROUP):
        v = row(N_EXPERT_GROUPS + (N_EXPERT_GROUPS - 1) * EXPERTS_PER_GROUP + e)
        for g in range(N_EXPERT_GROUPS - 2, -1, -1):
            v = jnp.where(gidx == g, row(N_EXPERT_GROUPS + g * EXPERTS_PER_GROUP + e), v)
        ig.append(v)
    v1, i1 = ig[0], jnp.zeros(ig[0].shape, jnp.int32)
    for e in range(1, EXPERTS_PER_GROUP):
        better = ig[e] > v1
        v1 = jnp.where(better, ig[e], v1)
        i1 = jnp.where(better, e, i1)
    v2 = jnp.where(i1 == 0, ig[1], ig[0])
    i2 = jnp.where(i1 == 0, 1, 0).astype(jnp.int32)
    for e in range(1, EXPERTS_PER_GROUP):
        better = jnp.logical_and(i1 != e, ig[e] > v2)
        v2 = jnp.where(better, ig[e], v2)
        i2 = jnp.where(better, e, i2)
    ev = jnp.exp(v2 - v1)
    ssum = 1.0 + ev
    w1 = 1.0 / ssum
    w2 = ev / ssum
    first_lo = i1 < i2
    lo = jnp.minimum(i1, i2)
    hi = jnp.maximum(i1, i2)
    c_lo = jnp.where(first_lo, w1, w2) * p_group
    c_hi = jnp.where(first_lo, w2, w1) * p_group
    pair = jnp.where(lo == 0, hi - 1, jnp.where(lo == 1, hi + 1, N_PAIRS - 1))
    bucket = gidx * N_PAIRS + pair
    return c_lo, c_hi, bucket.astype(F32)


def _mix_kernel(n_tiles, seq_edges_first, seq_edges_last, cast_steps, n_cast, *refs):
    (sink_ref, xn_ref, q_ref, kp_ref, km_ref, kn_ref, vp_ref, vm_ref, vn_ref,
     gu_ref, vgn_ref, sa_ref, sb_ref, ws_ref, bs_ref,
     wa_ref, wb_ref, wo_ref, g1_ref, b1_ref, wr_ref, br_ref) = refs[:22]
    cast_src = refs[22:22 + n_cast]
    x1t_ref, rt_ref = refs[22 + n_cast:24 + n_cast]
    cast_dst = refs[24 + n_cast:24 + 2 * n_cast]
    kf_scr, vf_scr, a_scr, sg_scr, mg_scr, mix_scr = refs[24 + 2 * n_cast:]
    _cast_side_job(cast_steps, cast_src, cast_dst)
    i = pl.program_id(0)
    nblk = TM // BLOCK
    cur = i % 2

    @pl.when(i == 0)
    def _():
        a_scr[1] = jnp.zeros(a_scr.shape[1:], a_scr.dtype)
        sg_scr[1] = jnp.zeros(sg_scr.shape[1:], sg_scr.dtype)
        mix_scr[0] = jnp.zeros(mix_scr.shape[1:], mix_scr.dtype)

    def stage_a():
        tile = jnp.minimum(i, n_tiles - 1)
        kf_scr[0:BLOCK, :] = kp_ref[...]
        kf_scr[BLOCK:BLOCK + TM, :] = km_ref[...]
        kf_scr[BLOCK + TM:, :] = kn_ref[...]
        vf_scr[0:BLOCK, :] = vp_ref[...]
        vf_scr[BLOCK:BLOCK + TM, :] = vm_ref[...]
        vf_scr[BLOCK + TM:, :] = vn_ref[...]
        first_blk = tile * nblk
        last_blk = tile * nblk + nblk - 1
        has_prev = jnp.logical_and(*[first_blk != e for e in seq_edges_first])
        has_next = jnp.logical_and(*[last_blk != e for e in seq_edges_last])
        kj = lax.broadcasted_iota(jnp.int32, (BLOCK, 3 * BLOCK), 1)
        qi = lax.broadcasted_iota(jnp.int32, (BLOCK, 3 * BLOCK), 0)
        dist = jnp.abs(kj - BLOCK - qi)
        in_window = dist <= WINDOW
        dist_f = dist.astype(F32)
        lo_key = jnp.where(has_prev, 0, BLOCK)
        hi_key = jnp.where(has_next, 3 * BLOCK, 2 * BLOCK)
        yield
        for j in range(nblk):
            mask = in_window
            if j == 0:
                mask = jnp.logical_and(mask, kj >= lo_key)
            if j == nblk - 1:
                mask = jnp.logical_and(mask, kj < hi_key)
            r0 = j * BLOCK
            for kvh in range(N_KV_HEADS):
                c0 = kvh * HEAD_DIM
                kb = kf_scr[r0:r0 + 3 * BLOCK, c0:c0 + HEAD_DIM]
                vb = vf_scr[r0:r0 + 3 * BLOCK, c0:c0 + HEAD_DIM]
                heads = [kvh * Q_PER_KV + g for g in range(Q_PER_KV)]
                qs = jnp.concatenate(
                    [q_ref[r0:r0 + BLOCK, h * HEAD_DIM:(h + 1) * HEAD_DIM] for h in heads], axis=0)
                s_all = lax.dot_general(qs, kb, (((1,), (1,)), ((), ())), preferred_element_type=F32)
                for g, h in enumerate(heads):
                    s = s_all[g * BLOCK:(g + 1) * BLOCK, :] * ATTN_SCALE
                    s = jnp.where(mask, s + dist_f * (-_SLOPES[h]), NEG_INF)
                    sink = sink_ref[h]
                    m = jnp.maximum(jnp.max(s, axis=-1, keepdims=True), sink)
                    p = jnp.exp(s - m)
                    denom = jnp.sum(p, axis=-1, keepdims=True) + jnp.exp(sink - m)
                    pn = (p * (1.0 / denom)).astype(BF16)
                    o = jnp.dot(pn, vb, preferred_element_type=F32)
                    a_scr[cur, r0:r0 + BLOCK, h * HEAD_DIM:(h + 1) * HEAD_DIM] = o.astype(BF16)
                    yield
            for grp in range(GMLP_GROUPS):
                c0 = grp * GMLP_GROUP_DIM
                sp = jnp.dot(ws_ref[grp], vgn_ref[r0:r0 + BLOCK, c0:c0 + GMLP_GROUP_DIM],
                             preferred_element_type=F32) + bs_ref[grp]
                u = gu_ref[r0:r0 + BLOCK, c0:c0 + GMLP_GROUP_DIM].astype(F32)
                sg_scr[cur, r0:r0 + BLOCK, c0:c0 + GMLP_GROUP_DIM] = (u * sp).astype(BF16)
            yield

    def stage_b():
        a = a_scr[1 - cur]
        sg = sg_scr[1 - cur]
        for c in range(0, D_MODEL, PROJ_CHUNK):
            ma = jnp.dot(a, wa_ref[:, c:c + PROJ_CHUNK], preferred_element_type=F32)
            mb = jnp.dot(sg, wb_ref[:, c:c + PROJ_CHUNK], preferred_element_type=F32)
            merged = (sa_ref[:, c:c + PROJ_CHUNK].astype(F32) * ma
                      + sb_ref[:, c:c + PROJ_CHUNK].astype(F32) * mb)
            mg_scr[:, c:c + PROJ_CHUNK] = merged.astype(BF16)
            yield
        mg = mg_scr[...]
        for c in range(0, D_MODEL, PROJ_CHUNK):
            mix = jnp.dot(mg, wo_ref[:, c:c + PROJ_CHUNK], preferred_element_type=F32)
            mix_scr[1 - cur, :, c:c + PROJ_CHUNK] = DEEPNORM_ALPHA * xn_ref[:, c:c + PROJ_CHUNK] + mix
            yield

    def stage_c():
        mu = jnp.mean(mix_scr[cur], axis=-1, keepdims=True)
        yield
        zc = mix_scr[cur] - mu
        inv = lax.rsqrt(jnp.mean(zc * zc, axis=-1, keepdims=True) + LN_EPS)
        yield
        r = None
        for c in range(0, D_MODEL, MIX_CHUNK):
            x1 = ((mix_scr[cur, :, c:c + MIX_CHUNK] - mu) * inv * g1_ref[:, c:c + MIX_CHUNK]
                  + b1_ref[:, c:c + MIX_CHUNK])
            for k in range(MIX_CHUNK // LANES):
                x1t_ref[pl.ds(c // LANES + k, TM, stride=ROW_PITCH), :] = x1[:, k * LANES:(k + 1) * LANES]
            x_hi = x1.astype(BF16)
            x_lo = (x1 - x_hi.astype(F32)).astype(BF16)
            part = (jnp.dot(x_hi, wr_ref[c:c + MIX_CHUNK, :], preferred_element_type=F32)
                    + jnp.dot(x_lo, wr_ref[c:c + MIX_CHUNK, :], preferred_element_type=F32))
            r = part if r is None else r + part
            yield
        logits = r[:, :LANES] + r[:, LANES:] + br_ref[...]
        c_lo, c_hi, bucket = _route(logits.T)
        routed = jnp.concatenate([c_lo, c_hi, bucket, jnp.zeros((5, TM), F32)], axis=0)
        rt_ref[...] = routed
        padded = jnp.concatenate([routed, jnp.zeros((LANES - 8, TM), F32)], axis=0)
        x1t_ref[pl.ds(X_ROWS, TM, stride=ROW_PITCH), :] = padded.T
        for k in range(X_ROWS + 1, ROW_PITCH):
            x1t_ref[pl.ds(k, TM, stride=ROW_PITCH), :] = jnp.zeros((TM, LANES), F32)
        yield

    n_chunks = D_MODEL // MIX_CHUNK
    _emit_interleaved((stage_b(), [512] * (2 * D_MODEL // PROJ_CHUNK)),
                      (stage_a(), [100] + ([600] * N_Q_HEADS + [500]) * nblk),
                      (stage_c(), [1000, 1000] + [550] * n_chunks + [1000]))


def _mix_call(t_prompt, sink, xn, q, k, v, gu, vgn, sa, sb, ws, bs, wa, wb, wo, g1, b1, wr, br, later_weights):
    t_all = xn.shape[0]
    nblk = TM // BLOCK
    blk_p, blk_all = t_prompt // BLOCK, t_all // BLOCK
    n_tiles = t_all // TM
    front = lambda i: jnp.minimum(i, n_tiles - 1)
    back = lambda i: jnp.clip(i - 1, 0, n_tiles - 1)
    last = lambda i: jnp.maximum(i - 2, 0)

    def tok(width, which):
        return pl.BlockSpec((TM, width), lambda i: (which(i), 0))

    prev_spec = pl.BlockSpec((BLOCK, KV_WIDTH), lambda i: (jnp.maximum(front(i) * nblk - 1, 0), 0))
    next_spec = pl.BlockSpec((BLOCK, KV_WIDTH),
                             lambda i: (jnp.minimum((front(i) + 1) * nblk, blk_all - 1), 0))
    kv_spec = tok(KV_WIDTH, front)
    cast_steps, cast_specs = _cast_plan(later_weights, n_tiles + 2)
    _, cast_out_specs = _cast_plan(later_weights, n_tiles + 2)
    kernel = functools.partial(_mix_kernel, n_tiles, (0, blk_p), (blk_p - 1, blk_all - 1),
                               cast_steps, len(later_weights))
    return pl.pallas_call(
        kernel,
        out_shape=[jax.ShapeDtypeStruct((t_all * ROW_PITCH, LANES), F32), jax.ShapeDtypeStruct((8, t_all), F32)]
        + [jax.ShapeDtypeStruct(w.shape, BF16) for w in later_weights],
        grid=(n_tiles + 2,),
        in_specs=[pl.BlockSpec(memory_space=pltpu.SMEM),
                  tok(D_MODEL, back), tok(ATTN_WIDTH, front), prev_spec, kv_spec, next_spec, prev_spec, kv_spec, next_spec,
                  tok(GMLP_WIDTH, front), tok(GMLP_WIDTH, front), tok(D_MODEL, back), tok(D_MODEL, back),
                  _const_spec((GMLP_GROUPS, BLOCK, BLOCK)), _const_spec((GMLP_GROUPS, BLOCK, BLOCK)),
                  _const_spec((ATTN_WIDTH, D_MODEL)), _const_spec((GMLP_WIDTH, D_MODEL)),
                  _const_spec((D_MODEL, D_MODEL)), _const_spec((1, D_MODEL)), _const_spec((1, D_MODEL)),
                  _const_spec((D_MODEL, 2 * LANES)), _const_spec((1, LANES))] + cast_specs,
        out_specs=[pl.BlockSpec((TM * ROW_PITCH, LANES), lambda i: (last(i), 0)),
                   pl.BlockSpec((8, TM), lambda i: (0, last(i)))] + cast_out_specs,
        scratch_shapes=[pltpu.VMEM((TM + 2 * BLOCK, KV_WIDTH), BF16),
                        pltpu.VMEM((TM + 2 * BLOCK, KV_WIDTH), BF16),
                        pltpu.VMEM((2, TM, ATTN_WIDTH), BF16),
                        pltpu.VMEM((2, TM, GMLP_WIDTH), BF16),
                        pltpu.VMEM((TM, D_MODEL), BF16),
                        pltpu.VMEM((2, TM, D_MODEL), F32)],
        compiler_params=pltpu.CompilerParams(dimension_semantics=("arbitrary",),
                                             vmem_limit_bytes=VMEM_LIMIT),
        name="mix",
    )(sink, xn, q, k, k, k, v, v, v, gu, vgn, sa, sb, ws, bs, wa, wb, wo, g1, b1, wr, br, *later_weights)


def _gather_copy(src_hbm, dst_buf, sem, src_row, dst_row, n_rows):
    return pltpu.make_async_copy(src_hbm.at[pl.ds(src_row, n_rows)], dst_buf.at[pl.ds(dst_row, n_rows)], sem)


def _start_gather(idx_ref, src_hbm, dst_buf, sem, n_rows, src_pitch=ROW_PITCH):
    def issue(r, carry):
        src_row = pl.multiple_of(idx_ref[0, r] * src_pitch, 8)
        dst_row = pl.multiple_of(r * ROW_PITCH, 8)
        _gather_copy(src_hbm, dst_buf, sem, src_row, dst_row, n_rows).start()
        return carry

    lax.fori_loop(0, TM, issue, 0, unroll=8)


def _wait_gather(src_hbm, dst_buf, sem, n_rows):
    _gather_copy(src_hbm, dst_buf, sem, 0, 0, TM * n_rows).wait()


def _idx_specs(n_tiles, lookahead=1):
    def spec(tile_of_step):
        return pl.BlockSpec((None, 1, TM), lambda i, *_: (jnp.minimum(tile_of_step(i), n_tiles - 1), 0, 0),
                            memory_space=pltpu.SMEM)

    return [spec(lambda i, t=t: t) for t in range(lookahead)] + [spec(lambda i: i + lookahead)]


def _unsort_kernel(*refs):
    first_idx_refs, idxn_ref = refs[:UNSORT_LOOKAHEAD], refs[UNSORT_LOOKAHEAD]
    ys_hbm, g2_ref, b2_ref, out_ref, buf, sem = refs[UNSORT_LOOKAHEAD + 1:]
    i = pl.program_id(0)
    n = pl.num_programs(0)
    n_buf = UNSORT_LOOKAHEAD + 1
    slot = i % n_buf

    def gather(tile, idx_ref, when):
        dst = tile % n_buf

        @pl.when(when)
        def _():
            _start_gather(idx_ref, ys_hbm, buf.at[dst], sem.at[dst], X_ROWS, src_pitch=X_ROWS)

    for tile, idx_ref in enumerate(first_idx_refs):
        gather(tile, idx_ref, jnp.logical_and(i == 0, tile < n))
    gather(i + UNSORT_LOOKAHEAD, idxn_ref, i + UNSORT_LOOKAHEAD < n)

    cur = buf.at[slot]
    _wait_gather(ys_hbm, cur, sem.at[slot], X_ROWS)
    for r in range(X_ROWS):
        out_ref[:, r * LANES:(r + 1) * LANES] = _load_token_major(cur, r)
    out_ref[...] = _layer_norm(out_ref[...], g2_ref[...], b2_ref[...])


def _unsort_call(idx, ysorted, g2, b2, name):
    n_tiles = idx.shape[0] // TM
    idx = idx.reshape(n_tiles, 1, TM)
    idx_specs = _idx_specs(n_tiles, UNSORT_LOOKAHEAD)
    n_buf = UNSORT_LOOKAHEAD + 1
    return pl.pallas_call(
        _unsort_kernel,
        out_shape=jax.ShapeDtypeStruct((n_tiles * TM, D_MODEL), F32),
        grid=(n_tiles,),
        in_specs=idx_specs + [pl.BlockSpec(memory_space=pl.ANY),
                              _const_spec((1, D_MODEL)), _const_spec((1, D_MODEL))],
        out_specs=pl.BlockSpec((TM, D_MODEL), lambda i: (i, 0)),
        scratch_shapes=[pltpu.VMEM((n_buf, TM * ROW_PITCH, LANES), F32), pltpu.SemaphoreType.DMA((n_buf,))],
        compiler_params=pltpu.CompilerParams(dimension_semantics=("arbitrary",),
                                             vmem_limit_bytes=VMEM_LIMIT),
        name=name,
    )(*([idx] * len(idx_specs)), ysorted, g2, b2)


def _moe_kernel(elo_ref, ehi_ref, nvalid_ref, *refs):
    del elo_ref, ehi_ref
    first_idx_refs, idxn_ref = refs[:MOE_LOOKAHEAD], refs[MOE_LOOKAHEAD]
    x1t_hbm, wg_lo, wu_lo, wd_lo, wg_hi, wu_hi, wd_hi, out_ref, buf, sem, xb_scr = refs[MOE_LOOKAHEAD + 1:]
    i = pl.program_id(0)
    n = pl.num_programs(0)
    slot = i % MOE_BUFFERS

    def gather_if_valid(tile, idx_ref, when):
        dst = tile % MOE_BUFFERS

        @pl.when(jnp.logical_and(when, nvalid_ref[jnp.minimum(tile, n - 1)] > 0))
        def _():
            _start_gather(idx_ref, x1t_hbm, buf.at[dst], sem.at[dst], ROW_PITCH)

    for tile, idx_ref in enumerate(first_idx_refs):
        gather_if_valid(tile, idx_ref, jnp.logical_and(i == 0, tile < n))

    def experts():
        cur = buf.at[slot]
        _wait_gather(x1t_hbm, cur, sem.at[slot], ROW_PITCH)
        for r in range(X_ROWS):
            xb_scr[:, r * LANES:(r + 1) * LANES] = _load_token_major(cur, r).astype(BF16)
        route = _load_token_major(cur, X_ROWS)
        xb = xb_scr[...]
        yield

        def hidden(wg, wu, c):
            gate = jnp.dot(xb, wg[...], preferred_element_type=F32)
            up = jnp.dot(xb, wu[...], preferred_element_type=F32)
            return ((jax.nn.silu(gate) * up) * c).astype(BF16)

        h_lo = hidden(wg_lo, wu_lo, route[:, 0:1])
        yield
        h_hi = hidden(wg_hi, wu_hi, route[:, 1:2])
        yield
        for c in range(0, D_MODEL, COL_CHUNK):
            y = (jnp.dot(h_lo, wd_lo[:, c:c + COL_CHUNK], preferred_element_type=F32)
                 + jnp.dot(h_hi, wd_hi[:, c:c + COL_CHUNK], preferred_element_type=F32))
            for r in range(c // LANES, (c + COL_CHUNK) // LANES):
                out_ref[pl.ds(r, TM, stride=X_ROWS), :] = (DEEPNORM_ALPHA * _load_token_major(cur, r)
                                                           + y[:, r * LANES - c:(r + 1) * LANES - c])
            yield

    def start_later_gather():
        dst = (i + MOE_LOOKAHEAD) % MOE_BUFFERS
        for r in range(TM):
            src_row = pl.multiple_of(idxn_ref[0, r] * ROW_PITCH, 8)
            _gather_copy(x1t_hbm, buf.at[dst], sem.at[dst], src_row, r * ROW_PITCH, ROW_PITCH).start()
            if r % ISSUE_GROUP == ISSUE_GROUP - 1:
                yield

    expert_costs = [1, 4, 4] + [2] * (D_MODEL // COL_CHUNK)
    later = jnp.minimum(i + MOE_LOOKAHEAD, n - 1)
    gather_later = jnp.logical_and(i + MOE_LOOKAHEAD < n, nvalid_ref[later] > 0)

    @pl.when(jnp.logical_and(nvalid_ref[i] > 0, gather_later))
    def _():
        _emit_interleaved((experts(), expert_costs), (start_later_gather(), [1] * (TM // ISSUE_GROUP)))

    @pl.when(jnp.logical_and(nvalid_ref[i] > 0, jnp.logical_not(gather_later)))
    def _():
        _emit_interleaved((experts(), expert_costs))

    @pl.when(nvalid_ref[i] == 0)
    def _():
        out_ref[...] = jnp.zeros(out_ref.shape, out_ref.dtype)


def _moe_call(e_lo, e_hi, nvalid, src, x1t, wg, wu, wd):
    n_tiles = src.shape[0] // TM
    src = src.reshape(n_tiles, 1, TM)
    idx_specs = _idx_specs(n_tiles, MOE_LOOKAHEAD)

    def w_spec(shape, which):
        def imap(i, elo, ehi, nv):
            return ((elo, ehi)[which][i], 0, 0)
        return pl.BlockSpec((None,) + shape, imap)

    up_shape, down_shape = (D_MODEL, EXPERT_FF), (EXPERT_FF, D_MODEL)
    grid_spec = pltpu.PrefetchScalarGridSpec(
        num_scalar_prefetch=3,
        grid=(n_tiles,),
        in_specs=idx_specs + [pl.BlockSpec(memory_space=pl.ANY),
                              w_spec(up_shape, 0), w_spec(up_shape, 0), w_spec(down_shape, 0),
                              w_spec(up_shape, 1), w_spec(up_shape, 1), w_spec(down_shape, 1)],
        out_specs=pl.BlockSpec((TM * X_ROWS, LANES), lambda i, *_: (i, 0)),
        scratch_shapes=[pltpu.VMEM((MOE_BUFFERS, TM * ROW_PITCH, LANES), F32),
                        pltpu.SemaphoreType.DMA((MOE_BUFFERS,)),
                        pltpu.VMEM((TM, D_MODEL), BF16)],
    )
    return pl.pallas_call(
        _moe_kernel,
        out_shape=jax.ShapeDtypeStruct((n_tiles * TM * X_ROWS, LANES), F32),
        grid_spec=grid_spec,
        compiler_params=pltpu.CompilerParams(dimension_semantics=("arbitrary",),
                                             vmem_limit_bytes=VMEM_LIMIT),
        name="moe",
    )(e_lo, e_hi, nvalid, *([src] * len(idx_specs)), x1t, wg, wu, wd, wg, wu, wd)


_PAIR_LO = np.array([0, 0, 0, 1, 1, 2], np.int32)
_PAIR_HI = np.array([1, 2, 3, 2, 3, 3], np.int32)


def _bucket_layout(bucket, n_slots):
    t_all = bucket.shape[0]
    n_tiles = n_slots // TM
    rows = t_all // LANES
    onehot = (bucket.reshape(rows, LANES, 1) == jnp.arange(N_BUCKETS, dtype=jnp.int32)).astype(F32)
    earlier = (jnp.arange(LANES)[:, None] > jnp.arange(LANES)[None, :]).astype(F32)
    within = jnp.einsum("ts,rsb->rtb", earlier, onehot)
    row_total = jnp.sum(onehot, axis=1)
    row_start = jnp.cumsum(row_total, axis=0) - row_total
    counts = jnp.sum(row_total, axis=0).astype(jnp.int32)
    padded = ((counts + TM - 1) // TM) * TM
    ends = jnp.cumsum(padded)
    starts = ends - padded
    slot = within + (row_start + starts.astype(F32))[:, None, :]
    dest = jnp.sum(slot * onehot, axis=-1).reshape(t_all).astype(jnp.int32)
    order = jnp.argsort(bucket, stable=True).astype(jnp.int32)
    slot_ids = jnp.arange(n_slots, dtype=jnp.int32)[:, None]
    in_bucket = jnp.logical_and(slot_ids >= starts[None, :], slot_ids < ends[None, :]).astype(jnp.int32)
    offset = slot_ids - starts[None, :]
    compact = jnp.sum(in_bucket * (offset + (jnp.cumsum(counts) - counts)[None, :]), axis=1)
    real = jnp.sum(in_bucket * (offset < counts[None, :]).astype(jnp.int32), axis=1)
    src = jnp.where(real > 0, order[jnp.clip(compact, 0, t_all - 1)], 0)
    tile_start = jnp.arange(n_tiles, dtype=jnp.int32) * TM
    owner_start = jnp.minimum(tile_start, jnp.maximum(ends[-1:] - TM, 0))[:, None]
    owner = jnp.logical_and(owner_start >= starts[None, :], owner_start < ends[None, :]).astype(jnp.int32)
    remaining = counts[None, :] - (tile_start[:, None] - starts[None, :])
    nvalid = jnp.sum(owner * jnp.clip(remaining, 0, TM), axis=1)
    bucket_ids = np.arange(N_BUCKETS)
    first_expert = (bucket_ids // N_PAIRS) * EXPERTS_PER_GROUP
    e_lo = jnp.sum(owner * jnp.asarray(first_expert + _PAIR_LO[bucket_ids % N_PAIRS], jnp.int32)[None, :], axis=1)
    e_hi = jnp.sum(owner * jnp.asarray(first_expert + _PAIR_HI[bucket_ids % N_PAIRS], jnp.int32)[None, :], axis=1)
    return src, dest, e_lo.astype(jnp.int32), e_hi.astype(jnp.int32), nvalid.astype(jnp.int32)


def _layer(xp, xs, in_ln_g, in_ln_b, w_in, attn_sink, gmlp_w_s, gmlp_b_s, gmlp_ln_g, gmlp_ln_b,
           w_attn_branch, w_gmlp_branch, w_out, ln1_g, ln1_b,
           router_w_group, router_b_group, router_w_expert, router_b_expert,
           w_expert_gate, w_expert_up, w_expert_down, ln2_g, ln2_b):
    t_p, t_s = xp.shape[0], xs.shape[0]
    t_all = t_p + t_s
    row = lambda p: p.reshape(1, -1).astype(F32)

    rows_of = lambda w: w.astype(F32).reshape(-1, w.shape[-1])
    xn, q, k, v, gu, vgn, sa, sb, wa, wb, wo, wd = _proj_call(
        xp, xs, row(in_ln_g), row(in_ln_b), w_in.astype(BF16), gmlp_ln_g.astype(F32), gmlp_ln_b.astype(F32),
        [rows_of(w_attn_branch), rows_of(w_gmlp_branch), rows_of(w_out), rows_of(w_expert_down)])

    wr = jnp.concatenate([router_w_group, router_w_expert], axis=1).astype(F32)
    wr = jnp.pad(wr, ((0, 0), (0, LANES - wr.shape[1])))
    wr_hi = wr.astype(BF16)
    wr_lo = (wr - wr_hi.astype(F32)).astype(BF16)
    br = jnp.pad(jnp.concatenate([router_b_group, router_b_expert]).astype(F32),
                 (0, LANES - N_EXPERT_GROUPS - N_EXPERTS)).reshape(1, LANES)
    bs = jnp.broadcast_to(gmlp_b_s.astype(F32)[:, :, None], (GMLP_GROUPS, BLOCK, BLOCK))

    x1t, routed, wg, wu = _mix_call(t_p, attn_sink.astype(F32), xn,
                                    q, k, v, gu, vgn, sa, sb, gmlp_w_s.astype(BF16), bs, wa, wb, wo,
                                    row(ln1_g), row(ln1_b), jnp.concatenate([wr_hi, wr_lo], axis=1), br,
                                    [rows_of(w_expert_gate), rows_of(w_expert_up)])

    n_slots = t_all + N_BUCKETS * TM
    bucket = routed[2].astype(jnp.int32)
    src, dest, e_lo, e_hi, nvalid = _bucket_layout(bucket, n_slots)
    zsorted = _moe_call(e_lo, e_hi, nvalid, src, x1t, wg.reshape(w_expert_gate.shape),
                        wu.reshape(w_expert_up.shape), wd.reshape(w_expert_down.shape))
    g2, b2 = row(ln2_g), row(ln2_b)
    return (_unsort_call(dest[:t_p], zsorted, g2, b2, "unsort_prompt"),
            _unsort_call(dest[t_p:], zsorted, g2, b2, "unsort_sample"))


def kernel(x_prompt, x_sample, in_ln_g, in_ln_b, w_in, attn_sink, gmlp_w_s, gmlp_b_s, gmlp_ln_g, gmlp_ln_b,
           w_attn_branch, w_gmlp_branch, w_out, ln1_g, ln1_b,
           router_w_group, router_b_group, router_w_expert, router_b_expert,
           w_expert_gate, w_expert_up, w_expert_down, ln2_g, ln2_b):
    bp, sp, d = x_prompt.shape
    bs, ss, _ = x_sample.shape
    assert bp == 1 and bs == 1 and d == D_MODEL and sp % TM == 0 and ss % TM == 0
    assert w_in.shape[0] == 1, "one layer"
    yp, ys = _layer(x_prompt.reshape(sp, d), x_sample.reshape(ss, d), in_ln_g, in_ln_b, w_in[0], attn_sink[0],
                    gmlp_w_s[0], gmlp_b_s[0], gmlp_ln_g[0], gmlp_ln_b[0],
                    w_attn_branch[0], w_gmlp_branch[0], w_out[0], ln1_g[0], ln1_b[0],
                    router_w_group[0], router_b_group[0], router_w_expert[0], router_b_expert[0],
                    w_expert_gate[0], w_expert_up[0], w_expert_down[0], ln2_g[0], ln2_b[0])
    return yp.reshape(1, sp, d), ys.reshape(1, ss, d)
```

```python
import functools

import numpy as np
import jax
import jax.numpy as jnp
from jax import lax
from jax.experimental import pallas as pl
from jax.experimental.pallas import tpu as pltpu

F32 = jnp.float32
BF16 = jnp.bfloat16

D_MODEL = 2048
HEAD_DIM = 128
N_Q_HEADS = 8
N_KV_HEADS = 2
Q_PER_KV = N_Q_HEADS // N_KV_HEADS
ATTN_WIDTH = N_Q_HEADS * HEAD_DIM
KV_WIDTH = N_KV_HEADS * HEAD_DIM
WINDOW = 128
BLOCK = 128
GMLP_WIDTH = D_MODEL // 2
GMLP_GROUPS = 8
GMLP_GROUP_DIM = GMLP_WIDTH // GMLP_GROUPS
N_EXPERT_GROUPS = 4
EXPERTS_PER_GROUP = 4
N_EXPERTS = N_EXPERT_GROUPS * EXPERTS_PER_GROUP
EXPERT_FF = 512
LN_EPS = 1e-5
DEEPNORM_ALPHA = 2.0 ** 0.25
NEG_INF = -1e9
ATTN_SCALE = HEAD_DIM ** -0.5

_C_Q = 0
_C_K = _C_Q + ATTN_WIDTH
_C_V = _C_K + KV_WIDTH
_C_U = _C_V + KV_WIDTH
_C_VG = _C_U + GMLP_WIDTH
_C_GA = _C_VG + GMLP_WIDTH
_C_GB = _C_GA + D_MODEL
IN_COLS = _C_GB + D_MODEL

LANES = 128
N_PAIRS = 6
N_BUCKETS = N_EXPERT_GROUPS * N_PAIRS
X_ROWS = D_MODEL // LANES
ROW_PITCH = 24
TM = 256
COL_CHUNK = 512
MIX_CHUNK = 256
PROJ_CHUNK = 256
MOE_LOOKAHEAD = 3
MOE_BUFFERS = MOE_LOOKAHEAD + 1
ISSUE_GROUP = 16
UNSORT_LOOKAHEAD = 4
VMEM_LIMIT = 60 * 1024 * 1024

_SLOPES = [float(2.0 ** (-8.0 * (h + 1) / N_Q_HEADS)) for h in range(N_Q_HEADS)]


def _layer_norm(x, g, b):
    mu = jnp.mean(x, axis=-1, keepdims=True)
    xc = x - mu
    var = jnp.mean(xc * xc, axis=-1, keepdims=True)
    return xc * lax.rsqrt(var + LN_EPS) * g + b


def _store_token_major(ref, x, pad_from):
    for r in range(X_ROWS):
        ref[pl.ds(r, TM, stride=ROW_PITCH), :] = x[:, r * LANES:(r + 1) * LANES]
    for r in range(pad_from, ROW_PITCH):
        ref[pl.ds(r, TM, stride=ROW_PITCH), :] = jnp.zeros((TM, LANES), ref.dtype)


def _load_token_major(ref, r):
    return ref[pl.ds(r, TM, stride=ROW_PITCH), :]


def _emit_interleaved(*streams):
    order = []
    for s, (_, costs) in enumerate(streams):
        done = 0.0
        for c in costs:
            order.append(((done + c / 2) / sum(costs), s))
            done += c
    for _, s in sorted(order):
        next(streams[s][0])
    for gen, _ in streams:
        assert next(gen, "done") == "done", "stream has more units than declared"


def _cast_plan(weights, n_steps):
    steps = 1 << (n_steps.bit_length() - 1)
    steps = min([steps] + [w.shape[0] // 16 for w in weights])
    specs = [pl.BlockSpec((w.shape[0] // steps, w.shape[1]), lambda i, *_: (jnp.minimum(i, steps - 1), 0))
             for w in weights]
    return steps, specs


def _cast_side_job(steps, src_refs, dst_refs):
    @pl.when(pl.program_id(0) < steps)
    def _():
        for src, dst in zip(src_refs, dst_refs):
            dst[...] = src[...].astype(dst.dtype)


def _const_spec(shape):
    nd = len(shape)
    return pl.BlockSpec(shape, lambda i, *_: (0,) * nd, pipeline_mode=pl.Buffered(1))


def _proj_kernel(n_prompt_tiles, cast_steps, n_cast, *refs):
    xp_ref, xs_ref, g0_ref, b0_ref, w_ref, lg_ref, lb_ref = refs[:7]
    cast_src = refs[7:7 + n_cast]
    xn_ref, q_ref, k_ref, v_ref, gu_ref, vgn_ref, sa_ref, sb_ref = refs[7 + n_cast:15 + n_cast]
    cast_dst = refs[15 + n_cast:15 + 2 * n_cast]
    xn_scr = refs[15 + 2 * n_cast]
    i = pl.program_id(0)
    _cast_side_job(cast_steps, cast_src, cast_dst)

    @pl.when(i < n_prompt_tiles)
    def _():
        xn_ref[...] = _layer_norm(xp_ref[...], g0_ref[...], b0_ref[...])

    @pl.when(i >= n_prompt_tiles)
    def _():
        xn_ref[...] = _layer_norm(xs_ref[...], g0_ref[...], b0_ref[...])

    xn_scr[...] = xn_ref[...].astype(BF16)
    xn = xn_scr[...]

    def proj(c0, width):
        return jnp.dot(xn, w_ref[:, c0:c0 + width], preferred_element_type=F32)

    for c in range(0, ATTN_WIDTH, COL_CHUNK):
        q_ref[:, c:c + COL_CHUNK] = proj(_C_Q + c, COL_CHUNK).astype(BF16)
    kv = proj(_C_K, 2 * KV_WIDTH)
    k_ref[...] = kv[:, :KV_WIDTH].astype(BF16)
    v_ref[...] = kv[:, KV_WIDTH:].astype(BF16)
    for c in range(0, GMLP_WIDTH, COL_CHUNK):
        gu_ref[:, c:c + COL_CHUNK] = jax.nn.gelu(proj(_C_U + c, COL_CHUNK)).astype(BF16)
    for c in range(0, GMLP_WIDTH, COL_CHUNK):
        vg = jax.nn.gelu(proj(_C_VG + c, COL_CHUNK))
        for j in range(COL_CHUNK // GMLP_GROUP_DIM):
            grp = c // GMLP_GROUP_DIM + j
            blk = vg[:, j * GMLP_GROUP_DIM:(j + 1) * GMLP_GROUP_DIM]
            y = _layer_norm(blk, lg_ref[grp:grp + 1, :], lb_ref[grp:grp + 1, :])
            vgn_ref[:, grp * GMLP_GROUP_DIM:(grp + 1) * GMLP_GROUP_DIM] = y.astype(BF16)
    for c in range(0, D_MODEL, COL_CHUNK):
        sa_ref[:, c:c + COL_CHUNK] = jax.nn.sigmoid(proj(_C_GA + c, COL_CHUNK)).astype(BF16)
    for c in range(0, D_MODEL, COL_CHUNK):
        sb_ref[:, c:c + COL_CHUNK] = jax.nn.sigmoid(proj(_C_GB + c, COL_CHUNK)).astype(BF16)


def _proj_call(xp, xs, g0, b0, w_in, lg, lb, later_weights):
    n_p, n_s = xp.shape[0] // TM, xs.shape[0] // TM
    t_all = xp.shape[0] + xs.shape[0]
    cast_steps, cast_specs = _cast_plan(later_weights, n_p + n_s)
    _, cast_out_specs = _cast_plan(later_weights, n_p + n_s)
    xp_spec = pl.BlockSpec((TM, D_MODEL), lambda i: (jnp.minimum(i, n_p - 1), 0))
    xs_spec = pl.BlockSpec((TM, D_MODEL), lambda i: (jnp.maximum(i - n_p, 0), 0))

    def tok(width):
        return pl.BlockSpec((TM, width), lambda i: (i, 0))

    widths = (ATTN_WIDTH, KV_WIDTH, KV_WIDTH, GMLP_WIDTH, GMLP_WIDTH, D_MODEL, D_MODEL)
    return pl.pallas_call(
        functools.partial(_proj_kernel, n_p, cast_steps, len(later_weights)),
        out_shape=([jax.ShapeDtypeStruct((t_all, D_MODEL), F32)]
                   + [jax.ShapeDtypeStruct((t_all, w), BF16) for w in widths]
                   + [jax.ShapeDtypeStruct(w.shape, BF16) for w in later_weights]),
        grid=(n_p + n_s,),
        in_specs=[xp_spec, xs_spec, _const_spec((1, D_MODEL)), _const_spec((1, D_MODEL)),
                  _const_spec((D_MODEL, IN_COLS)),
                  _const_spec((GMLP_GROUPS, GMLP_GROUP_DIM)), _const_spec((GMLP_GROUPS, GMLP_GROUP_DIM))]
        + cast_specs,
        out_specs=[tok(D_MODEL)] + [tok(w) for w in widths] + cast_out_specs,
        scratch_shapes=[pltpu.VMEM((TM, D_MODEL), BF16)],
        compiler_params=pltpu.CompilerParams(dimension_semantics=("arbitrary",),
                                             vmem_limit_bytes=VMEM_LIMIT),
        name="proj",
    )(xp, xs, g0, b0, w_in, lg, lb, *later_weights)


def _route(logits_t):
    row = lambda j: logits_t[j:j + 1, :]
    gl = [row(j) for j in range(N_EXPERT_GROUPS)]
    gmax, gidx = gl[0], jnp.zeros(gl[0].shape, jnp.int32)
    for j in range(1, N_EXPERT_GROUPS):
        better = gl[j] > gmax
        gmax = jnp.where(better, gl[j], gmax)
        gidx = jnp.where(better, j, gidx)
    gsum = jnp.exp(gl[0] - gmax)
    for j in range(1, N_EXPERT_GROUPS):
        gsum = gsum + jnp.exp(gl[j] - gmax)
    p_group = 1.0 / gsum

    ig = []
    for e in range(EXPERTS_PER_GROUP):
        v = row(N_EXPERT_GROUPS + (N_EXPERT_GROUPS - 1) * EXPERTS_PER_GROUP + e)
        for g in range(N_EXPERT_GROUPS - 2, -1, -1):
            v = jnp.where(gidx == g, row(N_EXPERT_GROUPS + g * EXPERTS_PER_GROUP + e), v)
        ig.append(v)
    v1, i1 = ig[0], jnp.zeros(ig[0].shape, jnp.int32)
    for e in range(1, EXPERTS_PER_GROUP):
        better = ig[e] > v1
        v1 = jnp.where(better, ig[e], v1)
        i1 = jnp.where(better, e, i1)
    v2 = jnp.where(i1 == 0, ig[1], ig[0])
    i2 = jnp.where(i1 == 0, 1, 0).astype(jnp.int32)
    for e in range(1, EXPERTS_PER_GROUP):
        better = jnp.logical_and(i1 != e, ig[e] > v2)
        v2 = jnp.where(better, ig[e], v2)
        i2 = jnp.where(better, e, i2)
    ev = jnp.exp(v2 - v1)
    ssum = 1.0 + ev
    w1 = 1.0 / ssum
    w2 = ev / ssum
    first_lo = i1 < i2
    lo = jnp.minimum(i1, i2)
    hi = jnp.maximum(i1, i2)
    c_lo = jnp.where(first_lo, w1, w2) * p_group
    c_hi = jnp.where(first_lo, w2, w1) * p_group
    pair = jnp.where(lo == 0, hi - 1, jnp.where(lo == 1, hi + 1, N_PAIRS - 1))
    bucket = gidx * N_PAIRS + pair
    return c_lo, c_hi, bucket.astype(F32)


def _mix_kernel(n_tiles, seq_edges_first, seq_edges_last, cast_steps, n_cast, *refs):
    (sink_ref, xn_ref, q_ref, kp_ref, km_ref, kn_ref, vp_ref, vm_ref, vn_ref,
     gu_ref, vgn_ref, sa_ref, sb_ref, ws_ref, bs_ref,
     wa_ref, wb_ref, wo_ref, g1_ref, b1_ref, wr_ref, br_ref) = refs[:22]
    cast_src = refs[22:22 + n_cast]
    x1t_ref, rt_ref = refs[22 + n_cast:24 + n_cast]
    cast_dst = refs[24 + n_cast:24 + 2 * n_cast]
    kf_scr, vf_scr, a_scr, sg_scr, mg_scr, mix_scr = refs[24 + 2 * n_cast:]
    _cast_side_job(cast_steps, cast_src, cast_dst)
    i = pl.program_id(0)
    nblk = TM // BLOCK
    cur = i % 2

    @pl.when(i == 0)
    def _():
        a_scr[1] = jnp.zeros(a_scr.shape[1:], a_scr.dtype)
        sg_scr[1] = jnp.zeros(sg_scr.shape[1:], sg_scr.dtype)
        mix_scr[0] = jnp.zeros(mix_scr.shape[1:], mix_scr.dtype)

    def stage_a():
        tile = jnp.minimum(i, n_tiles - 1)
        kf_scr[0:BLOCK, :] = kp_ref[...]
        kf_scr[BLOCK:BLOCK + TM, :] = km_ref[...]
        kf_scr[BLOCK + TM:, :] = kn_ref[...]
        vf_scr[0:BLOCK, :] = vp_ref[...]
        vf_scr[BLOCK:BLOCK + TM, :] = vm_ref[...]
        vf_scr[BLOCK + TM:, :] = vn_ref[...]
        first_blk = tile * nblk
        last_blk = tile * nblk + nblk - 1
        has_prev = jnp.logical_and(*[first_blk != e for e in seq_edges_first])
        has_next = jnp.logical_and(*[last_blk != e for e in seq_edges_last])
        kj = lax.broadcasted_iota(jnp.int32, (BLOCK, 3 * BLOCK), 1)
        qi = lax.broadcasted_iota(jnp.int32, (BLOCK, 3 * BLOCK), 0)
        dist = jnp.abs(kj - BLOCK - qi)
        in_window = dist <= WINDOW
        dist_f = dist.astype(F32)
        lo_key = jnp.where(has_prev, 0, BLOCK)
        hi_key = jnp.where(has_next, 3 * BLOCK, 2 * BLOCK)
        yield
        for j in range(nblk):
            mask = in_window
            if j == 0:
                mask = jnp.logical_and(mask, kj >= lo_key)
            if j == nblk - 1:
                mask = jnp.logical_and(mask, kj < hi_key)
            r0 = j * BLOCK
            for kvh in range(N_KV_HEADS):
                c0 = kvh * HEAD_DIM
                kb = kf_scr[r0:r0 + 3 * BLOCK, c0:c0 + HEAD_DIM]
                vb = vf_scr[r0:r0 + 3 * BLOCK, c0:c0 + HEAD_DIM]
                heads = [kvh * Q_PER_KV + g for g in range(Q_PER_KV)]
                qs = jnp.concatenate(
                    [q_ref[r0:r0 + BLOCK, h * HEAD_DIM:(h + 1) * HEAD_DIM] for h in heads], axis=0)
                s_all = lax.dot_general(qs, kb, (((1,), (1,)), ((), ())), preferred_element_type=F32)
                for g, h in enumerate(heads):
                    s = s_all[g * BLOCK:(g + 1) * BLOCK, :] * ATTN_SCALE
                    s = jnp.where(mask, s + dist_f * (-_SLOPES[h]), NEG_INF)
                    sink = sink_ref[h]
                    m = jnp.maximum(jnp.max(s, axis=-1, keepdims=True), sink)
                    p = jnp.exp(s - m)
                    denom = jnp.sum(p, axis=-1, keepdims=True) + jnp.exp(sink - m)
                    pn = (p * (1.0 / denom)).astype(BF16)
                    o = jnp.dot(pn, vb, preferred_element_type=F32)
                    a_scr[cur, r0:r0 + BLOCK, h * HEAD_DIM:(h + 1) * HEAD_DIM] = o.astype(BF16)
                    yield
            for grp in range(GMLP_GROUPS):
                c0 = grp * GMLP_GROUP_DIM
                sp = jnp.dot(ws_ref[grp], vgn_ref[r0:r0 + BLOCK, c0:c0 + GMLP_GROUP_DIM],
                             preferred_element_type=F32) + bs_ref[grp]
                u = gu_ref[r0:r0 + BLOCK, c0:c0 + GMLP_GROUP_DIM].astype(F32)
                sg_scr[cur, r0:r0 + BLOCK, c0:c0 + GMLP_GROUP_DIM] = (u * sp).astype(BF16)
            yield

    def stage_b():
        a = a_scr[1 - cur]
        sg = sg_scr[1 - cur]
        for c in range(0, D_MODEL, PROJ_CHUNK):
            ma = jnp.dot(a, wa_ref[:, c:c + PROJ_CHUNK], preferred_element_type=F32)
            mb = jnp.dot(sg, wb_ref[:, c:c + PROJ_CHUNK], preferred_element_type=F32)
            merged = (sa_ref[:, c:c + PROJ_CHUNK].astype(F32) * ma
                      + sb_ref[:, c:c + PROJ_CHUNK].astype(F32) * mb)
            mg_scr[:, c:c + PROJ_CHUNK] = merged.astype(BF16)
            yield
        mg = mg_scr[...]
        for c in range(0, D_MODEL, PROJ_CHUNK):
            mix = jnp.dot(mg, wo_ref[:, c:c + PROJ_CHUNK], preferred_element_type=F32)
            mix_scr[1 - cur, :, c:c + PROJ_CHUNK] = DEEPNORM_ALPHA * xn_ref[:, c:c + PROJ_CHUNK] + mix
            yield

    def stage_c():
        mu = jnp.mean(mix_scr[cur], axis=-1, keepdims=True)
        yield
        zc = mix_scr[cur] - mu
        inv = lax.rsqrt(jnp.mean(zc * zc, axis=-1, keepdims=True) + LN_EPS)
        yield
        r = None
        for c in range(0, D_MODEL, MIX_CHUNK):
            x1 = ((mix_scr[cur, :, c:c + MIX_CHUNK] - mu) * inv * g1_ref[:, c:c + MIX_CHUNK]
                  + b1_ref[:, c:c + MIX_CHUNK])
            for k in range(MIX_CHUNK // LANES):
                x1t_ref[pl.ds(c // LANES + k, TM, stride=ROW_PITCH), :] = x1[:, k * LANES:(k + 1) * LANES]
            x_hi = x1.astype(BF16)
            x_lo = (x1 - x_hi.astype(F32)).astype(BF16)
            part = (jnp.dot(x_hi, wr_ref[c:c + MIX_CHUNK, :], preferred_element_type=F32)
                    + jnp.dot(x_lo, wr_ref[c:c + MIX_CHUNK, :], preferred_element_type=F32))
            r = part if r is None else r + part
            yield
        logits = r[:, :LANES] + r[:, LANES:] + br_ref[...]
        c_lo, c_hi, bucket = _route(logits.T)
        routed = jnp.concatenate([c_lo, c_hi, bucket, jnp.zeros((5, TM), F32)], axis=0)
        rt_ref[...] = routed
        padded = jnp.concatenate([routed, jnp.zeros((LANES - 8, TM), F32)], axis=0)
        x1t_ref[pl.ds(X_ROWS, TM, stride=ROW_PITCH), :] = padded.T
        for k in range(X_ROWS + 1, ROW_PITCH):
            x1t_ref[pl.ds(k, TM, stride=ROW_PITCH), :] = jnp.zeros((TM, LANES), F32)
        yield

    n_chunks = D_MODEL // MIX_CHUNK
    _emit_interleaved((stage_b(), [512] * (2 * D_MODEL // PROJ_CHUNK)),
                      (stage_a(), [100] + ([600] * N_Q_HEADS + [500]) * nblk),
                      (stage_c(), [1000, 1000] + [550] * n_chunks + [1000]))


def _mix_call(t_prompt, sink, xn, q, k, v, gu, vgn, sa, sb, ws, bs, wa, wb, wo, g1, b1, wr, br, later_weights):
    t_all = xn.shape[0]
    nblk = TM // BLOCK
    blk_p, blk_all = t_prompt // BLOCK, t_all // BLOCK
    n_tiles = t_all // TM
    front = lambda i: jnp.minimum(i, n_tiles - 1)
    back = lambda i: jnp.clip(i - 1, 0, n_tiles - 1)
    last = lambda i: jnp.maximum(i - 2, 0)

    def tok(width, which):
        return pl.BlockSpec((TM, width), lambda i: (which(i), 0))

    prev_spec = pl.BlockSpec((BLOCK, KV_WIDTH), lambda i: (jnp.maximum(front(i) * nblk - 1, 0), 0))
    next_spec = pl.BlockSpec((BLOCK, KV_WIDTH),
                             lambda i: (jnp.minimum((front(i) + 1) * nblk, blk_all - 1), 0))
    kv_spec = tok(KV_WIDTH, front)
    cast_steps, cast_specs = _cast_plan(later_weights, n_tiles + 2)
    _, cast_out_specs = _cast_plan(later_weights, n_tiles + 2)
    kernel = functools.partial(_mix_kernel, n_tiles, (0, blk_p), (blk_p - 1, blk_all - 1),
                               cast_steps, len(later_weights))
    return pl.pallas_call(
        kernel,
        out_shape=[jax.ShapeDtypeStruct((t_all * ROW_PITCH, LANES), F32), jax.ShapeDtypeStruct((8, t_all), F32)]
        + [jax.ShapeDtypeStruct(w.shape, BF16) for w in later_weights],
        grid=(n_tiles + 2,),
        in_specs=[pl.BlockSpec(memory_space=pltpu.SMEM),
                  tok(D_MODEL, back), tok(ATTN_WIDTH, front), prev_spec, kv_spec, next_spec, prev_spec, kv_spec, next_spec,
                  tok(GMLP_WIDTH, front), tok(GMLP_WIDTH, front), tok(D_MODEL, back), tok(D_MODEL, back),
                  _const_spec((GMLP_GROUPS, BLOCK, BLOCK)), _const_spec((GMLP_GROUPS, BLOCK, BLOCK)),
                  _const_spec((ATTN_WIDTH, D_MODEL)), _const_spec((GMLP_WIDTH, D_MODEL)),
                  _const_spec((D_MODEL, D_MODEL)), _const_spec((1, D_MODEL)), _const_spec((1, D_MODEL)),
                  _const_spec((D_MODEL, 2 * LANES)), _const_spec((1, LANES))] + cast_specs,
        out_specs=[pl.BlockSpec((TM * ROW_PITCH, LANES), lambda i: (last(i), 0)),
                   pl.BlockSpec((8, TM), lambda i: (0, last(i)))] + cast_out_specs,
        scratch_shapes=[pltpu.VMEM((TM + 2 * BLOCK, KV_WIDTH), BF16),
                        pltpu.VMEM((TM + 2 * BLOCK, KV_WIDTH), BF16),
                        pltpu.VMEM((2, TM, ATTN_WIDTH), BF16),
                        pltpu.VMEM((2, TM, GMLP_WIDTH), BF16),
                        pltpu.VMEM((TM, D_MODEL), BF16),
                        pltpu.VMEM((2, TM, D_MODEL), F32)],
        compiler_params=pltpu.CompilerParams(dimension_semantics=("arbitrary",),
                                             vmem_limit_bytes=VMEM_LIMIT),
        name="mix",
    )(sink, xn, q, k, k, k, v, v, v, gu, vgn, sa, sb, ws, bs, wa, wb, wo, g1, b1, wr, br, *later_weights)


def _gather_copy(src_hbm, dst_buf, sem, src_row, dst_row, n_rows):
    return pltpu.make_async_copy(src_hbm.at[pl.ds(src_row, n_rows)], dst_buf.at[pl.ds(dst_row, n_rows)], sem)


def _start_gather(token_of, src_hbm, dst_buf, sem, n_rows, src_pitch=ROW_PITCH):
    def issue(r, carry):
        src_row = pl.multiple_of(token_of(r) * src_pitch, 8)
        dst_row = pl.multiple_of(r * ROW_PITCH, 8)
        _gather_copy(src_hbm, dst_buf, sem, src_row, dst_row, n_rows).start()
        return carry

    lax.fori_loop(0, TM, issue, 0, unroll=8)


def _wait_gather(src_hbm, dst_buf, sem, n_rows):
    _gather_copy(src_hbm, dst_buf, sem, 0, 0, TM * n_rows).wait()


def _idx_specs(n_tiles, lookahead=1):
    def spec(tile_of_step):
        return pl.BlockSpec((None, 1, TM), lambda i, *_: (jnp.minimum(tile_of_step(i), n_tiles - 1), 0, 0),
                            memory_space=pltpu.SMEM)

    return [spec(lambda i, t=t: t) for t in range(lookahead)] + [spec(lambda i: i + lookahead)]


def _unsort_kernel(*refs):
    first_idx_refs, idxn_ref = refs[:UNSORT_LOOKAHEAD], refs[UNSORT_LOOKAHEAD]
    ys_hbm, g2_ref, b2_ref, out_ref, buf, sem = refs[UNSORT_LOOKAHEAD + 1:]
    i = pl.program_id(0)
    n = pl.num_programs(0)
    n_buf = UNSORT_LOOKAHEAD + 1
    slot = i % n_buf

    def gather(tile, idx_ref, when):
        dst = tile % n_buf

        @pl.when(when)
        def _():
            _start_gather(lambda r: idx_ref[0, r], ys_hbm, buf.at[dst], sem.at[dst], X_ROWS, src_pitch=X_ROWS)

    for tile, idx_ref in enumerate(first_idx_refs):
        gather(tile, idx_ref, jnp.logical_and(i == 0, tile < n))
    gather(i + UNSORT_LOOKAHEAD, idxn_ref, i + UNSORT_LOOKAHEAD < n)

    cur = buf.at[slot]
    _wait_gather(ys_hbm, cur, sem.at[slot], X_ROWS)
    for r in range(X_ROWS):
        out_ref[:, r * LANES:(r + 1) * LANES] = _load_token_major(cur, r)
    out_ref[...] = _layer_norm(out_ref[...], g2_ref[...], b2_ref[...])


def _unsort_call(idx, ysorted, g2, b2, name):
    n_tiles = idx.shape[0] // TM
    idx = idx.reshape(n_tiles, 1, TM)
    idx_specs = _idx_specs(n_tiles, UNSORT_LOOKAHEAD)
    n_buf = UNSORT_LOOKAHEAD + 1
    return pl.pallas_call(
        _unsort_kernel,
        out_shape=jax.ShapeDtypeStruct((n_tiles * TM, D_MODEL), F32),
        grid=(n_tiles,),
        in_specs=idx_specs + [pl.BlockSpec(memory_space=pl.ANY),
                              _const_spec((1, D_MODEL)), _const_spec((1, D_MODEL))],
        out_specs=pl.BlockSpec((TM, D_MODEL), lambda i: (i, 0)),
        scratch_shapes=[pltpu.VMEM((n_buf, TM * ROW_PITCH, LANES), F32), pltpu.SemaphoreType.DMA((n_buf,))],
        compiler_params=pltpu.CompilerParams(dimension_semantics=("arbitrary",),
                                             vmem_limit_bytes=VMEM_LIMIT),
        name=name,
    )(*([idx] * len(idx_specs)), ysorted, g2, b2)


def _moe_kernel(elo_ref, ehi_ref, nvalid_ref, first_ref, order_ref, x1t_hbm,
                wg_lo, wu_lo, wd_lo, wg_hi, wu_hi, wd_hi, out_ref, buf, sem, xb_scr):
    del elo_ref, ehi_ref
    i = pl.program_id(0)
    n = pl.num_programs(0)
    slot = i % MOE_BUFFERS

    for tile in range(MOE_LOOKAHEAD):
        @pl.when(jnp.logical_and(jnp.logical_and(i == 0, tile < n), nvalid_ref[jnp.minimum(tile, n - 1)] > 0))
        def _(tile=tile):
            first = first_ref[tile]
            _start_gather(lambda r: order_ref[first + r], x1t_hbm, buf.at[tile], sem.at[tile], ROW_PITCH)

    def experts():
        cur = buf.at[slot]
        _wait_gather(x1t_hbm, cur, sem.at[slot], ROW_PITCH)
        for r in range(X_ROWS):
            xb_scr[:, r * LANES:(r + 1) * LANES] = _load_token_major(cur, r).astype(BF16)
        route = _load_token_major(cur, X_ROWS)
        xb = xb_scr[...]
        yield

        def hidden(wg, wu, c):
            gate = jnp.dot(xb, wg[...], preferred_element_type=F32)
            up = jnp.dot(xb, wu[...], preferred_element_type=F32)
            return ((jax.nn.silu(gate) * up) * c).astype(BF16)

        h_lo = hidden(wg_lo, wu_lo, route[:, 0:1])
        yield
        h_hi = hidden(wg_hi, wu_hi, route[:, 1:2])
        yield
        for c in range(0, D_MODEL, COL_CHUNK):
            y = (jnp.dot(h_lo, wd_lo[:, c:c + COL_CHUNK], preferred_element_type=F32)
                 + jnp.dot(h_hi, wd_hi[:, c:c + COL_CHUNK], preferred_element_type=F32))
            for r in range(c // LANES, (c + COL_CHUNK) // LANES):
                out_ref[pl.ds(r, TM, stride=X_ROWS), :] = (DEEPNORM_ALPHA * _load_token_major(cur, r)
                                                           + y[:, r * LANES - c:(r + 1) * LANES - c])
            yield

    def start_later_gather():
        dst = (i + MOE_LOOKAHEAD) % MOE_BUFFERS
        first = first_ref[later]
        for r in range(TM):
            src_row = pl.multiple_of(order_ref[first + r] * ROW_PITCH, 8)
            _gather_copy(x1t_hbm, buf.at[dst], sem.at[dst], src_row, r * ROW_PITCH, ROW_PITCH).start()
            if r % ISSUE_GROUP == ISSUE_GROUP - 1:
                yield

    expert_costs = [1, 4, 4] + [2] * (D_MODEL // COL_CHUNK)
    later = jnp.minimum(i + MOE_LOOKAHEAD, n - 1)
    gather_later = jnp.logical_and(i + MOE_LOOKAHEAD < n, nvalid_ref[later] > 0)

    @pl.when(jnp.logical_and(nvalid_ref[i] > 0, gather_later))
    def _():
        _emit_interleaved((experts(), expert_costs), (start_later_gather(), [1] * (TM // ISSUE_GROUP)))

    @pl.when(jnp.logical_and(nvalid_ref[i] > 0, jnp.logical_not(gather_later)))
    def _():
        _emit_interleaved((experts(), expert_costs))

    @pl.when(nvalid_ref[i] == 0)
    def _():
        out_ref[...] = jnp.zeros(out_ref.shape, out_ref.dtype)


def _moe_call(e_lo, e_hi, nvalid, first, order, x1t, wg, wu, wd):
    n_tiles = nvalid.shape[0]

    def w_spec(shape, which):
        def imap(i, elo, ehi, *_):
            return ((elo, ehi)[which][i], 0, 0)
        return pl.BlockSpec((None,) + shape, imap)

    up_shape, down_shape = (D_MODEL, EXPERT_FF), (EXPERT_FF, D_MODEL)
    grid_spec = pltpu.PrefetchScalarGridSpec(
        num_scalar_prefetch=5,
        grid=(n_tiles,),
        in_specs=[pl.BlockSpec(memory_space=pl.ANY),
                  w_spec(up_shape, 0), w_spec(up_shape, 0), w_spec(down_shape, 0),
                  w_spec(up_shape, 1), w_spec(up_shape, 1), w_spec(down_shape, 1)],
        out_specs=pl.BlockSpec((TM * X_ROWS, LANES), lambda i, *_: (i, 0)),
        scratch_shapes=[pltpu.VMEM((MOE_BUFFERS, TM * ROW_PITCH, LANES), F32),
                        pltpu.SemaphoreType.DMA((MOE_BUFFERS,)),
                        pltpu.VMEM((TM, D_MODEL), BF16)],
    )
    return pl.pallas_call(
        _moe_kernel,
        out_shape=jax.ShapeDtypeStruct((n_tiles * TM * X_ROWS, LANES), F32),
        grid_spec=grid_spec,
        compiler_params=pltpu.CompilerParams(dimension_semantics=("arbitrary",),
                                             vmem_limit_bytes=VMEM_LIMIT),
        name="moe",
    )(e_lo, e_hi, nvalid, first, order, x1t, wg, wu, wd, wg, wu, wd)


_PAIR_LO = np.array([0, 0, 0, 1, 1, 2], np.int32)
_PAIR_HI = np.array([1, 2, 3, 2, 3, 3], np.int32)


def _bucket_layout(bucket, n_slots):
    t_all = bucket.shape[0]
    n_tiles = n_slots // TM
    rows = t_all // LANES
    onehot = (bucket.reshape(rows, LANES, 1) == jnp.arange(N_BUCKETS, dtype=jnp.int32)).astype(F32)
    earlier = (jnp.arange(LANES)[:, None] > jnp.arange(LANES)[None, :]).astype(F32)
    within = jnp.einsum("ts,rsb->rtb", earlier, onehot)
    row_total = jnp.sum(onehot, axis=1)
    row_start = jnp.cumsum(row_total, axis=0) - row_total
    counts = jnp.sum(row_total, axis=0).astype(jnp.int32)
    padded = ((counts + TM - 1) // TM) * TM
    ends = jnp.cumsum(padded)
    starts = ends - padded
    slot = within + (row_start + starts.astype(F32))[:, None, :]
    dest = jnp.sum(slot * onehot, axis=-1).reshape(t_all).astype(jnp.int32)
    order = jnp.concatenate([jnp.argsort(bucket, stable=True).astype(jnp.int32), jnp.zeros((TM,), jnp.int32)])
    tile_start = jnp.arange(n_tiles, dtype=jnp.int32) * TM
    owner_start = jnp.minimum(tile_start, jnp.maximum(ends[-1:] - TM, 0))[:, None]
    owner = jnp.logical_and(owner_start >= starts[None, :], owner_start < ends[None, :]).astype(jnp.int32)
    remaining = counts[None, :] - (tile_start[:, None] - starts[None, :])
    nvalid = jnp.sum(owner * jnp.clip(remaining, 0, TM), axis=1)
    unpadded_start = jnp.cumsum(counts) - counts
    first = jnp.sum(owner * (unpadded_start[None, :] + tile_start[:, None] - starts[None, :]), axis=1)
    first = jnp.clip(first, 0, t_all)
    bucket_ids = np.arange(N_BUCKETS)
    first_expert = (bucket_ids // N_PAIRS) * EXPERTS_PER_GROUP
    e_lo = jnp.sum(owner * jnp.asarray(first_expert + _PAIR_LO[bucket_ids % N_PAIRS], jnp.int32)[None, :], axis=1)
    e_hi = jnp.sum(owner * jnp.asarray(first_expert + _PAIR_HI[bucket_ids % N_PAIRS], jnp.int32)[None, :], axis=1)
    return order, first.astype(jnp.int32), dest, e_lo.astype(jnp.int32), e_hi.astype(jnp.int32), nvalid.astype(jnp.int32)


def _layer(xp, xs, in_ln_g, in_ln_b, w_in, attn_sink, gmlp_w_s, gmlp_b_s, gmlp_ln_g, gmlp_ln_b,
           w_attn_branch, w_gmlp_branch, w_out, ln1_g, ln1_b,
           router_w_group, router_b_group, router_w_expert, router_b_expert,
           w_expert_gate, w_expert_up, w_expert_down, ln2_g, ln2_b):
    t_p, t_s = xp.shape[0], xs.shape[0]
    t_all = t_p + t_s
    row = lambda p: p.reshape(1, -1).astype(F32)

    rows_of = lambda w: w.astype(F32).reshape(-1, w.shape[-1])
    xn, q, k, v, gu, vgn, sa, sb, wa, wb, wo, wd = _proj_call(
        xp, xs, row(in_ln_g), row(in_ln_b), w_in.astype(BF16), gmlp_ln_g.astype(F32), gmlp_ln_b.astype(F32),
        [rows_of(w_attn_branch), rows_of(w_gmlp_branch), rows_of(w_out), rows_of(w_expert_down)])

    wr = jnp.concatenate([router_w_group, router_w_expert], axis=1).astype(F32)
    wr = jnp.pad(wr, ((0, 0), (0, LANES - wr.shape[1])))
    wr_hi = wr.astype(BF16)
    wr_lo = (wr - wr_hi.astype(F32)).astype(BF16)
    br = jnp.pad(jnp.concatenate([router_b_group, router_b_expert]).astype(F32),
                 (0, LANES - N_EXPERT_GROUPS - N_EXPERTS)).reshape(1, LANES)
    bs = jnp.broadcast_to(gmlp_b_s.astype(F32)[:, :, None], (GMLP_GROUPS, BLOCK, BLOCK))

    x1t, routed, wg, wu = _mix_call(t_p, attn_sink.astype(F32), xn,
                                    q, k, v, gu, vgn, sa, sb, gmlp_w_s.astype(BF16), bs, wa, wb, wo,
                                    row(ln1_g), row(ln1_b), jnp.concatenate([wr_hi, wr_lo], axis=1), br,
                                    [rows_of(w_expert_gate), rows_of(w_expert_up)])

    n_slots = t_all + N_BUCKETS * TM
    bucket = routed[2].astype(jnp.int32)
    order, first, dest, e_lo, e_hi, nvalid = _bucket_layout(bucket, n_slots)
    zsorted = _moe_call(e_lo, e_hi, nvalid, first, order, x1t, wg.reshape(w_expert_gate.shape),
                        wu.reshape(w_expert_up.shape), wd.reshape(w_expert_down.shape))
    g2, b2 = row(ln2_g), row(ln2_b)
    return (_unsort_call(dest[:t_p], zsorted, g2, b2, "unsort_prompt"),
            _unsort_call(dest[t_p:], zsorted, g2, b2, "unsort_sample"))


def kernel(x_prompt, x_sample, in_ln_g, in_ln_b, w_in, attn_sink, gmlp_w_s, gmlp_b_s, gmlp_ln_g, gmlp_ln_b,
           w_attn_branch, w_gmlp_branch, w_out, ln1_g, ln1_b,
           router_w_group, router_b_group, router_w_expert, router_b_expert,
           w_expert_gate, w_expert_up, w_expert_down, ln2_g, ln2_b):
    bp, sp, d = x_prompt.shape
    bs, ss, _ = x_sample.shape
    assert bp == 1 and bs == 1 and d == D_MODEL and sp % TM == 0 and ss % TM == 0
    assert w_in.shape[0] == 1, "one layer"
    yp, ys = _layer(x_prompt.reshape(sp, d), x_sample.reshape(ss, d), in_ln_g, in_ln_b, w_in[0], attn_sink[0],
                    gmlp_w_s[0], gmlp_b_s[0], gmlp_ln_g[0], gmlp_ln_b[0],
                    w_attn_branch[0], w_gmlp_branch[0], w_out[0], ln1_g[0], ln1_b[0],
                    router_w_group[0], router_b_group[0], router_w_expert[0], router_b_expert[0],
                    w_expert_gate[0], w_expert_up[0], w_expert_down[0], ln2_g[0], ln2_b[0])
    return yp.reshape(1, sp, d), ys.reshape(1, ss, d)
```

```python
import functools

import numpy as np
import jax
import jax.numpy as jnp
from jax import lax
from jax.experimental import pallas as pl
from jax.experimental.pallas import tpu as pltpu

F32 = jnp.float32
BF16 = jnp.bfloat16

D_MODEL = 2048
HEAD_DIM = 128
N_Q_HEADS = 8
N_KV_HEADS = 2
Q_PER_KV = N_Q_HEADS // N_KV_HEADS
ATTN_WIDTH = N_Q_HEADS * HEAD_DIM
KV_WIDTH = N_KV_HEADS * HEAD_DIM
WINDOW = 128
BLOCK = 128
GMLP_WIDTH = D_MODEL // 2
GMLP_GROUPS = 8
GMLP_GROUP_DIM = GMLP_WIDTH // GMLP_GROUPS
N_EXPERT_GROUPS = 4
EXPERTS_PER_GROUP = 4
N_EXPERTS = N_EXPERT_GROUPS * EXPERTS_PER_GROUP
EXPERT_FF = 512
LN_EPS = 1e-5
DEEPNORM_ALPHA = 2.0 ** 0.25
NEG_INF = -1e9
ATTN_SCALE = HEAD_DIM ** -0.5

_C_Q = 0
_C_K = _C_Q + ATTN_WIDTH
_C_V = _C_K + KV_WIDTH
_C_U = _C_V + KV_WIDTH
_C_VG = _C_U + GMLP_WIDTH
_C_GA = _C_VG + GMLP_WIDTH
_C_GB = _C_GA + D_MODEL
IN_COLS = _C_GB + D_MODEL

LANES = 128
BF16_SUBLANES = 16
N_PAIRS = 6
N_BUCKETS = N_EXPERT_GROUPS * N_PAIRS
X_ROWS = D_MODEL // LANES
ROW_PITCH = 24
TM = 256
COL_CHUNK = 512
MIX_CHUNK = 256
PROJ_CHUNK = 256
MOE_LOOKAHEAD = 3
MOE_BUFFERS = MOE_LOOKAHEAD + 1
ISSUE_GROUP = 16
UNSORT_LOOKAHEAD = 4
VMEM_LIMIT = 60 * 1024 * 1024

_SLOPES = [float(2.0 ** (-8.0 * (h + 1) / N_Q_HEADS)) for h in range(N_Q_HEADS)]


def _layer_norm(x, g, b):
    mu = jnp.mean(x, axis=-1, keepdims=True)
    xc = x - mu
    var = jnp.mean(xc * xc, axis=-1, keepdims=True)
    return xc * lax.rsqrt(var + LN_EPS) * g + b


def _load_token_major(ref, r):
    return ref[pl.ds(r, TM, stride=ROW_PITCH), :]


def _emit_interleaved(*streams):
    order = []
    for s, (_, costs) in enumerate(streams):
        done = 0.0
        for c in costs:
            order.append(((done + c / 2) / sum(costs), s))
            done += c
    for _, s in sorted(order):
        next(streams[s][0])
    for gen, _ in streams:
        assert next(gen, "done") == "done", "stream has more units than declared"


def _cast_plan(weights, n_steps):
    steps = 1 << (n_steps.bit_length() - 1)
    steps = min([steps] + [w.shape[0] // BF16_SUBLANES for w in weights])
    specs = [pl.BlockSpec((w.shape[0] // steps, w.shape[1]), lambda i, *_: (jnp.minimum(i, steps - 1), 0))
             for w in weights]
    return steps, specs


def _cast_side_job(steps, src_refs, dst_refs):
    @pl.when(pl.program_id(0) < steps)
    def _():
        if len(dst_refs) == len(src_refs):
            for src, dst in zip(src_refs, dst_refs):
                dst[...] = src[...].astype(dst.dtype)
        else:
            (dst,), col = dst_refs, 0
            for src in src_refs:
                dst[:, col:col + src.shape[1]] = src[...].astype(dst.dtype)
                col += src.shape[1]


def _const_spec(shape):
    nd = len(shape)
    return pl.BlockSpec(shape, lambda i, *_: (0,) * nd, pipeline_mode=pl.Buffered(1))


def _proj_kernel(n_prompt_tiles, cast_steps, n_cast, *refs):
    xp_ref, xs_ref, g0_ref, b0_ref, w_ref, lg_ref, lb_ref = refs[:7]
    cast_src = refs[7:7 + n_cast]
    xn_ref, q_ref, k_ref, v_ref, gu_ref, vgn_ref, sa_ref, sb_ref = refs[7 + n_cast:15 + n_cast]
    cast_dst = refs[15 + n_cast:15 + 2 * n_cast]
    xn_scr = refs[15 + 2 * n_cast]
    i = pl.program_id(0)
    _cast_side_job(cast_steps, cast_src, cast_dst)

    @pl.when(i < n_prompt_tiles)
    def _():
        xn_ref[...] = _layer_norm(xp_ref[...], g0_ref[...], b0_ref[...])

    @pl.when(i >= n_prompt_tiles)
    def _():
        xn_ref[...] = _layer_norm(xs_ref[...], g0_ref[...], b0_ref[...])

    xn_scr[...] = xn_ref[...].astype(BF16)
    xn = xn_scr[...]

    def proj(c0, width):
        return jnp.dot(xn, w_ref[:, c0:c0 + width], preferred_element_type=F32)

    for c in range(0, ATTN_WIDTH, COL_CHUNK):
        q_ref[:, c:c + COL_CHUNK] = proj(_C_Q + c, COL_CHUNK).astype(BF16)
    kv = proj(_C_K, 2 * KV_WIDTH)
    k_ref[...] = kv[:, :KV_WIDTH].astype(BF16)
    v_ref[...] = kv[:, KV_WIDTH:].astype(BF16)
    for c in range(0, GMLP_WIDTH, COL_CHUNK):
        gu_ref[:, c:c + COL_CHUNK] = jax.nn.gelu(proj(_C_U + c, COL_CHUNK)).astype(BF16)
    for c in range(0, GMLP_WIDTH, COL_CHUNK):
        vg = jax.nn.gelu(proj(_C_VG + c, COL_CHUNK))
        for j in range(COL_CHUNK // GMLP_GROUP_DIM):
            grp = c // GMLP_GROUP_DIM + j
            blk = vg[:, j * GMLP_GROUP_DIM:(j + 1) * GMLP_GROUP_DIM]
            y = _layer_norm(blk, lg_ref[grp:grp + 1, :], lb_ref[grp:grp + 1, :])
            vgn_ref[:, grp * GMLP_GROUP_DIM:(grp + 1) * GMLP_GROUP_DIM] = y.astype(BF16)
    for c in range(0, D_MODEL, COL_CHUNK):
        sa_ref[:, c:c + COL_CHUNK] = jax.nn.sigmoid(proj(_C_GA + c, COL_CHUNK)).astype(BF16)
    for c in range(0, D_MODEL, COL_CHUNK):
        sb_ref[:, c:c + COL_CHUNK] = jax.nn.sigmoid(proj(_C_GB + c, COL_CHUNK)).astype(BF16)


def _proj_call(xp, xs, g0, b0, w_in, lg, lb, later_weights):
    n_p, n_s = xp.shape[0] // TM, xs.shape[0] // TM
    t_all = xp.shape[0] + xs.shape[0]
    cast_steps, cast_specs = _cast_plan(later_weights, n_p + n_s)
    _, cast_out_specs = _cast_plan(later_weights, n_p + n_s)
    xp_spec = pl.BlockSpec((TM, D_MODEL), lambda i: (jnp.minimum(i, n_p - 1), 0))
    xs_spec = pl.BlockSpec((TM, D_MODEL), lambda i: (jnp.maximum(i - n_p, 0), 0))

    def tok(width):
        return pl.BlockSpec((TM, width), lambda i: (i, 0))

    widths = (ATTN_WIDTH, KV_WIDTH, KV_WIDTH, GMLP_WIDTH, GMLP_WIDTH, D_MODEL, D_MODEL)
    return pl.pallas_call(
        functools.partial(_proj_kernel, n_p, cast_steps, len(later_weights)),
        out_shape=([jax.ShapeDtypeStruct((t_all, D_MODEL), F32)]
                   + [jax.ShapeDtypeStruct((t_all, w), BF16) for w in widths]
                   + [jax.ShapeDtypeStruct(w.shape, BF16) for w in later_weights]),
        grid=(n_p + n_s,),
        in_specs=[xp_spec, xs_spec, _const_spec((1, D_MODEL)), _const_spec((1, D_MODEL)),
                  _const_spec((D_MODEL, IN_COLS)),
                  _const_spec((GMLP_GROUPS, GMLP_GROUP_DIM)), _const_spec((GMLP_GROUPS, GMLP_GROUP_DIM))]
        + cast_specs,
        out_specs=[tok(D_MODEL)] + [tok(w) for w in widths] + cast_out_specs,
        scratch_shapes=[pltpu.VMEM((TM, D_MODEL), BF16)],
        compiler_params=pltpu.CompilerParams(dimension_semantics=("arbitrary",),
                                             vmem_limit_bytes=VMEM_LIMIT),
        name="proj",
    )(xp, xs, g0, b0, w_in, lg, lb, *later_weights)


def _route(logits_t):
    row = lambda j: logits_t[j:j + 1, :]
    gl = [row(j) for j in range(N_EXPERT_GROUPS)]
    gmax, gidx = gl[0], jnp.zeros(gl[0].shape, jnp.int32)
    for j in range(1, N_EXPERT_GROUPS):
        better = gl[j] > gmax
        gmax = jnp.where(better, gl[j], gmax)
        gidx = jnp.where(better, j, gidx)
    gsum = jnp.exp(gl[0] - gmax)
    for j in range(1, N_EXPERT_GROUPS):
        gsum = gsum + jnp.exp(gl[j] - gmax)
    p_group = 1.0 / gsum

    ig = []
    for e in range(EXPERTS_PER_GROUP):
        v = row(N_EXPERT_GROUPS + (N_EXPERT_GROUPS - 1) * EXPERTS_PER_GROUP + e)
        for g in range(N_EXPERT_GROUPS - 2, -1, -1):
            v = jnp.where(gidx == g, row(N_EXPERT_GROUPS + g * EXPERTS_PER_GROUP + e), v)
        ig.append(v)
    v1, i1 = ig[0], jnp.zeros(ig[0].shape, jnp.int32)
    for e in range(1, EXPERTS_PER_GROUP):
        better = ig[e] > v1
        v1 = jnp.where(better, ig[e], v1)
        i1 = jnp.where(better, e, i1)
    v2 = jnp.where(i1 == 0, ig[1], ig[0])
    i2 = jnp.where(i1 == 0, 1, 0).astype(jnp.int32)
    for e in range(1, EXPERTS_PER_GROUP):
        better = jnp.logical_and(i1 != e, ig[e] > v2)
        v2 = jnp.where(better, ig[e], v2)
        i2 = jnp.where(better, e, i2)
    ev = jnp.exp(v2 - v1)
    ssum = 1.0 + ev
    w1 = 1.0 / ssum
    w2 = ev / ssum
    first_lo = i1 < i2
    lo = jnp.minimum(i1, i2)
    hi = jnp.maximum(i1, i2)
    c_lo = jnp.where(first_lo, w1, w2) * p_group
    c_hi = jnp.where(first_lo, w2, w1) * p_group
    pair = jnp.where(lo == 0, hi - 1, jnp.where(lo == 1, hi + 1, N_PAIRS - 1))
    bucket = gidx * N_PAIRS + pair
    return c_lo, c_hi, bucket.astype(F32)


def _mix_kernel(n_tiles, seq_edges_first, seq_edges_last, cast_steps, n_cast, *refs):
    (sink_ref, xn_ref, q_ref, kp_ref, km_ref, kn_ref, vp_ref, vm_ref, vn_ref,
     gu_ref, vgn_ref, sa_ref, sb_ref, ws_ref, bs_ref,
     wa_ref, wb_ref, wo_ref, g1_ref, b1_ref, wr_ref, br_ref) = refs[:22]
    cast_src = refs[22:22 + n_cast]
    x1t_ref, rt_ref = refs[22 + n_cast:24 + n_cast]
    cast_dst = refs[24 + n_cast:25 + n_cast]
    kf_scr, vf_scr, a_scr, sg_scr, mg_scr, mix_scr = refs[25 + n_cast:]
    _cast_side_job(cast_steps, cast_src, cast_dst)
    i = pl.program_id(0)
    nblk = TM // BLOCK
    cur = i % 2

    @pl.when(i == 0)
    def _():
        a_scr[1] = jnp.zeros(a_scr.shape[1:], a_scr.dtype)
        sg_scr[1] = jnp.zeros(sg_scr.shape[1:], sg_scr.dtype)
        mix_scr[0] = jnp.zeros(mix_scr.shape[1:], mix_scr.dtype)

    def stage_a():
        tile = jnp.minimum(i, n_tiles - 1)
        kf_scr[0:BLOCK, :] = kp_ref[...]
        kf_scr[BLOCK:BLOCK + TM, :] = km_ref[...]
        kf_scr[BLOCK + TM:, :] = kn_ref[...]
        vf_scr[0:BLOCK, :] = vp_ref[...]
        vf_scr[BLOCK:BLOCK + TM, :] = vm_ref[...]
        vf_scr[BLOCK + TM:, :] = vn_ref[...]
        first_blk = tile * nblk
        last_blk = tile * nblk + nblk - 1
        has_prev = jnp.logical_and(*[first_blk != e for e in seq_edges_first])
        has_next = jnp.logical_and(*[last_blk != e for e in seq_edges_last])
        kj = lax.broadcasted_iota(jnp.int32, (BLOCK, 3 * BLOCK), 1)
        qi = lax.broadcasted_iota(jnp.int32, (BLOCK, 3 * BLOCK), 0)
        dist = jnp.abs(kj - BLOCK - qi)
        in_window = dist <= WINDOW
        dist_f = dist.astype(F32)
        lo_key = jnp.where(has_prev, 0, BLOCK)
        hi_key = jnp.where(has_next, 3 * BLOCK, 2 * BLOCK)
        yield
        for j in range(nblk):
            mask = in_window
            if j == 0:
                mask = jnp.logical_and(mask, kj >= lo_key)
            if j == nblk - 1:
                mask = jnp.logical_and(mask, kj < hi_key)
            r0 = j * BLOCK
            for kvh in range(N_KV_HEADS):
                c0 = kvh * HEAD_DIM
                kb = kf_scr[r0:r0 + 3 * BLOCK, c0:c0 + HEAD_DIM]
                vb = vf_scr[r0:r0 + 3 * BLOCK, c0:c0 + HEAD_DIM]
                heads = [kvh * Q_PER_KV + g for g in range(Q_PER_KV)]
                qs = jnp.concatenate(
                    [q_ref[r0:r0 + BLOCK, h * HEAD_DIM:(h + 1) * HEAD_DIM] for h in heads], axis=0)
                s_all = lax.dot_general(qs, kb, (((1,), (1,)), ((), ())), preferred_element_type=F32)
                for g, h in enumerate(heads):
                    s = s_all[g * BLOCK:(g + 1) * BLOCK, :] * ATTN_SCALE
                    s = jnp.where(mask, s + dist_f * (-_SLOPES[h]), NEG_INF)
                    sink = sink_ref[h]
                    m = jnp.maximum(jnp.max(s, axis=-1, keepdims=True), sink)
                    p = jnp.exp(s - m)
                    denom = jnp.sum(p, axis=-1, keepdims=True) + jnp.exp(sink - m)
                    pn = (p * (1.0 / denom)).astype(BF16)
                    o = jnp.dot(pn, vb, preferred_element_type=F32)
                    a_scr[cur, r0:r0 + BLOCK, h * HEAD_DIM:(h + 1) * HEAD_DIM] = o.astype(BF16)
                    yield
            for grp in range(GMLP_GROUPS):
                c0 = grp * GMLP_GROUP_DIM
                sp = jnp.dot(ws_ref[grp], vgn_ref[r0:r0 + BLOCK, c0:c0 + GMLP_GROUP_DIM],
                             preferred_element_type=F32) + bs_ref[grp]
                u = gu_ref[r0:r0 + BLOCK, c0:c0 + GMLP_GROUP_DIM].astype(F32)
                sg_scr[cur, r0:r0 + BLOCK, c0:c0 + GMLP_GROUP_DIM] = (u * sp).astype(BF16)
            yield

    def stage_b():
        a = a_scr[1 - cur]
        sg = sg_scr[1 - cur]
        for c in range(0, D_MODEL, PROJ_CHUNK):
            ma = jnp.dot(a, wa_ref[:, c:c + PROJ_CHUNK], preferred_element_type=F32)
            mb = jnp.dot(sg, wb_ref[:, c:c + PROJ_CHUNK], preferred_element_type=F32)
            merged = (sa_ref[:, c:c + PROJ_CHUNK].astype(F32) * ma
                      + sb_ref[:, c:c + PROJ_CHUNK].astype(F32) * mb)
            mg_scr[:, c:c + PROJ_CHUNK] = merged.astype(BF16)
            yield
        mg = mg_scr[...]
        for c in range(0, D_MODEL, PROJ_CHUNK):
            mix = jnp.dot(mg, wo_ref[:, c:c + PROJ_CHUNK], preferred_element_type=F32)
            mix_scr[1 - cur, :, c:c + PROJ_CHUNK] = DEEPNORM_ALPHA * xn_ref[:, c:c + PROJ_CHUNK] + mix
            yield

    def stage_c():
        mu = jnp.mean(mix_scr[cur], axis=-1, keepdims=True)
        yield
        zc = mix_scr[cur] - mu
        inv = lax.rsqrt(jnp.mean(zc * zc, axis=-1, keepdims=True) + LN_EPS)
        yield
        r = None
        for c in range(0, D_MODEL, MIX_CHUNK):
            x1 = ((mix_scr[cur, :, c:c + MIX_CHUNK] - mu) * inv * g1_ref[:, c:c + MIX_CHUNK]
                  + b1_ref[:, c:c + MIX_CHUNK])
            for k in range(MIX_CHUNK // LANES):
                x1t_ref[pl.ds(c // LANES + k, TM, stride=ROW_PITCH), :] = x1[:, k * LANES:(k + 1) * LANES]
            x_hi = x1.astype(BF16)
            x_lo = (x1 - x_hi.astype(F32)).astype(BF16)
            part = (jnp.dot(x_hi, wr_ref[c:c + MIX_CHUNK, :], preferred_element_type=F32)
                    + jnp.dot(x_lo, wr_ref[c:c + MIX_CHUNK, :], preferred_element_type=F32))
            r = part if r is None else r + part
            yield
        logits = r[:, :LANES] + r[:, LANES:] + br_ref[...]
        c_lo, c_hi, bucket = _route(logits.T)
        routed = jnp.concatenate([c_lo, c_hi, bucket, jnp.zeros((5, TM), F32)], axis=0)
        rt_ref[...] = routed
        padded = jnp.concatenate([routed, jnp.zeros((LANES - 8, TM), F32)], axis=0)
        x1t_ref[pl.ds(X_ROWS, TM, stride=ROW_PITCH), :] = padded.T
        for k in range(X_ROWS + 1, ROW_PITCH):
            x1t_ref[pl.ds(k, TM, stride=ROW_PITCH), :] = jnp.zeros((TM, LANES), F32)
        yield

    n_chunks = D_MODEL // MIX_CHUNK
    _emit_interleaved((stage_b(), [512] * (2 * D_MODEL // PROJ_CHUNK)),
                      (stage_a(), [100] + ([600] * N_Q_HEADS + [500]) * nblk),
                      (stage_c(), [1000, 1000] + [550] * n_chunks + [1000]))


def _mix_call(t_prompt, sink, xn, q, k, v, gu, vgn, sa, sb, ws, bs, wa, wb, wo, g1, b1, wr, br, later_weights):
    t_all = xn.shape[0]
    nblk = TM // BLOCK
    blk_p, blk_all = t_prompt // BLOCK, t_all // BLOCK
    n_tiles = t_all // TM
    front = lambda i: jnp.minimum(i, n_tiles - 1)
    back = lambda i: jnp.clip(i - 1, 0, n_tiles - 1)
    last = lambda i: jnp.maximum(i - 2, 0)

    def tok(width, which):
        return pl.BlockSpec((TM, width), lambda i: (which(i), 0))

    prev_spec = pl.BlockSpec((BLOCK, KV_WIDTH), lambda i: (jnp.maximum(front(i) * nblk - 1, 0), 0))
    next_spec = pl.BlockSpec((BLOCK, KV_WIDTH),
                             lambda i: (jnp.minimum((front(i) + 1) * nblk, blk_all - 1), 0))
    kv_spec = tok(KV_WIDTH, front)
    cast_steps, cast_specs = _cast_plan(later_weights, n_tiles + 2)
    joined = jax.ShapeDtypeStruct((later_weights[0].shape[0], sum(w.shape[1] for w in later_weights)), BF16)
    joined_steps, cast_out_specs = _cast_plan([joined], n_tiles + 2)
    assert joined_steps == cast_steps
    kernel = functools.partial(_mix_kernel, n_tiles, (0, blk_p), (blk_p - 1, blk_all - 1),
                               cast_steps, len(later_weights))
    return pl.pallas_call(
        kernel,
        out_shape=[jax.ShapeDtypeStruct((t_all * ROW_PITCH, LANES), F32), jax.ShapeDtypeStruct((8, t_all), F32),
                   joined],
        grid=(n_tiles + 2,),
        in_specs=[pl.BlockSpec(memory_space=pltpu.SMEM),
                  tok(D_MODEL, back), tok(ATTN_WIDTH, front), prev_spec, kv_spec, next_spec, prev_spec, kv_spec, next_spec,
                  tok(GMLP_WIDTH, front), tok(GMLP_WIDTH, front), tok(D_MODEL, back), tok(D_MODEL, back),
                  _const_spec((GMLP_GROUPS, BLOCK, BLOCK)), _const_spec((GMLP_GROUPS, BLOCK, BLOCK)),
                  _const_spec((ATTN_WIDTH, D_MODEL)), _const_spec((GMLP_WIDTH, D_MODEL)),
                  _const_spec((D_MODEL, D_MODEL)), _const_spec((1, D_MODEL)), _const_spec((1, D_MODEL)),
                  _const_spec((D_MODEL, 2 * LANES)), _const_spec((1, LANES))] + cast_specs,
        out_specs=[pl.BlockSpec((TM * ROW_PITCH, LANES), lambda i: (last(i), 0)),
                   pl.BlockSpec((8, TM), lambda i: (0, last(i)))] + cast_out_specs,
        scratch_shapes=[pltpu.VMEM((TM + 2 * BLOCK, KV_WIDTH), BF16),
                        pltpu.VMEM((TM + 2 * BLOCK, KV_WIDTH), BF16),
                        pltpu.VMEM((2, TM, ATTN_WIDTH), BF16),
                        pltpu.VMEM((2, TM, GMLP_WIDTH), BF16),
                        pltpu.VMEM((TM, D_MODEL), BF16),
                        pltpu.VMEM((2, TM, D_MODEL), F32)],
        compiler_params=pltpu.CompilerParams(dimension_semantics=("arbitrary",),
                                             vmem_limit_bytes=VMEM_LIMIT),
        name="mix",
    )(sink, xn, q, k, k, k, v, v, v, gu, vgn, sa, sb, ws, bs, wa, wb, wo, g1, b1, wr, br, *later_weights)


def _gather_copy(src_hbm, dst_buf, sem, src_row, dst_row, n_rows):
    return pltpu.make_async_copy(src_hbm.at[pl.ds(src_row, n_rows)], dst_buf.at[pl.ds(dst_row, n_rows)], sem)


def _start_gather(token_of, src_hbm, dst_buf, sem, n_rows, src_pitch=ROW_PITCH):
    def issue(r, carry):
        src_row = pl.multiple_of(token_of(r) * src_pitch, 8)
        dst_row = pl.multiple_of(r * ROW_PITCH, 8)
        _gather_copy(src_hbm, dst_buf, sem, src_row, dst_row, n_rows).start()
        return carry

    lax.fori_loop(0, TM, issue, 0, unroll=8)


def _wait_gather(src_hbm, dst_buf, sem, n_rows):
    _gather_copy(src_hbm, dst_buf, sem, 0, 0, TM * n_rows).wait()


def _idx_specs(n_tiles, lookahead=1):
    def spec(tile_of_step):
        return pl.BlockSpec((None, 1, TM), lambda i, *_: (jnp.minimum(tile_of_step(i), n_tiles - 1), 0, 0),
                            memory_space=pltpu.SMEM)

    return [spec(lambda i, t=t: t) for t in range(lookahead)] + [spec(lambda i: i + lookahead)]


def _unsort_kernel(*refs):
    first_idx_refs, idxn_ref = refs[:UNSORT_LOOKAHEAD], refs[UNSORT_LOOKAHEAD]
    ys_hbm, g2_ref, b2_ref, out_ref, buf, sem = refs[UNSORT_LOOKAHEAD + 1:]
    i = pl.program_id(0)
    n = pl.num_programs(0)
    n_buf = UNSORT_LOOKAHEAD + 1
    slot = i % n_buf

    def gather(tile, idx_ref, when):
        dst = tile % n_buf

        @pl.when(when)
        def _():
            _start_gather(lambda r: idx_ref[0, r], ys_hbm, buf.at[dst], sem.at[dst], X_ROWS, src_pitch=X_ROWS)

    for tile, idx_ref in enumerate(first_idx_refs):
        gather(tile, idx_ref, jnp.logical_and(i == 0, tile < n))
    gather(i + UNSORT_LOOKAHEAD, idxn_ref, i + UNSORT_LOOKAHEAD < n)

    cur = buf.at[slot]
    _wait_gather(ys_hbm, cur, sem.at[slot], X_ROWS)
    for r in range(X_ROWS):
        out_ref[:, r * LANES:(r + 1) * LANES] = _load_token_major(cur, r)
    out_ref[...] = _layer_norm(out_ref[...], g2_ref[...], b2_ref[...])


def _unsort_call(idx, ysorted, g2, b2, name):
    n_tiles = idx.shape[0] // TM
    idx = idx.reshape(n_tiles, 1, TM)
    idx_specs = _idx_specs(n_tiles, UNSORT_LOOKAHEAD)
    n_buf = UNSORT_LOOKAHEAD + 1
    return pl.pallas_call(
        _unsort_kernel,
        out_shape=jax.ShapeDtypeStruct((n_tiles * TM, D_MODEL), F32),
        grid=(n_tiles,),
        in_specs=idx_specs + [pl.BlockSpec(memory_space=pl.ANY),
                              _const_spec((1, D_MODEL)), _const_spec((1, D_MODEL))],
        out_specs=pl.BlockSpec((TM, D_MODEL), lambda i: (i, 0)),
        scratch_shapes=[pltpu.VMEM((n_buf, TM * ROW_PITCH, LANES), F32), pltpu.SemaphoreType.DMA((n_buf,))],
        compiler_params=pltpu.CompilerParams(dimension_semantics=("arbitrary",),
                                             vmem_limit_bytes=VMEM_LIMIT),
        name=name,
    )(*([idx] * len(idx_specs)), ysorted, g2, b2)


def _moe_kernel(elo_ref, ehi_ref, nvalid_ref, first_ref, order_ref, x1t_hbm,
                wgu_lo, wd_lo, wgu_hi, wd_hi, out_ref, buf, sem, xb_scr):
    del elo_ref, ehi_ref
    i = pl.program_id(0)
    n = pl.num_programs(0)
    slot = i % MOE_BUFFERS

    for tile in range(MOE_LOOKAHEAD):
        @pl.when(jnp.logical_and(jnp.logical_and(i == 0, tile < n), nvalid_ref[jnp.minimum(tile, n - 1)] > 0))
        def _(tile=tile):
            first = first_ref[tile]
            _start_gather(lambda r: order_ref[first + r], x1t_hbm, buf.at[tile], sem.at[tile], ROW_PITCH)

    def experts():
        cur = buf.at[slot]
        _wait_gather(x1t_hbm, cur, sem.at[slot], ROW_PITCH)
        for r in range(X_ROWS):
            xb_scr[:, r * LANES:(r + 1) * LANES] = _load_token_major(cur, r).astype(BF16)
        route = _load_token_major(cur, X_ROWS)
        xb = xb_scr[...]
        yield

        def hidden(wgu, c):
            gate_up = jnp.dot(xb, wgu[...], preferred_element_type=F32)
            gate, up = gate_up[:, :EXPERT_FF], gate_up[:, EXPERT_FF:]
            return ((jax.nn.silu(gate) * up) * c).astype(BF16)

        h_lo = hidden(wgu_lo, route[:, 0:1])
        yield
        h_hi = hidden(wgu_hi, route[:, 1:2])
        yield
        for c in range(0, D_MODEL, COL_CHUNK):
            y = (jnp.dot(h_lo, wd_lo[:, c:c + COL_CHUNK], preferred_element_type=F32)
                 + jnp.dot(h_hi, wd_hi[:, c:c + COL_CHUNK], preferred_element_type=F32))
            for r in range(c // LANES, (c + COL_CHUNK) // LANES):
                out_ref[pl.ds(r, TM, stride=X_ROWS), :] = (DEEPNORM_ALPHA * _load_token_major(cur, r)
                                                           + y[:, r * LANES - c:(r + 1) * LANES - c])
            yield

    def start_later_gather():
        dst = (i + MOE_LOOKAHEAD) % MOE_BUFFERS
        first = first_ref[later]
        for r in range(TM):
            src_row = pl.multiple_of(order_ref[first + r] * ROW_PITCH, 8)
            _gather_copy(x1t_hbm, buf.at[dst], sem.at[dst], src_row, r * ROW_PITCH, ROW_PITCH).start()
            if r % ISSUE_GROUP == ISSUE_GROUP - 1:
                yield

    expert_costs = [1, 4, 4] + [2] * (D_MODEL // COL_CHUNK)
    later = jnp.minimum(i + MOE_LOOKAHEAD, n - 1)
    gather_later = jnp.logical_and(i + MOE_LOOKAHEAD < n, nvalid_ref[later] > 0)

    @pl.when(jnp.logical_and(nvalid_ref[i] > 0, gather_later))
    def _():
        _emit_interleaved((experts(), expert_costs), (start_later_gather(), [1] * (TM // ISSUE_GROUP)))

    @pl.when(jnp.logical_and(nvalid_ref[i] > 0, jnp.logical_not(gather_later)))
    def _():
        _emit_interleaved((experts(), expert_costs))

    @pl.when(nvalid_ref[i] == 0)
    def _():
        out_ref[...] = jnp.zeros(out_ref.shape, out_ref.dtype)


def _moe_call(e_lo, e_hi, nvalid, first, order, x1t, wgu, wd):
    n_tiles = nvalid.shape[0]

    def w_spec(shape, which):
        def imap(i, elo, ehi, *_):
            return ((elo, ehi)[which][i], 0, 0)
        return pl.BlockSpec((None,) + shape, imap)

    up_shape, down_shape = (D_MODEL, 2 * EXPERT_FF), (EXPERT_FF, D_MODEL)
    grid_spec = pltpu.PrefetchScalarGridSpec(
        num_scalar_prefetch=5,
        grid=(n_tiles,),
        in_specs=[pl.BlockSpec(memory_space=pl.ANY),
                  w_spec(up_shape, 0), w_spec(down_shape, 0), w_spec(up_shape, 1), w_spec(down_shape, 1)],
        out_specs=pl.BlockSpec((TM * X_ROWS, LANES), lambda i, *_: (i, 0)),
        scratch_shapes=[pltpu.VMEM((MOE_BUFFERS, TM * ROW_PITCH, LANES), F32),
                        pltpu.SemaphoreType.DMA((MOE_BUFFERS,)),
                        pltpu.VMEM((TM, D_MODEL), BF16)],
    )
    return pl.pallas_call(
        _moe_kernel,
        out_shape=jax.ShapeDtypeStruct((n_tiles * TM * X_ROWS, LANES), F32),
        grid_spec=grid_spec,
        compiler_params=pltpu.CompilerParams(dimension_semantics=("arbitrary",),
                                             vmem_limit_bytes=VMEM_LIMIT),
        name="moe",
    )(e_lo, e_hi, nvalid, first, order, x1t, wgu, wd, wgu, wd)


_PAIR_LO = np.array([0, 0, 0, 1, 1, 2], np.int32)
_PAIR_HI = np.array([1, 2, 3, 2, 3, 3], np.int32)


def _bucket_layout(bucket, n_slots):
    t_all = bucket.shape[0]
    n_tiles = n_slots // TM
    rows = t_all // LANES
    onehot = (bucket.reshape(rows, LANES, 1) == jnp.arange(N_BUCKETS, dtype=jnp.int32)).astype(F32)
    earlier = (jnp.arange(LANES)[:, None] > jnp.arange(LANES)[None, :]).astype(F32)
    within = jnp.einsum("ts,rsb->rtb", earlier, onehot)
    row_total = jnp.sum(onehot, axis=1)
    row_start = jnp.cumsum(row_total, axis=0) - row_total
    counts = jnp.sum(row_total, axis=0).astype(jnp.int32)
    padded = ((counts + TM - 1) // TM) * TM
    ends = jnp.cumsum(padded)
    starts = ends - padded
    slot = within + (row_start + starts.astype(F32))[:, None, :]
    dest = jnp.sum(slot * onehot, axis=-1).reshape(t_all).astype(jnp.int32)
    order = jnp.concatenate([jnp.argsort(bucket, stable=True).astype(jnp.int32), jnp.zeros((TM,), jnp.int32)])
    tile_start = jnp.arange(n_tiles, dtype=jnp.int32) * TM
    owner_start = jnp.minimum(tile_start, jnp.maximum(ends[-1:] - TM, 0))[:, None]
    owner = jnp.logical_and(owner_start >= starts[None, :], owner_start < ends[None, :]).astype(jnp.int32)
    remaining = counts[None, :] - (tile_start[:, None] - starts[None, :])
    nvalid = jnp.sum(owner * jnp.clip(remaining, 0, TM), axis=1)
    unpadded_start = jnp.cumsum(counts) - counts
    first = jnp.sum(owner * (unpadded_start[None, :] + tile_start[:, None] - starts[None, :]), axis=1)
    first = jnp.clip(first, 0, t_all)
    bucket_ids = np.arange(N_BUCKETS)
    first_expert = (bucket_ids // N_PAIRS) * EXPERTS_PER_GROUP
    e_lo = jnp.sum(owner * jnp.asarray(first_expert + _PAIR_LO[bucket_ids % N_PAIRS], jnp.int32)[None, :], axis=1)
    e_hi = jnp.sum(owner * jnp.asarray(first_expert + _PAIR_HI[bucket_ids % N_PAIRS], jnp.int32)[None, :], axis=1)
    return order, first.astype(jnp.int32), dest, e_lo.astype(jnp.int32), e_hi.astype(jnp.int32), nvalid.astype(jnp.int32)


def _layer(xp, xs, in_ln_g, in_ln_b, w_in, attn_sink, gmlp_w_s, gmlp_b_s, gmlp_ln_g, gmlp_ln_b,
           w_attn_branch, w_gmlp_branch, w_out, ln1_g, ln1_b,
           router_w_group, router_b_group, router_w_expert, router_b_expert,
           w_expert_gate, w_expert_up, w_expert_down, ln2_g, ln2_b):
    t_p, t_s = xp.shape[0], xs.shape[0]
    t_all = t_p + t_s
    row = lambda p: p.reshape(1, -1).astype(F32)

    rows_of = lambda w: w.astype(F32).reshape(-1, w.shape[-1])
    xn, q, k, v, gu, vgn, sa, sb, wa, wb, wo, wd = _proj_call(
        xp, xs, row(in_ln_g), row(in_ln_b), w_in.astype(BF16), gmlp_ln_g.astype(F32), gmlp_ln_b.astype(F32),
        [rows_of(w_attn_branch), rows_of(w_gmlp_branch), rows_of(w_out), rows_of(w_expert_down)])

    wr = jnp.concatenate([router_w_group, router_w_expert], axis=1).astype(F32)
    wr = jnp.pad(wr, ((0, 0), (0, LANES - wr.shape[1])))
    wr_hi = wr.astype(BF16)
    wr_lo = (wr - wr_hi.astype(F32)).astype(BF16)
    br = jnp.pad(jnp.concatenate([router_b_group, router_b_expert]).astype(F32),
                 (0, LANES - N_EXPERT_GROUPS - N_EXPERTS)).reshape(1, LANES)
    bs = jnp.broadcast_to(gmlp_b_s.astype(F32)[:, :, None], (GMLP_GROUPS, BLOCK, BLOCK))

    x1t, routed, wgu = _mix_call(t_p, attn_sink.astype(F32), xn,
                                    q, k, v, gu, vgn, sa, sb, gmlp_w_s.astype(BF16), bs, wa, wb, wo,
                                    row(ln1_g), row(ln1_b), jnp.concatenate([wr_hi, wr_lo], axis=1), br,
                                    [rows_of(w_expert_gate), rows_of(w_expert_up)])

    n_slots = t_all + N_BUCKETS * TM
    bucket = routed[2].astype(jnp.int32)
    order, first, dest, e_lo, e_hi, nvalid = _bucket_layout(bucket, n_slots)
    zsorted = _moe_call(e_lo, e_hi, nvalid, first, order, x1t,
                        wgu.reshape(N_EXPERTS, D_MODEL, 2 * EXPERT_FF), wd.reshape(w_expert_down.shape))
    g2, b2 = row(ln2_g), row(ln2_b)
    return (_unsort_call(dest[:t_p], zsorted, g2, b2, "unsort_prompt"),
            _unsort_call(dest[t_p:], zsorted, g2, b2, "unsort_sample"))


def kernel(x_prompt, x_sample, in_ln_g, in_ln_b, w_in, attn_sink, gmlp_w_s, gmlp_b_s, gmlp_ln_g, gmlp_ln_b,
           w_attn_branch, w_gmlp_branch, w_out, ln1_g, ln1_b,
           router_w_group, router_b_group, router_w_expert, router_b_expert,
           w_expert_gate, w_expert_up, w_expert_down, ln2_g, ln2_b):
    bp, sp, d = x_prompt.shape
    bs, ss, _ = x_sample.shape
    assert bp == 1 and bs == 1 and d == D_MODEL and sp % TM == 0 and ss % TM == 0
    assert w_in.shape[0] == 1, "one layer"
    yp, ys = _layer(x_prompt.reshape(sp, d), x_sample.reshape(ss, d), in_ln_g, in_ln_b, w_in[0], attn_sink[0],
                    gmlp_w_s[0], gmlp_b_s[0], gmlp_ln_g[0], gmlp_ln_b[0],
                    w_attn_branch[0], w_gmlp_branch[0], w_out[0], ln1_g[0], ln1_b[0],
                    router_w_group[0], router_b_group[0], router_w_expert[0], router_b_expert[0],
                    w_expert_gate[0], w_expert_up[0], w_expert_down[0], ln2_g[0], ln2_b[0])
    return yp.reshape(1, sp, d), ys.reshape(1, ss, d)
```

```python
import functools

import numpy as np
import jax
import jax.numpy as jnp
from jax import lax
from jax.experimental import pallas as pl
from jax.experimental.pallas import tpu as pltpu

F32 = jnp.float32
BF16 = jnp.bfloat16

D_MODEL = 2048
HEAD_DIM = 128
N_Q_HEADS = 8
N_KV_HEADS = 2
Q_PER_KV = N_Q_HEADS // N_KV_HEADS
ATTN_WIDTH = N_Q_HEADS * HEAD_DIM
KV_WIDTH = N_KV_HEADS * HEAD_DIM
WINDOW = 128
BLOCK = 128
GMLP_WIDTH = D_MODEL // 2
GMLP_GROUPS = 8
GMLP_GROUP_DIM = GMLP_WIDTH // GMLP_GROUPS
N_EXPERT_GROUPS = 4
EXPERTS_PER_GROUP = 4
N_EXPERTS = N_EXPERT_GROUPS * EXPERTS_PER_GROUP
EXPERT_FF = 512
LN_EPS = 1e-5
DEEPNORM_ALPHA = 2.0 ** 0.25
NEG_INF = -1e9
ATTN_SCALE = HEAD_DIM ** -0.5

_C_Q = 0
_C_K = _C_Q + ATTN_WIDTH
_C_V = _C_K + KV_WIDTH
_C_U = _C_V + KV_WIDTH
_C_VG = _C_U + GMLP_WIDTH
_C_GA = _C_VG + GMLP_WIDTH
_C_GB = _C_GA + D_MODEL
IN_COLS = _C_GB + D_MODEL

LANES = 128
SUBLANES = 8
BF16_SUBLANES = 16
ROUTE_ROWS = 3
N_PAIRS = 6
N_BUCKETS = N_EXPERT_GROUPS * N_PAIRS
X_ROWS = D_MODEL // LANES
ROW_PITCH = 24
TM = 256
COL_CHUNK = 512
MIX_CHUNK = 256
PROJ_CHUNK = 256
MOE_LOOKAHEAD = 3
MOE_BUFFERS = MOE_LOOKAHEAD + 1
ISSUE_GROUP = 16
UNSORT_LOOKAHEAD = 4
VMEM_LIMIT = 60 * 1024 * 1024

_SLOPES = [float(2.0 ** (-8.0 * (h + 1) / N_Q_HEADS)) for h in range(N_Q_HEADS)]


def _layer_norm(x, g, b):
    mu = jnp.mean(x, axis=-1, keepdims=True)
    xc = x - mu
    var = jnp.mean(xc * xc, axis=-1, keepdims=True)
    return xc * lax.rsqrt(var + LN_EPS) * g + b


def _load_token_major(ref, r):
    return ref[pl.ds(r, TM, stride=ROW_PITCH), :]


def _emit_interleaved(*streams):
    order = []
    for s, (_, costs) in enumerate(streams):
        done = 0.0
        for c in costs:
            order.append(((done + c / 2) / sum(costs), s))
            done += c
    for _, s in sorted(order):
        next(streams[s][0])
    for gen, _ in streams:
        assert next(gen, "done") == "done", "stream has more units than declared"


def _cast_plan(weights, n_steps):
    steps = 1 << (n_steps.bit_length() - 1)
    steps = min([steps] + [w.shape[0] // BF16_SUBLANES for w in weights])
    specs = [pl.BlockSpec((w.shape[0] // steps, w.shape[1]), lambda i, *_: (jnp.minimum(i, steps - 1), 0))
             for w in weights]
    return steps, specs


def _cast_side_job(steps, src_refs, dst_refs):
    @pl.when(pl.program_id(0) < steps)
    def _():
        if len(dst_refs) == len(src_refs):
            for src, dst in zip(src_refs, dst_refs):
                dst[...] = src[...].astype(dst.dtype)
        else:
            (dst,), col = dst_refs, 0
            for src in src_refs:
                dst[:, col:col + src.shape[1]] = src[...].astype(dst.dtype)
                col += src.shape[1]


def _const_spec(shape):
    nd = len(shape)
    return pl.BlockSpec(shape, lambda i, *_: (0,) * nd, pipeline_mode=pl.Buffered(1))


def _proj_kernel(n_prompt_tiles, cast_steps, n_cast, *refs):
    xp_ref, xs_ref, g0_ref, b0_ref, w_ref, lg_ref, lb_ref = refs[:7]
    cast_src = refs[7:7 + n_cast]
    xn_ref, q_ref, k_ref, v_ref, gu_ref, vgn_ref, sa_ref, sb_ref = refs[7 + n_cast:15 + n_cast]
    cast_dst = refs[15 + n_cast:15 + 2 * n_cast]
    xn_scr = refs[15 + 2 * n_cast]
    i = pl.program_id(0)
    _cast_side_job(cast_steps, cast_src, cast_dst)

    @pl.when(i < n_prompt_tiles)
    def _():
        xn_ref[...] = _layer_norm(xp_ref[...], g0_ref[...], b0_ref[...])

    @pl.when(i >= n_prompt_tiles)
    def _():
        xn_ref[...] = _layer_norm(xs_ref[...], g0_ref[...], b0_ref[...])

    xn_scr[...] = xn_ref[...].astype(BF16)
    xn = xn_scr[...]

    def proj(c0, width):
        return jnp.dot(xn, w_ref[:, c0:c0 + width], preferred_element_type=F32)

    for c in range(0, ATTN_WIDTH, COL_CHUNK):
        q_ref[:, c:c + COL_CHUNK] = proj(_C_Q + c, COL_CHUNK).astype(BF16)
    kv = proj(_C_K, 2 * KV_WIDTH)
    k_ref[...] = kv[:, :KV_WIDTH].astype(BF16)
    v_ref[...] = kv[:, KV_WIDTH:].astype(BF16)
    for c in range(0, GMLP_WIDTH, COL_CHUNK):
        gu_ref[:, c:c + COL_CHUNK] = jax.nn.gelu(proj(_C_U + c, COL_CHUNK)).astype(BF16)
    for c in range(0, GMLP_WIDTH, COL_CHUNK):
        vg = jax.nn.gelu(proj(_C_VG + c, COL_CHUNK))
        for j in range(COL_CHUNK // GMLP_GROUP_DIM):
            grp = c // GMLP_GROUP_DIM + j
            blk = vg[:, j * GMLP_GROUP_DIM:(j + 1) * GMLP_GROUP_DIM]
            y = _layer_norm(blk, lg_ref[grp:grp + 1, :], lb_ref[grp:grp + 1, :])
            vgn_ref[:, grp * GMLP_GROUP_DIM:(grp + 1) * GMLP_GROUP_DIM] = y.astype(BF16)
    for c in range(0, D_MODEL, COL_CHUNK):
        sa_ref[:, c:c + COL_CHUNK] = jax.nn.sigmoid(proj(_C_GA + c, COL_CHUNK)).astype(BF16)
    for c in range(0, D_MODEL, COL_CHUNK):
        sb_ref[:, c:c + COL_CHUNK] = jax.nn.sigmoid(proj(_C_GB + c, COL_CHUNK)).astype(BF16)


def _proj_call(xp, xs, g0, b0, w_in, lg, lb, later_weights):
    n_p, n_s = xp.shape[0] // TM, xs.shape[0] // TM
    t_all = xp.shape[0] + xs.shape[0]
    cast_steps, cast_specs = _cast_plan(later_weights, n_p + n_s)
    _, cast_out_specs = _cast_plan(later_weights, n_p + n_s)
    xp_spec = pl.BlockSpec((TM, D_MODEL), lambda i: (jnp.minimum(i, n_p - 1), 0))
    xs_spec = pl.BlockSpec((TM, D_MODEL), lambda i: (jnp.maximum(i - n_p, 0), 0))

    def tok(width):
        return pl.BlockSpec((TM, width), lambda i: (i, 0))

    widths = (ATTN_WIDTH, KV_WIDTH, KV_WIDTH, GMLP_WIDTH, GMLP_WIDTH, D_MODEL, D_MODEL)
    return pl.pallas_call(
        functools.partial(_proj_kernel, n_p, cast_steps, len(later_weights)),
        out_shape=([jax.ShapeDtypeStruct((t_all, D_MODEL), F32)]
                   + [jax.ShapeDtypeStruct((t_all, w), BF16) for w in widths]
                   + [jax.ShapeDtypeStruct(w.shape, BF16) for w in later_weights]),
        grid=(n_p + n_s,),
        in_specs=[xp_spec, xs_spec, _const_spec((1, D_MODEL)), _const_spec((1, D_MODEL)),
                  _const_spec((D_MODEL, IN_COLS)),
                  _const_spec((GMLP_GROUPS, GMLP_GROUP_DIM)), _const_spec((GMLP_GROUPS, GMLP_GROUP_DIM))]
        + cast_specs,
        out_specs=[tok(D_MODEL)] + [tok(w) for w in widths] + cast_out_specs,
        scratch_shapes=[pltpu.VMEM((TM, D_MODEL), BF16)],
        compiler_params=pltpu.CompilerParams(dimension_semantics=("arbitrary",),
                                             vmem_limit_bytes=VMEM_LIMIT),
        name="proj",
    )(xp, xs, g0, b0, w_in, lg, lb, *later_weights)


def _route(logits_t):
    row = lambda j: logits_t[j:j + 1, :]
    gl = [row(j) for j in range(N_EXPERT_GROUPS)]
    gmax, gidx = gl[0], jnp.zeros(gl[0].shape, jnp.int32)
    for j in range(1, N_EXPERT_GROUPS):
        better = gl[j] > gmax
        gmax = jnp.where(better, gl[j], gmax)
        gidx = jnp.where(better, j, gidx)
    gsum = jnp.exp(gl[0] - gmax)
    for j in range(1, N_EXPERT_GROUPS):
        gsum = gsum + jnp.exp(gl[j] - gmax)
    p_group = 1.0 / gsum

    ig = []
    for e in range(EXPERTS_PER_GROUP):
        v = row(N_EXPERT_GROUPS + (N_EXPERT_GROUPS - 1) * EXPERTS_PER_GROUP + e)
        for g in range(N_EXPERT_GROUPS - 2, -1, -1):
            v = jnp.where(gidx == g, row(N_EXPERT_GROUPS + g * EXPERTS_PER_GROUP + e), v)
        ig.append(v)
    v1, i1 = ig[0], jnp.zeros(ig[0].shape, jnp.int32)
    for e in range(1, EXPERTS_PER_GROUP):
        better = ig[e] > v1
        v1 = jnp.where(better, ig[e], v1)
        i1 = jnp.where(better, e, i1)
    v2 = jnp.where(i1 == 0, ig[1], ig[0])
    i2 = jnp.where(i1 == 0, 1, 0).astype(jnp.int32)
    for e in range(1, EXPERTS_PER_GROUP):
        better = jnp.logical_and(i1 != e, ig[e] > v2)
        v2 = jnp.where(better, ig[e], v2)
        i2 = jnp.where(better, e, i2)
    ev = jnp.exp(v2 - v1)
    ssum = 1.0 + ev
    w1 = 1.0 / ssum
    w2 = ev / ssum
    first_lo = i1 < i2
    lo = jnp.minimum(i1, i2)
    hi = jnp.maximum(i1, i2)
    c_lo = jnp.where(first_lo, w1, w2) * p_group
    c_hi = jnp.where(first_lo, w2, w1) * p_group
    pair = jnp.where(lo == 0, hi - 1, jnp.where(lo == 1, hi + 1, N_PAIRS - 1))
    bucket = gidx * N_PAIRS + pair
    return c_lo, c_hi, bucket.astype(F32)


def _mix_kernel(n_tiles, seq_edges_first, seq_edges_last, cast_steps, n_cast, *refs):
    (sink_ref, xn_ref, q_ref, kp_ref, km_ref, kn_ref, vp_ref, vm_ref, vn_ref,
     gu_ref, vgn_ref, sa_ref, sb_ref, ws_ref, bs_ref,
     wa_ref, wb_ref, wo_ref, g1_ref, b1_ref, wr_ref, br_ref) = refs[:22]
    cast_src = refs[22:22 + n_cast]
    x1t_ref, rt_ref = refs[22 + n_cast:24 + n_cast]
    cast_dst = refs[24 + n_cast:25 + n_cast]
    kf_scr, vf_scr, a_scr, sg_scr, mg_scr, mix_scr = refs[25 + n_cast:]
    _cast_side_job(cast_steps, cast_src, cast_dst)
    i = pl.program_id(0)
    nblk = TM // BLOCK
    cur = i % 2

    @pl.when(i == 0)
    def _():
        a_scr[1] = jnp.zeros(a_scr.shape[1:], a_scr.dtype)
        sg_scr[1] = jnp.zeros(sg_scr.shape[1:], sg_scr.dtype)
        mix_scr[0] = jnp.zeros(mix_scr.shape[1:], mix_scr.dtype)

    def stage_a():
        tile = jnp.minimum(i, n_tiles - 1)
        kf_scr[0:BLOCK, :] = kp_ref[...]
        kf_scr[BLOCK:BLOCK + TM, :] = km_ref[...]
        kf_scr[BLOCK + TM:, :] = kn_ref[...]
        vf_scr[0:BLOCK, :] = vp_ref[...]
        vf_scr[BLOCK:BLOCK + TM, :] = vm_ref[...]
        vf_scr[BLOCK + TM:, :] = vn_ref[...]
        first_blk = tile * nblk
        last_blk = tile * nblk + nblk - 1
        has_prev = jnp.logical_and(*[first_blk != e for e in seq_edges_first])
        has_next = jnp.logical_and(*[last_blk != e for e in seq_edges_last])
        kj = lax.broadcasted_iota(jnp.int32, (BLOCK, 3 * BLOCK), 1)
        qi = lax.broadcasted_iota(jnp.int32, (BLOCK, 3 * BLOCK), 0)
        dist = jnp.abs(kj - BLOCK - qi)
        in_window = dist <= WINDOW
        dist_f = dist.astype(F32)
        lo_key = jnp.where(has_prev, 0, BLOCK)
        hi_key = jnp.where(has_next, 3 * BLOCK, 2 * BLOCK)
        yield
        for j in range(nblk):
            mask = in_window
            if j == 0:
                mask = jnp.logical_and(mask, kj >= lo_key)
            if j == nblk - 1:
                mask = jnp.logical_and(mask, kj < hi_key)
            r0 = j * BLOCK
            for kvh in range(N_KV_HEADS):
                c0 = kvh * HEAD_DIM
                kb = kf_scr[r0:r0 + 3 * BLOCK, c0:c0 + HEAD_DIM]
                vb = vf_scr[r0:r0 + 3 * BLOCK, c0:c0 + HEAD_DIM]
                heads = [kvh * Q_PER_KV + g for g in range(Q_PER_KV)]
                qs = jnp.concatenate(
                    [q_ref[r0:r0 + BLOCK, h * HEAD_DIM:(h + 1) * HEAD_DIM] for h in heads], axis=0)
                s_all = lax.dot_general(qs, kb, (((1,), (1,)), ((), ())), preferred_element_type=F32)
                for g, h in enumerate(heads):
                    s = s_all[g * BLOCK:(g + 1) * BLOCK, :] * ATTN_SCALE
                    s = jnp.where(mask, s + dist_f * (-_SLOPES[h]), NEG_INF)
                    sink = sink_ref[h]
                    m = jnp.maximum(jnp.max(s, axis=-1, keepdims=True), sink)
                    p = jnp.exp(s - m)
                    denom = jnp.sum(p, axis=-1, keepdims=True) + jnp.exp(sink - m)
                    pn = (p * (1.0 / denom)).astype(BF16)
                    o = jnp.dot(pn, vb, preferred_element_type=F32)
                    a_scr[cur, r0:r0 + BLOCK, h * HEAD_DIM:(h + 1) * HEAD_DIM] = o.astype(BF16)
                    yield
            for grp in range(GMLP_GROUPS):
                c0 = grp * GMLP_GROUP_DIM
                sp = jnp.dot(ws_ref[grp], vgn_ref[r0:r0 + BLOCK, c0:c0 + GMLP_GROUP_DIM],
                             preferred_element_type=F32) + bs_ref[grp]
                u = gu_ref[r0:r0 + BLOCK, c0:c0 + GMLP_GROUP_DIM].astype(F32)
                sg_scr[cur, r0:r0 + BLOCK, c0:c0 + GMLP_GROUP_DIM] = (u * sp).astype(BF16)
            yield

    def stage_b():
        a = a_scr[1 - cur]
        sg = sg_scr[1 - cur]
        for c in range(0, D_MODEL, PROJ_CHUNK):
            ma = jnp.dot(a, wa_ref[:, c:c + PROJ_CHUNK], preferred_element_type=F32)
            mb = jnp.dot(sg, wb_ref[:, c:c + PROJ_CHUNK], preferred_element_type=F32)
            merged = (sa_ref[:, c:c + PROJ_CHUNK].astype(F32) * ma
                      + sb_ref[:, c:c + PROJ_CHUNK].astype(F32) * mb)
            mg_scr[:, c:c + PROJ_CHUNK] = merged.astype(BF16)
            yield
        mg = mg_scr[...]
        for c in range(0, D_MODEL, PROJ_CHUNK):
            mix = jnp.dot(mg, wo_ref[:, c:c + PROJ_CHUNK], preferred_element_type=F32)
            mix_scr[1 - cur, :, c:c + PROJ_CHUNK] = DEEPNORM_ALPHA * xn_ref[:, c:c + PROJ_CHUNK] + mix
            yield

    def stage_c():
        mu = jnp.mean(mix_scr[cur], axis=-1, keepdims=True)
        yield
        zc = mix_scr[cur] - mu
        inv = lax.rsqrt(jnp.mean(zc * zc, axis=-1, keepdims=True) + LN_EPS)
        yield
        r = None
        for c in range(0, D_MODEL, MIX_CHUNK):
            x1 = ((mix_scr[cur, :, c:c + MIX_CHUNK] - mu) * inv * g1_ref[:, c:c + MIX_CHUNK]
                  + b1_ref[:, c:c + MIX_CHUNK])
            for k in range(MIX_CHUNK // LANES):
                x1t_ref[pl.ds(c // LANES + k, TM, stride=ROW_PITCH), :] = x1[:, k * LANES:(k + 1) * LANES]
            x_hi = x1.astype(BF16)
            x_lo = (x1 - x_hi.astype(F32)).astype(BF16)
            part = (jnp.dot(x_hi, wr_ref[c:c + MIX_CHUNK, :], preferred_element_type=F32)
                    + jnp.dot(x_lo, wr_ref[c:c + MIX_CHUNK, :], preferred_element_type=F32))
            r = part if r is None else r + part
            yield
        logits = r[:, :LANES] + r[:, LANES:] + br_ref[...]
        c_lo, c_hi, bucket = _route(logits.T)
        routed = jnp.concatenate([c_lo, c_hi, bucket, jnp.zeros((SUBLANES - ROUTE_ROWS, TM), F32)], axis=0)
        rt_ref[...] = routed
        padded = jnp.concatenate([routed, jnp.zeros((LANES - SUBLANES, TM), F32)], axis=0)
        x1t_ref[pl.ds(X_ROWS, TM, stride=ROW_PITCH), :] = padded.T
        for k in range(X_ROWS + 1, ROW_PITCH):
            x1t_ref[pl.ds(k, TM, stride=ROW_PITCH), :] = jnp.zeros((TM, LANES), F32)
        yield

    n_chunks = D_MODEL // MIX_CHUNK
    _emit_interleaved((stage_b(), [512] * (2 * D_MODEL // PROJ_CHUNK)),
                      (stage_a(), [100] + ([600] * N_Q_HEADS + [500]) * nblk),
                      (stage_c(), [1000, 1000] + [550] * n_chunks + [1000]))


def _mix_call(t_prompt, sink, xn, q, k, v, gu, vgn, sa, sb, ws, bs, wa, wb, wo, g1, b1, wr, br, later_weights):
    t_all = xn.shape[0]
    nblk = TM // BLOCK
    blk_p, blk_all = t_prompt // BLOCK, t_all // BLOCK
    n_tiles = t_all // TM
    front = lambda i: jnp.minimum(i, n_tiles - 1)
    back = lambda i: jnp.clip(i - 1, 0, n_tiles - 1)
    last = lambda i: jnp.maximum(i - 2, 0)

    def tok(width, which):
        return pl.BlockSpec((TM, width), lambda i: (which(i), 0))

    prev_spec = pl.BlockSpec((BLOCK, KV_WIDTH), lambda i: (jnp.maximum(front(i) * nblk - 1, 0), 0))
    next_spec = pl.BlockSpec((BLOCK, KV_WIDTH),
                             lambda i: (jnp.minimum((front(i) + 1) * nblk, blk_all - 1), 0))
    kv_spec = tok(KV_WIDTH, front)
    cast_steps, cast_specs = _cast_plan(later_weights, n_tiles + 2)
    joined = jax.ShapeDtypeStruct((later_weights[0].shape[0], sum(w.shape[1] for w in later_weights)), BF16)
    joined_steps, cast_out_specs = _cast_plan([joined], n_tiles + 2)
    assert joined_steps == cast_steps
    kernel = functools.partial(_mix_kernel, n_tiles, (0, blk_p), (blk_p - 1, blk_all - 1),
                               cast_steps, len(later_weights))
    return pl.pallas_call(
        kernel,
        out_shape=[jax.ShapeDtypeStruct((t_all * ROW_PITCH, LANES), F32),
                   jax.ShapeDtypeStruct((SUBLANES, t_all), F32), joined],
        grid=(n_tiles + 2,),
        in_specs=[pl.BlockSpec(memory_space=pltpu.SMEM),
                  tok(D_MODEL, back), tok(ATTN_WIDTH, front), prev_spec, kv_spec, next_spec, prev_spec, kv_spec, next_spec,
                  tok(GMLP_WIDTH, front), tok(GMLP_WIDTH, front), tok(D_MODEL, back), tok(D_MODEL, back),
                  _const_spec((GMLP_GROUPS, BLOCK, BLOCK)), _const_spec((GMLP_GROUPS, BLOCK, BLOCK)),
                  _const_spec((ATTN_WIDTH, D_MODEL)), _const_spec((GMLP_WIDTH, D_MODEL)),
                  _const_spec((D_MODEL, D_MODEL)), _const_spec((1, D_MODEL)), _const_spec((1, D_MODEL)),
                  _const_spec((D_MODEL, 2 * LANES)), _const_spec((1, LANES))] + cast_specs,
        out_specs=[pl.BlockSpec((TM * ROW_PITCH, LANES), lambda i: (last(i), 0)),
                   pl.BlockSpec((SUBLANES, TM), lambda i: (0, last(i)))] + cast_out_specs,
        scratch_shapes=[pltpu.VMEM((TM + 2 * BLOCK, KV_WIDTH), BF16),
                        pltpu.VMEM((TM + 2 * BLOCK, KV_WIDTH), BF16),
                        pltpu.VMEM((2, TM, ATTN_WIDTH), BF16),
                        pltpu.VMEM((2, TM, GMLP_WIDTH), BF16),
                        pltpu.VMEM((TM, D_MODEL), BF16),
                        pltpu.VMEM((2, TM, D_MODEL), F32)],
        compiler_params=pltpu.CompilerParams(dimension_semantics=("arbitrary",),
                                             vmem_limit_bytes=VMEM_LIMIT),
        name="mix",
    )(sink, xn, q, k, k, k, v, v, v, gu, vgn, sa, sb, ws, bs, wa, wb, wo, g1, b1, wr, br, *later_weights)


def _gather_copy(src_hbm, dst_buf, sem, src_row, dst_row, n_rows):
    return pltpu.make_async_copy(src_hbm.at[pl.ds(src_row, n_rows)], dst_buf.at[pl.ds(dst_row, n_rows)], sem)


def _start_gather(token_of, src_hbm, dst_buf, sem, n_rows, src_pitch=ROW_PITCH):
    def issue(r, carry):
        src_row = pl.multiple_of(token_of(r) * src_pitch, SUBLANES)
        dst_row = pl.multiple_of(r * ROW_PITCH, SUBLANES)
        _gather_copy(src_hbm, dst_buf, sem, src_row, dst_row, n_rows).start()
        return carry

    lax.fori_loop(0, TM, issue, 0, unroll=8)


def _wait_gather(src_hbm, dst_buf, sem, n_rows):
    _gather_copy(src_hbm, dst_buf, sem, 0, 0, TM * n_rows).wait()


def _idx_specs(n_tiles, lookahead=1):
    def spec(tile_of_step):
        return pl.BlockSpec((None, 1, TM), lambda i, *_: (jnp.minimum(tile_of_step(i), n_tiles - 1), 0, 0),
                            memory_space=pltpu.SMEM)

    return [spec(lambda i, t=t: t) for t in range(lookahead)] + [spec(lambda i: i + lookahead)]


def _unsort_kernel(*refs):
    first_idx_refs, idxn_ref = refs[:UNSORT_LOOKAHEAD], refs[UNSORT_LOOKAHEAD]
    ys_hbm, g2_ref, b2_ref, out_ref, buf, sem = refs[UNSORT_LOOKAHEAD + 1:]
    i = pl.program_id(0)
    n = pl.num_programs(0)
    n_buf = UNSORT_LOOKAHEAD + 1
    slot = i % n_buf

    def gather(tile, idx_ref, when):
        dst = tile % n_buf

        @pl.when(when)
        def _():
            _start_gather(lambda r: idx_ref[0, r], ys_hbm, buf.at[dst], sem.at[dst], X_ROWS, src_pitch=X_ROWS)

    for tile, idx_ref in enumerate(first_idx_refs):
        gather(tile, idx_ref, jnp.logical_and(i == 0, tile < n))
    gather(i + UNSORT_LOOKAHEAD, idxn_ref, i + UNSORT_LOOKAHEAD < n)

    cur = buf.at[slot]
    _wait_gather(ys_hbm, cur, sem.at[slot], X_ROWS)
    for r in range(X_ROWS):
        out_ref[:, r * LANES:(r + 1) * LANES] = _load_token_major(cur, r)
    out_ref[...] = _layer_norm(out_ref[...], g2_ref[...], b2_ref[...])


def _unsort_call(idx, ysorted, g2, b2, name):
    n_tiles = idx.shape[0] // TM
    idx = idx.reshape(n_tiles, 1, TM)
    idx_specs = _idx_specs(n_tiles, UNSORT_LOOKAHEAD)
    n_buf = UNSORT_LOOKAHEAD + 1
    return pl.pallas_call(
        _unsort_kernel,
        out_shape=jax.ShapeDtypeStruct((n_tiles * TM, D_MODEL), F32),
        grid=(n_tiles,),
        in_specs=idx_specs + [pl.BlockSpec(memory_space=pl.ANY),
                              _const_spec((1, D_MODEL)), _const_spec((1, D_MODEL))],
        out_specs=pl.BlockSpec((TM, D_MODEL), lambda i: (i, 0)),
        scratch_shapes=[pltpu.VMEM((n_buf, TM * ROW_PITCH, LANES), F32), pltpu.SemaphoreType.DMA((n_buf,))],
        compiler_params=pltpu.CompilerParams(dimension_semantics=("arbitrary",),
                                             vmem_limit_bytes=VMEM_LIMIT),
        name=name,
    )(*([idx] * len(idx_specs)), ysorted, g2, b2)


def _moe_kernel(elo_ref, ehi_ref, nvalid_ref, first_ref, order_ref, x1t_hbm,
                wgu_lo, wd_lo, wgu_hi, wd_hi, out_ref, buf, sem, xb_scr):
    del elo_ref, ehi_ref
    i = pl.program_id(0)
    n = pl.num_programs(0)
    slot = i % MOE_BUFFERS

    for tile in range(MOE_LOOKAHEAD):
        @pl.when(jnp.logical_and(jnp.logical_and(i == 0, tile < n), nvalid_ref[jnp.minimum(tile, n - 1)] > 0))
        def _(tile=tile):
            first = first_ref[tile]
            _start_gather(lambda r: order_ref[first + r], x1t_hbm, buf.at[tile], sem.at[tile], ROW_PITCH)

    def experts():
        cur = buf.at[slot]
        _wait_gather(x1t_hbm, cur, sem.at[slot], ROW_PITCH)
        for r in range(X_ROWS):
            xb_scr[:, r * LANES:(r + 1) * LANES] = _load_token_major(cur, r).astype(BF16)
        route = _load_token_major(cur, X_ROWS)
        xb = xb_scr[...]
        yield

        def hidden(wgu, c):
            gate_up = jnp.dot(xb, wgu[...], preferred_element_type=F32)
            gate, up = gate_up[:, :EXPERT_FF], gate_up[:, EXPERT_FF:]
            return ((jax.nn.silu(gate) * up) * c).astype(BF16)

        h_lo = hidden(wgu_lo, route[:, 0:1])
        yield
        h_hi = hidden(wgu_hi, route[:, 1:2])
        yield
        for c in range(0, D_MODEL, COL_CHUNK):
            y = (jnp.dot(h_lo, wd_lo[:, c:c + COL_CHUNK], preferred_element_type=F32)
                 + jnp.dot(h_hi, wd_hi[:, c:c + COL_CHUNK], preferred_element_type=F32))
            for r in range(c // LANES, (c + COL_CHUNK) // LANES):
                out_ref[pl.ds(r, TM, stride=X_ROWS), :] = (DEEPNORM_ALPHA * _load_token_major(cur, r)
                                                           + y[:, r * LANES - c:(r + 1) * LANES - c])
            yield

    def start_later_gather():
        dst = (i + MOE_LOOKAHEAD) % MOE_BUFFERS
        first = first_ref[later]
        for r in range(TM):
            src_row = pl.multiple_of(order_ref[first + r] * ROW_PITCH, SUBLANES)
            _gather_copy(x1t_hbm, buf.at[dst], sem.at[dst], src_row, r * ROW_PITCH, ROW_PITCH).start()
            if r % ISSUE_GROUP == ISSUE_GROUP - 1:
                yield

    expert_costs = [1, 4, 4] + [2] * (D_MODEL // COL_CHUNK)
    later = jnp.minimum(i + MOE_LOOKAHEAD, n - 1)
    gather_later = jnp.logical_and(i + MOE_LOOKAHEAD < n, nvalid_ref[later] > 0)

    @pl.when(jnp.logical_and(nvalid_ref[i] > 0, gather_later))
    def _():
        _emit_interleaved((experts(), expert_costs), (start_later_gather(), [1] * (TM // ISSUE_GROUP)))

    @pl.when(jnp.logical_and(nvalid_ref[i] > 0, jnp.logical_not(gather_later)))
    def _():
        _emit_interleaved((experts(), expert_costs))

    @pl.when(nvalid_ref[i] == 0)
    def _():
        out_ref[...] = jnp.zeros(out_ref.shape, out_ref.dtype)


def _moe_call(e_lo, e_hi, nvalid, first, order, x1t, wgu, wd):
    n_tiles = nvalid.shape[0]

    def w_spec(shape, which):
        def imap(i, elo, ehi, *_):
            return ((elo, ehi)[which][i], 0, 0)
        return pl.BlockSpec((None,) + shape, imap)

    up_shape, down_shape = (D_MODEL, 2 * EXPERT_FF), (EXPERT_FF, D_MODEL)
    grid_spec = pltpu.PrefetchScalarGridSpec(
        num_scalar_prefetch=5,
        grid=(n_tiles,),
        in_specs=[pl.BlockSpec(memory_space=pl.ANY),
                  w_spec(up_shape, 0), w_spec(down_shape, 0), w_spec(up_shape, 1), w_spec(down_shape, 1)],
        out_specs=pl.BlockSpec((TM * X_ROWS, LANES), lambda i, *_: (i, 0)),
        scratch_shapes=[pltpu.VMEM((MOE_BUFFERS, TM * ROW_PITCH, LANES), F32),
                        pltpu.SemaphoreType.DMA((MOE_BUFFERS,)),
                        pltpu.VMEM((TM, D_MODEL), BF16)],
    )
    return pl.pallas_call(
        _moe_kernel,
        out_shape=jax.ShapeDtypeStruct((n_tiles * TM * X_ROWS, LANES), F32),
        grid_spec=grid_spec,
        compiler_params=pltpu.CompilerParams(dimension_semantics=("arbitrary",),
                                             vmem_limit_bytes=VMEM_LIMIT),
        name="moe",
    )(e_lo, e_hi, nvalid, first, order, x1t, wgu, wd, wgu, wd)


_PAIR_LO = np.array([0, 0, 0, 1, 1, 2], np.int32)
_PAIR_HI = np.array([1, 2, 3, 2, 3, 3], np.int32)


def _bucket_layout(bucket, n_slots):
    t_all = bucket.shape[0]
    n_tiles = n_slots // TM
    rows = t_all // LANES
    onehot = (bucket.reshape(rows, LANES, 1) == jnp.arange(N_BUCKETS, dtype=jnp.int32)).astype(F32)
    earlier = (jnp.arange(LANES)[:, None] > jnp.arange(LANES)[None, :]).astype(F32)
    within = jnp.einsum("ts,rsb->rtb", earlier, onehot)
    row_total = jnp.sum(onehot, axis=1)
    row_start = jnp.cumsum(row_total, axis=0) - row_total
    counts = jnp.sum(row_total, axis=0).astype(jnp.int32)
    padded = ((counts + TM - 1) // TM) * TM
    ends = jnp.cumsum(padded)
    starts = ends - padded
    slot = within + (row_start + starts.astype(F32))[:, None, :]
    dest = jnp.sum(slot * onehot, axis=-1).reshape(t_all).astype(jnp.int32)
    order = jnp.concatenate([jnp.argsort(bucket, stable=True).astype(jnp.int32), jnp.zeros((TM,), jnp.int32)])
    tile_start = jnp.arange(n_tiles, dtype=jnp.int32) * TM
    owner_start = jnp.minimum(tile_start, jnp.maximum(ends[-1:] - TM, 0))[:, None]
    owner = jnp.logical_and(owner_start >= starts[None, :], owner_start < ends[None, :]).astype(jnp.int32)
    remaining = counts[None, :] - (tile_start[:, None] - starts[None, :])
    nvalid = jnp.sum(owner * jnp.clip(remaining, 0, TM), axis=1)
    unpadded_start = jnp.cumsum(counts) - counts
    first = jnp.sum(owner * (unpadded_start[None, :] + tile_start[:, None] - starts[None, :]), axis=1)
    first = jnp.clip(first, 0, t_all)
    bucket_ids = np.arange(N_BUCKETS)
    first_expert = (bucket_ids // N_PAIRS) * EXPERTS_PER_GROUP
    e_lo = jnp.sum(owner * jnp.asarray(first_expert + _PAIR_LO[bucket_ids % N_PAIRS], jnp.int32)[None, :], axis=1)
    e_hi = jnp.sum(owner * jnp.asarray(first_expert + _PAIR_HI[bucket_ids % N_PAIRS], jnp.int32)[None, :], axis=1)
    return order, first.astype(jnp.int32), dest, e_lo.astype(jnp.int32), e_hi.astype(jnp.int32), nvalid.astype(jnp.int32)


def _layer(xp, xs, in_ln_g, in_ln_b, w_in, attn_sink, gmlp_w_s, gmlp_b_s, gmlp_ln_g, gmlp_ln_b,
           w_attn_branch, w_gmlp_branch, w_out, ln1_g, ln1_b,
           router_w_group, router_b_group, router_w_expert, router_b_expert,
           w_expert_gate, w_expert_up, w_expert_down, ln2_g, ln2_b):
    t_p, t_s = xp.shape[0], xs.shape[0]
    t_all = t_p + t_s
    row = lambda p: p.reshape(1, -1).astype(F32)

    rows_of = lambda w: w.astype(F32).reshape(-1, w.shape[-1])
    xn, q, k, v, gu, vgn, sa, sb, wa, wb, wo, wd = _proj_call(
        xp, xs, row(in_ln_g), row(in_ln_b), w_in.astype(BF16), gmlp_ln_g.astype(F32), gmlp_ln_b.astype(F32),
        [rows_of(w_attn_branch), rows_of(w_gmlp_branch), rows_of(w_out), rows_of(w_expert_down)])

    wr = jnp.concatenate([router_w_group, router_w_expert], axis=1).astype(F32)
    wr = jnp.pad(wr, ((0, 0), (0, LANES - wr.shape[1])))
    wr_hi = wr.astype(BF16)
    wr_lo = (wr - wr_hi.astype(F32)).astype(BF16)
    br = jnp.pad(jnp.concatenate([router_b_group, router_b_expert]).astype(F32),
                 (0, LANES - N_EXPERT_GROUPS - N_EXPERTS)).reshape(1, LANES)
    bs = jnp.broadcast_to(gmlp_b_s.astype(F32)[:, :, None], (GMLP_GROUPS, BLOCK, BLOCK))

    x1t, routed, wgu = _mix_call(t_p, attn_sink.astype(F32), xn,
                                    q, k, v, gu, vgn, sa, sb, gmlp_w_s.astype(BF16), bs, wa, wb, wo,
                                    row(ln1_g), row(ln1_b), jnp.concatenate([wr_hi, wr_lo], axis=1), br,
                                    [rows_of(w_expert_gate), rows_of(w_expert_up)])

    n_slots = t_all + N_BUCKETS * TM
    bucket = routed[ROUTE_ROWS - 1].astype(jnp.int32)
    order, first, dest, e_lo, e_hi, nvalid = _bucket_layout(bucket, n_slots)
    zsorted = _moe_call(e_lo, e_hi, nvalid, first, order, x1t,
                        wgu.reshape(N_EXPERTS, D_MODEL, 2 * EXPERT_FF), wd.reshape(w_expert_down.shape))
    g2, b2 = row(ln2_g), row(ln2_b)
    return (_unsort_call(dest[:t_p], zsorted, g2, b2, "unsort_prompt"),
            _unsort_call(dest[t_p:], zsorted, g2, b2, "unsort_sample"))


def kernel(x_prompt, x_sample, in_ln_g, in_ln_b, w_in, attn_sink, gmlp_w_s, gmlp_b_s, gmlp_ln_g, gmlp_ln_b,
           w_attn_branch, w_gmlp_branch, w_out, ln1_g, ln1_b,
           router_w_group, router_b_group, router_w_expert, router_b_expert,
           w_expert_gate, w_expert_up, w_expert_down, ln2_g, ln2_b):
    bp, sp, d = x_prompt.shape
    bs, ss, _ = x_sample.shape
    assert bp == 1 and bs == 1 and d == D_MODEL and sp % TM == 0 and ss % TM == 0
    assert w_in.shape[0] == 1, "one layer"
    yp, ys = _layer(x_prompt.reshape(sp, d), x_sample.reshape(ss, d), in_ln_g, in_ln_b, w_in[0], attn_sink[0],
                    gmlp_w_s[0], gmlp_b_s[0], gmlp_ln_g[0], gmlp_ln_b[0],
                    w_attn_branch[0], w_gmlp_branch[0], w_out[0], ln1_g[0], ln1_b[0],
                    router_w_group[0], router_b_group[0], router_w_expert[0], router_b_expert[0],
                    w_expert_gate[0], w_expert_up[0], w_expert_down[0], ln2_g[0], ln2_b[0])
    return yp.reshape(1, sp, d), ys.reshape(1, ss, d)
```

```python
import functools

import numpy as np
import jax
import jax.numpy as jnp
from jax import lax
from jax.experimental import pallas as pl
from jax.experimental.pallas import tpu as pltpu

F32 = jnp.float32
BF16 = jnp.bfloat16

D_MODEL = 2048
HEAD_DIM = 128
N_Q_HEADS = 8
N_KV_HEADS = 2
Q_PER_KV = N_Q_HEADS // N_KV_HEADS
ATTN_WIDTH = N_Q_HEADS * HEAD_DIM
KV_WIDTH = N_KV_HEADS * HEAD_DIM
WINDOW = 128
BLOCK = 128
GMLP_WIDTH = D_MODEL // 2
GMLP_GROUPS = 8
GMLP_GROUP_DIM = GMLP_WIDTH // GMLP_GROUPS
N_EXPERT_GROUPS = 4
EXPERTS_PER_GROUP = 4
N_EXPERTS = N_EXPERT_GROUPS * EXPERTS_PER_GROUP
EXPERT_FF = 512
LN_EPS = 1e-5
DEEPNORM_ALPHA = 2.0 ** 0.25
NEG_INF = -1e9
ATTN_SCALE = HEAD_DIM ** -0.5

_C_Q = 0
_C_K = _C_Q + ATTN_WIDTH
_C_V = _C_K + KV_WIDTH
_C_U = _C_V + KV_WIDTH
_C_VG = _C_U + GMLP_WIDTH
_C_GA = _C_VG + GMLP_WIDTH
_C_GB = _C_GA + D_MODEL
IN_COLS = _C_GB + D_MODEL

LANES = 128
SUBLANES = 8
BF16_SUBLANES = 16
ROUTE_ROWS = 3
N_PAIRS = 6
N_BUCKETS = N_EXPERT_GROUPS * N_PAIRS
X_ROWS = D_MODEL // LANES
ROW_PITCH = 24
TM = 256
COL_CHUNK = 512
MIX_CHUNK = 256
PROJ_CHUNK = 256
MOE_LOOKAHEAD = 3
MOE_BUFFERS = MOE_LOOKAHEAD + 1
ISSUE_GROUP = 16
UNSORT_LOOKAHEAD = 4
VMEM_LIMIT = 60 * 1024 * 1024

_SLOPES = [float(2.0 ** (-8.0 * (h + 1) / N_Q_HEADS)) for h in range(N_Q_HEADS)]


def _layer_norm(x, g, b):
    mu = jnp.mean(x, axis=-1, keepdims=True)
    xc = x - mu
    var = jnp.mean(xc * xc, axis=-1, keepdims=True)
    return xc * lax.rsqrt(var + LN_EPS) * g + b


def _load_token_major(ref, r):
    return ref[pl.ds(r, TM, stride=ROW_PITCH), :]


def _emit_interleaved(*streams):
    order = []
    for s, (_, costs) in enumerate(streams):
        done = 0.0
        for c in costs:
            order.append(((done + c / 2) / sum(costs), s))
            done += c
    for _, s in sorted(order):
        next(streams[s][0])
    for gen, _ in streams:
        assert next(gen, "done") == "done", "stream has more units than declared"


def _cast_plan(weights, n_steps):
    steps = 1 << (n_steps.bit_length() - 1)
    steps = min([steps] + [w.shape[0] // BF16_SUBLANES for w in weights])
    specs = [pl.BlockSpec((w.shape[0] // steps, w.shape[1]), lambda i, *_: (jnp.minimum(i, steps - 1), 0))
             for w in weights]
    return steps, specs


def _cast_side_job(steps, src_refs, dst_refs):
    @pl.when(pl.program_id(0) < steps)
    def _():
        if len(dst_refs) == len(src_refs):
            for src, dst in zip(src_refs, dst_refs):
                dst[...] = src[...].astype(dst.dtype)
        else:
            (dst,), col = dst_refs, 0
            for src in src_refs:
                dst[:, col:col + src.shape[1]] = src[...].astype(dst.dtype)
                col += src.shape[1]


def _const_spec(shape):
    nd = len(shape)
    return pl.BlockSpec(shape, lambda i, *_: (0,) * nd, pipeline_mode=pl.Buffered(1))


def _proj_kernel(n_prompt_tiles, cast_steps, n_cast, *refs):
    xp_ref, xs_ref, g0_ref, b0_ref, w_ref, lg_ref, lb_ref = refs[:7]
    cast_src = refs[7:7 + n_cast]
    xn_ref, q_ref, k_ref, v_ref, gu_ref, vgn_ref, sa_ref, sb_ref = refs[7 + n_cast:15 + n_cast]
    cast_dst = refs[15 + n_cast:15 + 2 * n_cast]
    xn_scr = refs[15 + 2 * n_cast]
    i = pl.program_id(0)
    _cast_side_job(cast_steps, cast_src, cast_dst)

    @pl.when(i < n_prompt_tiles)
    def _():
        xn_ref[...] = _layer_norm(xp_ref[...], g0_ref[...], b0_ref[...])

    @pl.when(i >= n_prompt_tiles)
    def _():
        xn_ref[...] = _layer_norm(xs_ref[...], g0_ref[...], b0_ref[...])

    xn_scr[...] = xn_ref[...].astype(BF16)
    xn = xn_scr[...]

    def proj(c0, width):
        return jnp.dot(xn, w_ref[:, c0:c0 + width], preferred_element_type=F32)

    for c in range(0, ATTN_WIDTH, COL_CHUNK):
        q_ref[:, c:c + COL_CHUNK] = proj(_C_Q + c, COL_CHUNK).astype(BF16)
    kv = proj(_C_K, 2 * KV_WIDTH)
    k_ref[...] = kv[:, :KV_WIDTH].astype(BF16)
    v_ref[...] = kv[:, KV_WIDTH:].astype(BF16)
    for c in range(0, GMLP_WIDTH, COL_CHUNK):
        gu_ref[:, c:c + COL_CHUNK] = jax.nn.gelu(proj(_C_U + c, COL_CHUNK)).astype(BF16)
    for c in range(0, GMLP_WIDTH, COL_CHUNK):
        vg = jax.nn.gelu(proj(_C_VG + c, COL_CHUNK))
        for j in range(COL_CHUNK // GMLP_GROUP_DIM):
            grp = c // GMLP_GROUP_DIM + j
            blk = vg[:, j * GMLP_GROUP_DIM:(j + 1) * GMLP_GROUP_DIM]
            y = _layer_norm(blk, lg_ref[grp:grp + 1, :], lb_ref[grp:grp + 1, :])
            vgn_ref[:, grp * GMLP_GROUP_DIM:(grp + 1) * GMLP_GROUP_DIM] = y.astype(BF16)
    for c in range(0, D_MODEL, COL_CHUNK):
        sa_ref[:, c:c + COL_CHUNK] = jax.nn.sigmoid(proj(_C_GA + c, COL_CHUNK)).astype(BF16)
    for c in range(0, D_MODEL, COL_CHUNK):
        sb_ref[:, c:c + COL_CHUNK] = jax.nn.sigmoid(proj(_C_GB + c, COL_CHUNK)).astype(BF16)


def _proj_call(xp, xs, g0, b0, w_in, lg, lb, later_weights):
    n_p, n_s = xp.shape[0] // TM, xs.shape[0] // TM
    t_all = xp.shape[0] + xs.shape[0]
    cast_steps, cast_specs = _cast_plan(later_weights, n_p + n_s)
    _, cast_out_specs = _cast_plan(later_weights, n_p + n_s)
    xp_spec = pl.BlockSpec((TM, D_MODEL), lambda i: (jnp.minimum(i, n_p - 1), 0))
    xs_spec = pl.BlockSpec((TM, D_MODEL), lambda i: (jnp.maximum(i - n_p, 0), 0))

    def tok(width):
        return pl.BlockSpec((TM, width), lambda i: (i, 0))

    widths = (ATTN_WIDTH, KV_WIDTH, KV_WIDTH, GMLP_WIDTH, GMLP_WIDTH, D_MODEL, D_MODEL)
    return pl.pallas_call(
        functools.partial(_proj_kernel, n_p, cast_steps, len(later_weights)),
        out_shape=([jax.ShapeDtypeStruct((t_all, D_MODEL), F32)]
                   + [jax.ShapeDtypeStruct((t_all, w), BF16) for w in widths]
                   + [jax.ShapeDtypeStruct(w.shape, BF16) for w in later_weights]),
        grid=(n_p + n_s,),
        in_specs=[xp_spec, xs_spec, _const_spec((1, D_MODEL)), _const_spec((1, D_MODEL)),
                  _const_spec((D_MODEL, IN_COLS)),
                  _const_spec((GMLP_GROUPS, GMLP_GROUP_DIM)), _const_spec((GMLP_GROUPS, GMLP_GROUP_DIM))]
        + cast_specs,
        out_specs=[tok(D_MODEL)] + [tok(w) for w in widths] + cast_out_specs,
        scratch_shapes=[pltpu.VMEM((TM, D_MODEL), BF16)],
        compiler_params=pltpu.CompilerParams(dimension_semantics=("arbitrary",),
                                             vmem_limit_bytes=VMEM_LIMIT),
        name="proj",
    )(xp, xs, g0, b0, w_in, lg, lb, *later_weights)


def _route(logits_t):
    row = lambda j: logits_t[j:j + 1, :]
    gl = [row(j) for j in range(N_EXPERT_GROUPS)]
    gmax, gidx = gl[0], jnp.zeros(gl[0].shape, jnp.int32)
    for j in range(1, N_EXPERT_GROUPS):
        better = gl[j] > gmax
        gmax = jnp.where(better, gl[j], gmax)
        gidx = jnp.where(better, j, gidx)
    gsum = jnp.exp(gl[0] - gmax)
    for j in range(1, N_EXPERT_GROUPS):
        gsum = gsum + jnp.exp(gl[j] - gmax)
    p_group = 1.0 / gsum

    ig = []
    for e in range(EXPERTS_PER_GROUP):
        v = row(N_EXPERT_GROUPS + (N_EXPERT_GROUPS - 1) * EXPERTS_PER_GROUP + e)
        for g in range(N_EXPERT_GROUPS - 2, -1, -1):
            v = jnp.where(gidx == g, row(N_EXPERT_GROUPS + g * EXPERTS_PER_GROUP + e), v)
        ig.append(v)
    v1, i1 = ig[0], jnp.zeros(ig[0].shape, jnp.int32)
    for e in range(1, EXPERTS_PER_GROUP):
        better = ig[e] > v1
        v1 = jnp.where(better, ig[e], v1)
        i1 = jnp.where(better, e, i1)
    v2 = jnp.where(i1 == 0, ig[1], ig[0])
    i2 = jnp.where(i1 == 0, 1, 0).astype(jnp.int32)
    for e in range(1, EXPERTS_PER_GROUP):
        better = jnp.logical_and(i1 != e, ig[e] > v2)
        v2 = jnp.where(better, ig[e], v2)
        i2 = jnp.where(better, e, i2)
    ev = jnp.exp(v2 - v1)
    ssum = 1.0 + ev
    w1 = 1.0 / ssum
    w2 = ev / ssum
    first_lo = i1 < i2
    lo = jnp.minimum(i1, i2)
    hi = jnp.maximum(i1, i2)
    c_lo = jnp.where(first_lo, w1, w2) * p_group
    c_hi = jnp.where(first_lo, w2, w1) * p_group
    pair = jnp.where(lo == 0, hi - 1, jnp.where(lo == 1, hi + 1, N_PAIRS - 1))
    bucket = gidx * N_PAIRS + pair
    return c_lo, c_hi, bucket.astype(F32)


def _mix_kernel(n_tiles, seq_edges_first, seq_edges_last, cast_steps, n_cast, *refs):
    (sink_ref, xn_ref, q_ref, kp_ref, km_ref, kn_ref, vp_ref, vm_ref, vn_ref,
     gu_ref, vgn_ref, sa_ref, sb_ref, ws_ref, bs_ref,
     wa_ref, wb_ref, wo_ref, g1_ref, b1_ref, wr_ref, br_ref) = refs[:22]
    cast_src = refs[22:22 + n_cast]
    x1t_ref, rt_ref = refs[22 + n_cast:24 + n_cast]
    cast_dst = refs[24 + n_cast:25 + n_cast]
    kf_scr, vf_scr, a_scr, sg_scr, mg_scr, mix_scr = refs[25 + n_cast:]
    _cast_side_job(cast_steps, cast_src, cast_dst)
    i = pl.program_id(0)
    nblk = TM // BLOCK
    cur = i % 2

    @pl.when(i == 0)
    def _():
        a_scr[1] = jnp.zeros(a_scr.shape[1:], a_scr.dtype)
        sg_scr[1] = jnp.zeros(sg_scr.shape[1:], sg_scr.dtype)
        mix_scr[0] = jnp.zeros(mix_scr.shape[1:], mix_scr.dtype)

    def stage_a():
        tile = jnp.minimum(i, n_tiles - 1)
        kf_scr[0:BLOCK, :] = kp_ref[...]
        kf_scr[BLOCK:BLOCK + TM, :] = km_ref[...]
        kf_scr[BLOCK + TM:, :] = kn_ref[...]
        vf_scr[0:BLOCK, :] = vp_ref[...]
        vf_scr[BLOCK:BLOCK + TM, :] = vm_ref[...]
        vf_scr[BLOCK + TM:, :] = vn_ref[...]
        first_blk = tile * nblk
        last_blk = tile * nblk + nblk - 1
        has_prev = jnp.logical_and(*[first_blk != e for e in seq_edges_first])
        has_next = jnp.logical_and(*[last_blk != e for e in seq_edges_last])
        kj = lax.broadcasted_iota(jnp.int32, (BLOCK, 3 * BLOCK), 1)
        qi = lax.broadcasted_iota(jnp.int32, (BLOCK, 3 * BLOCK), 0)
        dist = jnp.abs(kj - BLOCK - qi)
        in_window = dist <= WINDOW
        dist_f = dist.astype(F32)
        lo_key = jnp.where(has_prev, 0, BLOCK)
        hi_key = jnp.where(has_next, 3 * BLOCK, 2 * BLOCK)
        yield
        for j in range(nblk):
            mask = in_window
            if j == 0:
                mask = jnp.logical_and(mask, kj >= lo_key)
            if j == nblk - 1:
                mask = jnp.logical_and(mask, kj < hi_key)
            r0 = j * BLOCK
            for kvh in range(N_KV_HEADS):
                c0 = kvh * HEAD_DIM
                kb = kf_scr[r0:r0 + 3 * BLOCK, c0:c0 + HEAD_DIM]
                vb = vf_scr[r0:r0 + 3 * BLOCK, c0:c0 + HEAD_DIM]
                heads = [kvh * Q_PER_KV + g for g in range(Q_PER_KV)]
                qs = jnp.concatenate(
                    [q_ref[r0:r0 + BLOCK, h * HEAD_DIM:(h + 1) * HEAD_DIM] for h in heads], axis=0)
                s_all = lax.dot_general(qs, kb, (((1,), (1,)), ((), ())), preferred_element_type=F32)
                pns = []
                for g, h in enumerate(heads):
                    s = s_all[g * BLOCK:(g + 1) * BLOCK, :] * ATTN_SCALE
                    s = jnp.where(mask, s + dist_f * (-_SLOPES[h]), NEG_INF)
                    sink = sink_ref[h]
                    m = jnp.maximum(jnp.max(s, axis=-1, keepdims=True), sink)
                    p = jnp.exp(s - m)
                    denom = jnp.sum(p, axis=-1, keepdims=True) + jnp.exp(sink - m)
                    pns.append((p * (1.0 / denom)).astype(BF16))
                    if g < Q_PER_KV - 1:
                        yield
                o_all = jnp.dot(jnp.concatenate(pns, axis=0), vb, preferred_element_type=F32)
                for g, h in enumerate(heads):
                    a_scr[cur, r0:r0 + BLOCK, h * HEAD_DIM:(h + 1) * HEAD_DIM] = (
                        o_all[g * BLOCK:(g + 1) * BLOCK, :].astype(BF16))
                yield
            for grp in range(GMLP_GROUPS):
                c0 = grp * GMLP_GROUP_DIM
                sp = jnp.dot(ws_ref[grp], vgn_ref[r0:r0 + BLOCK, c0:c0 + GMLP_GROUP_DIM],
                             preferred_element_type=F32) + bs_ref[grp]
                u = gu_ref[r0:r0 + BLOCK, c0:c0 + GMLP_GROUP_DIM].astype(F32)
                sg_scr[cur, r0:r0 + BLOCK, c0:c0 + GMLP_GROUP_DIM] = (u * sp).astype(BF16)
            yield

    def stage_b():
        a = a_scr[1 - cur]
        sg = sg_scr[1 - cur]
        for c in range(0, D_MODEL, PROJ_CHUNK):
            ma = jnp.dot(a, wa_ref[:, c:c + PROJ_CHUNK], preferred_element_type=F32)
            mb = jnp.dot(sg, wb_ref[:, c:c + PROJ_CHUNK], preferred_element_type=F32)
            merged = (sa_ref[:, c:c + PROJ_CHUNK].astype(F32) * ma
                      + sb_ref[:, c:c + PROJ_CHUNK].astype(F32) * mb)
            mg_scr[:, c:c + PROJ_CHUNK] = merged.astype(BF16)
            yield
        mg = mg_scr[...]
        for c in range(0, D_MODEL, PROJ_CHUNK):
            mix = jnp.dot(mg, wo_ref[:, c:c + PROJ_CHUNK], preferred_element_type=F32)
            mix_scr[1 - cur, :, c:c + PROJ_CHUNK] = DEEPNORM_ALPHA * xn_ref[:, c:c + PROJ_CHUNK] + mix
            yield

    def stage_c():
        mu = jnp.mean(mix_scr[cur], axis=-1, keepdims=True)
        yield
        zc = mix_scr[cur] - mu
        inv = lax.rsqrt(jnp.mean(zc * zc, axis=-1, keepdims=True) + LN_EPS)
        yield
        r = None
        for c in range(0, D_MODEL, MIX_CHUNK):
            x1 = ((mix_scr[cur, :, c:c + MIX_CHUNK] - mu) * inv * g1_ref[:, c:c + MIX_CHUNK]
                  + b1_ref[:, c:c + MIX_CHUNK])
            for k in range(MIX_CHUNK // LANES):
                x1t_ref[pl.ds(c // LANES + k, TM, stride=ROW_PITCH), :] = x1[:, k * LANES:(k + 1) * LANES]
            x_hi = x1.astype(BF16)
            x_lo = (x1 - x_hi.astype(F32)).astype(BF16)
            part = (jnp.dot(x_hi, wr_ref[c:c + MIX_CHUNK, :], preferred_element_type=F32)
                    + jnp.dot(x_lo, wr_ref[c:c + MIX_CHUNK, :], preferred_element_type=F32))
            r = part if r is None else r + part
            yield
        logits = r[:, :LANES] + r[:, LANES:] + br_ref[...]
        c_lo, c_hi, bucket = _route(logits.T)
        routed = jnp.concatenate([c_lo, c_hi, bucket, jnp.zeros((SUBLANES - ROUTE_ROWS, TM), F32)], axis=0)
        rt_ref[...] = routed
        padded = jnp.concatenate([routed, jnp.zeros((LANES - SUBLANES, TM), F32)], axis=0)
        x1t_ref[pl.ds(X_ROWS, TM, stride=ROW_PITCH), :] = padded.T
        for k in range(X_ROWS + 1, ROW_PITCH):
            x1t_ref[pl.ds(k, TM, stride=ROW_PITCH), :] = jnp.zeros((TM, LANES), F32)
        yield

    n_chunks = D_MODEL // MIX_CHUNK
    _emit_interleaved((stage_b(), [512] * (2 * D_MODEL // PROJ_CHUNK)),
                      (stage_a(), [100] + ([600] * N_Q_HEADS + [500]) * nblk),
                      (stage_c(), [1000, 1000] + [550] * n_chunks + [1000]))


def _mix_call(t_prompt, sink, xn, q, k, v, gu, vgn, sa, sb, ws, bs, wa, wb, wo, g1, b1, wr, br, later_weights):
    t_all = xn.shape[0]
    nblk = TM // BLOCK
    blk_p, blk_all = t_prompt // BLOCK, t_all // BLOCK
    n_tiles = t_all // TM
    front = lambda i: jnp.minimum(i, n_tiles - 1)
    back = lambda i: jnp.clip(i - 1, 0, n_tiles - 1)
    last = lambda i: jnp.maximum(i - 2, 0)

    def tok(width, which):
        return pl.BlockSpec((TM, width), lambda i: (which(i), 0))

    prev_spec = pl.BlockSpec((BLOCK, KV_WIDTH), lambda i: (jnp.maximum(front(i) * nblk - 1, 0), 0))
    next_spec = pl.BlockSpec((BLOCK, KV_WIDTH),
                             lambda i: (jnp.minimum((front(i) + 1) * nblk, blk_all - 1), 0))
    kv_spec = tok(KV_WIDTH, front)
    cast_steps, cast_specs = _cast_plan(later_weights, n_tiles + 2)
    joined = jax.ShapeDtypeStruct((later_weights[0].shape[0], sum(w.shape[1] for w in later_weights)), BF16)
    joined_steps, cast_out_specs = _cast_plan([joined], n_tiles + 2)
    assert joined_steps == cast_steps
    kernel = functools.partial(_mix_kernel, n_tiles, (0, blk_p), (blk_p - 1, blk_all - 1),
                               cast_steps, len(later_weights))
    return pl.pallas_call(
        kernel,
        out_shape=[jax.ShapeDtypeStruct((t_all * ROW_PITCH, LANES), F32),
                   jax.ShapeDtypeStruct((SUBLANES, t_all), F32), joined],
        grid=(n_tiles + 2,),
        in_specs=[pl.BlockSpec(memory_space=pltpu.SMEM),
                  tok(D_MODEL, back), tok(ATTN_WIDTH, front), prev_spec, kv_spec, next_spec, prev_spec, kv_spec, next_spec,
                  tok(GMLP_WIDTH, front), tok(GMLP_WIDTH, front), tok(D_MODEL, back), tok(D_MODEL, back),
                  _const_spec((GMLP_GROUPS, BLOCK, BLOCK)), _const_spec((GMLP_GROUPS, BLOCK, BLOCK)),
                  _const_spec((ATTN_WIDTH, D_MODEL)), _const_spec((GMLP_WIDTH, D_MODEL)),
                  _const_spec((D_MODEL, D_MODEL)), _const_spec((1, D_MODEL)), _const_spec((1, D_MODEL)),
                  _const_spec((D_MODEL, 2 * LANES)), _const_spec((1, LANES))] + cast_specs,
        out_specs=[pl.BlockSpec((TM * ROW_PITCH, LANES), lambda i: (last(i), 0)),
                   pl.BlockSpec((SUBLANES, TM), lambda i: (0, last(i)))] + cast_out_specs,
        scratch_shapes=[pltpu.VMEM((TM + 2 * BLOCK, KV_WIDTH), BF16),
                        pltpu.VMEM((TM + 2 * BLOCK, KV_WIDTH), BF16),
                        pltpu.VMEM((2, TM, ATTN_WIDTH), BF16),
                        pltpu.VMEM((2, TM, GMLP_WIDTH), BF16),
                        pltpu.VMEM((TM, D_MODEL), BF16),
                        pltpu.VMEM((2, TM, D_MODEL), F32)],
        compiler_params=pltpu.CompilerParams(dimension_semantics=("arbitrary",),
                                             vmem_limit_bytes=VMEM_LIMIT),
        name="mix",
    )(sink, xn, q, k, k, k, v, v, v, gu, vgn, sa, sb, ws, bs, wa, wb, wo, g1, b1, wr, br, *later_weights)


def _gather_copy(src_hbm, dst_buf, sem, src_row, dst_row, n_rows):
    return pltpu.make_async_copy(src_hbm.at[pl.ds(src_row, n_rows)], dst_buf.at[pl.ds(dst_row, n_rows)], sem)


def _start_gather(token_of, src_hbm, dst_buf, sem, n_rows, src_pitch=ROW_PITCH):
    def issue(r, carry):
        src_row = pl.multiple_of(token_of(r) * src_pitch, SUBLANES)
        dst_row = pl.multiple_of(r * ROW_PITCH, SUBLANES)
        _gather_copy(src_hbm, dst_buf, sem, src_row, dst_row, n_rows).start()
        return carry

    lax.fori_loop(0, TM, issue, 0, unroll=8)


def _wait_gather(src_hbm, dst_buf, sem, n_rows):
    _gather_copy(src_hbm, dst_buf, sem, 0, 0, TM * n_rows).wait()


def _idx_specs(n_tiles, lookahead=1):
    def spec(tile_of_step):
        return pl.BlockSpec((None, 1, TM), lambda i, *_: (jnp.minimum(tile_of_step(i), n_tiles - 1), 0, 0),
                            memory_space=pltpu.SMEM)

    return [spec(lambda i, t=t: t) for t in range(lookahead)] + [spec(lambda i: i + lookahead)]


def _unsort_kernel(*refs):
    first_idx_refs, idxn_ref = refs[:UNSORT_LOOKAHEAD], refs[UNSORT_LOOKAHEAD]
    ys_hbm, g2_ref, b2_ref, out_ref, buf, sem = refs[UNSORT_LOOKAHEAD + 1:]
    i = pl.program_id(0)
    n = pl.num_programs(0)
    n_buf = UNSORT_LOOKAHEAD + 1
    slot = i % n_buf

    def gather(tile, idx_ref, when):
        dst = tile % n_buf

        @pl.when(when)
        def _():
            _start_gather(lambda r: idx_ref[0, r], ys_hbm, buf.at[dst], sem.at[dst], X_ROWS, src_pitch=X_ROWS)

    for tile, idx_ref in enumerate(first_idx_refs):
        gather(tile, idx_ref, jnp.logical_and(i == 0, tile < n))
    gather(i + UNSORT_LOOKAHEAD, idxn_ref, i + UNSORT_LOOKAHEAD < n)

    cur = buf.at[slot]
    _wait_gather(ys_hbm, cur, sem.at[slot], X_ROWS)
    for r in range(X_ROWS):
        out_ref[:, r * LANES:(r + 1) * LANES] = _load_token_major(cur, r)
    out_ref[...] = _layer_norm(out_ref[...], g2_ref[...], b2_ref[...])


def _unsort_call(idx, ysorted, g2, b2, name):
    n_tiles = idx.shape[0] // TM
    idx = idx.reshape(n_tiles, 1, TM)
    idx_specs = _idx_specs(n_tiles, UNSORT_LOOKAHEAD)
    n_buf = UNSORT_LOOKAHEAD + 1
    return pl.pallas_call(
        _unsort_kernel,
        out_shape=jax.ShapeDtypeStruct((n_tiles * TM, D_MODEL), F32),
        grid=(n_tiles,),
        in_specs=idx_specs + [pl.BlockSpec(memory_space=pl.ANY),
                              _const_spec((1, D_MODEL)), _const_spec((1, D_MODEL))],
        out_specs=pl.BlockSpec((TM, D_MODEL), lambda i: (i, 0)),
        scratch_shapes=[pltpu.VMEM((n_buf, TM * ROW_PITCH, LANES), F32), pltpu.SemaphoreType.DMA((n_buf,))],
        compiler_params=pltpu.CompilerParams(dimension_semantics=("arbitrary",),
                                             vmem_limit_bytes=VMEM_LIMIT),
        name=name,
    )(*([idx] * len(idx_specs)), ysorted, g2, b2)


def _moe_kernel(elo_ref, ehi_ref, nvalid_ref, first_ref, order_ref, x1t_hbm,
                wgu_lo, wd_lo, wgu_hi, wd_hi, out_ref, buf, sem, xb_scr):
    del elo_ref, ehi_ref
    i = pl.program_id(0)
    n = pl.num_programs(0)
    slot = i % MOE_BUFFERS

    for tile in range(MOE_LOOKAHEAD):
        @pl.when(jnp.logical_and(jnp.logical_and(i == 0, tile < n), nvalid_ref[jnp.minimum(tile, n - 1)] > 0))
        def _(tile=tile):
            first = first_ref[tile]
            _start_gather(lambda r: order_ref[first + r], x1t_hbm, buf.at[tile], sem.at[tile], ROW_PITCH)

    def experts():
        cur = buf.at[slot]
        _wait_gather(x1t_hbm, cur, sem.at[slot], ROW_PITCH)
        for r in range(X_ROWS):
            xb_scr[:, r * LANES:(r + 1) * LANES] = _load_token_major(cur, r).astype(BF16)
        route = _load_token_major(cur, X_ROWS)
        xb = xb_scr[...]
        yield

        def hidden(wgu, c):
            gate_up = jnp.dot(xb, wgu[...], preferred_element_type=F32)
            gate, up = gate_up[:, :EXPERT_FF], gate_up[:, EXPERT_FF:]
            return ((jax.nn.silu(gate) * up) * c).astype(BF16)

        h_lo = hidden(wgu_lo, route[:, 0:1])
        yield
        h_hi = hidden(wgu_hi, route[:, 1:2])
        yield
        for c in range(0, D_MODEL, COL_CHUNK):
            y = (jnp.dot(h_lo, wd_lo[:, c:c + COL_CHUNK], preferred_element_type=F32)
                 + jnp.dot(h_hi, wd_hi[:, c:c + COL_CHUNK], preferred_element_type=F32))
            for r in range(c // LANES, (c + COL_CHUNK) // LANES):
                out_ref[pl.ds(r, TM, stride=X_ROWS), :] = (DEEPNORM_ALPHA * _load_token_major(cur, r)
                                                           + y[:, r * LANES - c:(r + 1) * LANES - c])
            yield

    def start_later_gather():
        dst = (i + MOE_LOOKAHEAD) % MOE_BUFFERS
        first = first_ref[later]
        for r in range(TM):
            src_row = pl.multiple_of(order_ref[first + r] * ROW_PITCH, SUBLANES)
            _gather_copy(x1t_hbm, buf.at[dst], sem.at[dst], src_row, r * ROW_PITCH, ROW_PITCH).start()
            if r % ISSUE_GROUP == ISSUE_GROUP - 1:
                yield

    expert_costs = [1, 4, 4] + [2] * (D_MODEL // COL_CHUNK)
    later = jnp.minimum(i + MOE_LOOKAHEAD, n - 1)
    gather_later = jnp.logical_and(i + MOE_LOOKAHEAD < n, nvalid_ref[later] > 0)

    @pl.when(jnp.logical_and(nvalid_ref[i] > 0, gather_later))
    def _():
        _emit_interleaved((experts(), expert_costs), (start_later_gather(), [1] * (TM // ISSUE_GROUP)))

    @pl.when(jnp.logical_and(nvalid_ref[i] > 0, jnp.logical_not(gather_later)))
    def _():
        _emit_interleaved((experts(), expert_costs))

    @pl.when(nvalid_ref[i] == 0)
    def _():
        out_ref[...] = jnp.zeros(out_ref.shape, out_ref.dtype)


def _moe_call(e_lo, e_hi, nvalid, first, order, x1t, wgu, wd):
    n_tiles = nvalid.shape[0]

    def w_spec(shape, which):
        def imap(i, elo, ehi, *_):
            return ((elo, ehi)[which][i], 0, 0)
        return pl.BlockSpec((None,) + shape, imap)

    up_shape, down_shape = (D_MODEL, 2 * EXPERT_FF), (EXPERT_FF, D_MODEL)
    grid_spec = pltpu.PrefetchScalarGridSpec(
        num_scalar_prefetch=5,
        grid=(n_tiles,),
        in_specs=[pl.BlockSpec(memory_space=pl.ANY),
                  w_spec(up_shape, 0), w_spec(down_shape, 0), w_spec(up_shape, 1), w_spec(down_shape, 1)],
        out_specs=pl.BlockSpec((TM * X_ROWS, LANES), lambda i, *_: (i, 0)),
        scratch_shapes=[pltpu.VMEM((MOE_BUFFERS, TM * ROW_PITCH, LANES), F32),
                        pltpu.SemaphoreType.DMA((MOE_BUFFERS,)),
                        pltpu.VMEM((TM, D_MODEL), BF16)],
    )
    return pl.pallas_call(
        _moe_kernel,
        out_shape=jax.ShapeDtypeStruct((n_tiles * TM * X_ROWS, LANES), F32),
        grid_spec=grid_spec,
        compiler_params=pltpu.CompilerParams(dimension_semantics=("arbitrary",),
                                             vmem_limit_bytes=VMEM_LIMIT),
        name="moe",
    )(e_lo, e_hi, nvalid, first, order, x1t, wgu, wd, wgu, wd)


_PAIR_LO = np.array([0, 0, 0, 1, 1, 2], np.int32)
_PAIR_HI = np.array([1, 2, 3, 2, 3, 3], np.int32)


def _bucket_layout(bucket, n_slots):
    t_all = bucket.shape[0]
    n_tiles = n_slots // TM
    rows = t_all // LANES
    onehot = (bucket.reshape(rows, LANES, 1) == jnp.arange(N_BUCKETS, dtype=jnp.int32)).astype(F32)
    earlier = (jnp.arange(LANES)[:, None] > jnp.arange(LANES)[None, :]).astype(F32)
    within = jnp.einsum("ts,rsb->rtb", earlier, onehot)
    row_total = jnp.sum(onehot, axis=1)
    row_start = jnp.cumsum(row_total, axis=0) - row_total
    counts = jnp.sum(row_total, axis=0).astype(jnp.int32)
    padded = ((counts + TM - 1) // TM) * TM
    ends = jnp.cumsum(padded)
    starts = ends - padded
    slot = within + (row_start + starts.astype(F32))[:, None, :]
    dest = jnp.sum(slot * onehot, axis=-1).reshape(t_all).astype(jnp.int32)
    order = jnp.concatenate([jnp.argsort(bucket, stable=True).astype(jnp.int32), jnp.zeros((TM,), jnp.int32)])
    tile_start = jnp.arange(n_tiles, dtype=jnp.int32) * TM
    owner_start = jnp.minimum(tile_start, jnp.maximum(ends[-1:] - TM, 0))[:, None]
    owner = jnp.logical_and(owner_start >= starts[None, :], owner_start < ends[None, :]).astype(jnp.int32)
    remaining = counts[None, :] - (tile_start[:, None] - starts[None, :])
    nvalid = jnp.sum(owner * jnp.clip(remaining, 0, TM), axis=1)
    unpadded_start = jnp.cumsum(counts) - counts
    first = jnp.sum(owner * (unpadded_start[None, :] + tile_start[:, None] - starts[None, :]), axis=1)
    first = jnp.clip(first, 0, t_all)
    bucket_ids = np.arange(N_BUCKETS)
    first_expert = (bucket_ids // N_PAIRS) * EXPERTS_PER_GROUP
    e_lo = jnp.sum(owner * jnp.asarray(first_expert + _PAIR_LO[bucket_ids % N_PAIRS], jnp.int32)[None, :], axis=1)
    e_hi = jnp.sum(owner * jnp.asarray(first_expert + _PAIR_HI[bucket_ids % N_PAIRS], jnp.int32)[None, :], axis=1)
    return order, first.astype(jnp.int32), dest, e_lo.astype(jnp.int32), e_hi.astype(jnp.int32), nvalid.astype(jnp.int32)


def _layer(xp, xs, in_ln_g, in_ln_b, w_in, attn_sink, gmlp_w_s, gmlp_b_s, gmlp_ln_g, gmlp_ln_b,
           w_attn_branch, w_gmlp_branch, w_out, ln1_g, ln1_b,
           router_w_group, router_b_group, router_w_expert, router_b_expert,
           w_expert_gate, w_expert_up, w_expert_down, ln2_g, ln2_b):
    t_p, t_s = xp.shape[0], xs.shape[0]
    t_all = t_p + t_s
    row = lambda p: p.reshape(1, -1).astype(F32)

    rows_of = lambda w: w.astype(F32).reshape(-1, w.shape[-1])
    xn, q, k, v, gu, vgn, sa, sb, wa, wb, wo, wd = _proj_call(
        xp, xs, row(in_ln_g), row(in_ln_b), w_in.astype(BF16), gmlp_ln_g.astype(F32), gmlp_ln_b.astype(F32),
        [rows_of(w_attn_branch), rows_of(w_gmlp_branch), rows_of(w_out), rows_of(w_expert_down)])

    wr = jnp.concatenate([router_w_group, router_w_expert], axis=1).astype(F32)
    wr = jnp.pad(wr, ((0, 0), (0, LANES - wr.shape[1])))
    wr_hi = wr.astype(BF16)
    wr_lo = (wr - wr_hi.astype(F32)).astype(BF16)
    br = jnp.pad(jnp.concatenate([router_b_group, router_b_expert]).astype(F32),
                 (0, LANES - N_EXPERT_GROUPS - N_EXPERTS)).reshape(1, LANES)
    bs = jnp.broadcast_to(gmlp_b_s.astype(F32)[:, :, None], (GMLP_GROUPS, BLOCK, BLOCK))

    x1t, routed, wgu = _mix_call(t_p, attn_sink.astype(F32), xn,
                                    q, k, v, gu, vgn, sa, sb, gmlp_w_s.astype(BF16), bs, wa, wb, wo,
                                    row(ln1_g), row(ln1_b), jnp.concatenate([wr_hi, wr_lo], axis=1), br,
                                    [rows_of(w_expert_gate), rows_of(w_expert_up)])

    n_slots = t_all + N_BUCKETS * TM
    bucket = routed[ROUTE_ROWS - 1].astype(jnp.int32)
    order, first, dest, e_lo, e_hi, nvalid = _bucket_layout(bucket, n_slots)
    zsorted = _moe_call(e_lo, e_hi, nvalid, first, order, x1t,
                        wgu.reshape(N_EXPERTS, D_MODEL, 2 * EXPERT_FF), wd.reshape(w_expert_down.shape))
    g2, b2 = row(ln2_g), row(ln2_b)
    return (_unsort_call(dest[:t_p], zsorted, g2, b2, "unsort_prompt"),
            _unsort_call(dest[t_p:], zsorted, g2, b2, "unsort_sample"))


def kernel(x_prompt, x_sample, in_ln_g, in_ln_b, w_in, attn_sink, gmlp_w_s, gmlp_b_s, gmlp_ln_g, gmlp_ln_b,
           w_attn_branch, w_gmlp_branch, w_out, ln1_g, ln1_b,
           router_w_group, router_b_group, router_w_expert, router_b_expert,
           w_expert_gate, w_expert_up, w_expert_down, ln2_g, ln2_b):
    bp, sp, d = x_prompt.shape
    bs, ss, _ = x_sample.shape
    assert bp == 1 and bs == 1 and d == D_MODEL and sp % TM == 0 and ss % TM == 0
    assert w_in.shape[0] == 1, "one layer"
    yp, ys = _layer(x_prompt.reshape(sp, d), x_sample.reshape(ss, d), in_ln_g, in_ln_b, w_in[0], attn_sink[0],
                    gmlp_w_s[0], gmlp_b_s[0], gmlp_ln_g[0], gmlp_ln_b[0],
                    w_attn_branch[0], w_gmlp_branch[0], w_out[0], ln1_g[0], ln1_b[0],
                    router_w_group[0], router_b_group[0], router_w_expert[0], router_b_expert[0],
                    w_expert_gate[0], w_expert_up[0], w_expert_down[0], ln2_g[0], ln2_b[0])
    return yp.reshape(1, sp, d), ys.reshape(1, ss, d)
```

```python
import functools

import numpy as np
import jax
import jax.numpy as jnp
from jax import lax
from jax.experimental import pallas as pl
from jax.experimental.pallas import tpu as pltpu

F32 = jnp.float32
BF16 = jnp.bfloat16

D_MODEL = 2048
HEAD_DIM = 128
N_Q_HEADS = 8
N_KV_HEADS = 2
Q_PER_KV = N_Q_HEADS // N_KV_HEADS
ATTN_WIDTH = N_Q_HEADS * HEAD_DIM
KV_WIDTH = N_KV_HEADS * HEAD_DIM
WINDOW = 128
BLOCK = 128
GMLP_WIDTH = D_MODEL // 2
GMLP_GROUPS = 8
GMLP_GROUP_DIM = GMLP_WIDTH // GMLP_GROUPS
N_EXPERT_GROUPS = 4
EXPERTS_PER_GROUP = 4
N_EXPERTS = N_EXPERT_GROUPS * EXPERTS_PER_GROUP
EXPERT_FF = 512
LN_EPS = 1e-5
DEEPNORM_ALPHA = 2.0 ** 0.25
NEG_INF = -1e9
ATTN_SCALE = HEAD_DIM ** -0.5

_C_Q = 0
_C_K = _C_Q + ATTN_WIDTH
_C_V = _C_K + KV_WIDTH
_C_U = _C_V + KV_WIDTH
_C_VG = _C_U + GMLP_WIDTH
_C_GA = _C_VG + GMLP_WIDTH
_C_GB = _C_GA + D_MODEL
IN_COLS = _C_GB + D_MODEL

LANES = 128
SUBLANES = 8
BF16_SUBLANES = 16
ROUTE_ROWS = 3
N_PAIRS = 6
N_BUCKETS = N_EXPERT_GROUPS * N_PAIRS
X_ROWS = D_MODEL // LANES
ROW_PITCH = 24
TM = 256
COL_CHUNK = 512
MIX_CHUNK = 256
PROJ_CHUNK = 256
MOE_LOOKAHEAD = 3
MOE_BUFFERS = MOE_LOOKAHEAD + 1
ISSUE_GROUP = 16
UNSORT_LOOKAHEAD = 4
VMEM_LIMIT = 60 * 1024 * 1024

_SLOPES = [float(2.0 ** (-8.0 * (h + 1) / N_Q_HEADS)) for h in range(N_Q_HEADS)]


def _layer_norm(x, g, b):
    mu = jnp.mean(x, axis=-1, keepdims=True)
    xc = x - mu
    var = jnp.mean(xc * xc, axis=-1, keepdims=True)
    return xc * lax.rsqrt(var + LN_EPS) * g + b


def _load_token_major(ref, r):
    return ref[pl.ds(r, TM, stride=ROW_PITCH), :]


def _emit_interleaved(*streams):
    order = []
    for s, (_, costs) in enumerate(streams):
        done = 0.0
        for c in costs:
            order.append(((done + c / 2) / sum(costs), s))
            done += c
    for _, s in sorted(order):
        next(streams[s][0])
    for gen, _ in streams:
        assert next(gen, "done") == "done", "stream has more units than declared"


def _cast_plan(weights, n_steps):
    steps = 1 << (n_steps.bit_length() - 1)
    steps = min([steps] + [w.shape[0] // BF16_SUBLANES for w in weights])
    specs = [pl.BlockSpec((w.shape[0] // steps, w.shape[1]), lambda i, *_: (jnp.minimum(i, steps - 1), 0))
             for w in weights]
    return steps, specs


def _cast_side_job(steps, src_refs, dst_refs):
    @pl.when(pl.program_id(0) < steps)
    def _():
        if len(dst_refs) == len(src_refs):
            for src, dst in zip(src_refs, dst_refs):
                dst[...] = src[...].astype(dst.dtype)
        else:
            (dst,), col = dst_refs, 0
            for src in src_refs:
                dst[:, col:col + src.shape[1]] = src[...].astype(dst.dtype)
                col += src.shape[1]


def _const_spec(shape):
    nd = len(shape)
    return pl.BlockSpec(shape, lambda i, *_: (0,) * nd, pipeline_mode=pl.Buffered(1))


def _proj_kernel(n_prompt_tiles, cast_steps, n_cast, *refs):
    xp_ref, xs_ref, g0_ref, b0_ref, w_ref, lg_ref, lb_ref = refs[:7]
    cast_src = refs[7:7 + n_cast]
    xn_ref, q_ref, k_ref, v_ref, gu_ref, vgn_ref, sa_ref, sb_ref = refs[7 + n_cast:15 + n_cast]
    cast_dst = refs[15 + n_cast:15 + 2 * n_cast]
    xn_scr = refs[15 + 2 * n_cast]
    i = pl.program_id(0)
    _cast_side_job(cast_steps, cast_src, cast_dst)

    @pl.when(i < n_prompt_tiles)
    def _():
        xn_ref[...] = _layer_norm(xp_ref[...], g0_ref[...], b0_ref[...])

    @pl.when(i >= n_prompt_tiles)
    def _():
        xn_ref[...] = _layer_norm(xs_ref[...], g0_ref[...], b0_ref[...])

    xn_scr[...] = xn_ref[...].astype(BF16)
    xn = xn_scr[...]

    def proj(c0, width):
        return jnp.dot(xn, w_ref[:, c0:c0 + width], preferred_element_type=F32)

    for c in range(0, ATTN_WIDTH, COL_CHUNK):
        q_ref[:, c:c + COL_CHUNK] = proj(_C_Q + c, COL_CHUNK).astype(BF16)
    kv = proj(_C_K, 2 * KV_WIDTH)
    k_ref[...] = kv[:, :KV_WIDTH].astype(BF16)
    v_ref[...] = kv[:, KV_WIDTH:].astype(BF16)
    for c in range(0, GMLP_WIDTH, COL_CHUNK):
        gu_ref[:, c:c + COL_CHUNK] = jax.nn.gelu(proj(_C_U + c, COL_CHUNK)).astype(BF16)
    for c in range(0, GMLP_WIDTH, COL_CHUNK):
        vg = jax.nn.gelu(proj(_C_VG + c, COL_CHUNK))
        for j in range(COL_CHUNK // GMLP_GROUP_DIM):
            grp = c // GMLP_GROUP_DIM + j
            blk = vg[:, j * GMLP_GROUP_DIM:(j + 1) * GMLP_GROUP_DIM]
            y = _layer_norm(blk, lg_ref[grp:grp + 1, :], lb_ref[grp:grp + 1, :])
            vgn_ref[:, grp * GMLP_GROUP_DIM:(grp + 1) * GMLP_GROUP_DIM] = y.astype(BF16)
    for c in range(0, D_MODEL, COL_CHUNK):
        sa_ref[:, c:c + COL_CHUNK] = jax.nn.sigmoid(proj(_C_GA + c, COL_CHUNK)).astype(BF16)
    for c in range(0, D_MODEL, COL_CHUNK):
        sb_ref[:, c:c + COL_CHUNK] = jax.nn.sigmoid(proj(_C_GB + c, COL_CHUNK)).astype(BF16)


def _proj_call(xp, xs, g0, b0, w_in, lg, lb, later_weights):
    n_p, n_s = xp.shape[0] // TM, xs.shape[0] // TM
    t_all = xp.shape[0] + xs.shape[0]
    cast_steps, cast_specs = _cast_plan(later_weights, n_p + n_s)
    _, cast_out_specs = _cast_plan(later_weights, n_p + n_s)
    xp_spec = pl.BlockSpec((TM, D_MODEL), lambda i: (jnp.minimum(i, n_p - 1), 0))
    xs_spec = pl.BlockSpec((TM, D_MODEL), lambda i: (jnp.maximum(i - n_p, 0), 0))

    def tok(width):
        return pl.BlockSpec((TM, width), lambda i: (i, 0))

    widths = (ATTN_WIDTH, KV_WIDTH, KV_WIDTH, GMLP_WIDTH, GMLP_WIDTH, D_MODEL, D_MODEL)
    return pl.pallas_call(
        functools.partial(_proj_kernel, n_p, cast_steps, len(later_weights)),
        out_shape=([jax.ShapeDtypeStruct((t_all, D_MODEL), F32)]
                   + [jax.ShapeDtypeStruct((t_all, w), BF16) for w in widths]
                   + [jax.ShapeDtypeStruct(w.shape, BF16) for w in later_weights]),
        grid=(n_p + n_s,),
        in_specs=[xp_spec, xs_spec, _const_spec((1, D_MODEL)), _const_spec((1, D_MODEL)),
                  _const_spec((D_MODEL, IN_COLS)),
                  _const_spec((GMLP_GROUPS, GMLP_GROUP_DIM)), _const_spec((GMLP_GROUPS, GMLP_GROUP_DIM))]
        + cast_specs,
        out_specs=[tok(D_MODEL)] + [tok(w) for w in widths] + cast_out_specs,
        scratch_shapes=[pltpu.VMEM((TM, D_MODEL), BF16)],
        compiler_params=pltpu.CompilerParams(dimension_semantics=("arbitrary",),
                                             vmem_limit_bytes=VMEM_LIMIT),
        name="proj",
    )(xp, xs, g0, b0, w_in, lg, lb, *later_weights)


def _route(logits_t):
    row = lambda j: logits_t[j:j + 1, :]
    gl = [row(j) for j in range(N_EXPERT_GROUPS)]
    gmax, gidx = gl[0], jnp.zeros(gl[0].shape, jnp.int32)
    for j in range(1, N_EXPERT_GROUPS):
        better = gl[j] > gmax
        gmax = jnp.where(better, gl[j], gmax)
        gidx = jnp.where(better, j, gidx)
    gsum = jnp.exp(gl[0] - gmax)
    for j in range(1, N_EXPERT_GROUPS):
        gsum = gsum + jnp.exp(gl[j] - gmax)
    p_group = 1.0 / gsum

    ig = []
    for e in range(EXPERTS_PER_GROUP):
        v = row(N_EXPERT_GROUPS + (N_EXPERT_GROUPS - 1) * EXPERTS_PER_GROUP + e)
        for g in range(N_EXPERT_GROUPS - 2, -1, -1):
            v = jnp.where(gidx == g, row(N_EXPERT_GROUPS + g * EXPERTS_PER_GROUP + e), v)
        ig.append(v)
    v1, i1 = ig[0], jnp.zeros(ig[0].shape, jnp.int32)
    for e in range(1, EXPERTS_PER_GROUP):
        better = ig[e] > v1
        v1 = jnp.where(better, ig[e], v1)
        i1 = jnp.where(better, e, i1)
    v2 = jnp.where(i1 == 0, ig[1], ig[0])
    i2 = jnp.where(i1 == 0, 1, 0).astype(jnp.int32)
    for e in range(1, EXPERTS_PER_GROUP):
        better = jnp.logical_and(i1 != e, ig[e] > v2)
        v2 = jnp.where(better, ig[e], v2)
        i2 = jnp.where(better, e, i2)
    ev = jnp.exp(v2 - v1)
    ssum = 1.0 + ev
    w1 = 1.0 / ssum
    w2 = ev / ssum
    first_lo = i1 < i2
    lo = jnp.minimum(i1, i2)
    hi = jnp.maximum(i1, i2)
    c_lo = jnp.where(first_lo, w1, w2) * p_group
    c_hi = jnp.where(first_lo, w2, w1) * p_group
    pair = jnp.where(lo == 0, hi - 1, jnp.where(lo == 1, hi + 1, N_PAIRS - 1))
    bucket = gidx * N_PAIRS + pair
    return c_lo, c_hi, bucket.astype(F32)


def _mix_kernel(n_tiles, seq_edges_first, seq_edges_last, cast_steps, n_cast, *refs):
    (sink_ref, xn_ref, q_ref, kp_ref, km_ref, kn_ref, vp_ref, vm_ref, vn_ref,
     gu_ref, vgn_ref, sa_ref, sb_ref, ws_ref, bs_ref,
     wa_ref, wb_ref, wo_ref, g1_ref, b1_ref, wr_ref, br_ref) = refs[:22]
    cast_src = refs[22:22 + n_cast]
    x1t_ref, rt_ref = refs[22 + n_cast:24 + n_cast]
    cast_dst = refs[24 + n_cast:25 + n_cast]
    kf_scr, vf_scr, a_scr, sg_scr, mg_scr, mix_scr, s_scr = refs[25 + n_cast:]
    _cast_side_job(cast_steps, cast_src, cast_dst)
    i = pl.program_id(0)
    nblk = TM // BLOCK
    cur = i % 2

    @pl.when(i == 0)
    def _():
        a_scr[1] = jnp.zeros(a_scr.shape[1:], a_scr.dtype)
        sg_scr[1] = jnp.zeros(sg_scr.shape[1:], sg_scr.dtype)
        mix_scr[0] = jnp.zeros(mix_scr.shape[1:], mix_scr.dtype)

    def stage_a():
        tile = jnp.minimum(i, n_tiles - 1)
        kf_scr[0:BLOCK, :] = kp_ref[...]
        kf_scr[BLOCK:BLOCK + TM, :] = km_ref[...]
        kf_scr[BLOCK + TM:, :] = kn_ref[...]
        vf_scr[0:BLOCK, :] = vp_ref[...]
        vf_scr[BLOCK:BLOCK + TM, :] = vm_ref[...]
        vf_scr[BLOCK + TM:, :] = vn_ref[...]
        first_blk = tile * nblk
        last_blk = tile * nblk + nblk - 1
        has_prev = jnp.logical_and(*[first_blk != e for e in seq_edges_first])
        has_next = jnp.logical_and(*[last_blk != e for e in seq_edges_last])
        kj = lax.broadcasted_iota(jnp.int32, (BLOCK, 3 * BLOCK), 1)
        qi = lax.broadcasted_iota(jnp.int32, (BLOCK, 3 * BLOCK), 0)
        dist = jnp.abs(kj - BLOCK - qi)
        in_window = dist <= WINDOW
        dist_f = dist.astype(F32)
        lo_key = jnp.where(has_prev, 0, BLOCK)
        hi_key = jnp.where(has_next, 3 * BLOCK, 2 * BLOCK)
        yield
        for j in range(nblk):
            mask = in_window
            if j == 0:
                mask = jnp.logical_and(mask, kj >= lo_key)
            if j == nblk - 1:
                mask = jnp.logical_and(mask, kj < hi_key)
            r0 = j * BLOCK
            for kvh in range(N_KV_HEADS):
                c0 = kvh * HEAD_DIM
                kb = kf_scr[r0:r0 + 3 * BLOCK, c0:c0 + HEAD_DIM]
                vb = vf_scr[r0:r0 + 3 * BLOCK, c0:c0 + HEAD_DIM]
                heads = [kvh * Q_PER_KV + g for g in range(Q_PER_KV)]
                qs = jnp.concatenate(
                    [q_ref[r0:r0 + BLOCK, h * HEAD_DIM:(h + 1) * HEAD_DIM] for h in heads], axis=0)
                s_scr[...] = lax.dot_general(qs, kb, (((1,), (1,)), ((), ())), preferred_element_type=F32)
                pns = []
                for g, h in enumerate(heads):
                    s = s_scr[g * BLOCK:(g + 1) * BLOCK, :] * ATTN_SCALE
                    s = jnp.where(mask, s + dist_f * (-_SLOPES[h]), NEG_INF)
                    sink = sink_ref[h]
                    m = jnp.maximum(jnp.max(s, axis=-1, keepdims=True), sink)
                    p = jnp.exp(s - m)
                    denom = jnp.sum(p, axis=-1, keepdims=True) + jnp.exp(sink - m)
                    pns.append((p * (1.0 / denom)).astype(BF16))
                    if g < Q_PER_KV - 1:
                        yield
                o_all = jnp.dot(jnp.concatenate(pns, axis=0), vb, preferred_element_type=F32)
                for g, h in enumerate(heads):
                    a_scr[cur, r0:r0 + BLOCK, h * HEAD_DIM:(h + 1) * HEAD_DIM] = (
                        o_all[g * BLOCK:(g + 1) * BLOCK, :].astype(BF16))
                yield
            for grp in range(GMLP_GROUPS):
                c0 = grp * GMLP_GROUP_DIM
                sp = jnp.dot(ws_ref[grp], vgn_ref[r0:r0 + BLOCK, c0:c0 + GMLP_GROUP_DIM],
                             preferred_element_type=F32) + bs_ref[grp]
                u = gu_ref[r0:r0 + BLOCK, c0:c0 + GMLP_GROUP_DIM].astype(F32)
                sg_scr[cur, r0:r0 + BLOCK, c0:c0 + GMLP_GROUP_DIM] = (u * sp).astype(BF16)
            yield

    def stage_b():
        for c in range(0, D_MODEL, PROJ_CHUNK):
            ma = jnp.dot(a_scr[1 - cur], wa_ref[:, c:c + PROJ_CHUNK], preferred_element_type=F32)
            mb = jnp.dot(sg_scr[1 - cur], wb_ref[:, c:c + PROJ_CHUNK], preferred_element_type=F32)
            merged = (sa_ref[:, c:c + PROJ_CHUNK].astype(F32) * ma
                      + sb_ref[:, c:c + PROJ_CHUNK].astype(F32) * mb)
            mg_scr[:, c:c + PROJ_CHUNK] = merged.astype(BF16)
            yield
        for c in range(0, D_MODEL, PROJ_CHUNK):
            mix = jnp.dot(mg_scr[...], wo_ref[:, c:c + PROJ_CHUNK], preferred_element_type=F32)
            mix_scr[1 - cur, :, c:c + PROJ_CHUNK] = DEEPNORM_ALPHA * xn_ref[:, c:c + PROJ_CHUNK] + mix
            yield

    def stage_c():
        mu = jnp.mean(mix_scr[cur], axis=-1, keepdims=True)
        yield
        zc = mix_scr[cur] - mu
        inv = lax.rsqrt(jnp.mean(zc * zc, axis=-1, keepdims=True) + LN_EPS)
        yield
        r = None
        for c in range(0, D_MODEL, MIX_CHUNK):
            x1 = ((mix_scr[cur, :, c:c + MIX_CHUNK] - mu) * inv * g1_ref[:, c:c + MIX_CHUNK]
                  + b1_ref[:, c:c + MIX_CHUNK])
            for k in range(MIX_CHUNK // LANES):
                x1t_ref[pl.ds(c // LANES + k, TM, stride=ROW_PITCH), :] = x1[:, k * LANES:(k + 1) * LANES]
            x_hi = x1.astype(BF16)
            x_lo = (x1 - x_hi.astype(F32)).astype(BF16)
            part = (jnp.dot(x_hi, wr_ref[c:c + MIX_CHUNK, :], preferred_element_type=F32)
                    + jnp.dot(x_lo, wr_ref[c:c + MIX_CHUNK, :], preferred_element_type=F32))
            r = part if r is None else r + part
            yield
        logits = r[:, :LANES] + r[:, LANES:] + br_ref[...]
        c_lo, c_hi, bucket = _route(logits.T)
        routed = jnp.concatenate([c_lo, c_hi, bucket, jnp.zeros((SUBLANES - ROUTE_ROWS, TM), F32)], axis=0)
        rt_ref[...] = routed
        padded = jnp.concatenate([routed, jnp.zeros((LANES - SUBLANES, TM), F32)], axis=0)
        x1t_ref[pl.ds(X_ROWS, TM, stride=ROW_PITCH), :] = padded.T
        for k in range(X_ROWS + 1, ROW_PITCH):
            x1t_ref[pl.ds(k, TM, stride=ROW_PITCH), :] = jnp.zeros((TM, LANES), F32)
        yield

    n_chunks = D_MODEL // MIX_CHUNK
    _emit_interleaved((stage_b(), [512] * (2 * D_MODEL // PROJ_CHUNK)),
                      (stage_a(), [100] + ([600] * N_Q_HEADS + [500]) * nblk),
                      (stage_c(), [1000, 1000] + [550] * n_chunks + [1000]))


def _mix_call(t_prompt, sink, xn, q, k, v, gu, vgn, sa, sb, ws, bs, wa, wb, wo, g1, b1, wr, br, later_weights):
    t_all = xn.shape[0]
    nblk = TM // BLOCK
    blk_p, blk_all = t_prompt // BLOCK, t_all // BLOCK
    n_tiles = t_all // TM
    front = lambda i: jnp.minimum(i, n_tiles - 1)
    back = lambda i: jnp.clip(i - 1, 0, n_tiles - 1)
    last = lambda i: jnp.maximum(i - 2, 0)

    def tok(width, which):
        return pl.BlockSpec((TM, width), lambda i: (which(i), 0))

    prev_spec = pl.BlockSpec((BLOCK, KV_WIDTH), lambda i: (jnp.maximum(front(i) * nblk - 1, 0), 0))
    next_spec = pl.BlockSpec((BLOCK, KV_WIDTH),
                             lambda i: (jnp.minimum((front(i) + 1) * nblk, blk_all - 1), 0))
    kv_spec = tok(KV_WIDTH, front)
    cast_steps, cast_specs = _cast_plan(later_weights, n_tiles + 2)
    joined = jax.ShapeDtypeStruct((later_weights[0].shape[0], sum(w.shape[1] for w in later_weights)), BF16)
    joined_steps, cast_out_specs = _cast_plan([joined], n_tiles + 2)
    assert joined_steps == cast_steps
    kernel = functools.partial(_mix_kernel, n_tiles, (0, blk_p), (blk_p - 1, blk_all - 1),
                               cast_steps, len(later_weights))
    return pl.pallas_call(
        kernel,
        out_shape=[jax.ShapeDtypeStruct((t_all * ROW_PITCH, LANES), F32),
                   jax.ShapeDtypeStruct((SUBLANES, t_all), F32), joined],
        grid=(n_tiles + 2,),
        in_specs=[pl.BlockSpec(memory_space=pltpu.SMEM),
                  tok(D_MODEL, back), tok(ATTN_WIDTH, front), prev_spec, kv_spec, next_spec, prev_spec, kv_spec, next_spec,
                  tok(GMLP_WIDTH, front), tok(GMLP_WIDTH, front), tok(D_MODEL, back), tok(D_MODEL, back),
                  _const_spec((GMLP_GROUPS, BLOCK, BLOCK)), _const_spec((GMLP_GROUPS, BLOCK, BLOCK)),
                  _const_spec((ATTN_WIDTH, D_MODEL)), _const_spec((GMLP_WIDTH, D_MODEL)),
                  _const_spec((D_MODEL, D_MODEL)), _const_spec((1, D_MODEL)), _const_spec((1, D_MODEL)),
                  _const_spec((D_MODEL, 2 * LANES)), _const_spec((1, LANES))] + cast_specs,
        out_specs=[pl.BlockSpec((TM * ROW_PITCH, LANES), lambda i: (last(i), 0)),
                   pl.BlockSpec((SUBLANES, TM), lambda i: (0, last(i)))] + cast_out_specs,
        scratch_shapes=[pltpu.VMEM((TM + 2 * BLOCK, KV_WIDTH), BF16),
                        pltpu.VMEM((TM + 2 * BLOCK, KV_WIDTH), BF16),
                        pltpu.VMEM((2, TM, ATTN_WIDTH), BF16),
                        pltpu.VMEM((2, TM, GMLP_WIDTH), BF16),
                        pltpu.VMEM((TM, D_MODEL), BF16),
                        pltpu.VMEM((2, TM, D_MODEL), F32),
                        pltpu.VMEM((Q_PER_KV * BLOCK, 3 * BLOCK), F32)],
        compiler_params=pltpu.CompilerParams(dimension_semantics=("arbitrary",),
                                             vmem_limit_bytes=VMEM_LIMIT),
        name="mix",
    )(sink, xn, q, k, k, k, v, v, v, gu, vgn, sa, sb, ws, bs, wa, wb, wo, g1, b1, wr, br, *later_weights)


def _gather_copy(src_hbm, dst_buf, sem, src_row, dst_row, n_rows):
    return pltpu.make_async_copy(src_hbm.at[pl.ds(src_row, n_rows)], dst_buf.at[pl.ds(dst_row, n_rows)], sem)


def _start_gather(token_of, src_hbm, dst_buf, sem, n_rows, src_pitch=ROW_PITCH):
    def issue(r, carry):
        src_row = pl.multiple_of(token_of(r) * src_pitch, SUBLANES)
        dst_row = pl.multiple_of(r * ROW_PITCH, SUBLANES)
        _gather_copy(src_hbm, dst_buf, sem, src_row, dst_row, n_rows).start()
        return carry

    lax.fori_loop(0, TM, issue, 0, unroll=8)


def _wait_gather(src_hbm, dst_buf, sem, n_rows):
    _gather_copy(src_hbm, dst_buf, sem, 0, 0, TM * n_rows).wait()


def _idx_specs(n_tiles, lookahead=1):
    def spec(tile_of_step):
        return pl.BlockSpec((None, 1, TM), lambda i, *_: (jnp.minimum(tile_of_step(i), n_tiles - 1), 0, 0),
                            memory_space=pltpu.SMEM)

    return [spec(lambda i, t=t: t) for t in range(lookahead)] + [spec(lambda i: i + lookahead)]


def _unsort_kernel(*refs):
    first_idx_refs, idxn_ref = refs[:UNSORT_LOOKAHEAD], refs[UNSORT_LOOKAHEAD]
    ys_hbm, g2_ref, b2_ref, out_ref, buf, sem = refs[UNSORT_LOOKAHEAD + 1:]
    i = pl.program_id(0)
    n = pl.num_programs(0)
    n_buf = UNSORT_LOOKAHEAD + 1
    slot = i % n_buf

    def gather(tile, idx_ref, when):
        dst = tile % n_buf

        @pl.when(when)
        def _():
            _start_gather(lambda r: idx_ref[0, r], ys_hbm, buf.at[dst], sem.at[dst], X_ROWS, src_pitch=X_ROWS)

    for tile, idx_ref in enumerate(first_idx_refs):
        gather(tile, idx_ref, jnp.logical_and(i == 0, tile < n))
    gather(i + UNSORT_LOOKAHEAD, idxn_ref, i + UNSORT_LOOKAHEAD < n)

    cur = buf.at[slot]
    _wait_gather(ys_hbm, cur, sem.at[slot], X_ROWS)
    for r in range(X_ROWS):
        out_ref[:, r * LANES:(r + 1) * LANES] = _load_token_major(cur, r)
    out_ref[...] = _layer_norm(out_ref[...], g2_ref[...], b2_ref[...])


def _unsort_call(idx, ysorted, g2, b2, name):
    n_tiles = idx.shape[0] // TM
    idx = idx.reshape(n_tiles, 1, TM)
    idx_specs = _idx_specs(n_tiles, UNSORT_LOOKAHEAD)
    n_buf = UNSORT_LOOKAHEAD + 1
    return pl.pallas_call(
        _unsort_kernel,
        out_shape=jax.ShapeDtypeStruct((n_tiles * TM, D_MODEL), F32),
        grid=(n_tiles,),
        in_specs=idx_specs + [pl.BlockSpec(memory_space=pl.ANY),
                              _const_spec((1, D_MODEL)), _const_spec((1, D_MODEL))],
        out_specs=pl.BlockSpec((TM, D_MODEL), lambda i: (i, 0)),
        scratch_shapes=[pltpu.VMEM((n_buf, TM * ROW_PITCH, LANES), F32), pltpu.SemaphoreType.DMA((n_buf,))],
        compiler_params=pltpu.CompilerParams(dimension_semantics=("arbitrary",),
                                             vmem_limit_bytes=VMEM_LIMIT),
        name=name,
    )(*([idx] * len(idx_specs)), ysorted, g2, b2)


def _moe_kernel(elo_ref, ehi_ref, nvalid_ref, first_ref, order_ref, x1t_hbm,
                wgu_lo, wd_lo, wgu_hi, wd_hi, out_ref, buf, sem, xb_scr):
    del elo_ref, ehi_ref
    i = pl.program_id(0)
    n = pl.num_programs(0)
    slot = i % MOE_BUFFERS

    for tile in range(MOE_LOOKAHEAD):
        @pl.when(jnp.logical_and(jnp.logical_and(i == 0, tile < n), nvalid_ref[jnp.minimum(tile, n - 1)] > 0))
        def _(tile=tile):
            first = first_ref[tile]
            _start_gather(lambda r: order_ref[first + r], x1t_hbm, buf.at[tile], sem.at[tile], ROW_PITCH)

    def experts():
        cur = buf.at[slot]
        _wait_gather(x1t_hbm, cur, sem.at[slot], ROW_PITCH)
        for r in range(X_ROWS):
            xb_scr[:, r * LANES:(r + 1) * LANES] = _load_token_major(cur, r).astype(BF16)
        route = _load_token_major(cur, X_ROWS)
        xb = xb_scr[...]
        yield

        def hidden(wgu, c):
            gate_up = jnp.dot(xb, wgu[...], preferred_element_type=F32)
            gate, up = gate_up[:, :EXPERT_FF], gate_up[:, EXPERT_FF:]
            return ((jax.nn.silu(gate) * up) * c).astype(BF16)

        h_lo = hidden(wgu_lo, route[:, 0:1])
        yield
        h_hi = hidden(wgu_hi, route[:, 1:2])
        yield
        for c in range(0, D_MODEL, COL_CHUNK):
            y = (jnp.dot(h_lo, wd_lo[:, c:c + COL_CHUNK], preferred_element_type=F32)
                 + jnp.dot(h_hi, wd_hi[:, c:c + COL_CHUNK], preferred_element_type=F32))
            for r in range(c // LANES, (c + COL_CHUNK) // LANES):
                out_ref[pl.ds(r, TM, stride=X_ROWS), :] = (DEEPNORM_ALPHA * _load_token_major(cur, r)
                                                           + y[:, r * LANES - c:(r + 1) * LANES - c])
            yield

    def start_later_gather():
        dst = (i + MOE_LOOKAHEAD) % MOE_BUFFERS
        first = first_ref[later]
        for r in range(TM):
            src_row = pl.multiple_of(order_ref[first + r] * ROW_PITCH, SUBLANES)
            _gather_copy(x1t_hbm, buf.at[dst], sem.at[dst], src_row, r * ROW_PITCH, ROW_PITCH).start()
            if r % ISSUE_GROUP == ISSUE_GROUP - 1:
                yield

    expert_costs = [1, 4, 4] + [2] * (D_MODEL // COL_CHUNK)
    later = jnp.minimum(i + MOE_LOOKAHEAD, n - 1)
    gather_later = jnp.logical_and(i + MOE_LOOKAHEAD < n, nvalid_ref[later] > 0)

    @pl.when(jnp.logical_and(nvalid_ref[i] > 0, gather_later))
    def _():
        _emit_interleaved((experts(), expert_costs), (start_later_gather(), [1] * (TM // ISSUE_GROUP)))

    @pl.when(jnp.logical_and(nvalid_ref[i] > 0, jnp.logical_not(gather_later)))
    def _():
        _emit_interleaved((experts(), expert_costs))

    @pl.when(nvalid_ref[i] == 0)
    def _():
        out_ref[...] = jnp.zeros(out_ref.shape, out_ref.dtype)


def _moe_call(e_lo, e_hi, nvalid, first, order, x1t, wgu, wd):
    n_tiles = nvalid.shape[0]

    def w_spec(shape, which):
        def imap(i, elo, ehi, *_):
            return ((elo, ehi)[which][i], 0, 0)
        return pl.BlockSpec((None,) + shape, imap)

    up_shape, down_shape = (D_MODEL, 2 * EXPERT_FF), (EXPERT_FF, D_MODEL)
    grid_spec = pltpu.PrefetchScalarGridSpec(
        num_scalar_prefetch=5,
        grid=(n_tiles,),
        in_specs=[pl.BlockSpec(memory_space=pl.ANY),
                  w_spec(up_shape, 0), w_spec(down_shape, 0), w_spec(up_shape, 1), w_spec(down_shape, 1)],
        out_specs=pl.BlockSpec((TM * X_ROWS, LANES), lambda i, *_: (i, 0)),
        scratch_shapes=[pltpu.VMEM((MOE_BUFFERS, TM * ROW_PITCH, LANES), F32),
                        pltpu.SemaphoreType.DMA((MOE_BUFFERS,)),
                        pltpu.VMEM((TM, D_MODEL), BF16)],
    )
    return pl.pallas_call(
        _moe_kernel,
        out_shape=jax.ShapeDtypeStruct((n_tiles * TM * X_ROWS, LANES), F32),
        grid_spec=grid_spec,
        compiler_params=pltpu.CompilerParams(dimension_semantics=("arbitrary",),
                                             vmem_limit_bytes=VMEM_LIMIT),
        name="moe",
    )(e_lo, e_hi, nvalid, first, order, x1t, wgu, wd, wgu, wd)


_PAIR_LO = np.array([0, 0, 0, 1, 1, 2], np.int32)
_PAIR_HI = np.array([1, 2, 3, 2, 3, 3], np.int32)


def _bucket_layout(bucket, n_slots):
    t_all = bucket.shape[0]
    n_tiles = n_slots // TM
    rows = t_all // LANES
    onehot = (bucket.reshape(rows, LANES, 1) == jnp.arange(N_BUCKETS, dtype=jnp.int32)).astype(F32)
    earlier = (jnp.arange(LANES)[:, None] > jnp.arange(LANES)[None, :]).astype(F32)
    within = jnp.einsum("ts,rsb->rtb", earlier, onehot)
    row_total = jnp.sum(onehot, axis=1)
    row_start = jnp.cumsum(row_total, axis=0) - row_total
    counts = jnp.sum(row_total, axis=0).astype(jnp.int32)
    padded = ((counts + TM - 1) // TM) * TM
    ends = jnp.cumsum(padded)
    starts = ends - padded
    slot = within + (row_start + starts.astype(F32))[:, None, :]
    dest = jnp.sum(slot * onehot, axis=-1).reshape(t_all).astype(jnp.int32)
    order = jnp.concatenate([jnp.argsort(bucket, stable=True).astype(jnp.int32), jnp.zeros((TM,), jnp.int32)])
    tile_start = jnp.arange(n_tiles, dtype=jnp.int32) * TM
    owner_start = jnp.minimum(tile_start, jnp.maximum(ends[-1:] - TM, 0))[:, None]
    owner = jnp.logical_and(owner_start >= starts[None, :], owner_start < ends[None, :]).astype(jnp.int32)
    remaining = counts[None, :] - (tile_start[:, None] - starts[None, :])
    nvalid = jnp.sum(owner * jnp.clip(remaining, 0, TM), axis=1)
    unpadded_start = jnp.cumsum(counts) - counts
    first = jnp.sum(owner * (unpadded_start[None, :] + tile_start[:, None] - starts[None, :]), axis=1)
    first = jnp.clip(first, 0, t_all)
    bucket_ids = np.arange(N_BUCKETS)
    first_expert = (bucket_ids // N_PAIRS) * EXPERTS_PER_GROUP
    e_lo = jnp.sum(owner * jnp.asarray(first_expert + _PAIR_LO[bucket_ids % N_PAIRS], jnp.int32)[None, :], axis=1)
    e_hi = jnp.sum(owner * jnp.asarray(first_expert + _PAIR_HI[bucket_ids % N_PAIRS], jnp.int32)[None, :], axis=1)
    return order, first.astype(jnp.int32), dest, e_lo.astype(jnp.int32), e_hi.astype(jnp.int32), nvalid.astype(jnp.int32)


def _layer(xp, xs, in_ln_g, in_ln_b, w_in, attn_sink, gmlp_w_s, gmlp_b_s, gmlp_ln_g, gmlp_ln_b,
           w_attn_branch, w_gmlp_branch, w_out, ln1_g, ln1_b,
           router_w_group, router_b_group, router_w_expert, router_b_expert,
           w_expert_gate, w_expert_up, w_expert_down, ln2_g, ln2_b):
    t_p, t_s = xp.shape[0], xs.shape[0]
    t_all = t_p + t_s
    row = lambda p: p.reshape(1, -1).astype(F32)

    rows_of = lambda w: w.astype(F32).reshape(-1, w.shape[-1])
    xn, q, k, v, gu, vgn, sa, sb, wa, wb, wo, wd = _proj_call(
        xp, xs, row(in_ln_g), row(in_ln_b), w_in.astype(BF16), gmlp_ln_g.astype(F32), gmlp_ln_b.astype(F32),
        [rows_of(w_attn_branch), rows_of(w_gmlp_branch), rows_of(w_out), rows_of(w_expert_down)])

    wr = jnp.concatenate([router_w_group, router_w_expert], axis=1).astype(F32)
    wr = jnp.pad(wr, ((0, 0), (0, LANES - wr.shape[1])))
    wr_hi = wr.astype(BF16)
    wr_lo = (wr - wr_hi.astype(F32)).astype(BF16)
    br = jnp.pad(jnp.concatenate([router_b_group, router_b_expert]).astype(F32),
                 (0, LANES - N_EXPERT_GROUPS - N_EXPERTS)).reshape(1, LANES)
    bs = jnp.broadcast_to(gmlp_b_s.astype(F32)[:, :, None], (GMLP_GROUPS, BLOCK, BLOCK))

    x1t, routed, wgu = _mix_call(t_p, attn_sink.astype(F32), xn,
                                    q, k, v, gu, vgn, sa, sb, gmlp_w_s.astype(BF16), bs, wa, wb, wo,
                                    row(ln1_g), row(ln1_b), jnp.concatenate([wr_hi, wr_lo], axis=1), br,
                                    [rows_of(w_expert_gate), rows_of(w_expert_up)])

    n_slots = t_all + N_BUCKETS * TM
    bucket = routed[ROUTE_ROWS - 1].astype(jnp.int32)
    order, first, dest, e_lo, e_hi, nvalid = _bucket_layout(bucket, n_slots)
    zsorted = _moe_call(e_lo, e_hi, nvalid, first, order, x1t,
                        wgu.reshape(N_EXPERTS, D_MODEL, 2 * EXPERT_FF), wd.reshape(w_expert_down.shape))
    g2, b2 = row(ln2_g), row(ln2_b)
    return (_unsort_call(dest[:t_p], zsorted, g2, b2, "unsort_prompt"),
            _unsort_call(dest[t_p:], zsorted, g2, b2, "unsort_sample"))


def kernel(x_prompt, x_sample, in_ln_g, in_ln_b, w_in, attn_sink, gmlp_w_s, gmlp_b_s, gmlp_ln_g, gmlp_ln_b,
           w_attn_branch, w_gmlp_branch, w_out, ln1_g, ln1_b,
           router_w_group, router_b_group, router_w_expert, router_b_expert,
           w_expert_gate, w_expert_up, w_expert_down, ln2_g, ln2_b):
    bp, sp, d = x_prompt.shape
    bs, ss, _ = x_sample.shape
    assert bp == 1 and bs == 1 and d == D_MODEL and sp % TM == 0 and ss % TM == 0
    assert w_in.shape[0] == 1, "one layer"
    yp, ys = _layer(x_prompt.reshape(sp, d), x_sample.reshape(ss, d), in_ln_g, in_ln_b, w_in[0], attn_sink[0],
                    gmlp_w_s[0], gmlp_b_s[0], gmlp_ln_g[0], gmlp_ln_b[0],
                    w_attn_branch[0], w_gmlp_branch[0], w_out[0], ln1_g[0], ln1_b[0],
                    router_w_group[0], router_b_group[0], router_w_expert[0], router_b_expert[0],
                    w_expert_gate[0], w_expert_up[0], w_expert_down[0], ln2_g[0], ln2_b[0])
    return yp.reshape(1, sp, d), ys.reshape(1, ss, d)
```
